```python
import math
import jax, jax.numpy as jnp
from jax import lax
import numpy as np

D_MODEL = 1024
BATCH = 8
SEQ = 2048
DEPTH = 2

GRID_W = 64
D_MIX = 256
N_BRANCH = 4
EPS = 1e-6
HY_ORDER = 2
HY_SHORT = 3
HY_BANDS = 16
HY_EMB = 2 * HY_BANDS + 1
HY_FFN = 64
HY_TARGET = 1e-2
HY_FAST_DECAY = 0.3
HY_SLOW_DECAY = 1.5
FN_GROUPS = 4
MLA_HEADS = 4
MLA_NOPE = 64
MLA_ROPE = 32
MLA_V = 64
MLA_Q_RANK = 256
MLA_KV_RANK = 128
ROPE_THETA = 10000.0
Q_BLOCK = 128
NA_HEADS = 4
NA_HEAD_DIM = D_MIX // NA_HEADS
NA_WIN_R = 8
NA_WIN_C = 16
N_EXPERTS = 16
EC_CAPACITY_FACTOR = 2
D_FF_EXPERT = 1024
PLE_DIM = 256

HY_COLS = 3 * D_MIX
FN_COLS = D_MIX
NA_COLS = 3 * D_MIX
GATE_COLS = N_BRANCH * D_MODEL
IN_SIZES = (HY_COLS, FN_COLS, MLA_Q_RANK, MLA_KV_RANK, MLA_ROPE, NA_COLS, GATE_COLS)
IN_COLS = sum(IN_SIZES)

kernel_name = "hybrid_hyena_fnet_mla_natten_ec_moe_encoder"


def rmsnorm(x, g):
    xf = x.astype(jnp.float32)
    y = xf * lax.rsqrt(jnp.mean(xf * xf, axis=-1, keepdims=True) + EPS)
    return (y * g.astype(jnp.float32)).astype(x.dtype)


def hyena_filters(L, w1, b1, freq, w2, b2, w3):
    f32 = jnp.float32
    t01 = jnp.linspace(0.0, 1.0, L, dtype=f32)[:, None]
    bands = jnp.linspace(1e-4, HY_BANDS - 1, HY_BANDS, dtype=f32)
    ang = 2.0 * math.pi * jnp.arange(L, dtype=f32)[:, None] * bands / L
    z = jnp.concatenate([t01, jnp.cos(ang), -jnp.sin(ang)], axis=-1)
    freq = freq.astype(f32)
    hf = jnp.sin(freq[0] * (z @ w1.astype(f32) + b1.astype(f32)))
    hf = jnp.sin(freq[1] * (hf @ w2.astype(f32) + b2.astype(f32)))
    hf = hf @ w3.astype(f32)
    max_decay = math.log(HY_TARGET) / HY_FAST_DECAY
    min_decay = math.log(HY_TARGET) / HY_SLOW_DECAY
    deltas = jnp.linspace(min_decay, max_decay, D_MIX, dtype=f32)
    window = jnp.exp(-t01 * jnp.abs(deltas))
    hf = hf.reshape(L, HY_ORDER, 2, D_MIX) * window[:, None, None, :]
    fwd, bwd = hf[:, :, 0], hf[:, :, 1]
    k = jnp.concatenate([fwd, jnp.zeros((1, HY_ORDER, D_MIX), f32), bwd[1:][::-1]], axis=0)
    k = k * lax.rsqrt(jnp.sum(k * k, axis=0, keepdims=True) + EPS)
    return k.transpose(1, 0, 2)


def fft_long_conv(z, k, skip):
    L = z.shape[1]
    zf = jnp.fft.rfft(z, n=2 * L, axis=1)
    kf = jnp.fft.rfft(k, axis=0)
    y = jnp.fft.irfft(zf * kf, n=2 * L, axis=1)[:, :L]
    return y + z * skip.astype(jnp.float32)


def hyena_branch(u, conv_w, conv_b, w1, b1, freq, w2, b2, w3, skip):
    L = u.shape[1]
    up = jnp.pad(u, ((0, 0), (1, 1), (0, 0)))
    uc = up[:, :-2] * conv_w[0] + up[:, 1:-1] * conv_w[1] + up[:, 2:] * conv_w[2] + conv_b
    x1, x2, v = jnp.split(uc.astype(jnp.float32), 3, axis=-1)
    filt = hyena_filters(L, w1, b1, freq, w2, b2, w3)
    z = x1 * fft_long_conv(v, filt[0], skip[0])
    z = x2 * fft_long_conv(z, filt[1], skip[1])
    return z.astype(u.dtype)


def fnet_branch(u):
    B, L, _ = u.shape
    ug = u.astype(jnp.float32).reshape(B, L, FN_GROUPS, D_MIX // FN_GROUPS)
    y = jnp.fft.fft2(ug, axes=(1, 3), norm="ortho").real
    return y.reshape(B, L, D_MIX).astype(u.dtype)


def rope_tables(L):
    inv = ROPE_THETA ** (-jnp.arange(0, MLA_ROPE, 2, dtype=jnp.float32) / MLA_ROPE)
    ang = jnp.arange(L, dtype=jnp.float32)[:, None] * inv
    return jnp.cos(ang), jnp.sin(ang)


def apply_rope(x, cos, sin):
    x1, x2 = jnp.split(x, 2, axis=-1)
    return jnp.concatenate([x1 * cos - x2 * sin, x2 * cos + x1 * sin], axis=-1).astype(x.dtype)


def mla_branch(c_q_raw, c_kv_raw, k_pe_raw, q_norm_g, w_uq, kv_norm_g, w_ukv):
    B, L, _ = c_q_raw.shape
    H = MLA_HEADS
    cos, sin = rope_tables(L)
    q = (rmsnorm(c_q_raw, q_norm_g) @ w_uq).reshape(B, L, H, MLA_NOPE + MLA_ROPE)
    q_nope, q_pe = q[..., :MLA_NOPE], apply_rope(q[..., MLA_NOPE:], cos[:, None], sin[:, None])
    kv = (rmsnorm(c_kv_raw, kv_norm_g) @ w_ukv).reshape(B, L, H, MLA_NOPE + MLA_V)
    k_nope, v = kv[..., :MLA_NOPE], kv[..., MLA_NOPE:]
    k_pe = apply_rope(k_pe_raw, cos, sin)
    scale = (MLA_NOPE + MLA_ROPE) ** -0.5
    nb = L // Q_BLOCK

    def block(args):
        qn, qp = args
        s = jnp.einsum('bqhd,bkhd->bhqk', qn, k_nope) + jnp.einsum('bqhr,bkr->bhqk', qp, k_pe)
        pr = jax.nn.softmax(s.astype(jnp.float32) * scale, axis=-1).astype(v.dtype)
        return jnp.einsum('bhqk,bkhd->bqhd', pr, v)

    qn_b = q_nope.reshape(B, nb, Q_BLOCK, H, MLA_NOPE).swapaxes(0, 1)
    qp_b = q_pe.reshape(B, nb, Q_BLOCK, H, MLA_ROPE).swapaxes(0, 1)
    o = lax.map(block, (qn_b, qp_b))
    return o.swapaxes(0, 1).reshape(B, L, H * MLA_V)


def neighborhood_branch(u, rpb):
    B, L, _ = u.shape
    H, d = NA_HEADS, NA_HEAD_DIM
    R = L // GRID_W
    wr = min(NA_WIN_R, R)
    q, k, v = jnp.split(u, 3, axis=-1)
    r = jnp.arange(R)
    rows = jnp.clip(r - wr // 2, 0, R - wr)[:, None] + jnp.arange(wr)
    c = jnp.arange(GRID_W)
    cs = jnp.clip(c - NA_WIN_C // 2, 0, GRID_W - NA_WIN_C)
    col_mask = (c[None, :] >= cs[:, None]) & (c[None, :] < cs[:, None] + NA_WIN_C)
    qg = q.reshape(B, R, GRID_W, H, d)
    kg = k.reshape(B, R, GRID_W, H, d)[:, rows]
    vg = v.reshape(B, R, GRID_W, H, d)[:, rows]
    s = jnp.einsum('brqhd,brikhd->bhrqik', qg, kg).astype(jnp.float32) * (d ** -0.5)
    dr = rows - r[:, None] + (NA_WIN_R - 1)
    dc = jnp.clip(c[None, :] - c[:, None] + (NA_WIN_C - 1), 0, 2 * NA_WIN_C - 2)
    bias = rpb.astype(jnp.float32)[:, dr][..., dc]
    s = s + bias.transpose(0, 1, 3, 2, 4)[None]
    s = jnp.where(col_mask[:, None, :], s, -jnp.inf)
    pr = jax.nn.softmax(s, axis=(-2, -1)).astype(u.dtype)
    o = jnp.einsum('bhrqik,brikhd->brqhd', pr, vg)
    return o.reshape(B, L, H * d)


def expert_choice_ffn(h, w_router, w_gate, w_up, w_down):
    B, L, D = h.shape
    cap = EC_CAPACITY_FACTOR * L // N_EXPERTS
    logits = jnp.einsum('bld,de->ble', h, w_router).astype(jnp.float32)
    aff = jax.nn.softmax(logits, axis=-1)
    top_aff, top_idx = lax.top_k(aff.transpose(0, 2, 1), cap)
    xe = jax.vmap(lambda hb, ib: hb[ib])(h, top_idx)
    g = jnp.einsum('becd,edf->becf', xe, w_gate)
    up = jnp.einsum('becd,edf->becf', xe, w_up)
    ye = jnp.einsum('becf,efd->becd', jax.nn.silu(g) * up, w_down)
    ye = ye * top_aff[..., None].astype(h.dtype)
    return jnp.zeros_like(h).at[jnp.arange(B)[:, None, None], top_idx].add(ye)


def mixer_sublayer(x, norm1_g, w_in, b_gate, hy_conv_w, hy_conv_b, hf_w1, hf_b1, hf_freq,
                   hf_w2, hf_b2, hf_w3, hy_skip, q_norm_g, w_uq, kv_norm_g, w_ukv, rpb,
                   w_br, w_out):
    B, L, D = x.shape
    h = rmsnorm(x, norm1_g)
    u = h @ w_in
    offsets = np.cumsum(IN_SIZES)[:-1].tolist()
    u_hy, u_fn, u_cq, u_ckv, u_kpe, u_na, u_gate = jnp.split(u, offsets, axis=-1)
    y_hy = hyena_branch(u_hy, hy_conv_w, hy_conv_b, hf_w1, hf_b1, hf_freq, hf_w2, hf_b2, hf_w3, hy_skip)
    y_fn = fnet_branch(u_fn)
    y_mla = mla_branch(u_cq, u_ckv, u_kpe, q_norm_g, w_uq, kv_norm_g, w_ukv)
    y_na = neighborhood_branch(u_na, rpb)
    branches = jnp.stack([y_hy, y_fn, y_mla, y_na], axis=2)
    proj = jnp.einsum('blnc,ncd->blnd', branches, w_br)
    gates = jax.nn.sigmoid(u_gate + b_gate).reshape(B, L, N_BRANCH, D)
    merged = jnp.sum(gates * proj, axis=2)
    return x + merged @ w_out


def setup_inputs(seed: int = 0) -> dict:
    key = jax.random.key(seed)
    ks = jax.random.split(key, 30)
    f32 = jnp.float32

    def nrm(k, shape, scale):
        return jax.random.normal(k, shape, f32) * scale

    def gain(k, shape):
        return 1.0 + 0.01 * jax.random.normal(k, shape, f32)

    return {
        "x": nrm(ks[0], (BATCH, SEQ, D_MODEL), 1.0),
        "p": nrm(ks[1], (DEPTH, BATCH, SEQ, PLE_DIM), 1.0),
        "norm1_g": gain(ks[2], (DEPTH, D_MODEL)),
        "w_in": nrm(ks[3], (DEPTH, D_MODEL, IN_COLS), D_MODEL ** -0.5),
        "b_gate": nrm(ks[4], (DEPTH, GATE_COLS), 0.02),
        "hy_conv_w": nrm(ks[5], (DEPTH, HY_SHORT, HY_COLS), HY_SHORT ** -0.5),
        "hy_conv_b": nrm(ks[6], (DEPTH, HY_COLS), 0.02),
        "hf_w1": nrm(ks[7], (DEPTH, HY_EMB, HY_FFN), HY_EMB ** -0.5),
        "hf_b1": nrm(ks[8], (DEPTH, HY_FFN), 0.02),
        "hf_freq": gain(ks[9], (DEPTH, 2, HY_FFN)),
        "hf_w2": nrm(ks[10], (DEPTH, HY_FFN, HY_FFN), HY_FFN ** -0.5),
        "hf_b2": nrm(ks[11], (DEPTH, HY_FFN), 0.02),
        "hf_w3": nrm(ks[12], (DEPTH, HY_FFN, HY_ORDER * 2 * D_MIX), HY_FFN ** -0.5),
        "hy_skip": nrm(ks[13], (DEPTH, HY_ORDER, D_MIX), 0.5),
        "q_norm_g": gain(ks[14], (DEPTH, MLA_Q_RANK)),
        "w_uq": nrm(ks[15], (DEPTH, MLA_Q_RANK, MLA_HEADS * (MLA_NOPE + MLA_ROPE)), MLA_Q_RANK ** -0.5),
        "kv_norm_g": gain(ks[16], (DEPTH, MLA_KV_RANK)),
        "w_ukv": nrm(ks[17], (DEPTH, MLA_KV_RANK, MLA_HEADS * (MLA_NOPE + MLA_V)), MLA_KV_RANK ** -0.5),
        "rpb": nrm(ks[18], (DEPTH, NA_HEADS, 2 * NA_WIN_R - 1, 2 * NA_WIN_C - 1), 0.02),
        "w_br": nrm(ks[19], (DEPTH, N_BRANCH, D_MIX, D_MODEL), D_MIX ** -0.5),
        "w_out": nrm(ks[20], (DEPTH, D_MODEL, D_MODEL), D_MODEL ** -0.5),
        "norm2_g": gain(ks[21], (DEPTH, D_MODEL)),
        "w_router": nrm(ks[22], (DEPTH, D_MODEL, N_EXPERTS), D_MODEL ** -0.5),
        "w_e_gate": nrm(ks[23], (DEPTH, N_EXPERTS, D_MODEL, D_FF_EXPERT), D_MODEL ** -0.5),
        "w_e_up": nrm(ks[24], (DEPTH, N_EXPERTS, D_MODEL, D_FF_EXPERT), D_MODEL ** -0.5),
        "w_e_down": nrm(ks[25], (DEPTH, N_EXPERTS, D_FF_EXPERT, D_MODEL), D_FF_EXPERT ** -0.5),
        "norm3_g": gain(ks[26], (DEPTH, D_MODEL)),
        "w_ple_gate": nrm(ks[27], (DEPTH, D_MODEL, D_MODEL), D_MODEL ** -0.5),
        "w_ple_proj": nrm(ks[28], (DEPTH, PLE_DIM, D_MODEL), PLE_DIM ** -0.5),
        "final_g": gain(ks[29], (D_MODEL,)),
    }


def reference(x, p, norm1_g, w_in, b_gate, hy_conv_w, hy_conv_b, hf_w1, hf_b1, hf_freq,
              hf_w2, hf_b2, hf_w3, hy_skip, q_norm_g, w_uq, kv_norm_g, w_ukv, rpb, w_br,
              w_out, norm2_g, w_router, w_e_gate, w_e_up, w_e_down, norm3_g, w_ple_gate,
              w_ple_proj, final_g):
    for i in range(DEPTH):
        x = mixer_sublayer(x, norm1_g[i], w_in[i], b_gate[i], hy_conv_w[i], hy_conv_b[i],
                           hf_w1[i], hf_b1[i], hf_freq[i], hf_w2[i], hf_b2[i], hf_w3[i],
                           hy_skip[i], q_norm_g[i], w_uq[i], kv_norm_g[i], w_ukv[i], rpb[i],
                           w_br[i], w_out[i])
        x = x + expert_choice_ffn(rmsnorm(x, norm2_g[i]), w_router[i], w_e_gate[i],
                                  w_e_up[i], w_e_down[i])
        gate = jax.nn.sigmoid(rmsnorm(x, norm3_g[i]) @ w_ple_gate[i])
        x = x + gate * (p[i] @ w_ple_proj[i])
    return rmsnorm(x, final_g)
```

```python
import functools
import math

import jax
import jax.numpy as jnp
from jax import lax
from jax.experimental import pallas as pl
from jax.experimental.pallas import tpu as pltpu

F32 = jnp.float32
BF16 = jnp.bfloat16

D_MODEL = 1024
BATCH = 8
SEQ = 2048
DEPTH = 2
TOKENS = BATCH * SEQ

GRID_W = 64
GRID_R = SEQ // GRID_W
D_MIX = 256
N_BRANCH = 4
EPS = 1e-6
HY_ORDER = 2
HY_BANDS = 16
HY_EMB = 2 * HY_BANDS + 1
HY_FFN = 64
HY_TARGET = 1e-2
HY_FAST_DECAY = 0.3
HY_SLOW_DECAY = 1.5
FN_GROUPS = 4
MLA_HEADS = 4
MLA_NOPE = 64
MLA_ROPE = 32
MLA_V = 64
MLA_Q_RANK = 256
MLA_KV_RANK = 128
ROPE_THETA = 10000.0
NA_HEADS = 4
NA_HEAD_DIM = D_MIX // NA_HEADS
NA_WIN_R = 8
NA_WIN_C = 16
N_EXPERTS = 16
CAPACITY = 2 * SEQ // N_EXPERTS
D_FF = 1024
PLE_DIM = 256

HY_COLS = 3 * D_MIX
OFF_FN = HY_COLS
OFF_CQ = OFF_FN + D_MIX
OFF_CKV = OFF_CQ + MLA_Q_RANK
OFF_KPE = OFF_CKV + MLA_KV_RANK
OFF_NA = OFF_KPE + MLA_ROPE
OFF_GATE = OFF_NA + 3 * D_MIX

LANES = 128
MLA_HEAD_PAD = 128
MLA_QK = MLA_HEADS * MLA_HEAD_PAD
WA_COLS = 1536
NEG_BIG = -1e30

TM = 512
NA_QROWS = 8
NA_KROWS = 16
NA_QBLK = NA_QROWS * GRID_W
NA_KBLK = NA_KROWS * GRID_W
VMEM_LIMIT = 56 * 1024 * 1024


def _params(*sem):
    return pltpu.CompilerParams(dimension_semantics=sem, vmem_limit_bytes=VMEM_LIMIT)


def _const_spec(shape):
    nd = len(shape)
    return pl.BlockSpec(shape, lambda *_: (0,) * nd, pipeline_mode=pl.Buffered(1))


def _rms(x, g):
    return x * lax.rsqrt(jnp.mean(x * x, axis=-1, keepdims=True) + EPS) * g


def _dot(a, b):
    return jnp.dot(a, b, preferred_element_type=F32)


def _dot_nt(a, b):
    return lax.dot_general(a, b, (((1,), (1,)), ((), ())), preferred_element_type=F32)


def _split2(x):
    hi = x.astype(BF16)
    lo = (x - hi.astype(F32)).astype(BF16)
    return hi, lo


def _inproj_kernel(x_ref, g1_ref, wa_ref, wna_ref, qg_ref, wqa_ref, wqb_ref, kvg_ref,
                   wk_ref, wv_ref, epe_ref, cosq_ref, sinq_ref, csk_ref,
                   uhy_ref, ufn_ref, q_ref, k_ref, v_ref, naq_ref, nak_ref, nav_ref):
    h = _rms(x_ref[...], g1_ref[...]).astype(BF16)
    ua = _dot(h, wa_ref[...])
    uhy_ref[...] = ua[:, :HY_COLS]
    ufn_ref[...] = ua[:, OFF_FN:OFF_CQ]
    cqn = _rms(ua[:, OFF_CQ:OFF_CKV], qg_ref[...]).astype(BF16)
    q = _dot(cqn, wqa_ref[...]) * cosq_ref[...] + _dot(cqn, wqb_ref[...]) * sinq_ref[...]
    q_ref[...] = q.astype(BF16)
    kvn = _rms(ua[:, OFF_CKV:OFF_KPE], kvg_ref[...]).astype(BF16)
    kpe = ua[:, OFF_KPE:WA_COLS] * csk_ref[...]
    k = _dot(kvn, wk_ref[...]) + _dot(kpe.astype(BF16), epe_ref[...])
    k_ref[...] = k.astype(BF16)
    v_ref[...] = _dot(kvn, wv_ref[...]).astype(BF16)
    una = _dot(h, wna_ref[...])
    naq_ref[...] = (una[:, :D_MIX] * (NA_HEAD_DIM ** -0.5)).astype(BF16)
    nak_ref[...] = una[:, D_MIX:2 * D_MIX].astype(BF16)
    nav_ref[...] = una[:, 2 * D_MIX:].astype(BF16)


def _inproj(x, g1, wa, wna, qg, wqa, wqb, kvg, wk, wv, epe, cosq, sinq, csk):
    nt = SEQ // TM
    row = lambda n: pl.BlockSpec((TM, n), lambda i: (i, 0))
    pos = lambda n: pl.BlockSpec((TM, n), lambda i: (i % nt, 0))
    outs = [(HY_COLS, F32), (D_MIX, F32), (MLA_QK, BF16), (MLA_QK, BF16), (D_MIX, BF16),
            (D_MIX, BF16), (D_MIX, BF16), (D_MIX, BF16)]
    return pl.pallas_call(
        _inproj_kernel,
        grid=(TOKENS // TM,),
        in_specs=[row(D_MODEL), _const_spec(g1.shape), _const_spec(wa.shape),
                  _const_spec(wna.shape), _const_spec(qg.shape), _const_spec(wqa.shape),
                  _const_spec(wqb.shape), _const_spec(kvg.shape), _const_spec(wk.shape),
                  _const_spec(wv.shape), _const_spec(epe.shape),
                  pos(MLA_QK), pos(MLA_QK), pos(LANES)],
        out_specs=[row(n) for n, _ in outs],
        out_shape=[jax.ShapeDtypeStruct((TOKENS, n), dt) for n, dt in outs],
        compiler_params=_params("parallel"),
        name="inproj",
    )(x, g1, wa, wna, qg, wqa, wqb, kvg, wk, wv, epe, cosq, sinq, csk)


def _hyena_filter_kernel(z_ref, win_ref, w1_ref, b1_ref, freq_ref, w2_ref, b2_ref, w3_ref,
                         c_ref, s_ref, kf_ref, kny_ref):
    hp = lax.Precision.HIGHEST
    freq = freq_ref[...]
    hf = jnp.sin(freq[0:1] * (jnp.dot(z_ref[...], w1_ref[...], precision=hp,
                                      preferred_element_type=F32) + b1_ref[...]))
    hf = jnp.sin(freq[1:2] * (jnp.dot(hf, w2_ref[...], precision=hp,
                                      preferred_element_type=F32) + b2_ref[...]))
    hf = jnp.dot(hf, w3_ref[...], precision=hp, preferred_element_type=F32)
    win = win_ref[...]
    t = lax.broadcasted_iota(jnp.int32, (SEQ, D_MIX), 0)
    sgn = (1 - 2 * (t & 1)).astype(F32)
    wf = jnp.where(t == 0, 1.0 / (2 * SEQ), 2.0 / (2 * SEQ)).astype(F32)
    fwd = hf[:, :D_MIX] * win
    bwd = jnp.where(t == 0, 0.0, hf[:, D_MIX:] * win)
    nrm = lax.rsqrt(jnp.sum(fwd * fwd + bwd * bwd, axis=0, keepdims=True) + EPS)
    ksum = (fwd + bwd) * nrm
    kdif = (bwd - fwd) * nrm
    ks_hi, ks_lo = _split2(ksum)
    kd_hi, kd_lo = _split2(kdif)
    kf_ref[0] = (_dot(c_ref[...], ks_hi) + _dot(c_ref[...], ks_lo)) * wf
    kf_ref[1] = (_dot(s_ref[...], kd_hi) + _dot(s_ref[...], kd_lo)) * wf
    kny = jnp.sum(ksum * sgn, axis=0, keepdims=True) * (1.0 / (2 * SEQ))
    kny_ref[...] = jnp.broadcast_to(kny, (8, D_MIX))


def _hyena_filter(zfeat, window, w1, b1, freq, w2, b2, w3, cmat, smat):
    consts = (zfeat, window, w1, b1, freq, w2, b2)
    return pl.pallas_call(
        _hyena_filter_kernel,
        grid=(HY_ORDER,),
        in_specs=[_const_spec(a.shape) for a in consts]
        + [pl.BlockSpec((HY_FFN, 2 * D_MIX), lambda o: (0, o)),
           _const_spec(cmat.shape), _const_spec(smat.shape)],
        out_specs=[pl.BlockSpec((None, 2, SEQ, D_MIX), lambda o: (o, 0, 0, 0)),
                   pl.BlockSpec((None, 8, D_MIX), lambda o: (o, 0, 0))],
        out_shape=[jax.ShapeDtypeStruct((HY_ORDER, 2, SEQ, D_MIX), F32),
                   jax.ShapeDtypeStruct((HY_ORDER, 8, D_MIX), F32)],
        compiler_params=_params("parallel"),
        name="hyena_filter",
    )(*consts, w3, cmat, smat)


def _short_conv(u, w, b):
    t = lax.broadcasted_iota(jnp.int32, u.shape, 0)
    prev = jnp.where(t == 0, 0.0, pltpu.roll(u, 1, 0))
    nxt = jnp.where(t == SEQ - 1, 0.0, pltpu.roll(u, SEQ - 1, 0))
    return prev * w[0:1] + u * w[1:2] + nxt * w[2:3] + b


def _hyena_stage_kernel(gate_ref, src_ref, wg_ref, bg_ref, ws_ref, bs_ref, c_ref, s_ref,
                        kf_ref, kny_ref, skip_ref, out_ref, *, conv_src):
    gate = _short_conv(gate_ref[...], wg_ref[...], bg_ref[...])
    z = src_ref[...]
    if conv_src:
        z = _short_conv(z, ws_ref[...], bs_ref[...])
    cmat = c_ref[...]
    smat = s_ref[...]
    kre = kf_ref[0]
    kim = kf_ref[1]
    zb = z.astype(BF16)
    a = _dot(cmat, zb)
    b = _dot(smat, zb)
    yre = (a * kre + b * kim).astype(BF16)
    yim = (a * kim - b * kre).astype(BF16)
    t = lax.broadcasted_iota(jnp.int32, z.shape, 0)
    sgn = (1 - 2 * (t & 1)).astype(F32)
    nyq = jnp.sum(z * sgn, axis=0, keepdims=True) * kny_ref[0:1]
    y = _dot(cmat, yre) - _dot(smat, yim) + sgn * nyq
    out_ref[...] = (gate * (y + z * skip_ref[...])).astype(out_ref.dtype)


def _hyena_stage(u_hy, gate_blk, src, src_blk, conv_w, conv_b, cmat, smat, kf, kny, skip,
                 conv_src, out_dtype):
    col = lambda blk: pl.BlockSpec((SEQ, D_MIX), lambda b: (b, blk))
    wcol = lambda blk, r: pl.BlockSpec((r, D_MIX), lambda b: (0, blk))
    ws_blk = src_blk if conv_src else 0
    return pl.pallas_call(
        functools.partial(_hyena_stage_kernel, conv_src=conv_src),
        grid=(BATCH,),
        in_specs=[col(gate_blk), col(src_blk), wcol(gate_blk, 3), wcol(gate_blk, 1),
                  wcol(ws_blk, 3), wcol(ws_blk, 1), _const_spec(cmat.shape),
                  _const_spec(smat.shape), _const_spec(kf.shape), _const_spec(kny.shape),
                  _const_spec(skip.shape)],
        out_specs=pl.BlockSpec((SEQ, D_MIX), lambda b: (b, 0)),
        out_shape=jax.ShapeDtypeStruct((TOKENS, D_MIX), out_dtype),
        compiler_params=_params("parallel"),
        name="hyena_stage",
    )(u_hy, src, conv_w, conv_b, conv_w, conv_b, cmat, smat, kf, kny, skip)


def _fnet_kernel(x_ref, c_ref, s_ref, cg_ref, sg_ref, out_ref):
    xb = x_ref[...].astype(BF16)
    xc = _dot(xb, cg_ref[...]).astype(BF16)
    xs = _dot(xb, sg_ref[...]).astype(BF16)
    y = _dot(c_ref[...], xc) - _dot(s_ref[...], xs)
    out_ref[...] = (y * ((SEQ * D_MIX // FN_GROUPS) ** -0.5)).astype(out_ref.dtype)


def _fnet(u_fn, cmat, smat, cg, sg):
    return pl.pallas_call(
        _fnet_kernel,
        grid=(BATCH,),
        in_specs=[pl.BlockSpec((SEQ, D_MIX), lambda b: (b, 0)), _const_spec(cmat.shape),
                  _const_spec(smat.shape), _const_spec(cg.shape), _const_spec(sg.shape)],
        out_specs=pl.BlockSpec((SEQ, D_MIX), lambda b: (b, 0)),
        out_shape=jax.ShapeDtypeStruct((TOKENS, D_MIX), BF16),
        compiler_params=_params("parallel"),
        name="fnet",
    )(u_fn, cmat, smat, cg, sg)


def _softmax_pv(s, v):
    m = jnp.max(s, axis=-1, keepdims=True)
    p = jnp.exp(s - m)
    l = jnp.sum(p, axis=-1, keepdims=True)
    return _dot(p.astype(BF16), v) / l


def _mla_kernel(q_ref, k_ref, v_ref, out_ref):
    v = v_ref[...]
    head = lax.broadcasted_iota(jnp.int32, (TM, D_MIX), 1) // MLA_V
    acc = jnp.zeros((TM, D_MIX), F32)
    for h in range(MLA_HEADS):
        sl = slice(h * MLA_HEAD_PAD, (h + 1) * MLA_HEAD_PAD)
        s = _dot_nt(q_ref[:, sl], k_ref[:, sl])
        acc = jnp.where(head == h, _softmax_pv(s, v), acc)
    out_ref[...] = acc.astype(out_ref.dtype)


def _mla(q, k, v):
    nt = SEQ // TM
    return pl.pallas_call(
        _mla_kernel,
        grid=(BATCH, nt),
        in_specs=[pl.BlockSpec((TM, MLA_QK), lambda b, i: (b * nt + i, 0)),
                  pl.BlockSpec((SEQ, MLA_QK), lambda b, i: (b, 0)),
                  pl.BlockSpec((SEQ, D_MIX), lambda b, i: (b, 0))],
        out_specs=pl.BlockSpec((TM, D_MIX), lambda b, i: (b * nt + i, 0)),
        out_shape=jax.ShapeDtypeStruct((TOKENS, D_MIX), BF16),
        compiler_params=_params("parallel", "parallel"),
        name="mla_attention",
    )(q, k, v)


def _na_key_row0(j):
    return jnp.clip(j * NA_QROWS - NA_WIN_R // 2, 0, GRID_R - NA_KROWS)


NA_PAIRS = 2 * NA_WIN_R


def _na_kernel(q_ref, k_ref, v_ref, tile_ref, out_ref):
    j = pl.program_id(0)
    krow0 = _na_key_row0(j)
    off = pl.multiple_of(krow0 * GRID_W, GRID_W)
    q = q_ref[...]
    k = k_ref[pl.ds(off, NA_KBLK), :]
    v = v_ref[pl.ds(off, NA_KBLK), :]
    rq = j * NA_QROWS + lax.broadcasted_iota(jnp.int32, (NA_QBLK, NA_KBLK), 0) // GRID_W
    rk = krow0 + lax.broadcasted_iota(jnp.int32, (NA_QBLK, NA_KBLK), 1) // GRID_W
    rs = jnp.clip(rq - NA_WIN_R // 2, 0, GRID_R - NA_WIN_R)
    rowmask = jnp.where(rk < rs, NEG_BIG, jnp.where(rk >= rs + NA_WIN_R, NEG_BIG, 0.0))
    base = krow0 - j * NA_QROWS + NA_WIN_R
    head = lax.broadcasted_iota(jnp.int32, (NA_QBLK, D_MIX), 1) // NA_HEAD_DIM
    acc = jnp.zeros((NA_QBLK, D_MIX), F32)
    for h in range(NA_HEADS):
        bias = jnp.concatenate(
            [jnp.concatenate(
                [tile_ref[h, jnp.clip(base + 2 * kp - r, 0, NA_PAIRS - 1)]
                 for kp in range(NA_KROWS // 2)], axis=1)
             for r in range(NA_QROWS)], axis=0)
        qh = jnp.where(head == h, q, jnp.zeros_like(q))
        s = _dot_nt(qh, k) + bias + rowmask
        acc = jnp.where(head == h, _softmax_pv(s, v), acc)
    out_ref[...] = acc.astype(out_ref.dtype)


def _neighborhood(q, k, v, tiles):
    nj = SEQ // NA_QBLK
    return pl.pallas_call(
        _na_kernel,
        grid=(nj, BATCH),
        in_specs=[pl.BlockSpec((NA_QBLK, D_MIX), lambda j, b: (b * nj + j, 0)),
                  pl.BlockSpec((SEQ, D_MIX), lambda j, b: (b, 0)),
                  pl.BlockSpec((SEQ, D_MIX), lambda j, b: (b, 0)),
                  _const_spec(tiles.shape)],
        out_specs=pl.BlockSpec((NA_QBLK, D_MIX), lambda j, b: (b * nj + j, 0)),
        out_shape=jax.ShapeDtypeStruct((TOKENS, D_MIX), BF16),
        compiler_params=_params("parallel", "parallel"),
        name="neighborhood_attention",
    )(q, k, v, tiles)


def _na_bias_tiles(rpb):
    c = jnp.arange(GRID_W)
    cs = jnp.clip(c - NA_WIN_C // 2, 0, GRID_W - NA_WIN_C)
    col_ok = (c[None, :] >= cs[:, None]) & (c[None, :] < cs[:, None] + NA_WIN_C)
    dc = jnp.clip(c[None, :] - c[:, None] + (NA_WIN_C - 1), 0, 2 * NA_WIN_C - 2)
    t = jnp.where(col_ok, rpb.astype(F32)[:, :, dc], NEG_BIG)
    t = jnp.pad(t, ((0, 0), (1, 1), (0, 0), (0, 0)))
    return jnp.concatenate([t[:, :-1], t[:, 1:]], axis=-1)


def _merge_kernel(x_ref, g1_ref, yhy_ref, yfn_ref, ymla_ref, yna_ref, wg_ref, bg_ref,
                  wbr_ref, wout_ref, out_ref):
    x = x_ref[...]
    h = _rms(x, g1_ref[...]).astype(BF16)
    merged = jnp.zeros((TM, D_MODEL), F32)
    for n, y_ref in enumerate((yhy_ref, yfn_ref, ymla_ref, yna_ref)):
        sl = slice(n * D_MODEL, (n + 1) * D_MODEL)
        gate = jax.nn.sigmoid(_dot(h, wg_ref[:, sl]) + bg_ref[:, sl])
        merged = merged + gate * _dot(y_ref[...], wbr_ref[n])
    out_ref[...] = x + _dot(merged.astype(BF16), wout_ref[...])


def _merge(x, g1, yhy, yfn, ymla, yna, wg, bg, wbr, wout):
    row = lambda n: pl.BlockSpec((TM, n), lambda i: (i, 0))
    return pl.pallas_call(
        _merge_kernel,
        grid=(TOKENS // TM,),
        in_specs=[row(D_MODEL), _const_spec(g1.shape), row(D_MIX), row(D_MIX), row(D_MIX),
                  row(D_MIX), _const_spec(wg.shape), _const_spec(bg.shape),
                  _const_spec(wbr.shape), _const_spec(wout.shape)],
        out_specs=row(D_MODEL),
        out_shape=jax.ShapeDtypeStruct((TOKENS, D_MODEL), F32),
        compiler_params=_params("parallel"),
        name="merge",
    )(x, g1, yhy, yfn, ymla, yna, wg, bg, wbr, wout)


def _prefix_count(m):
    r = lax.broadcasted_iota(jnp.int32, (LANES, LANES), 0)
    c = lax.broadcasted_iota(jnp.int32, (LANES, LANES), 1)
    upper = jnp.where(r < c, 1.0, 0.0).astype(BF16)
    run = jnp.zeros((m.shape[0], 1), F32)
    parts = []
    for i in range(SEQ // LANES):
        chunk = m[:, i * LANES:(i + 1) * LANES]
        parts.append(_dot(chunk.astype(BF16), upper) + run)
        run = run + jnp.sum(chunk, axis=1, keepdims=True)
    return jnp.concatenate(parts, axis=1)


def _router_kernel(x_ref, g2_ref, wr_ref, hb_ref, logit_ref):
    h = _rms(x_ref[...], g2_ref[...])
    h_hi, h_lo = _split2(h)
    hb_ref[...] = h_hi
    w_hi, w_lo = _split2(wr_ref[...])
    logits = _dot_nt(w_hi, h_hi) + _dot_nt(w_hi, h_lo) + _dot_nt(w_lo, h_hi)
    logit_ref[...] = logits[:N_EXPERTS]


def _router(x, g2, wr_t):
    nt = SEQ // TM
    return pl.pallas_call(
        _router_kernel,
        grid=(TOKENS // TM,),
        in_specs=[pl.BlockSpec((TM, D_MODEL), lambda i: (i, 0)),
                  _const_spec(g2.shape), _const_spec(wr_t.shape)],
        out_specs=[pl.BlockSpec((TM, D_MODEL), lambda i: (i, 0)),
                   pl.BlockSpec((None, N_EXPERTS, TM), lambda i: (i // nt, 0, i % nt))],
        out_shape=[jax.ShapeDtypeStruct((TOKENS, D_MODEL), BF16),
                   jax.ShapeDtypeStruct((BATCH, N_EXPERTS, SEQ), F32)],
        compiler_params=_params("parallel"),
        name="router",
    )(x, g2, wr_t)


SELECT_MAX_ITERS = 192


def _select_kernel(logit_ref, slot_row_ref, slot_col_ref, w_col_ref, slot_s, w_s):
    b = pl.program_id(0)
    rows = BATCH * N_EXPERTS

    @pl.when(b == 0)
    def _():
        logits = logit_ref[...]
        ex = jnp.exp(logits - jnp.max(logits, axis=1, keepdims=True))
        aff = (ex / jnp.sum(ex, axis=1, keepdims=True)).reshape(rows, SEQ)

        def bisect(c):
            it, lo, hi, _ = c
            mid = 0.5 * (lo + hi)
            cnt = jnp.sum(jnp.where(aff >= mid, 1.0, 0.0), axis=1, keepdims=True)
            moving = jnp.where(mid == lo, 0.0, jnp.where(mid == hi, 0.0, 1.0))
            enough = cnt >= CAPACITY
            return (it + 1, jnp.where(enough, mid, lo), jnp.where(enough, hi, mid),
                    (jnp.max(moving) > 0).astype(jnp.int32))

        _, lo, hi, _ = lax.while_loop(
            lambda c: (c[0] < SELECT_MAX_ITERS) & (c[3] > 0), bisect,
            (jnp.int32(0), jnp.zeros((rows, 1), F32), jnp.full((rows, 1), 2.0, F32),
             jnp.int32(1)))
        above = jnp.where(aff >= hi, 1.0, 0.0)
        band = jnp.where(aff >= lo, 1.0, 0.0) - above
        need = CAPACITY - jnp.sum(above, axis=1, keepdims=True)
        sel = above + band * jnp.where(_prefix_count(band) < need, 1.0, 0.0)
        slot_s[...] = jnp.where(sel > 0, _prefix_count(sel), -1.0)
        w_s[...] = sel * aff

    r0 = pl.multiple_of(b * N_EXPERTS, N_EXPERTS)
    slot = slot_s[pl.ds(r0, N_EXPERTS), :]
    slot_row_ref[...] = slot.astype(jnp.int32)
    pad = jnp.zeros((LANES - N_EXPERTS, SEQ), F32)
    slot_col_ref[...] = jnp.concatenate([slot, pad - 1.0], axis=0).T.astype(jnp.int32)
    w_col_ref[...] = jnp.concatenate([w_s[pl.ds(r0, N_EXPERTS), :], pad], axis=0).T


def _select(logits):
    return pl.pallas_call(
        _select_kernel,
        grid=(BATCH,),
        in_specs=[_const_spec(logits.shape)],
        out_specs=[pl.BlockSpec((None, N_EXPERTS, SEQ), lambda b: (b, 0, 0)),
                   pl.BlockSpec((None, SEQ, LANES), lambda b: (b, 0, 0)),
                   pl.BlockSpec((None, SEQ, LANES), lambda b: (b, 0, 0))],
        out_shape=[jax.ShapeDtypeStruct((BATCH, N_EXPERTS, SEQ), jnp.int32),
                   jax.ShapeDtypeStruct((BATCH, SEQ, LANES), jnp.int32),
                   jax.ShapeDtypeStruct((BATCH, SEQ, LANES), F32)],
        scratch_shapes=[pltpu.VMEM((BATCH * N_EXPERTS, SEQ), F32),
                        pltpu.VMEM((BATCH * N_EXPERTS, SEQ), F32)],
        compiler_params=_params("arbitrary"),
        name="expert_select",
    )(logits)


def _gather_kernel(slot_ref, hb_ref, xe_ref):
    c = lax.broadcasted_iota(jnp.int32, (CAPACITY, SEQ), 0)
    onehot = jnp.where(slot_ref[...] == c, 1.0, 0.0).astype(BF16)
    xe_ref[...] = _dot(onehot, hb_ref[...]).astype(BF16)


def _gather(slot_row, hb):
    return pl.pallas_call(
        _gather_kernel,
        grid=(BATCH, N_EXPERTS),
        in_specs=[pl.BlockSpec((None, None, 1, SEQ), lambda b, e: (b, e, 0, 0)),
                  pl.BlockSpec((SEQ, D_MODEL), lambda b, e: (b, 0))],
        out_specs=pl.BlockSpec((None, None, CAPACITY, D_MODEL), lambda b, e: (e, b, 0, 0)),
        out_shape=jax.ShapeDtypeStruct((N_EXPERTS, BATCH, CAPACITY, D_MODEL), BF16),
        compiler_params=_params("parallel", "parallel"),
        name="expert_gather",
    )(slot_row.reshape(BATCH, N_EXPERTS, 1, SEQ), hb)


def _expert_kernel(xe_ref, wg_ref, wu_ref, wd_ref, ye_ref, wg_s, wu_s, wd_s):
    @pl.when(pl.program_id(1) == 0)
    def _():
        wg_s[...] = wg_ref[...].astype(BF16)
        wu_s[...] = wu_ref[...].astype(BF16)
        wd_s[...] = wd_ref[...].astype(BF16)

    xe = xe_ref[...]
    g = _dot(xe, wg_s[...])
    u = _dot(xe, wu_s[...])
    act = (g * jax.nn.sigmoid(g) * u).astype(BF16)
    ye_ref[...] = _dot(act, wd_s[...]).astype(BF16)


def _experts(xe, wg, wu, wd, layer):
    rows = BATCH * CAPACITY
    nt = rows // TM
    wspec = lambda a: pl.BlockSpec((None, None) + a.shape[2:], lambda e, i: (layer, e, 0, 0))
    return pl.pallas_call(
        _expert_kernel,
        grid=(N_EXPERTS, nt),
        in_specs=[pl.BlockSpec((None, TM, D_MODEL), lambda e, i: (e, i, 0)),
                  wspec(wg), wspec(wu), wspec(wd)],
        out_specs=pl.BlockSpec((None, TM, D_MODEL), lambda e, i: (e, i, 0)),
        out_shape=jax.ShapeDtypeStruct((N_EXPERTS, rows, D_MODEL), BF16),
        scratch_shapes=[pltpu.VMEM((D_MODEL, D_FF), BF16), pltpu.VMEM((D_MODEL, D_FF), BF16),
                        pltpu.VMEM((D_FF, D_MODEL), BF16)],
        compiler_params=_params("parallel", "arbitrary"),
        name="expert_ffn",
    )(xe.reshape(N_EXPERTS, rows, D_MODEL), wg, wu, wd)


def _combine_kernel(x_ref, ye_ref, slot_ref, w_ref, p_ref, g3_ref, wpg_ref, wpp_ref,
                    gf_ref, out_ref, *, final_norm):
    acc = x_ref[...]
    slot = slot_ref[...]
    w = w_ref[...]
    c = lax.broadcasted_iota(jnp.int32, (TM, CAPACITY), 1)
    for e in range(N_EXPERTS):
        onehot = jnp.where(slot[:, e:e + 1] == c, 1.0, 0.0).astype(BF16)
        acc = acc + w[:, e:e + 1] * _dot(onehot, ye_ref[e])
    h = _rms(acc, g3_ref[...]).astype(BF16)
    gate = jax.nn.sigmoid(_dot(h, wpg_ref[...]))
    y = acc + gate * _dot(p_ref[...].astype(BF16), wpp_ref[...])
    if final_norm:
        y = _rms(y, gf_ref[...])
    out_ref[...] = y


def _combine(x, ye, slot_col, w_col, p, layer, g3, wpg, wpp, gf, final_norm):
    nt = SEQ // TM
    p0 = layer * (TOKENS // TM)
    return pl.pallas_call(
        functools.partial(_combine_kernel, final_norm=final_norm),
        grid=(BATCH, nt),
        in_specs=[pl.BlockSpec((TM, D_MODEL), lambda b, i: (b * nt + i, 0)),
                  pl.BlockSpec((N_EXPERTS, None, CAPACITY, D_MODEL), lambda b, i: (0, b, 0, 0)),
                  pl.BlockSpec((None, TM, LANES), lambda b, i: (b, i, 0)),
                  pl.BlockSpec((None, TM, LANES), lambda b, i: (b, i, 0)),
                  pl.BlockSpec((TM, PLE_DIM), lambda b, i: (p0 + b * nt + i, 0)),
                  _const_spec(g3.shape), _const_spec(wpg.shape), _const_spec(wpp.shape),
                  _const_spec(gf.shape)],
        out_specs=pl.BlockSpec((TM, D_MODEL), lambda b, i: (b * nt + i, 0)),
        out_shape=jax.ShapeDtypeStruct((TOKENS, D_MODEL), F32),
        compiler_params=_params("parallel", "parallel"),
        name="combine",
    )(x, ye.reshape(N_EXPERTS, BATCH, CAPACITY, D_MODEL), slot_col, w_col, p, g3, wpg, wpp, gf)


DFT_FINE = 64
DFT_STEP = 4


def _dft_kernel(ca_ref, sa_ref, cb_ref, sb_ref, c_ref, s_ref):
    cb = cb_ref[...]
    sb = sb_ref[...]
    for r in range(DFT_STEP):
        ca = ca_ref[r]
        sa = sa_ref[r]
        rows = slice(r * DFT_FINE, (r + 1) * DFT_FINE)
        c_ref[rows, :] = (ca * cb - sa * sb).astype(BF16)
        s_ref[rows, :] = (sa * cb + ca * sb).astype(BF16)


def _dft_tables(n_points):
    t = jnp.arange(SEQ, dtype=jnp.int32)[None, :]
    coarse = jnp.arange(SEQ // DFT_FINE, dtype=jnp.int32)[:, None] * DFT_FINE
    fine = jnp.arange(DFT_FINE, dtype=jnp.int32)[:, None]
    ang = lambda f: ((f * t) % n_points).astype(F32) * (2.0 * math.pi / n_points)
    ca = jnp.cos(ang(coarse))[:, None, :]
    sa = jnp.sin(ang(coarse))[:, None, :]
    cb = jnp.cos(ang(fine))
    sb = jnp.sin(ang(fine))
    blk = DFT_STEP * DFT_FINE
    return pl.pallas_call(
        _dft_kernel,
        grid=(SEQ // blk,),
        in_specs=[pl.BlockSpec((DFT_STEP, 1, SEQ), lambda i: (i, 0, 0)),
                  pl.BlockSpec((DFT_STEP, 1, SEQ), lambda i: (i, 0, 0)),
                  _const_spec(cb.shape), _const_spec(sb.shape)],
        out_specs=[pl.BlockSpec((blk, SEQ), lambda i: (i, 0)),
                   pl.BlockSpec((blk, SEQ), lambda i: (i, 0))],
        out_shape=[jax.ShapeDtypeStruct((SEQ, SEQ), BF16), jax.ShapeDtypeStruct((SEQ, SEQ), BF16)],
        compiler_params=_params("parallel"),
        name="dft_tables",
    )(ca, sa, cb, sb)


def _fnet_group_tables():
    gc = D_MIX // FN_GROUPS
    i = lax.broadcasted_iota(jnp.int32, (D_MIX, D_MIX), 0)
    j = lax.broadcasted_iota(jnp.int32, (D_MIX, D_MIX), 1)
    same = (i // gc) == (j // gc)
    ang = (((i % gc) * (j % gc)) % gc).astype(F32) * (2.0 * math.pi / gc)
    return (jnp.where(same, jnp.cos(ang), 0.0).astype(BF16),
            jnp.where(same, jnp.sin(ang), 0.0).astype(BF16))


def _hyena_features():
    t01 = jnp.linspace(0.0, 1.0, SEQ, dtype=F32)[:, None]
    bands = jnp.linspace(1e-4, HY_BANDS - 1, HY_BANDS, dtype=F32)
    ang = 2.0 * math.pi * jnp.arange(SEQ, dtype=F32)[:, None] * bands / SEQ
    z = jnp.concatenate([t01, jnp.cos(ang), -jnp.sin(ang)], axis=-1)
    z = jnp.pad(z, ((0, 0), (0, LANES - HY_EMB)))
    max_decay = math.log(HY_TARGET) / HY_FAST_DECAY
    min_decay = math.log(HY_TARGET) / HY_SLOW_DECAY
    deltas = jnp.linspace(min_decay, max_decay, D_MIX, dtype=F32)
    window = jnp.exp(-t01 * jnp.abs(deltas))
    return z, window


def _rot_cols(w):
    half = w.shape[-1] // 2
    return jnp.concatenate([-w[..., half:], w[..., :half]], axis=-1)


def _rope_tables():
    inv = ROPE_THETA ** (-jnp.arange(0, MLA_ROPE, 2, dtype=F32) / MLA_ROPE)
    ang = jnp.arange(SEQ, dtype=F32)[:, None] * inv
    cos = jnp.concatenate([jnp.cos(ang), jnp.cos(ang)], axis=-1)
    sin = jnp.concatenate([jnp.sin(ang), jnp.sin(ang)], axis=-1)
    scale = (MLA_NOPE + MLA_ROPE) ** -0.5
    pad = MLA_HEAD_PAD - MLA_NOPE - MLA_ROPE
    one = jnp.ones((SEQ, MLA_NOPE), F32)
    zero = jnp.zeros((SEQ, MLA_NOPE), F32)
    zpad = jnp.zeros((SEQ, pad), F32)
    cosq = jnp.tile(jnp.concatenate([one, cos, zpad], axis=-1) * scale, (1, MLA_HEADS))
    sinq = jnp.tile(jnp.concatenate([zero, sin, zpad], axis=-1) * scale, (1, MLA_HEADS))
    csk = jnp.concatenate([cos, sin, jnp.zeros((SEQ, LANES - 2 * MLA_ROPE), F32)], axis=-1)
    return cosq, sinq, csk


def _mla_weights(w_uq, w_ukv):
    pad = MLA_HEAD_PAD - MLA_NOPE - MLA_ROPE
    wq = w_uq.reshape(MLA_Q_RANK, MLA_HEADS, MLA_NOPE + MLA_ROPE)
    nope, pe = wq[..., :MLA_NOPE], wq[..., MLA_NOPE:]
    zp = jnp.zeros((MLA_Q_RANK, MLA_HEADS, pad), F32)
    wqa = jnp.concatenate([nope, pe, zp], axis=-1).reshape(MLA_Q_RANK, MLA_QK)
    wqb = jnp.concatenate([jnp.zeros_like(nope), _rot_cols(pe), zp], axis=-1)
    wqb = wqb.reshape(MLA_Q_RANK, MLA_QK)
    wkv = w_ukv.reshape(MLA_KV_RANK, MLA_HEADS, MLA_NOPE + MLA_V)
    knope, v = wkv[..., :MLA_NOPE], wkv[..., MLA_NOPE:]
    wk = jnp.concatenate(
        [knope, jnp.zeros((MLA_KV_RANK, MLA_HEADS, MLA_HEAD_PAD - MLA_NOPE), F32)], axis=-1)
    wk = wk.reshape(MLA_KV_RANK, MLA_QK)
    wv = v.reshape(MLA_KV_RANK, MLA_HEADS * MLA_V)
    r = lax.broadcasted_iota(jnp.int32, (LANES, MLA_QK), 0)
    c = lax.broadcasted_iota(jnp.int32, (LANES, MLA_QK), 1)
    epe = jnp.where((r < 2 * MLA_ROPE) & (c % MLA_HEAD_PAD == MLA_NOPE + r % MLA_ROPE), 1.0, 0.0)
    return (wqa.astype(BF16), wqb.astype(BF16), wk.astype(BF16), wv.astype(BF16),
            epe.astype(BF16))


def kernel(x, p, norm1_g, w_in, b_gate, hy_conv_w, hy_conv_b, hf_w1, hf_b1, hf_freq, hf_w2,
           hf_b2, hf_w3, hy_skip, q_norm_g, w_uq, kv_norm_g, w_ukv, rpb, w_br, w_out, norm2_g,
           w_router, w_e_gate, w_e_up, w_e_down, norm3_g, w_ple_gate, w_ple_proj, final_g):
    conv_c, conv_s = _dft_tables(2 * SEQ)
    fnet_c, fnet_s = _dft_tables(SEQ)
    fnet_cg, fnet_sg = _fnet_group_tables()
    zfeat, window = _hyena_features()
    cosq, sinq, csk = _rope_tables()
    row = lambda a: a.reshape(1, -1)

    xt = x.reshape(TOKENS, D_MODEL)
    for i in range(DEPTH):
        g1 = row(norm1_g[i])
        w = w_in[i]
        wa = jnp.concatenate(
            [w[:, :OFF_NA], _rot_cols(w[:, OFF_KPE:OFF_NA]),
             jnp.zeros((D_MODEL, WA_COLS - OFF_NA - MLA_ROPE), F32)], axis=-1).astype(BF16)
        wna = w[:, OFF_NA:OFF_GATE].astype(BF16)
        wgate = w[:, OFF_GATE:].astype(BF16)
        wqa, wqb, wk, wv, epe = _mla_weights(w_uq[i], w_ukv[i])

        u_hy, u_fn, q, k, v, naq, nak, nav = _inproj(
            xt, g1, wa, wna, row(q_norm_g[i]), wqa, wqb, row(kv_norm_g[i]), wk, wv, epe,
            cosq, sinq, csk)

        w1 = jnp.pad(hf_w1[i], ((0, LANES - HY_EMB), (0, 0)))
        kf, kny = _hyena_filter(zfeat, window, w1, row(hf_b1[i]), hf_freq[i], hf_w2[i],
                                row(hf_b2[i]), hf_w3[i], conv_c, conv_s)
        conv_b = row(hy_conv_b[i])
        z1 = _hyena_stage(u_hy, 0, u_hy, 2, hy_conv_w[i], conv_b, conv_c, conv_s, kf[0],
                          kny[0], row(hy_skip[i, 0]), True, F32)
        y_hy = _hyena_stage(u_hy, 1, z1, 0, hy_conv_w[i], conv_b, conv_c, conv_s, kf[1],
                            kny[1], row(hy_skip[i, 1]), False, BF16)
        y_fn = _fnet(u_fn, fnet_c, fnet_s, fnet_cg, fnet_sg)
        y_mla = _mla(q, k, v)
        y_na = _neighborhood(naq, nak, nav, _na_bias_tiles(rpb[i]))
        xt = _merge(xt, g1, y_hy, y_fn, y_mla, y_na, wgate, row(b_gate[i]),
                    w_br[i].astype(BF16), w_out[i].astype(BF16))

        wr_t = jnp.pad(w_router[i].T, ((0, LANES - N_EXPERTS), (0, 0)))
        hb, logits = _router(xt, row(norm2_g[i]), wr_t)
        slot_row, slot_col, w_col = _select(logits)
        xe = _gather(slot_row, hb)
        ye = _experts(xe, w_e_gate, w_e_up, w_e_down, i)
        xt = _combine(xt, ye, slot_col, w_col, p.reshape(DEPTH * TOKENS, PLE_DIM), i,
                      row(norm3_g[i]), w_ple_gate[i].astype(BF16),
                      w_ple_proj[i].astype(BF16), row(final_g), i == DEPTH - 1)
    return xt.reshape(BATCH, SEQ, D_MODEL)
```

```python
import functools
import math

import jax
import jax.numpy as jnp
from jax import lax
from jax.experimental import pallas as pl
from jax.experimental.pallas import tpu as pltpu

F32 = jnp.float32
BF16 = jnp.bfloat16

D_MODEL = 1024
BATCH = 8
SEQ = 2048
DEPTH = 2
TOKENS = BATCH * SEQ

GRID_W = 64
GRID_R = SEQ // GRID_W
D_MIX = 256
N_BRANCH = 4
EPS = 1e-6
HY_ORDER = 2
HY_BANDS = 16
HY_EMB = 2 * HY_BANDS + 1
HY_FFN = 64
HY_TARGET = 1e-2
HY_FAST_DECAY = 0.3
HY_SLOW_DECAY = 1.5
FN_GROUPS = 4
MLA_HEADS = 4
MLA_NOPE = 64
MLA_ROPE = 32
MLA_V = 64
MLA_Q_RANK = 256
MLA_KV_RANK = 128
ROPE_THETA = 10000.0
NA_HEADS = 4
NA_HEAD_DIM = D_MIX // NA_HEADS
NA_WIN_R = 8
NA_WIN_C = 16
N_EXPERTS = 16
CAPACITY = 2 * SEQ // N_EXPERTS
D_FF = 1024
PLE_DIM = 256

HY_COLS = 3 * D_MIX
OFF_FN = HY_COLS
OFF_CQ = OFF_FN + D_MIX
OFF_CKV = OFF_CQ + MLA_Q_RANK
OFF_KPE = OFF_CKV + MLA_KV_RANK
OFF_NA = OFF_KPE + MLA_ROPE
OFF_GATE = OFF_NA + 3 * D_MIX

LANES = 128
MLA_HEAD_PAD = 128
MLA_QK = MLA_HEADS * MLA_HEAD_PAD
WA_COLS = 1536
NEG_BIG = -1e30

TILE = dict(inproj=512, mla=512, merge=512, router=512, expert=512, combine=512)
NA_QROWS = 8
NA_KROWS = 16
NA_QBLK = NA_QROWS * GRID_W
NA_KBLK = NA_KROWS * GRID_W
VMEM_LIMIT = 56 * 1024 * 1024


def _params(*sem):
    return pltpu.CompilerParams(dimension_semantics=sem, vmem_limit_bytes=VMEM_LIMIT)


def _const_spec(shape):
    nd = len(shape)
    return pl.BlockSpec(shape, lambda *_: (0,) * nd, pipeline_mode=pl.Buffered(1))


def _rms(x, g):
    return x * lax.rsqrt(jnp.mean(x * x, axis=-1, keepdims=True) + EPS) * g


def _dot(a, b):
    return jnp.dot(a, b, preferred_element_type=F32)


def _dot_nt(a, b):
    return lax.dot_general(a, b, (((1,), (1,)), ((), ())), preferred_element_type=F32)


def _split2(x):
    hi = x.astype(BF16)
    lo = (x - hi.astype(F32)).astype(BF16)
    return hi, lo


def _inproj_kernel(x_ref, g1_ref, wa_ref, wna_ref, qg_ref, wqa_ref, wqb_ref, kvg_ref,
                   wk_ref, wv_ref, epe_ref, cosq_ref, sinq_ref, csk_ref,
                   uhy_ref, ufn_ref, q_ref, k_ref, v_ref, naq_ref, nak_ref, nav_ref):
    h = _rms(x_ref[...], g1_ref[...]).astype(BF16)
    ua = _dot(h, wa_ref[...])
    uhy_ref[...] = ua[:, :HY_COLS]
    ufn_ref[...] = ua[:, OFF_FN:OFF_CQ]
    cqn = _rms(ua[:, OFF_CQ:OFF_CKV], qg_ref[...]).astype(BF16)
    q = _dot(cqn, wqa_ref[...]) * cosq_ref[...] + _dot(cqn, wqb_ref[...]) * sinq_ref[...]
    q_ref[...] = q.astype(BF16)
    kvn = _rms(ua[:, OFF_CKV:OFF_KPE], kvg_ref[...]).astype(BF16)
    kpe = ua[:, OFF_KPE:WA_COLS] * csk_ref[...]
    k = _dot(kvn, wk_ref[...]) + _dot(kpe.astype(BF16), epe_ref[...])
    k_ref[...] = k.astype(BF16)
    v_ref[...] = _dot(kvn, wv_ref[...]).astype(BF16)
    una = _dot(h, wna_ref[...])
    naq_ref[...] = (una[:, :D_MIX] * (NA_HEAD_DIM ** -0.5)).astype(BF16)
    nak_ref[...] = una[:, D_MIX:2 * D_MIX].astype(BF16)
    nav_ref[...] = una[:, 2 * D_MIX:].astype(BF16)


def _inproj(x, g1, wa, wna, qg, wqa, wqb, kvg, wk, wv, epe, cosq, sinq, csk):
    tm = TILE["inproj"]
    nt = SEQ // tm
    row = lambda n: pl.BlockSpec((tm, n), lambda i: (i, 0))
    pos = lambda n: pl.BlockSpec((tm, n), lambda i: (i % nt, 0))
    outs = [(HY_COLS, F32), (D_MIX, F32), (MLA_QK, BF16), (MLA_QK, BF16), (D_MIX, BF16),
            (D_MIX, BF16), (D_MIX, BF16), (D_MIX, BF16)]
    return pl.pallas_call(
        _inproj_kernel,
        grid=(TOKENS // tm,),
        in_specs=[row(D_MODEL), _const_spec(g1.shape), _const_spec(wa.shape),
                  _const_spec(wna.shape), _const_spec(qg.shape), _const_spec(wqa.shape),
                  _const_spec(wqb.shape), _const_spec(kvg.shape), _const_spec(wk.shape),
                  _const_spec(wv.shape), _const_spec(epe.shape),
                  pos(MLA_QK), pos(MLA_QK), pos(LANES)],
        out_specs=[row(n) for n, _ in outs],
        out_shape=[jax.ShapeDtypeStruct((TOKENS, n), dt) for n, dt in outs],
        compiler_params=_params("parallel"),
        name="inproj",
    )(x, g1, wa, wna, qg, wqa, wqb, kvg, wk, wv, epe, cosq, sinq, csk)


WPREP_ROWS = 128


def _inproj_weights_kernel(w_ref, wa_ref, wna_ref, wg_ref):
    half = MLA_ROPE // 2
    wa_ref[:, :OFF_KPE] = w_ref[:, :OFF_KPE].astype(BF16)
    kpe_group = jnp.concatenate(
        [w_ref[:, OFF_KPE:OFF_NA], -w_ref[:, OFF_KPE + half:OFF_NA],
         w_ref[:, OFF_KPE:OFF_KPE + half],
         jnp.zeros((WPREP_ROWS, WA_COLS - OFF_NA - MLA_ROPE), F32)], axis=1)
    wa_ref[:, OFF_KPE:] = kpe_group.astype(BF16)
    wna_ref[...] = w_ref[:, OFF_NA:OFF_GATE].astype(BF16)
    wg_ref[...] = w_ref[:, OFF_GATE:].astype(BF16)


def _inproj_weights(w_in, layer):
    cols = (WA_COLS, OFF_GATE - OFF_NA, w_in.shape[-1] - OFF_GATE)
    return pl.pallas_call(
        _inproj_weights_kernel,
        grid=(D_MODEL // WPREP_ROWS,),
        in_specs=[pl.BlockSpec((None, WPREP_ROWS, w_in.shape[-1]), lambda r: (layer, r, 0))],
        out_specs=[pl.BlockSpec((WPREP_ROWS, n), lambda r: (r, 0)) for n in cols],
        out_shape=[jax.ShapeDtypeStruct((D_MODEL, n), BF16) for n in cols],
        compiler_params=_params("parallel"),
        name="inproj_weights",
    )(w_in)


def _hyena_filter_kernel(z_ref, win_ref, w1_ref, b1_ref, freq_ref, w2_ref, b2_ref, w3_ref,
                         c_ref, s_ref, kf_ref, kny_ref):
    hp = lax.Precision.HIGHEST
    freq = freq_ref[...]
    hf = jnp.sin(freq[0:1] * (jnp.dot(z_ref[...], w1_ref[...], precision=hp,
                                      preferred_element_type=F32) + b1_ref[...]))
    hf = jnp.sin(freq[1:2] * (jnp.dot(hf, w2_ref[...], precision=hp,
                                      preferred_element_type=F32) + b2_ref[...]))
    hf = jnp.dot(hf, w3_ref[...], precision=hp, preferred_element_type=F32)
    win = win_ref[...]
    t = lax.broadcasted_iota(jnp.int32, (SEQ, D_MIX), 0)
    sgn = (1 - 2 * (t & 1)).astype(F32)
    wf = jnp.where(t == 0, 1.0 / (2 * SEQ), 2.0 / (2 * SEQ)).astype(F32)
    fwd = hf[:, :D_MIX] * win
    bwd = jnp.where(t == 0, 0.0, hf[:, D_MIX:] * win)
    nrm = lax.rsqrt(jnp.sum(fwd * fwd + bwd * bwd, axis=0, keepdims=True) + EPS)
    ksum = (fwd + bwd) * nrm
    kdif = (bwd - fwd) * nrm
    ks_hi, ks_lo = _split2(ksum)
    kd_hi, kd_lo = _split2(kdif)
    kf_ref[0] = (_dot(c_ref[...], ks_hi) + _dot(c_ref[...], ks_lo)) * wf
    kf_ref[1] = (_dot(s_ref[...], kd_hi) + _dot(s_ref[...], kd_lo)) * wf
    kny = jnp.sum(ksum * sgn, axis=0, keepdims=True) * (1.0 / (2 * SEQ))
    kny_ref[...] = jnp.broadcast_to(kny, (8, D_MIX))


def _hyena_filter(zfeat, window, w1, b1, freq, w2, b2, w3, cmat, smat):
    consts = (zfeat, window, w1, b1, freq, w2, b2)
    return pl.pallas_call(
        _hyena_filter_kernel,
        grid=(HY_ORDER,),
        in_specs=[_const_spec(a.shape) for a in consts]
        + [pl.BlockSpec((HY_FFN, 2 * D_MIX), lambda o: (0, o)),
           _const_spec(cmat.shape), _const_spec(smat.shape)],
        out_specs=[pl.BlockSpec((None, 2, SEQ, D_MIX), lambda o: (o, 0, 0, 0)),
                   pl.BlockSpec((None, 8, D_MIX), lambda o: (o, 0, 0))],
        out_shape=[jax.ShapeDtypeStruct((HY_ORDER, 2, SEQ, D_MIX), F32),
                   jax.ShapeDtypeStruct((HY_ORDER, 8, D_MIX), F32)],
        compiler_params=_params("parallel"),
        name="hyena_filter",
    )(*consts, w3, cmat, smat)


def _short_conv(u, w, b):
    t = lax.broadcasted_iota(jnp.int32, u.shape, 0)
    prev = jnp.where(t == 0, 0.0, pltpu.roll(u, 1, 0))
    nxt = jnp.where(t == SEQ - 1, 0.0, pltpu.roll(u, SEQ - 1, 0))
    return prev * w[0:1] + u * w[1:2] + nxt * w[2:3] + b


def _hyena_stage_kernel(gate_ref, src_ref, wg_ref, bg_ref, ws_ref, bs_ref, c_ref, s_ref,
                        kf_ref, kny_ref, skip_ref, out_ref, *, conv_src):
    gate = _short_conv(gate_ref[...], wg_ref[...], bg_ref[...])
    z = src_ref[...]
    if conv_src:
        z = _short_conv(z, ws_ref[...], bs_ref[...])
    cmat = c_ref[...]
    smat = s_ref[...]
    kre = kf_ref[0]
    kim = kf_ref[1]
    zb = z.astype(BF16)
    a = _dot(cmat, zb)
    b = _dot(smat, zb)
    yre = (a * kre + b * kim).astype(BF16)
    yim = (a * kim - b * kre).astype(BF16)
    t = lax.broadcasted_iota(jnp.int32, z.shape, 0)
    sgn = (1 - 2 * (t & 1)).astype(F32)
    nyq = jnp.sum(z * sgn, axis=0, keepdims=True) * kny_ref[0:1]
    y = _dot(cmat, yre) - _dot(smat, yim) + sgn * nyq
    out_ref[...] = (gate * (y + z * skip_ref[...])).astype(out_ref.dtype)


def _hyena_stage(u_hy, gate_blk, src, src_blk, conv_w, conv_b, cmat, smat, kf, kny, skip,
                 conv_src, out_dtype):
    col = lambda blk: pl.BlockSpec((SEQ, D_MIX), lambda b: (b, blk))
    wcol = lambda blk, r: pl.BlockSpec((r, D_MIX), lambda b: (0, blk))
    ws_blk = src_blk if conv_src else 0
    return pl.pallas_call(
        functools.partial(_hyena_stage_kernel, conv_src=conv_src),
        grid=(BATCH,),
        in_specs=[col(gate_blk), col(src_blk), wcol(gate_blk, 3), wcol(gate_blk, 1),
                  wcol(ws_blk, 3), wcol(ws_blk, 1), _const_spec(cmat.shape),
                  _const_spec(smat.shape), _const_spec(kf.shape), _const_spec(kny.shape),
                  _const_spec(skip.shape)],
        out_specs=pl.BlockSpec((SEQ, D_MIX), lambda b: (b, 0)),
        out_shape=jax.ShapeDtypeStruct((TOKENS, D_MIX), out_dtype),
        compiler_params=_params("parallel"),
        name="hyena_stage",
    )(u_hy, src, conv_w, conv_b, conv_w, conv_b, cmat, smat, kf, kny, skip)


def _fnet_kernel(x_ref, c_ref, s_ref, cg_ref, sg_ref, out_ref):
    xb = x_ref[...].astype(BF16)
    xc = _dot(xb, cg_ref[...]).astype(BF16)
    xs = _dot(xb, sg_ref[...]).astype(BF16)
    y = _dot(c_ref[...], xc) - _dot(s_ref[...], xs)
    out_ref[...] = (y * ((SEQ * D_MIX // FN_GROUPS) ** -0.5)).astype(out_ref.dtype)


def _fnet(u_fn, cmat, smat, cg, sg):
    return pl.pallas_call(
        _fnet_kernel,
        grid=(BATCH,),
        in_specs=[pl.BlockSpec((SEQ, D_MIX), lambda b: (b, 0)), _const_spec(cmat.shape),
                  _const_spec(smat.shape), _const_spec(cg.shape), _const_spec(sg.shape)],
        out_specs=pl.BlockSpec((SEQ, D_MIX), lambda b: (b, 0)),
        out_shape=jax.ShapeDtypeStruct((TOKENS, D_MIX), BF16),
        compiler_params=_params("parallel"),
        name="fnet",
    )(u_fn, cmat, smat, cg, sg)


def _softmax_pv(s, v):
    m = jnp.max(s, axis=-1, keepdims=True)
    p = jnp.exp(s - m)
    l = jnp.sum(p, axis=-1, keepdims=True)
    return _dot(p.astype(BF16), v) / l


def _mla_kernel(q_ref, k_ref, v_ref, out_ref):
    v = v_ref[...]
    head = lax.broadcasted_iota(jnp.int32, out_ref.shape, 1) // MLA_V
    acc = jnp.zeros(out_ref.shape, F32)
    for h in range(MLA_HEADS):
        sl = slice(h * MLA_HEAD_PAD, (h + 1) * MLA_HEAD_PAD)
        s = _dot_nt(q_ref[:, sl], k_ref[:, sl])
        acc = jnp.where(head == h, _softmax_pv(s, v), acc)
    out_ref[...] = acc.astype(out_ref.dtype)


def _mla(q, k, v):
    tm = TILE["mla"]
    nt = SEQ // tm
    return pl.pallas_call(
        _mla_kernel,
        grid=(BATCH, nt),
        in_specs=[pl.BlockSpec((tm, MLA_QK), lambda b, i: (b * nt + i, 0)),
                  pl.BlockSpec((SEQ, MLA_QK), lambda b, i: (b, 0)),
                  pl.BlockSpec((SEQ, D_MIX), lambda b, i: (b, 0))],
        out_specs=pl.BlockSpec((tm, D_MIX), lambda b, i: (b * nt + i, 0)),
        out_shape=jax.ShapeDtypeStruct((TOKENS, D_MIX), BF16),
        compiler_params=_params("parallel", "parallel"),
        name="mla_attention",
    )(q, k, v)


def _na_key_row0(j):
    return jnp.clip(j * NA_QROWS - NA_WIN_R // 2, 0, GRID_R - NA_KROWS)


NA_PAIRS = 2 * NA_WIN_R


def _na_kernel(q_ref, k_ref, v_ref, tile_ref, out_ref):
    j = pl.program_id(0)
    krow0 = _na_key_row0(j)
    off = pl.multiple_of(krow0 * GRID_W, GRID_W)
    q = q_ref[...]
    k = k_ref[pl.ds(off, NA_KBLK), :]
    v = v_ref[pl.ds(off, NA_KBLK), :]
    rq = j * NA_QROWS + lax.broadcasted_iota(jnp.int32, (NA_QBLK, NA_KBLK), 0) // GRID_W
    rk = krow0 + lax.broadcasted_iota(jnp.int32, (NA_QBLK, NA_KBLK), 1) // GRID_W
    rs = jnp.clip(rq - NA_WIN_R // 2, 0, GRID_R - NA_WIN_R)
    rowmask = jnp.where(rk < rs, NEG_BIG, jnp.where(rk >= rs + NA_WIN_R, NEG_BIG, 0.0))
    base = krow0 - j * NA_QROWS + NA_WIN_R
    head = lax.broadcasted_iota(jnp.int32, (NA_QBLK, D_MIX), 1) // NA_HEAD_DIM
    acc = jnp.zeros((NA_QBLK, D_MIX), F32)
    for h in range(NA_HEADS):
        bias = jnp.concatenate(
            [jnp.concatenate(
                [tile_ref[h, jnp.clip(base + 2 * kp - r, 0, NA_PAIRS - 1)]
                 for kp in range(NA_KROWS // 2)], axis=1)
             for r in range(NA_QROWS)], axis=0)
        qh = jnp.where(head == h, q, jnp.zeros_like(q))
        s = _dot_nt(qh, k) + bias + rowmask
        acc = jnp.where(head == h, _softmax_pv(s, v), acc)
    out_ref[...] = acc.astype(out_ref.dtype)


def _neighborhood(q, k, v, tiles):
    nj = SEQ // NA_QBLK
    return pl.pallas_call(
        _na_kernel,
        grid=(nj, BATCH),
        in_specs=[pl.BlockSpec((NA_QBLK, D_MIX), lambda j, b: (b * nj + j, 0)),
                  pl.BlockSpec((SEQ, D_MIX), lambda j, b: (b, 0)),
                  pl.BlockSpec((SEQ, D_MIX), lambda j, b: (b, 0)),
                  _const_spec(tiles.shape)],
        out_specs=pl.BlockSpec((NA_QBLK, D_MIX), lambda j, b: (b * nj + j, 0)),
        out_shape=jax.ShapeDtypeStruct((TOKENS, D_MIX), BF16),
        compiler_params=_params("parallel", "parallel"),
        name="neighborhood_attention",
    )(q, k, v, tiles)


def _na_bias_tiles(rpb):
    c = jnp.arange(GRID_W)
    cs = jnp.clip(c - NA_WIN_C // 2, 0, GRID_W - NA_WIN_C)
    col_ok = (c[None, :] >= cs[:, None]) & (c[None, :] < cs[:, None] + NA_WIN_C)
    dc = jnp.clip(c[None, :] - c[:, None] + (NA_WIN_C - 1), 0, 2 * NA_WIN_C - 2)
    pick = (dc[None] == jnp.arange(2 * NA_WIN_C - 1)[:, None, None]).astype(F32)
    t = jnp.einsum('hrd,dqk->hrqk', rpb.astype(F32), pick, precision=lax.Precision.HIGHEST)
    t = jnp.where(col_ok, t, NEG_BIG)
    t = jnp.pad(t, ((0, 0), (1, 1), (0, 0), (0, 0)))
    return jnp.concatenate([t[:, :-1], t[:, 1:]], axis=-1)


def _merge_kernel(x_ref, g1_ref, yhy_ref, yfn_ref, ymla_ref, yna_ref, wg_ref, bg_ref,
                  wbr_ref, wout_ref, out_ref):
    x = x_ref[...]
    h = _rms(x, g1_ref[...]).astype(BF16)
    merged = jnp.zeros(x.shape, F32)
    for n, y_ref in enumerate((yhy_ref, yfn_ref, ymla_ref, yna_ref)):
        sl = slice(n * D_MODEL, (n + 1) * D_MODEL)
        gate = jax.nn.sigmoid(_dot(h, wg_ref[:, sl]) + bg_ref[:, sl])
        merged = merged + gate * _dot(y_ref[...], wbr_ref[n])
    out_ref[...] = x + _dot(merged.astype(BF16), wout_ref[...])


def _merge(x, g1, yhy, yfn, ymla, yna, wg, bg, wbr, wout):
    tm = TILE["merge"]
    row = lambda n: pl.BlockSpec((tm, n), lambda i: (i, 0))
    return pl.pallas_call(
        _merge_kernel,
        grid=(TOKENS // tm,),
        in_specs=[row(D_MODEL), _const_spec(g1.shape), row(D_MIX), row(D_MIX), row(D_MIX),
                  row(D_MIX), _const_spec(wg.shape), _const_spec(bg.shape),
                  _const_spec(wbr.shape), _const_spec(wout.shape)],
        out_specs=row(D_MODEL),
        out_shape=jax.ShapeDtypeStruct((TOKENS, D_MODEL), F32),
        compiler_params=_params("parallel"),
        name="merge",
    )(x, g1, yhy, yfn, ymla, yna, wg, bg, wbr, wout)


def _prefix_count(m):
    r = lax.broadcasted_iota(jnp.int32, (LANES, LANES), 0)
    c = lax.broadcasted_iota(jnp.int32, (LANES, LANES), 1)
    upper = jnp.where(r < c, 1.0, 0.0).astype(BF16)
    run = jnp.zeros((m.shape[0], 1), F32)
    parts = []
    for i in range(SEQ // LANES):
        chunk = m[:, i * LANES:(i + 1) * LANES]
        parts.append(_dot(chunk.astype(BF16), upper) + run)
        run = run + jnp.sum(chunk, axis=1, keepdims=True)
    return jnp.concatenate(parts, axis=1)


def _router_kernel(x_ref, g2_ref, wr_ref, hb_ref, logit_ref):
    h = _rms(x_ref[...], g2_ref[...])
    h_hi, h_lo = _split2(h)
    hb_ref[...] = h_hi
    w_hi, w_lo = _split2(wr_ref[...])
    logits = _dot_nt(w_hi, h_hi) + _dot_nt(w_hi, h_lo) + _dot_nt(w_lo, h_hi)
    logit_ref[...] = logits[:N_EXPERTS]


def _router(x, g2, wr_t):
    tm = TILE["router"]
    nt = SEQ // tm
    return pl.pallas_call(
        _router_kernel,
        grid=(TOKENS // tm,),
        in_specs=[pl.BlockSpec((tm, D_MODEL), lambda i: (i, 0)),
                  _const_spec(g2.shape), _const_spec(wr_t.shape)],
        out_specs=[pl.BlockSpec((tm, D_MODEL), lambda i: (i, 0)),
                   pl.BlockSpec((None, N_EXPERTS, tm), lambda i: (i // nt, 0, i % nt))],
        out_shape=[jax.ShapeDtypeStruct((TOKENS, D_MODEL), BF16),
                   jax.ShapeDtypeStruct((BATCH, N_EXPERTS, SEQ), F32)],
        compiler_params=_params("parallel"),
        name="router",
    )(x, g2, wr_t)


SELECT_MAX_ITERS = 192


def _select_kernel(logit_ref, slot_row_ref, slot_col_ref, w_row_ref, slot_s, w_s):
    b = pl.program_id(0)
    rows = BATCH * N_EXPERTS

    @pl.when(b == 0)
    def _():
        logits = logit_ref[...]
        ex = jnp.exp(logits - jnp.max(logits, axis=1, keepdims=True))
        aff = (ex / jnp.sum(ex, axis=1, keepdims=True)).reshape(rows, SEQ)

        def bisect(c):
            it, lo, hi, _ = c
            mid = 0.5 * (lo + hi)
            cnt = jnp.sum(jnp.where(aff >= mid, 1.0, 0.0), axis=1, keepdims=True)
            moving = jnp.where(mid == lo, 0.0, jnp.where(mid == hi, 0.0, 1.0))
            enough = cnt >= CAPACITY
            return (it + 1, jnp.where(enough, mid, lo), jnp.where(enough, hi, mid),
                    (jnp.max(moving) > 0).astype(jnp.int32))

        _, lo, hi, _ = lax.while_loop(
            lambda c: (c[0] < SELECT_MAX_ITERS) & (c[3] > 0), bisect,
            (jnp.int32(0), jnp.zeros((rows, 1), F32), jnp.full((rows, 1), 2.0, F32),
             jnp.int32(1)))
        above = jnp.where(aff >= hi, 1.0, 0.0)
        band = jnp.where(aff >= lo, 1.0, 0.0) - above
        need = CAPACITY - jnp.sum(above, axis=1, keepdims=True)
        sel = above + band * jnp.where(_prefix_count(band) < need, 1.0, 0.0)
        slot_s[...] = jnp.where(sel > 0, _prefix_count(sel), -1.0)
        w_s[...] = sel * aff

    r0 = pl.multiple_of(b * N_EXPERTS, N_EXPERTS)
    slot = slot_s[pl.ds(r0, N_EXPERTS), :]
    slot_row_ref[...] = slot.astype(jnp.int32)
    pad = jnp.full((LANES - N_EXPERTS, SEQ), -1.0, F32)
    slot_col_ref[...] = jnp.concatenate([slot, pad], axis=0).T.astype(jnp.int32)
    w_row_ref[...] = w_s[pl.ds(r0, N_EXPERTS), :]


def _select(logits):
    return pl.pallas_call(
        _select_kernel,
        grid=(BATCH,),
        in_specs=[_const_spec(logits.shape)],
        out_specs=[pl.BlockSpec((None, N_EXPERTS, SEQ), lambda b: (b, 0, 0)),
                   pl.BlockSpec((None, SEQ, LANES), lambda b: (b, 0, 0)),
                   pl.BlockSpec((None, N_EXPERTS, SEQ), lambda b: (b, 0, 0))],
        out_shape=[jax.ShapeDtypeStruct((BATCH, N_EXPERTS, SEQ), jnp.int32),
                   jax.ShapeDtypeStruct((BATCH, SEQ, LANES), jnp.int32),
                   jax.ShapeDtypeStruct((BATCH, N_EXPERTS, SEQ), F32)],
        scratch_shapes=[pltpu.VMEM((BATCH * N_EXPERTS, SEQ), F32),
                        pltpu.VMEM((BATCH * N_EXPERTS, SEQ), F32)],
        compiler_params=_params("arbitrary"),
        name="expert_select",
    )(logits)


GATHER_EXPERTS = 8


def _gather_kernel(slot_ref, w_ref, hb_ref, xe_ref, wsl_ref):
    ge = GATHER_EXPERTS
    c = lax.broadcasted_iota(jnp.int32, (CAPACITY, SEQ), 0)
    hits = [slot_ref[e:e + 1, :] == c for e in range(ge)]
    onehot = jnp.concatenate([jnp.where(m, 1.0, 0.0).astype(BF16) for m in hits], axis=0)
    rows = _dot(onehot, hb_ref[...]).astype(BF16)
    for e in range(ge):
        xe_ref[e] = rows[e * CAPACITY:(e + 1) * CAPACITY]
        wslot = jnp.sum(jnp.where(hits[e], w_ref[e:e + 1, :], 0.0), axis=1, keepdims=True)
        wsl_ref[e] = jnp.broadcast_to(wslot, (CAPACITY, LANES))


def _gather(slot_row, w_row, hb):
    ge = GATHER_EXPERTS
    return pl.pallas_call(
        _gather_kernel,
        grid=(BATCH, N_EXPERTS // ge),
        in_specs=[pl.BlockSpec((None, ge, SEQ), lambda b, g: (b, g, 0)),
                  pl.BlockSpec((None, ge, SEQ), lambda b, g: (b, g, 0)),
                  pl.BlockSpec((SEQ, D_MODEL), lambda b, g: (b, 0))],
        out_specs=[pl.BlockSpec((ge, None, CAPACITY, D_MODEL), lambda b, g: (g, b, 0, 0)),
                   pl.BlockSpec((ge, None, CAPACITY, LANES), lambda b, g: (g, b, 0, 0))],
        out_shape=[jax.ShapeDtypeStruct((N_EXPERTS, BATCH, CAPACITY, D_MODEL), BF16),
                   jax.ShapeDtypeStruct((N_EXPERTS, BATCH, CAPACITY, LANES), F32)],
        compiler_params=_params("parallel", "parallel"),
        name="expert_gather",
    )(slot_row, w_row, hb)


def _expert_kernel(xe_ref, wsl_ref, wg_ref, wu_ref, wd_ref, ye_ref, wg_s, wu_s, wd_s, *, span):
    ph = pl.program_id(0)
    i = pl.program_id(1)
    rows = pl.ds(pl.multiple_of(i * span, span), span)
    nxt = ph % 2
    wg_s[nxt, rows, :] = wg_ref[rows, :].astype(BF16)
    wu_s[nxt, rows, :] = wu_ref[rows, :].astype(BF16)
    wd_s[nxt, rows, :] = wd_ref[rows, :].astype(BF16)

    @pl.when(ph > 0)
    def _():
        cur = (ph + 1) % 2
        xe = xe_ref[...]
        g = _dot(xe, wg_s[cur])
        u = _dot(xe, wu_s[cur])
        act = (g * jax.nn.sigmoid(g) * u).astype(BF16)
        ye_ref[...] = (_dot(act, wd_s[cur]) * wsl_ref[:, 0:1]).astype(BF16)


def _experts(xe, wsl, wg, wu, wd, layer):
    rows = BATCH * CAPACITY
    tm = TILE["expert"]
    nt = rows // tm
    last = N_EXPERTS - 1
    wspec = lambda a: pl.BlockSpec((None, None) + a.shape[2:],
                                   lambda ph, i: (layer, jnp.minimum(ph, last), 0, 0))
    data = lambda n: pl.BlockSpec(
        (None, tm, n), lambda ph, i: (jnp.maximum(ph - 1, 0), jnp.where(ph > 0, i, 0), 0))
    return pl.pallas_call(
        functools.partial(_expert_kernel, span=D_MODEL // nt),
        grid=(N_EXPERTS + 1, nt),
        in_specs=[data(D_MODEL), data(LANES), wspec(wg), wspec(wu), wspec(wd)],
        out_specs=data(D_MODEL),
        out_shape=jax.ShapeDtypeStruct((N_EXPERTS, rows, D_MODEL), BF16),
        scratch_shapes=[pltpu.VMEM((2, D_MODEL, D_FF), BF16), pltpu.VMEM((2, D_MODEL, D_FF), BF16),
                        pltpu.VMEM((2, D_FF, D_MODEL), BF16)],
        compiler_params=_params("arbitrary", "arbitrary"),
        name="expert_ffn",
    )(xe.reshape(N_EXPERTS, rows, D_MODEL), wsl.reshape(N_EXPERTS, rows, LANES), wg, wu, wd)


def _combine_kernel(x_ref, ye_ref, slot_ref, p_ref, g3_ref, wpg_ref, wpp_ref,
                    gf_ref, out_ref, *, final_norm):
    slot = slot_ref[...]
    c = lax.broadcasted_iota(jnp.int32, (slot.shape[0], CAPACITY), 1)
    onehot = jnp.concatenate(
        [jnp.where(slot[:, e:e + 1] == c, 1.0, 0.0).astype(BF16) for e in range(N_EXPERTS)], axis=1)
    ye = ye_ref[...].reshape(N_EXPERTS * CAPACITY, D_MODEL)
    acc = x_ref[...] + _dot(onehot, ye)
    h = _rms(acc, g3_ref[...]).astype(BF16)
    gate = jax.nn.sigmoid(_dot(h, wpg_ref[...]))
    y = acc + gate * _dot(p_ref[...].astype(BF16), wpp_ref[...])
    if final_norm:
        y = _rms(y, gf_ref[...])
    out_ref[...] = y


def _combine(x, ye, slot_col, p, layer, g3, wpg, wpp, gf, final_norm):
    tm = TILE["combine"]
    nt = SEQ // tm
    p0 = layer * (TOKENS // tm)
    return pl.pallas_call(
        functools.partial(_combine_kernel, final_norm=final_norm),
        grid=(BATCH, nt),
        in_specs=[pl.BlockSpec((tm, D_MODEL), lambda b, i: (b * nt + i, 0)),
                  pl.BlockSpec((N_EXPERTS, None, CAPACITY, D_MODEL), lambda b, i: (0, b, 0, 0)),
                  pl.BlockSpec((None, tm, LANES), lambda b, i: (b, i, 0)),
                  pl.BlockSpec((tm, PLE_DIM), lambda b, i: (p0 + b * nt + i, 0)),
                  _const_spec(g3.shape), _const_spec(wpg.shape), _const_spec(wpp.shape),
                  _const_spec(gf.shape)],
        out_specs=pl.BlockSpec((tm, D_MODEL), lambda b, i: (b * nt + i, 0)),
        out_shape=jax.ShapeDtypeStruct((TOKENS, D_MODEL), F32),
        compiler_params=_params("parallel", "parallel"),
        name="combine",
    )(x, ye.reshape(N_EXPERTS, BATCH, CAPACITY, D_MODEL), slot_col, p, g3, wpg, wpp, gf)


DFT_FINE = 64
DFT_STEP = 4


def _dft_kernel(ca_ref, sa_ref, cb_ref, sb_ref, c_ref, s_ref):
    cb = cb_ref[...]
    sb = sb_ref[...]
    for r in range(DFT_STEP):
        ca = ca_ref[r]
        sa = sa_ref[r]
        rows = slice(r * DFT_FINE, (r + 1) * DFT_FINE)
        c_ref[rows, :] = (ca * cb - sa * sb).astype(BF16)
        s_ref[rows, :] = (sa * cb + ca * sb).astype(BF16)


def _dft_tables(n_points):
    t = jnp.arange(SEQ, dtype=jnp.int32)[None, :]
    coarse = jnp.arange(SEQ // DFT_FINE, dtype=jnp.int32)[:, None] * DFT_FINE
    fine = jnp.arange(DFT_FINE, dtype=jnp.int32)[:, None]
    ang = lambda f: ((f * t) % n_points).astype(F32) * (2.0 * math.pi / n_points)
    ca = jnp.cos(ang(coarse))[:, None, :]
    sa = jnp.sin(ang(coarse))[:, None, :]
    cb = jnp.cos(ang(fine))
    sb = jnp.sin(ang(fine))
    blk = DFT_STEP * DFT_FINE
    return pl.pallas_call(
        _dft_kernel,
        grid=(SEQ // blk,),
        in_specs=[pl.BlockSpec((DFT_STEP, 1, SEQ), lambda i: (i, 0, 0)),
                  pl.BlockSpec((DFT_STEP, 1, SEQ), lambda i: (i, 0, 0)),
                  _const_spec(cb.shape), _const_spec(sb.shape)],
        out_specs=[pl.BlockSpec((blk, SEQ), lambda i: (i, 0)),
                   pl.BlockSpec((blk, SEQ), lambda i: (i, 0))],
        out_shape=[jax.ShapeDtypeStruct((SEQ, SEQ), BF16), jax.ShapeDtypeStruct((SEQ, SEQ), BF16)],
        compiler_params=_params("parallel"),
        name="dft_tables",
    )(ca, sa, cb, sb)


def _fnet_group_tables():
    gc = D_MIX // FN_GROUPS
    i = lax.broadcasted_iota(jnp.int32, (D_MIX, D_MIX), 0)
    j = lax.broadcasted_iota(jnp.int32, (D_MIX, D_MIX), 1)
    same = (i // gc) == (j // gc)
    ang = (((i % gc) * (j % gc)) % gc).astype(F32) * (2.0 * math.pi / gc)
    return (jnp.where(same, jnp.cos(ang), 0.0).astype(BF16),
            jnp.where(same, jnp.sin(ang), 0.0).astype(BF16))


def _hyena_features():
    t01 = jnp.linspace(0.0, 1.0, SEQ, dtype=F32)[:, None]
    bands = jnp.linspace(1e-4, HY_BANDS - 1, HY_BANDS, dtype=F32)
    ang = 2.0 * math.pi * jnp.arange(SEQ, dtype=F32)[:, None] * bands / SEQ
    z = jnp.concatenate([t01, jnp.cos(ang), -jnp.sin(ang)], axis=-1)
    z = jnp.pad(z, ((0, 0), (0, LANES - HY_EMB)))
    max_decay = math.log(HY_TARGET) / HY_FAST_DECAY
    min_decay = math.log(HY_TARGET) / HY_SLOW_DECAY
    deltas = jnp.linspace(min_decay, max_decay, D_MIX, dtype=F32)
    window = jnp.exp(-t01 * jnp.abs(deltas))
    return z, window


def _rot_cols(w):
    half = w.shape[-1] // 2
    return jnp.concatenate([-w[..., half:], w[..., :half]], axis=-1)


def _rope_tables():
    inv = ROPE_THETA ** (-jnp.arange(0, MLA_ROPE, 2, dtype=F32) / MLA_ROPE)
    ang = jnp.arange(SEQ, dtype=F32)[:, None] * inv
    cos = jnp.concatenate([jnp.cos(ang), jnp.cos(ang)], axis=-1)
    sin = jnp.concatenate([jnp.sin(ang), jnp.sin(ang)], axis=-1)
    scale = (MLA_NOPE + MLA_ROPE) ** -0.5
    pad = MLA_HEAD_PAD - MLA_NOPE - MLA_ROPE
    one = jnp.ones((SEQ, MLA_NOPE), F32)
    zero = jnp.zeros((SEQ, MLA_NOPE), F32)
    zpad = jnp.zeros((SEQ, pad), F32)
    cosq = jnp.tile(jnp.concatenate([one, cos, zpad], axis=-1) * scale, (1, MLA_HEADS))
    sinq = jnp.tile(jnp.concatenate([zero, sin, zpad], axis=-1) * scale, (1, MLA_HEADS))
    csk = jnp.concatenate([cos, sin, jnp.zeros((SEQ, LANES - 2 * MLA_ROPE), F32)], axis=-1)
    return cosq, sinq, csk


def _mla_weights(w_uq, w_ukv):
    pad = MLA_HEAD_PAD - MLA_NOPE - MLA_ROPE
    wq = w_uq.reshape(MLA_Q_RANK, MLA_HEADS, MLA_NOPE + MLA_ROPE)
    nope, pe = wq[..., :MLA_NOPE], wq[..., MLA_NOPE:]
    zp = jnp.zeros((MLA_Q_RANK, MLA_HEADS, pad), F32)
    wqa = jnp.concatenate([nope, pe, zp], axis=-1).reshape(MLA_Q_RANK, MLA_QK)
    wqb = jnp.concatenate([jnp.zeros_like(nope), _rot_cols(pe), zp], axis=-1)
    wqb = wqb.reshape(MLA_Q_RANK, MLA_QK)
    wkv = w_ukv.reshape(MLA_KV_RANK, MLA_HEADS, MLA_NOPE + MLA_V)
    knope, v = wkv[..., :MLA_NOPE], wkv[..., MLA_NOPE:]
    wk = jnp.concatenate(
        [knope, jnp.zeros((MLA_KV_RANK, MLA_HEADS, MLA_HEAD_PAD - MLA_NOPE), F32)], axis=-1)
    wk = wk.reshape(MLA_KV_RANK, MLA_QK)
    wv = v.reshape(MLA_KV_RANK, MLA_HEADS * MLA_V)
    r = lax.broadcasted_iota(jnp.int32, (LANES, MLA_QK), 0)
    c = lax.broadcasted_iota(jnp.int32, (LANES, MLA_QK), 1)
    epe = jnp.where((r < 2 * MLA_ROPE) & (c % MLA_HEAD_PAD == MLA_NOPE + r % MLA_ROPE), 1.0, 0.0)
    return (wqa.astype(BF16), wqb.astype(BF16), wk.astype(BF16), wv.astype(BF16),
            epe.astype(BF16))


def kernel(x, p, norm1_g, w_in, b_gate, hy_conv_w, hy_conv_b, hf_w1, hf_b1, hf_freq, hf_w2,
           hf_b2, hf_w3, hy_skip, q_norm_g, w_uq, kv_norm_g, w_ukv, rpb, w_br, w_out, norm2_g,
           w_router, w_e_gate, w_e_up, w_e_down, norm3_g, w_ple_gate, w_ple_proj, final_g):
    conv_c, conv_s = _dft_tables(2 * SEQ)
    fnet_c, fnet_s = _dft_tables(SEQ)
    fnet_cg, fnet_sg = _fnet_group_tables()
    zfeat, window = _hyena_features()
    cosq, sinq, csk = _rope_tables()
    row = lambda a: a.reshape(1, -1)

    xt = x.reshape(TOKENS, D_MODEL)
    for i in range(DEPTH):
        g1 = row(norm1_g[i])
        wa, wna, wgate = _inproj_weights(w_in, i)
        wqa, wqb, wk, wv, epe = _mla_weights(w_uq[i], w_ukv[i])

        u_hy, u_fn, q, k, v, naq, nak, nav = _inproj(
            xt, g1, wa, wna, row(q_norm_g[i]), wqa, wqb, row(kv_norm_g[i]), wk, wv, epe,
            cosq, sinq, csk)

        w1 = jnp.pad(hf_w1[i], ((0, LANES - HY_EMB), (0, 0)))
        kf, kny = _hyena_filter(zfeat, window, w1, row(hf_b1[i]), hf_freq[i], hf_w2[i],
                                row(hf_b2[i]), hf_w3[i], conv_c, conv_s)
        conv_b = row(hy_conv_b[i])
        z1 = _hyena_stage(u_hy, 0, u_hy, 2, hy_conv_w[i], conv_b, conv_c, conv_s, kf[0],
                          kny[0], row(hy_skip[i, 0]), True, F32)
        y_hy = _hyena_stage(u_hy, 1, z1, 0, hy_conv_w[i], conv_b, conv_c, conv_s, kf[1],
                            kny[1], row(hy_skip[i, 1]), False, BF16)
        y_fn = _fnet(u_fn, fnet_c, fnet_s, fnet_cg, fnet_sg)
        y_mla = _mla(q, k, v)
        y_na = _neighborhood(naq, nak, nav, _na_bias_tiles(rpb[i]))
        xt = _merge(xt, g1, y_hy, y_fn, y_mla, y_na, wgate, row(b_gate[i]),
                    w_br[i].astype(BF16), w_out[i].astype(BF16))

        wr_t = jnp.pad(w_router[i].T, ((0, LANES - N_EXPERTS), (0, 0)))
        hb, logits = _router(xt, row(norm2_g[i]), wr_t)
        slot_row, slot_col, w_row = _select(logits)
        xe, wsl = _gather(slot_row, w_row, hb)
        ye = _experts(xe, wsl, w_e_gate, w_e_up, w_e_down, i)
        xt = _combine(xt, ye, slot_col, p.reshape(DEPTH * TOKENS, PLE_DIM), i,
                      row(norm3_g[i]), w_ple_gate[i].astype(BF16),
                      w_ple_proj[i].astype(BF16), row(final_g), i == DEPTH - 1)
    return xt.reshape(BATCH, SEQ, D_MODEL)
```

```python
import functools
import math

import jax
import jax.numpy as jnp
from jax import lax
from jax.experimental import pallas as pl
from jax.experimental.pallas import tpu as pltpu

F32 = jnp.float32
BF16 = jnp.bfloat16

D_MODEL = 1024
BATCH = 8
SEQ = 2048
DEPTH = 2
TOKENS = BATCH * SEQ

GRID_W = 64
GRID_R = SEQ // GRID_W
D_MIX = 256
N_BRANCH = 4
EPS = 1e-6
HY_ORDER = 2
HY_BANDS = 16
HY_EMB = 2 * HY_BANDS + 1
HY_FFN = 64
HY_TARGET = 1e-2
HY_FAST_DECAY = 0.3
HY_SLOW_DECAY = 1.5
FN_GROUPS = 4
MLA_HEADS = 4
MLA_NOPE = 64
MLA_ROPE = 32
MLA_V = 64
MLA_Q_RANK = 256
MLA_KV_RANK = 128
ROPE_THETA = 10000.0
NA_HEADS = 4
NA_HEAD_DIM = D_MIX // NA_HEADS
NA_WIN_R = 8
NA_WIN_C = 16
N_EXPERTS = 16
CAPACITY = 2 * SEQ // N_EXPERTS
D_FF = 1024
PLE_DIM = 256

HY_COLS = 3 * D_MIX
OFF_FN = HY_COLS
OFF_CQ = OFF_FN + D_MIX
OFF_CKV = OFF_CQ + MLA_Q_RANK
OFF_KPE = OFF_CKV + MLA_KV_RANK
OFF_NA = OFF_KPE + MLA_ROPE
OFF_GATE = OFF_NA + 3 * D_MIX

LANES = 128
MLA_HEAD_PAD = 128
MLA_QK = MLA_HEADS * MLA_HEAD_PAD
WA_COLS = 1536
NEG_BIG = -1e30
LOG2E = math.log2(math.e)

TILE = dict(inproj=512, mla=512, merge=512, router=512, expert=512, combine=512)
NA_QROWS = 8
NA_KROWS = 16
NA_QBLK = NA_QROWS * GRID_W
NA_KBLK = NA_KROWS * GRID_W
VMEM_LIMIT = 56 * 1024 * 1024


def _params(*sem):
    return pltpu.CompilerParams(dimension_semantics=sem, vmem_limit_bytes=VMEM_LIMIT)


def _const_spec(shape):
    nd = len(shape)
    return pl.BlockSpec(shape, lambda *_: (0,) * nd, pipeline_mode=pl.Buffered(1))


def _rms(x, g):
    return x * lax.rsqrt(jnp.mean(x * x, axis=-1, keepdims=True) + EPS) * g


def _dot(a, b):
    return jnp.dot(a, b, preferred_element_type=F32)


def _dot_nt(a, b):
    return lax.dot_general(a, b, (((1,), (1,)), ((), ())), preferred_element_type=F32)


def _split2(x):
    hi = x.astype(BF16)
    lo = (x - hi.astype(F32)).astype(BF16)
    return hi, lo


def _inproj_kernel(x_ref, g1_ref, wa_ref, wna_ref, qg_ref, wqa_ref, wqb_ref, kvg_ref,
                   wk_ref, wv_ref, epe_ref, cosq_ref, sinq_ref, csk_ref,
                   uhy_ref, ufn_ref, q_ref, k_ref, v_ref, naq_ref, nak_ref, nav_ref):
    h = _rms(x_ref[...], g1_ref[...]).astype(BF16)
    ua = _dot(h, wa_ref[...])
    uhy_ref[...] = ua[:, :HY_COLS]
    ufn_ref[...] = ua[:, OFF_FN:OFF_CQ]
    cqn = _rms(ua[:, OFF_CQ:OFF_CKV], qg_ref[...]).astype(BF16)
    q = _dot(cqn, wqa_ref[...]) * cosq_ref[...] + _dot(cqn, wqb_ref[...]) * sinq_ref[...]
    q_ref[...] = q.astype(BF16)
    kvn = _rms(ua[:, OFF_CKV:OFF_KPE], kvg_ref[...]).astype(BF16)
    kpe = ua[:, OFF_KPE:WA_COLS] * csk_ref[...]
    k = _dot(kvn, wk_ref[...]) + _dot(kpe.astype(BF16), epe_ref[...])
    k_ref[...] = k.astype(BF16)
    v_ref[...] = _dot(kvn, wv_ref[...]).astype(BF16)
    una = _dot(h, wna_ref[...])
    naq_ref[...] = (una[:, :D_MIX] * (NA_HEAD_DIM ** -0.5 * LOG2E)).astype(BF16)
    nak_ref[...] = una[:, D_MIX:2 * D_MIX].astype(BF16)
    nav_ref[...] = una[:, 2 * D_MIX:].astype(BF16)


def _inproj(x, g1, wa, wna, qg, wqa, wqb, kvg, wk, wv, epe, cosq, sinq, csk):
    tm = TILE["inproj"]
    nt = SEQ // tm
    row = lambda n: pl.BlockSpec((tm, n), lambda i: (i, 0))
    pos = lambda n: pl.BlockSpec((tm, n), lambda i: (i % nt, 0))
    outs = [(HY_COLS, F32), (D_MIX, F32), (MLA_QK, BF16), (MLA_QK, BF16), (D_MIX, BF16),
            (D_MIX, BF16), (D_MIX, BF16), (D_MIX, BF16)]
    return pl.pallas_call(
        _inproj_kernel,
        grid=(TOKENS // tm,),
        in_specs=[row(D_MODEL), _const_spec(g1.shape), _const_spec(wa.shape),
                  _const_spec(wna.shape), _const_spec(qg.shape), _const_spec(wqa.shape),
                  _const_spec(wqb.shape), _const_spec(kvg.shape), _const_spec(wk.shape),
                  _const_spec(wv.shape), _const_spec(epe.shape),
                  pos(MLA_QK), pos(MLA_QK), pos(LANES)],
        out_specs=[row(n) for n, _ in outs],
        out_shape=[jax.ShapeDtypeStruct((TOKENS, n), dt) for n, dt in outs],
        compiler_params=_params("parallel"),
        name="inproj",
    )(x, g1, wa, wna, qg, wqa, wqb, kvg, wk, wv, epe, cosq, sinq, csk)


WPREP_ROWS = 128


def _inproj_weights_kernel(w_ref, wa_ref, wna_ref, wg_ref):
    half = MLA_ROPE // 2
    wa_ref[:, :OFF_KPE] = w_ref[:, :OFF_KPE].astype(BF16)
    kpe_group = jnp.concatenate(
        [w_ref[:, OFF_KPE:OFF_NA], -w_ref[:, OFF_KPE + half:OFF_NA],
         w_ref[:, OFF_KPE:OFF_KPE + half],
         jnp.zeros((WPREP_ROWS, WA_COLS - OFF_NA - MLA_ROPE), F32)], axis=1)
    wa_ref[:, OFF_KPE:] = kpe_group.astype(BF16)
    wna_ref[...] = w_ref[:, OFF_NA:OFF_GATE].astype(BF16)
    wg_ref[...] = w_ref[:, OFF_GATE:].astype(BF16)


def _inproj_weights(w_in, layer):
    cols = (WA_COLS, OFF_GATE - OFF_NA, w_in.shape[-1] - OFF_GATE)
    return pl.pallas_call(
        _inproj_weights_kernel,
        grid=(D_MODEL // WPREP_ROWS,),
        in_specs=[pl.BlockSpec((None, WPREP_ROWS, w_in.shape[-1]), lambda r: (layer, r, 0))],
        out_specs=[pl.BlockSpec((WPREP_ROWS, n), lambda r: (r, 0)) for n in cols],
        out_shape=[jax.ShapeDtypeStruct((D_MODEL, n), BF16) for n in cols],
        compiler_params=_params("parallel"),
        name="inproj_weights",
    )(w_in)


HALF = SEQ // 2
HY_SEQS = 1


def _residue_rows(ref, r, nres):
    rows = pl.ds(r, SEQ // nres, stride=nres)
    if isinstance(ref, tuple):
        return jnp.concatenate([h[rows, :] for h in ref], axis=1)
    return jnp.concatenate([ref[j, rows, :] for j in range(ref.shape[0])], axis=1)


def _store_slabs(ref, value, rows=slice(None)):
    for j in range(ref.shape[0]):
        ref[j, rows, :] = value[:, j * LANES:(j + 1) * LANES]


def _table_cols(cs_ref, r, nres, part=None):
    blk = SEQ // nres
    lo = 2 * blk * r
    if part is None:
        return cs_ref[:, lo:lo + 2 * blk]
    return cs_ref[:, lo + part * blk:lo + (part + 1) * blk]


def _dft_fwd(cs_ref, src_ref, want_cos=True, want_sin=True, split=False):
    def prod(part, z, r):
        tab = _table_cols(cs_ref, r, 4, part)
        if split:
            hi, lo = _split2(z)
            return _dot(tab, hi) + _dot(tab, lo)
        return _dot(tab, z.astype(BF16))

    pc, ps = [None] * 4, [None] * 4
    for r in range(4):
        z = _residue_rows(src_ref, r, 4)
        odd = r % 2 == 1
        if want_cos or odd:
            pc[r] = prod(0, z, r)
        if want_sin or odd:
            ps[r] = prod(1, z, r)
    a = b = None
    if want_cos:
        a = ((pc[0] + pc[2]) + (pc[1] + pc[3]), (pc[0] - pc[2]) + (ps[3] - ps[1]))
    if want_sin:
        b = ((ps[0] + ps[2]) + (ps[1] + ps[3]), (ps[0] - ps[2]) + (pc[1] - pc[3]))
    return a, b


def _dft_inv(cs_ref, yre_ref, yim_ref):
    g, h = [], {}
    for r in range(4):
        yr = _residue_rows(yre_ref, r, 4).astype(BF16)
        yi = _residue_rows(yim_ref, r, 4).astype(BF16)
        tab = _table_cols(cs_ref, r, 4)
        g.append(_dot(tab, jnp.concatenate([yr, -yi], axis=0)))
        if r % 2 == 1:
            h[r] = _dot(tab, jnp.concatenate([yi, yr], axis=0))
    return (g[0] + g[2]) + (g[1] + g[3]), (g[0] - g[2]) + (h[3] - h[1])


def _hyena_filter_kernel(z_ref, win_ref, w1_ref, b1_ref, freq_ref, w2_ref, b2_ref, w3_ref,
                         cs_ref, kf_ref, kny_ref, ksum_s, kdif_s):
    hp = lax.Precision.HIGHEST
    freq = freq_ref[...]
    hf = jnp.sin(freq[0:1] * (jnp.dot(z_ref[...], w1_ref[...], precision=hp,
                                      preferred_element_type=F32) + b1_ref[...]))
    hf = jnp.sin(freq[1:2] * (jnp.dot(hf, w2_ref[...], precision=hp,
                                      preferred_element_type=F32) + b2_ref[...]))
    hf = jnp.dot(hf, w3_ref[...], precision=hp, preferred_element_type=F32)
    win = win_ref[...]
    t = lax.broadcasted_iota(jnp.int32, (SEQ, D_MIX), 0)
    sgn = (1 - 2 * (t & 1)).astype(F32)
    fwd = hf[:, :D_MIX] * win
    bwd = jnp.where(t == 0, 0.0, hf[:, D_MIX:] * win)
    nrm = lax.rsqrt(jnp.sum(fwd * fwd + bwd * bwd, axis=0, keepdims=True) + EPS)
    ksum = (fwd + bwd) * nrm
    _store_slabs(ksum_s, ksum)
    _store_slabs(kdif_s, (bwd - fwd) * nrm)
    kre, _ = _dft_fwd(cs_ref, ksum_s, want_sin=False, split=True)
    _, kim = _dft_fwd(cs_ref, kdif_s, want_cos=False, split=True)
    wf = 2.0 / (2 * SEQ)
    for part, spec in enumerate((kre, kim)):
        lo = spec[0] * wf
        kf_ref[part, :HALF, :] = lo
        kf_ref[part, 0:1, :] = lo[0:1] * 0.5
        kf_ref[part, HALF:, :] = spec[1] * wf
    kny = jnp.sum(ksum * sgn, axis=0, keepdims=True) * (1.0 / (2 * SEQ))
    kny_ref[...] = jnp.broadcast_to(kny, (8, D_MIX))


def _hyena_filter(zfeat, window, w1, b1, freq, w2, b2, w3, table):
    consts = (zfeat, window, w1, b1, freq, w2, b2)
    return pl.pallas_call(
        _hyena_filter_kernel,
        grid=(HY_ORDER,),
        in_specs=[_const_spec(a.shape) for a in consts]
        + [pl.BlockSpec((HY_FFN, 2 * D_MIX), lambda o: (0, o)),
           _const_spec(table.shape)],
        out_specs=[pl.BlockSpec((None, 2, SEQ, D_MIX), lambda o: (o, 0, 0, 0)),
                   pl.BlockSpec((None, 8, D_MIX), lambda o: (o, 0, 0))],
        out_shape=[jax.ShapeDtypeStruct((HY_ORDER, 2, SEQ, D_MIX), F32),
                   jax.ShapeDtypeStruct((HY_ORDER, 8, D_MIX), F32)],
        scratch_shapes=[pltpu.VMEM((D_MIX // LANES, SEQ, LANES), F32)] * 2,
        compiler_params=_params("arbitrary"),
        name="hyena_filter",
    )(*consts, w3, table)


def _short_conv(u, w, b):
    t = lax.broadcasted_iota(jnp.int32, u.shape, 0)
    prev = jnp.where(t == 0, 0.0, pltpu.roll(u, 1, 0))
    nxt = jnp.where(t == SEQ - 1, 0.0, pltpu.roll(u, SEQ - 1, 0))
    return prev * w[0:1] + u * w[1:2] + nxt * w[2:3] + b


def _hyena_stage_kernel(gate_ref, src_ref, wg_ref, bg_ref, ws_ref, bs_ref, cs_ref,
                        kf_ref, kny_ref, skip_ref, out_ref, z_s, yre_s, yim_s, gate_s, rest_s,
                        *, conv_src):
    for q in range(HY_SEQS):
        seq = slice(q * SEQ, (q + 1) * SEQ)
        zq, yre_q, yim_q = z_s.at[q], yre_s.at[q], yim_s.at[q]
        z = src_ref[seq, :]
        if conv_src:
            z = _short_conv(z, ws_ref[...], bs_ref[...])
        _store_slabs(zq, z)
        t = lax.broadcasted_iota(jnp.int32, z.shape, 0)
        sgn = (1 - 2 * (t & 1)).astype(F32)
        nyq = jnp.sum(z * sgn, axis=0, keepdims=True) * kny_ref[0:1]
        gate = _short_conv(gate_ref[seq, :], wg_ref[...], bg_ref[...])
        gate_s[seq, :] = gate
        rest_s[seq, :] = gate * (sgn * nyq + z * skip_ref[...])
        a, b = _dft_fwd(cs_ref, zq)
        for half in range(2):
            rows = slice(half * HALF, (half + 1) * HALF)
            kre = kf_ref[0, rows, :]
            kim = kf_ref[1, rows, :]
            _store_slabs(yre_q, a[half] * kre + b[half] * kim, rows)
            _store_slabs(yim_q, a[half] * kim - b[half] * kre, rows)
        for half, y in enumerate(_dft_inv(cs_ref, yre_q, yim_q)):
            rows = slice(q * SEQ + half * HALF, q * SEQ + (half + 1) * HALF)
            out_ref[rows, :] = (gate_s[rows, :] * y + rest_s[rows, :]).astype(out_ref.dtype)


def _hyena_stage(u_hy, gate_blk, src, src_blk, conv_w, conv_b, table, kf, kny, skip,
                 conv_src, out_dtype):
    rows = HY_SEQS * SEQ
    col = lambda blk: pl.BlockSpec((rows, D_MIX), lambda b: (b, blk))
    wcol = lambda blk, r: pl.BlockSpec((r, D_MIX), lambda b: (0, blk))
    ws_blk = src_blk if conv_src else 0
    return pl.pallas_call(
        functools.partial(_hyena_stage_kernel, conv_src=conv_src),
        grid=(BATCH // HY_SEQS,),
        in_specs=[col(gate_blk), col(src_blk), wcol(gate_blk, 3), wcol(gate_blk, 1),
                  wcol(ws_blk, 3), wcol(ws_blk, 1), _const_spec(table.shape),
                  _const_spec(kf.shape), _const_spec(kny.shape),
                  _const_spec(skip.shape)],
        out_specs=pl.BlockSpec((rows, D_MIX), lambda b: (b, 0)),
        out_shape=jax.ShapeDtypeStruct((TOKENS, D_MIX), out_dtype),
        scratch_shapes=[pltpu.VMEM((HY_SEQS, D_MIX // LANES, SEQ, LANES), F32)] * 3
        + [pltpu.VMEM((rows, D_MIX), F32)] * 2,
        compiler_params=_params("parallel"),
        name="hyena_stage",
    )(u_hy, src, conv_w, conv_b, conv_w, conv_b, table, kf, kny, skip)


def _fnet_kernel(xa_ref, xb_ref, cs_ref, cg_ref, sg_ref, out_ref):
    parts = []
    for r in range(2):
        xb = _residue_rows((xa_ref, xb_ref), r, 2).astype(BF16)
        xc = _dot(xb, cg_ref[...]).astype(BF16)
        xs = _dot(xb, sg_ref[...]).astype(BF16)
        parts.append(_dot(_table_cols(cs_ref, r, 2), jnp.concatenate([xc, -xs], axis=0)))
    scale = (SEQ * D_MIX // FN_GROUPS) ** -0.5
    out_ref[:HALF, :] = ((parts[0] + parts[1]) * scale).astype(out_ref.dtype)
    out_ref[HALF:, :] = ((parts[0] - parts[1]) * scale).astype(out_ref.dtype)


def _fnet(u_fn, table, cg, sg):
    return pl.pallas_call(
        _fnet_kernel,
        grid=(BATCH,),
        in_specs=[pl.BlockSpec((SEQ, LANES), lambda b: (b, 0)),
                  pl.BlockSpec((SEQ, LANES), lambda b: (b, 1)), _const_spec(table.shape),
                  _const_spec(cg.shape), _const_spec(sg.shape)],
        out_specs=pl.BlockSpec((SEQ, D_MIX), lambda b: (b, 0)),
        out_shape=jax.ShapeDtypeStruct((TOKENS, D_MIX), BF16),
        compiler_params=_params("parallel"),
        name="fnet",
    )(u_fn, u_fn, table, cg, sg)


def _softmax2_pv(s2, v):
    m = jnp.max(s2, axis=-1, keepdims=True)
    p = jnp.exp2(s2 - m)
    l = jnp.sum(p, axis=-1, keepdims=True)
    return _dot(p.astype(BF16), v) / l


def _mla_kernel(q_ref, k_ref, v_ref, out_ref):
    v = v_ref[...]
    head = lax.broadcasted_iota(jnp.int32, out_ref.shape, 1) // MLA_V
    acc = jnp.zeros(out_ref.shape, F32)
    for h in range(MLA_HEADS):
        sl = slice(h * MLA_HEAD_PAD, (h + 1) * MLA_HEAD_PAD)
        s2 = _dot_nt(q_ref[:, sl], k_ref[:, sl])
        acc = jnp.where(head == h, _softmax2_pv(s2, v), acc)
    out_ref[...] = acc.astype(out_ref.dtype)


def _mla(q, k, v):
    tm = TILE["mla"]
    nt = SEQ // tm
    return pl.pallas_call(
        _mla_kernel,
        grid=(BATCH, nt),
        in_specs=[pl.BlockSpec((tm, MLA_QK), lambda b, i: (b * nt + i, 0)),
                  pl.BlockSpec((SEQ, MLA_QK), lambda b, i: (b, 0)),
                  pl.BlockSpec((SEQ, D_MIX), lambda b, i: (b, 0))],
        out_specs=pl.BlockSpec((tm, D_MIX), lambda b, i: (b * nt + i, 0)),
        out_shape=jax.ShapeDtypeStruct((TOKENS, D_MIX), BF16),
        compiler_params=_params("parallel", "parallel"),
        name="mla_attention",
    )(q, k, v)


def _na_key_row0(j):
    return jnp.clip(j * NA_QROWS - NA_WIN_R // 2, 0, GRID_R - NA_KROWS)


NA_PAIRS = 2 * NA_WIN_R


def _na_kernel(q_ref, k_ref, v_ref, tile_ref, out_ref, bias_s):
    j = pl.program_id(0)
    krow0 = _na_key_row0(j)

    @pl.when(pl.program_id(1) == 0)
    def _():
        rq = j * NA_QROWS + lax.broadcasted_iota(jnp.int32, (NA_QBLK, NA_KBLK), 0) // GRID_W
        rk = krow0 + lax.broadcasted_iota(jnp.int32, (NA_QBLK, NA_KBLK), 1) // GRID_W
        rs = jnp.clip(rq - NA_WIN_R // 2, 0, GRID_R - NA_WIN_R)
        rowmask = jnp.where(rk < rs, NEG_BIG, jnp.where(rk >= rs + NA_WIN_R, NEG_BIG, 0.0))
        base = krow0 - j * NA_QROWS + NA_WIN_R
        for h in range(NA_HEADS):
            bias = jnp.concatenate(
                [jnp.concatenate(
                    [tile_ref[h, jnp.clip(base + 2 * kp - r, 0, NA_PAIRS - 1)]
                     for kp in range(NA_KROWS // 2)], axis=1)
                 for r in range(NA_QROWS)], axis=0)
            bias_s[h] = bias + rowmask

    off = pl.multiple_of(krow0 * GRID_W, GRID_W)
    q = q_ref[...]
    k = k_ref[pl.ds(off, NA_KBLK), :]
    v = v_ref[pl.ds(off, NA_KBLK), :]
    head = lax.broadcasted_iota(jnp.int32, (NA_QBLK, D_MIX), 1) // NA_HEAD_DIM
    acc = jnp.zeros((NA_QBLK, D_MIX), F32)
    for h in range(NA_HEADS):
        qh = jnp.where(head == h, q, jnp.zeros_like(q))
        s2 = _dot_nt(qh, k) + bias_s[h]
        acc = jnp.where(head == h, _softmax2_pv(s2, v), acc)
    out_ref[...] = acc.astype(out_ref.dtype)


def _neighborhood(q, k, v, tiles):
    nj = SEQ // NA_QBLK
    return pl.pallas_call(
        _na_kernel,
        grid=(nj, BATCH),
        in_specs=[pl.BlockSpec((NA_QBLK, D_MIX), lambda j, b: (b * nj + j, 0)),
                  pl.BlockSpec((SEQ, D_MIX), lambda j, b: (b, 0)),
                  pl.BlockSpec((SEQ, D_MIX), lambda j, b: (b, 0)),
                  _const_spec(tiles.shape)],
        out_specs=pl.BlockSpec((NA_QBLK, D_MIX), lambda j, b: (b * nj + j, 0)),
        out_shape=jax.ShapeDtypeStruct((TOKENS, D_MIX), BF16),
        scratch_shapes=[pltpu.VMEM((NA_HEADS, NA_QBLK, NA_KBLK), F32)],
        compiler_params=_params("parallel", "arbitrary"),
        name="neighborhood_attention",
    )(q, k, v, tiles)


def _na_bias_tiles(rpb):
    c = jnp.arange(GRID_W)
    cs = jnp.clip(c - NA_WIN_C // 2, 0, GRID_W - NA_WIN_C)
    col_ok = (c[None, :] >= cs[:, None]) & (c[None, :] < cs[:, None] + NA_WIN_C)
    dc = jnp.clip(c[None, :] - c[:, None] + (NA_WIN_C - 1), 0, 2 * NA_WIN_C - 2)
    pick = (dc[None] == jnp.arange(2 * NA_WIN_C - 1)[:, None, None]).astype(F32)
    t = jnp.einsum('hrd,dqk->hrqk', rpb.astype(F32), pick, precision=lax.Precision.HIGHEST)
    t = jnp.where(col_ok, t * LOG2E, NEG_BIG)
    t = jnp.pad(t, ((0, 0), (1, 1), (0, 0), (0, 0)))
    return jnp.concatenate([t[:, :-1], t[:, 1:]], axis=-1)


def _merge_kernel(x_ref, g1_ref, yhy_ref, yfn_ref, ymla_ref, yna_ref, wg_ref, bg_ref,
                  wbr_ref, wout_ref, out_ref):
    x = x_ref[...]
    h = _rms(x, g1_ref[...]).astype(BF16)
    merged = jnp.zeros(x.shape, F32)
    for n, y_ref in enumerate((yhy_ref, yfn_ref, ymla_ref, yna_ref)):
        sl = slice(n * D_MODEL, (n + 1) * D_MODEL)
        gate = jax.nn.sigmoid(_dot(h, wg_ref[:, sl]) + bg_ref[:, sl])
        merged = merged + gate * _dot(y_ref[...], wbr_ref[n])
    out_ref[...] = x + _dot(merged.astype(BF16), wout_ref[...])


def _merge(x, g1, yhy, yfn, ymla, yna, wg, bg, wbr, wout):
    tm = TILE["merge"]
    row = lambda n: pl.BlockSpec((tm, n), lambda i: (i, 0))
    return pl.pallas_call(
        _merge_kernel,
        grid=(TOKENS // tm,),
        in_specs=[row(D_MODEL), _const_spec(g1.shape), row(D_MIX), row(D_MIX), row(D_MIX),
                  row(D_MIX), _const_spec(wg.shape), _const_spec(bg.shape),
                  _const_spec(wbr.shape), _const_spec(wout.shape)],
        out_specs=row(D_MODEL),
        out_shape=jax.ShapeDtypeStruct((TOKENS, D_MODEL), F32),
        compiler_params=_params("parallel"),
        name="merge",
    )(x, g1, yhy, yfn, ymla, yna, wg, bg, wbr, wout)


def _prefix_count(m):
    r = lax.broadcasted_iota(jnp.int32, (LANES, LANES), 0)
    c = lax.broadcasted_iota(jnp.int32, (LANES, LANES), 1)
    upper = jnp.where(r < c, 1.0, 0.0).astype(BF16)
    run = jnp.zeros((m.shape[0], 1), F32)
    parts = []
    for i in range(SEQ // LANES):
        chunk = m[:, i * LANES:(i + 1) * LANES]
        parts.append(_dot(chunk.astype(BF16), upper) + run)
        run = run + jnp.sum(chunk, axis=1, keepdims=True)
    return jnp.concatenate(parts, axis=1)


def _router_kernel(x_ref, g2_ref, wr_ref, hb_ref, logit_ref):
    h = _rms(x_ref[...], g2_ref[...])
    h_hi, h_lo = _split2(h)
    hb_ref[...] = h_hi
    w_hi, w_lo = _split2(wr_ref[...])
    logits = _dot_nt(w_hi, h_hi) + _dot_nt(w_hi, h_lo) + _dot_nt(w_lo, h_hi)
    logit_ref[...] = logits[:N_EXPERTS]


def _router(x, g2, wr_t):
    tm = TILE["router"]
    nt = SEQ // tm
    return pl.pallas_call(
        _router_kernel,
        grid=(TOKENS // tm,),
        in_specs=[pl.BlockSpec((tm, D_MODEL), lambda i: (i, 0)),
                  _const_spec(g2.shape), _const_spec(wr_t.shape)],
        out_specs=[pl.BlockSpec((tm, D_MODEL), lambda i: (i, 0)),
                   pl.BlockSpec((None, N_EXPERTS, tm), lambda i: (i // nt, 0, i % nt))],
        out_shape=[jax.ShapeDtypeStruct((TOKENS, D_MODEL), BF16),
                   jax.ShapeDtypeStruct((BATCH, N_EXPERTS, SEQ), F32)],
        compiler_params=_params("parallel"),
        name="router",
    )(x, g2, wr_t)


SELECT_MAX_ITERS = 192


def _select_kernel(logit_ref, slot_row_ref, slot_col_ref, w_row_ref, slot_s, w_s):
    b = pl.program_id(0)
    rows = BATCH * N_EXPERTS

    @pl.when(b == 0)
    def _():
        logits = logit_ref[...]
        ex = jnp.exp(logits - jnp.max(logits, axis=1, keepdims=True))
        aff = (ex / jnp.sum(ex, axis=1, keepdims=True)).reshape(rows, SEQ)

        def bisect(c):
            it, lo, hi, _ = c
            mid = 0.5 * (lo + hi)
            cnt = jnp.sum(jnp.where(aff >= mid, 1.0, 0.0), axis=1, keepdims=True)
            moving = jnp.where(mid == lo, 0.0, jnp.where(mid == hi, 0.0, 1.0))
            enough = cnt >= CAPACITY
            return (it + 1, jnp.where(enough, mid, lo), jnp.where(enough, hi, mid),
                    (jnp.max(moving) > 0).astype(jnp.int32))

        _, lo, hi, _ = lax.while_loop(
            lambda c: (c[0] < SELECT_MAX_ITERS) & (c[3] > 0), bisect,
            (jnp.int32(0), jnp.zeros((rows, 1), F32), jnp.full((rows, 1), 2.0, F32),
             jnp.int32(1)))
        above = jnp.where(aff >= hi, 1.0, 0.0)
        band = jnp.where(aff >= lo, 1.0, 0.0) - above
        need = CAPACITY - jnp.sum(above, axis=1, keepdims=True)
        sel = above + band * jnp.where(_prefix_count(band) < need, 1.0, 0.0)
        slot_s[...] = jnp.where(sel > 0, _prefix_count(sel), -1.0)
        w_s[...] = sel * aff

    r0 = pl.multiple_of(b * N_EXPERTS, N_EXPERTS)
    slot = slot_s[pl.ds(r0, N_EXPERTS), :]
    slot_row_ref[...] = slot.astype(jnp.int32)
    pad = jnp.full((LANES - N_EXPERTS, SEQ), -1.0, F32)
    slot_col_ref[...] = jnp.concatenate([slot, pad], axis=0).T.astype(jnp.int32)
    w_row_ref[...] = w_s[pl.ds(r0, N_EXPERTS), :]


def _select(logits):
    return pl.pallas_call(
        _select_kernel,
        grid=(BATCH,),
        in_specs=[_const_spec(logits.shape)],
        out_specs=[pl.BlockSpec((None, N_EXPERTS, SEQ), lambda b: (b, 0, 0)),
                   pl.BlockSpec((None, SEQ, LANES), lambda b: (b, 0, 0)),
                   pl.BlockSpec((None, N_EXPERTS, SEQ), lambda b: (b, 0, 0))],
        out_shape=[jax.ShapeDtypeStruct((BATCH, N_EXPERTS, SEQ), jnp.int32),
                   jax.ShapeDtypeStruct((BATCH, SEQ, LANES), jnp.int32),
                   jax.ShapeDtypeStruct((BATCH, N_EXPERTS, SEQ), F32)],
        scratch_shapes=[pltpu.VMEM((BATCH * N_EXPERTS, SEQ), F32),
                        pltpu.VMEM((BATCH * N_EXPERTS, SEQ), F32)],
        compiler_params=_params("arbitrary"),
        name="expert_select",
    )(logits)


GATHER_EXPERTS = 8


def _gather_kernel(slot_ref, w_ref, hb_ref, xe_ref, wsl_ref):
    ge = GATHER_EXPERTS
    c = lax.broadcasted_iota(jnp.int32, (CAPACITY, SEQ), 0)
    hits = [slot_ref[e:e + 1, :] == c for e in range(ge)]
    onehot = jnp.concatenate([jnp.where(m, 1.0, 0.0).astype(BF16) for m in hits], axis=0)
    rows = _dot(onehot, hb_ref[...]).astype(BF16)
    for e in range(ge):
        xe_ref[e] = rows[e * CAPACITY:(e + 1) * CAPACITY]
        wslot = jnp.sum(jnp.where(hits[e], w_ref[e:e + 1, :], 0.0), axis=1, keepdims=True)
        wsl_ref[e] = jnp.broadcast_to(wslot, (CAPACITY, LANES))


def _gather(slot_row, w_row, hb):
    ge = GATHER_EXPERTS
    return pl.pallas_call(
        _gather_kernel,
        grid=(BATCH, N_EXPERTS // ge),
        in_specs=[pl.BlockSpec((None, ge, SEQ), lambda b, g: (b, g, 0)),
                  pl.BlockSpec((None, ge, SEQ), lambda b, g: (b, g, 0)),
                  pl.BlockSpec((SEQ, D_MODEL), lambda b, g: (b, 0))],
        out_specs=[pl.BlockSpec((ge, None, CAPACITY, D_MODEL), lambda b, g: (g, b, 0, 0)),
                   pl.BlockSpec((ge, None, CAPACITY, LANES), lambda b, g: (g, b, 0, 0))],
        out_shape=[jax.ShapeDtypeStruct((N_EXPERTS, BATCH, CAPACITY, D_MODEL), BF16),
                   jax.ShapeDtypeStruct((N_EXPERTS, BATCH, CAPACITY, LANES), F32)],
        compiler_params=_params("parallel", "parallel"),
        name="expert_gather",
    )(slot_row, w_row, hb)


def _expert_kernel(xe_ref, wsl_ref, wg_ref, wu_ref, wd_ref, ye_ref, wg_s, wu_s, wd_s, *, span):
    ph = pl.program_id(0)
    i = pl.program_id(1)
    rows = pl.ds(pl.multiple_of(i * span, span), span)
    nxt = ph % 2
    wg_s[nxt, rows, :] = wg_ref[rows, :].astype(BF16)
    wu_s[nxt, rows, :] = wu_ref[rows, :].astype(BF16)
    wd_s[nxt, rows, :] = wd_ref[rows, :].astype(BF16)

    @pl.when(ph > 0)
    def _():
        cur = (ph + 1) % 2
        xe = xe_ref[...]
        g = _dot(xe, wg_s[cur])
        u = _dot(xe, wu_s[cur])
        act = (g * jax.nn.sigmoid(g) * u).astype(BF16)
        ye_ref[...] = (_dot(act, wd_s[cur]) * wsl_ref[:, 0:1]).astype(BF16)


def _experts(xe, wsl, wg, wu, wd, layer):
    rows = BATCH * CAPACITY
    tm = TILE["expert"]
    nt = rows // tm
    last = N_EXPERTS - 1
    wspec = lambda a: pl.BlockSpec((None, None) + a.shape[2:],
                                   lambda ph, i: (layer, jnp.minimum(ph, last), 0, 0))
    data = lambda n: pl.BlockSpec(
        (None, tm, n), lambda ph, i: (jnp.maximum(ph - 1, 0), jnp.where(ph > 0, i, 0), 0))
    return pl.pallas_call(
        functools.partial(_expert_kernel, span=D_MODEL // nt),
        grid=(N_EXPERTS + 1, nt),
        in_specs=[data(D_MODEL), data(LANES), wspec(wg), wspec(wu), wspec(wd)],
        out_specs=data(D_MODEL),
        out_shape=jax.ShapeDtypeStruct((N_EXPERTS, rows, D_MODEL), BF16),
        scratch_shapes=[pltpu.VMEM((2, D_MODEL, D_FF), BF16), pltpu.VMEM((2, D_MODEL, D_FF), BF16),
                        pltpu.VMEM((2, D_FF, D_MODEL), BF16)],
        compiler_params=_params("arbitrary", "arbitrary"),
        name="expert_ffn",
    )(xe.reshape(N_EXPERTS, rows, D_MODEL), wsl.reshape(N_EXPERTS, rows, LANES), wg, wu, wd)


def _combine_kernel(x_ref, ye_ref, slot_ref, p_ref, g3_ref, wpg_ref, wpp_ref,
                    gf_ref, out_ref, *, final_norm):
    slot = slot_ref[...]
    c = lax.broadcasted_iota(jnp.int32, (slot.shape[0], CAPACITY), 1)
    onehot = jnp.concatenate(
        [jnp.where(slot[:, e:e + 1] == c, 1.0, 0.0).astype(BF16) for e in range(N_EXPERTS)], axis=1)
    ye = ye_ref[...].reshape(N_EXPERTS * CAPACITY, D_MODEL)
    acc = x_ref[...] + _dot(onehot, ye)
    h = _rms(acc, g3_ref[...]).astype(BF16)
    gate = jax.nn.sigmoid(_dot(h, wpg_ref[...]))
    y = acc + gate * _dot(p_ref[...].astype(BF16), wpp_ref[...])
    if final_norm:
        y = _rms(y, gf_ref[...])
    out_ref[...] = y


def _combine(x, ye, slot_col, p, layer, g3, wpg, wpp, gf, final_norm):
    tm = TILE["combine"]
    nt = SEQ // tm
    p0 = layer * (TOKENS // tm)
    return pl.pallas_call(
        functools.partial(_combine_kernel, final_norm=final_norm),
        grid=(BATCH, nt),
        in_specs=[pl.BlockSpec((tm, D_MODEL), lambda b, i: (b * nt + i, 0)),
                  pl.BlockSpec((N_EXPERTS, None, CAPACITY, D_MODEL), lambda b, i: (0, b, 0, 0)),
                  pl.BlockSpec((None, tm, LANES), lambda b, i: (b, i, 0)),
                  pl.BlockSpec((tm, PLE_DIM), lambda b, i: (p0 + b * nt + i, 0)),
                  _const_spec(g3.shape), _const_spec(wpg.shape), _const_spec(wpp.shape),
                  _const_spec(gf.shape)],
        out_specs=pl.BlockSpec((tm, D_MODEL), lambda b, i: (b * nt + i, 0)),
        out_shape=jax.ShapeDtypeStruct((TOKENS, D_MODEL), F32),
        compiler_params=_params("parallel", "parallel"),
        name="combine",
    )(x, ye.reshape(N_EXPERTS, BATCH, CAPACITY, D_MODEL), slot_col, p, g3, wpg, wpp, gf)


DFT_FINE = 64
DFT_STEP = 4


def _dft_kernel(ca_ref, sa_ref, cb_ref, sb_ref, out_ref):
    cb = cb_ref[...]
    sb = sb_ref[...]
    for r in range(DFT_STEP):
        rows = slice(r * DFT_FINE, (r + 1) * DFT_FINE)
        out_ref[rows, :] = (ca_ref[r] * cb - sa_ref[r] * sb).astype(BF16)


def _dft_tables(n_points, nres):
    blk = SEQ // nres
    t = jnp.arange(SEQ, dtype=jnp.int32).reshape(blk, nres).T
    t = jnp.concatenate([t, t], axis=1).reshape(1, 2 * SEQ)
    is_sin = (jnp.arange(2 * SEQ) // blk % 2 == 1)[None, :]
    coarse = jnp.arange(HALF // DFT_FINE, dtype=jnp.int32)[:, None] * DFT_FINE
    fine = jnp.arange(DFT_FINE, dtype=jnp.int32)[:, None]
    ang = lambda f: ((f * t) % n_points).astype(F32) * (2.0 * math.pi / n_points)
    ca = jnp.cos(ang(coarse))[:, None, :]
    sa = jnp.sin(ang(coarse))[:, None, :]
    cb = jnp.where(is_sin, jnp.sin(ang(fine)), jnp.cos(ang(fine)))
    sb = jnp.where(is_sin, -jnp.cos(ang(fine)), jnp.sin(ang(fine)))
    rows = DFT_STEP * DFT_FINE
    return pl.pallas_call(
        _dft_kernel,
        grid=(HALF // rows,),
        in_specs=[pl.BlockSpec((DFT_STEP, 1, 2 * SEQ), lambda i: (i, 0, 0)),
                  pl.BlockSpec((DFT_STEP, 1, 2 * SEQ), lambda i: (i, 0, 0)),
                  _const_spec(cb.shape), _const_spec(sb.shape)],
        out_specs=pl.BlockSpec((rows, 2 * SEQ), lambda i: (i, 0)),
        out_shape=jax.ShapeDtypeStruct((HALF, 2 * SEQ), BF16),
        compiler_params=_params("parallel"),
        name="dft_tables",
    )(ca, sa, cb, sb)


def _fnet_group_tables():
    gc = D_MIX // FN_GROUPS
    i = lax.broadcasted_iota(jnp.int32, (D_MIX, D_MIX), 0)
    j = lax.broadcasted_iota(jnp.int32, (D_MIX, D_MIX), 1)
    same = (i // gc) == (j // gc)
    ang = (((i % gc) * (j % gc)) % gc).astype(F32) * (2.0 * math.pi / gc)
    return (jnp.where(same, jnp.cos(ang), 0.0).astype(BF16),
            jnp.where(same, jnp.sin(ang), 0.0).astype(BF16))


def _hyena_features():
    t01 = jnp.linspace(0.0, 1.0, SEQ, dtype=F32)[:, None]
    bands = jnp.linspace(1e-4, HY_BANDS - 1, HY_BANDS, dtype=F32)
    ang = 2.0 * math.pi * jnp.arange(SEQ, dtype=F32)[:, None] * bands / SEQ
    z = jnp.concatenate([t01, jnp.cos(ang), -jnp.sin(ang)], axis=-1)
    z = jnp.pad(z, ((0, 0), (0, LANES - HY_EMB)))
    max_decay = math.log(HY_TARGET) / HY_FAST_DECAY
    min_decay = math.log(HY_TARGET) / HY_SLOW_DECAY
    deltas = jnp.linspace(min_decay, max_decay, D_MIX, dtype=F32)
    window = jnp.exp(-t01 * jnp.abs(deltas))
    return z, window


def _rot_cols(w):
    half = w.shape[-1] // 2
    return jnp.concatenate([-w[..., half:], w[..., :half]], axis=-1)


def _rope_tables():
    inv = ROPE_THETA ** (-jnp.arange(0, MLA_ROPE, 2, dtype=F32) / MLA_ROPE)
    ang = jnp.arange(SEQ, dtype=F32)[:, None] * inv
    cos = jnp.concatenate([jnp.cos(ang), jnp.cos(ang)], axis=-1)
    sin = jnp.concatenate([jnp.sin(ang), jnp.sin(ang)], axis=-1)
    scale = (MLA_NOPE + MLA_ROPE) ** -0.5 * LOG2E
    pad = MLA_HEAD_PAD - MLA_NOPE - MLA_ROPE
    one = jnp.ones((SEQ, MLA_NOPE), F32)
    zero = jnp.zeros((SEQ, MLA_NOPE), F32)
    zpad = jnp.zeros((SEQ, pad), F32)
    cosq = jnp.tile(jnp.concatenate([one, cos, zpad], axis=-1) * scale, (1, MLA_HEADS))
    sinq = jnp.tile(jnp.concatenate([zero, sin, zpad], axis=-1) * scale, (1, MLA_HEADS))
    csk = jnp.concatenate([cos, sin, jnp.zeros((SEQ, LANES - 2 * MLA_ROPE), F32)], axis=-1)
    return cosq, sinq, csk


def _mla_weights(w_uq, w_ukv):
    pad = MLA_HEAD_PAD - MLA_NOPE - MLA_ROPE
    wq = w_uq.reshape(MLA_Q_RANK, MLA_HEADS, MLA_NOPE + MLA_ROPE)
    nope, pe = wq[..., :MLA_NOPE], wq[..., MLA_NOPE:]
    zp = jnp.zeros((MLA_Q_RANK, MLA_HEADS, pad), F32)
    wqa = jnp.concatenate([nope, pe, zp], axis=-1).reshape(MLA_Q_RANK, MLA_QK)
    wqb = jnp.concatenate([jnp.zeros_like(nope), _rot_cols(pe), zp], axis=-1)
    wqb = wqb.reshape(MLA_Q_RANK, MLA_QK)
    wkv = w_ukv.reshape(MLA_KV_RANK, MLA_HEADS, MLA_NOPE + MLA_V)
    knope, v = wkv[..., :MLA_NOPE], wkv[..., MLA_NOPE:]
    wk = jnp.concatenate(
        [knope, jnp.zeros((MLA_KV_RANK, MLA_HEADS, MLA_HEAD_PAD - MLA_NOPE), F32)], axis=-1)
    wk = wk.reshape(MLA_KV_RANK, MLA_QK)
    wv = v.reshape(MLA_KV_RANK, MLA_HEADS * MLA_V)
    r = lax.broadcasted_iota(jnp.int32, (LANES, MLA_QK), 0)
    c = lax.broadcasted_iota(jnp.int32, (LANES, MLA_QK), 1)
    epe = jnp.where((r < 2 * MLA_ROPE) & (c % MLA_HEAD_PAD == MLA_NOPE + r % MLA_ROPE), 1.0, 0.0)
    return (wqa.astype(BF16), wqb.astype(BF16), wk.astype(BF16), wv.astype(BF16),
            epe.astype(BF16))


def kernel(x, p, norm1_g, w_in, b_gate, hy_conv_w, hy_conv_b, hf_w1, hf_b1, hf_freq, hf_w2,
           hf_b2, hf_w3, hy_skip, q_norm_g, w_uq, kv_norm_g, w_ukv, rpb, w_br, w_out, norm2_g,
           w_router, w_e_gate, w_e_up, w_e_down, norm3_g, w_ple_gate, w_ple_proj, final_g):
    conv_tab = _dft_tables(2 * SEQ, 4)
    fnet_tab = _dft_tables(SEQ, 2)
    fnet_cg, fnet_sg = _fnet_group_tables()
    zfeat, window = _hyena_features()
    cosq, sinq, csk = _rope_tables()
    row = lambda a: a.reshape(1, -1)

    xt = x.reshape(TOKENS, D_MODEL)
    for i in range(DEPTH):
        g1 = row(norm1_g[i])
        wa, wna, wgate = _inproj_weights(w_in, i)
        wqa, wqb, wk, wv, epe = _mla_weights(w_uq[i], w_ukv[i])

        u_hy, u_fn, q, k, v, naq, nak, nav = _inproj(
            xt, g1, wa, wna, row(q_norm_g[i]), wqa, wqb, row(kv_norm_g[i]), wk, wv, epe,
            cosq, sinq, csk)

        w1 = jnp.pad(hf_w1[i], ((0, LANES - HY_EMB), (0, 0)))
        kf, kny = _hyena_filter(zfeat, window, w1, row(hf_b1[i]), hf_freq[i], hf_w2[i],
                                row(hf_b2[i]), hf_w3[i], conv_tab)
        conv_b = row(hy_conv_b[i])
        z1 = _hyena_stage(u_hy, 0, u_hy, 2, hy_conv_w[i], conv_b, conv_tab, kf[0],
                          kny[0], row(hy_skip[i, 0]), True, F32)
        y_hy = _hyena_stage(u_hy, 1, z1, 0, hy_conv_w[i], conv_b, conv_tab, kf[1],
                            kny[1], row(hy_skip[i, 1]), False, BF16)
        y_fn = _fnet(u_fn, fnet_tab, fnet_cg, fnet_sg)
        y_mla = _mla(q, k, v)
        y_na = _neighborhood(naq, nak, nav, _na_bias_tiles(rpb[i]))
        xt = _merge(xt, g1, y_hy, y_fn, y_mla, y_na, wgate, row(b_gate[i]),
                    w_br[i].astype(BF16), w_out[i].astype(BF16))

        wr_t = jnp.pad(w_router[i].T, ((0, LANES - N_EXPERTS), (0, 0)))
        hb, logits = _router(xt, row(norm2_g[i]), wr_t)
        slot_row, slot_col, w_row = _select(logits)
        xe, wsl = _gather(slot_row, w_row, hb)
        ye = _experts(xe, wsl, w_e_gate, w_e_up, w_e_down, i)
        xt = _combine(xt, ye, slot_col, p.reshape(DEPTH * TOKENS, PLE_DIM), i,
                      row(norm3_g[i]), w_ple_gate[i].astype(BF16),
                      w_ple_proj[i].astype(BF16), row(final_g), i == DEPTH - 1)
    return xt.reshape(BATCH, SEQ, D_MODEL)
```

```python
import functools
import math

import jax
import jax.numpy as jnp
from jax import lax
from jax.experimental import pallas as pl
from jax.experimental.pallas import tpu as pltpu

F32 = jnp.float32
BF16 = jnp.bfloat16

D_MODEL = 1024
BATCH = 8
SEQ = 2048
DEPTH = 2
TOKENS = BATCH * SEQ

GRID_W = 64
GRID_R = SEQ // GRID_W
D_MIX = 256
N_BRANCH = 4
EPS = 1e-6
HY_ORDER = 2
HY_BANDS = 16
HY_EMB = 2 * HY_BANDS + 1
HY_FFN = 64
HY_TARGET = 1e-2
HY_FAST_DECAY = 0.3
HY_SLOW_DECAY = 1.5
FN_GROUPS = 4
MLA_HEADS = 4
MLA_NOPE = 64
MLA_ROPE = 32
MLA_V = 64
MLA_Q_RANK = 256
MLA_KV_RANK = 128
ROPE_THETA = 10000.0
NA_HEADS = 4
NA_HEAD_DIM = D_MIX // NA_HEADS
NA_WIN_R = 8
NA_WIN_C = 16
N_EXPERTS = 16
CAPACITY = 2 * SEQ // N_EXPERTS
D_FF = 1024
PLE_DIM = 256

HY_COLS = 3 * D_MIX
OFF_FN = HY_COLS
OFF_CQ = OFF_FN + D_MIX
OFF_CKV = OFF_CQ + MLA_Q_RANK
OFF_KPE = OFF_CKV + MLA_KV_RANK
OFF_NA = OFF_KPE + MLA_ROPE
OFF_GATE = OFF_NA + 3 * D_MIX

LANES = 128
MLA_HEAD_PAD = 128
MLA_QK = MLA_HEADS * MLA_HEAD_PAD
WA_COLS = 1536
NEG_BIG = -1e30
LOG2E = math.log2(math.e)

TILE = dict(inproj=512, mla=512, merge=512, expert=512, combine=512)
NA_QROWS = 8
NA_KROWS = 16
NA_QBLK = NA_QROWS * GRID_W
NA_KBLK = NA_KROWS * GRID_W
VMEM_LIMIT = 56 * 1024 * 1024


def _params(*sem):
    return pltpu.CompilerParams(dimension_semantics=sem, vmem_limit_bytes=VMEM_LIMIT)


def _const_spec(shape):
    nd = len(shape)
    return pl.BlockSpec(shape, lambda *_: (0,) * nd, pipeline_mode=pl.Buffered(1))


def _rms(x, g):
    return x * lax.rsqrt(jnp.mean(x * x, axis=-1, keepdims=True) + EPS) * g


def _dot(a, b):
    return jnp.dot(a, b, preferred_element_type=F32)


def _dot_nt(a, b):
    return lax.dot_general(a, b, (((1,), (1,)), ((), ())), preferred_element_type=F32)


def _split2(x):
    hi = x.astype(BF16)
    lo = (x - hi.astype(F32)).astype(BF16)
    return hi, lo


def _inproj_kernel(x_ref, g1_ref, wa_ref, wna_ref, qg_ref, wqa_ref, wqb_ref, kvg_ref,
                   wk_ref, wv_ref, epe_ref, cosq_ref, sinq_ref, csk_ref,
                   uhy_ref, ufn_ref, q_ref, k_ref, v_ref, naq_ref, nak_ref, nav_ref):
    h = _rms(x_ref[...], g1_ref[...]).astype(BF16)
    ua = _dot(h, wa_ref[...])
    uhy_ref[...] = ua[:, :HY_COLS]
    ufn_ref[...] = ua[:, OFF_FN:OFF_CQ]
    cqn = _rms(ua[:, OFF_CQ:OFF_CKV], qg_ref[...]).astype(BF16)
    q = _dot(cqn, wqa_ref[...]) * cosq_ref[...] + _dot(cqn, wqb_ref[...]) * sinq_ref[...]
    q_ref[...] = q.astype(BF16)
    kvn = _rms(ua[:, OFF_CKV:OFF_KPE], kvg_ref[...]).astype(BF16)
    kpe = ua[:, OFF_KPE:WA_COLS] * csk_ref[...]
    k = _dot(kvn, wk_ref[...]) + _dot(kpe.astype(BF16), epe_ref[...])
    k_ref[...] = k.astype(BF16)
    v_ref[...] = _dot(kvn, wv_ref[...]).astype(BF16)
    una = _dot(h, wna_ref[...])
    naq_ref[...] = (una[:, :D_MIX] * (NA_HEAD_DIM ** -0.5 * LOG2E)).astype(BF16)
    nak_ref[...] = una[:, D_MIX:2 * D_MIX].astype(BF16)
    nav_ref[...] = una[:, 2 * D_MIX:].astype(BF16)


def _inproj(x, g1, wa, wna, qg, wqa, wqb, kvg, wk, wv, epe, cosq, sinq, csk):
    tm = TILE["inproj"]
    nt = SEQ // tm
    row = lambda n: pl.BlockSpec((tm, n), lambda i: (i, 0))
    pos = lambda n: pl.BlockSpec((tm, n), lambda i: (i % nt, 0))
    outs = [(HY_COLS, F32), (D_MIX, F32), (MLA_QK, BF16), (MLA_QK, BF16), (D_MIX, BF16),
            (D_MIX, BF16), (D_MIX, BF16), (D_MIX, BF16)]
    return pl.pallas_call(
        _inproj_kernel,
        grid=(TOKENS // tm,),
        in_specs=[row(D_MODEL), _const_spec(g1.shape), _const_spec(wa.shape),
                  _const_spec(wna.shape), _const_spec(qg.shape), _const_spec(wqa.shape),
                  _const_spec(wqb.shape), _const_spec(kvg.shape), _const_spec(wk.shape),
                  _const_spec(wv.shape), _const_spec(epe.shape),
                  pos(MLA_QK), pos(MLA_QK), pos(LANES)],
        out_specs=[row(n) for n, _ in outs],
        out_shape=[jax.ShapeDtypeStruct((TOKENS, n), dt) for n, dt in outs],
        compiler_params=_params("parallel"),
        name="inproj",
    )(x, g1, wa, wna, qg, wqa, wqb, kvg, wk, wv, epe, cosq, sinq, csk)


WPREP_ROWS = 128


def _inproj_weights_kernel(w_ref, wa_ref, wna_ref, wg_ref):
    half = MLA_ROPE // 2
    wa_ref[:, :OFF_KPE] = w_ref[:, :OFF_KPE].astype(BF16)
    kpe_group = jnp.concatenate(
        [w_ref[:, OFF_KPE:OFF_NA], -w_ref[:, OFF_KPE + half:OFF_NA],
         w_ref[:, OFF_KPE:OFF_KPE + half],
         jnp.zeros((WPREP_ROWS, WA_COLS - OFF_NA - MLA_ROPE), F32)], axis=1)
    wa_ref[:, OFF_KPE:] = kpe_group.astype(BF16)
    wna_ref[...] = w_ref[:, OFF_NA:OFF_GATE].astype(BF16)
    wg_ref[...] = w_ref[:, OFF_GATE:].astype(BF16)


def _inproj_weights(w_in, layer):
    cols = (WA_COLS, OFF_GATE - OFF_NA, w_in.shape[-1] - OFF_GATE)
    return pl.pallas_call(
        _inproj_weights_kernel,
        grid=(D_MODEL // WPREP_ROWS,),
        in_specs=[pl.BlockSpec((None, WPREP_ROWS, w_in.shape[-1]), lambda r: (layer, r, 0))],
        out_specs=[pl.BlockSpec((WPREP_ROWS, n), lambda r: (r, 0)) for n in cols],
        out_shape=[jax.ShapeDtypeStruct((D_MODEL, n), BF16) for n in cols],
        compiler_params=_params("parallel"),
        name="inproj_weights",
    )(w_in)


HALF = SEQ // 2
HY_SEQS = 1


def _residue_rows(ref, r, nres):
    rows = pl.ds(r, SEQ // nres, stride=nres)
    if isinstance(ref, tuple):
        return jnp.concatenate([h[rows, :] for h in ref], axis=1)
    return jnp.concatenate([ref[j, rows, :] for j in range(ref.shape[0])], axis=1)


def _store_slabs(ref, value, rows=slice(None)):
    for j in range(ref.shape[0]):
        ref[j, rows, :] = value[:, j * LANES:(j + 1) * LANES]


def _table_cols(cs_ref, r, nres, part=None):
    blk = SEQ // nres
    lo = 2 * blk * r
    if part is None:
        return cs_ref[:, lo:lo + 2 * blk]
    return cs_ref[:, lo + part * blk:lo + (part + 1) * blk]


def _dft_fwd(cs_ref, src_ref, want_cos=True, want_sin=True, split=False):
    def prod(part, z, r):
        tab = _table_cols(cs_ref, r, 4, part)
        if split:
            hi, lo = _split2(z)
            return _dot(tab, hi) + _dot(tab, lo)
        return _dot(tab, z.astype(BF16))

    pc, ps = [None] * 4, [None] * 4
    for r in range(4):
        z = _residue_rows(src_ref, r, 4)
        odd = r % 2 == 1
        if want_cos or odd:
            pc[r] = prod(0, z, r)
        if want_sin or odd:
            ps[r] = prod(1, z, r)
    a = b = None
    if want_cos:
        a = ((pc[0] + pc[2]) + (pc[1] + pc[3]), (pc[0] - pc[2]) + (ps[3] - ps[1]))
    if want_sin:
        b = ((ps[0] + ps[2]) + (ps[1] + ps[3]), (ps[0] - ps[2]) + (pc[1] - pc[3]))
    return a, b


def _dft_inv(cs_ref, yre_ref, yim_ref):
    g, h = [], {}
    for r in range(4):
        yr = _residue_rows(yre_ref, r, 4).astype(BF16)
        yi = _residue_rows(yim_ref, r, 4).astype(BF16)
        tab = _table_cols(cs_ref, r, 4)
        g.append(_dot(tab, jnp.concatenate([yr, -yi], axis=0)))
        if r % 2 == 1:
            h[r] = _dot(tab, jnp.concatenate([yi, yr], axis=0))
    return (g[0] + g[2]) + (g[1] + g[3]), (g[0] - g[2]) + (h[3] - h[1])


def _hyena_filter_kernel(z_ref, win_ref, w1_ref, b1_ref, freq_ref, w2_ref, b2_ref, w3_ref,
                         cs_ref, kf_ref, kny_ref, ksum_s, kdif_s):
    hp = lax.Precision.HIGHEST
    freq = freq_ref[...]
    hf = jnp.sin(freq[0:1] * (jnp.dot(z_ref[...], w1_ref[...], precision=hp,
                                      preferred_element_type=F32) + b1_ref[...]))
    hf = jnp.sin(freq[1:2] * (jnp.dot(hf, w2_ref[...], precision=hp,
                                      preferred_element_type=F32) + b2_ref[...]))
    hf = jnp.dot(hf, w3_ref[...], precision=hp, preferred_element_type=F32)
    win = win_ref[...]
    t = lax.broadcasted_iota(jnp.int32, (SEQ, D_MIX), 0)
    sgn = (1 - 2 * (t & 1)).astype(F32)
    fwd = hf[:, :D_MIX] * win
    bwd = jnp.where(t == 0, 0.0, hf[:, D_MIX:] * win)
    nrm = lax.rsqrt(jnp.sum(fwd * fwd + bwd * bwd, axis=0, keepdims=True) + EPS)
    ksum = (fwd + bwd) * nrm
    _store_slabs(ksum_s, ksum)
    _store_slabs(kdif_s, (bwd - fwd) * nrm)
    kre, _ = _dft_fwd(cs_ref, ksum_s, want_sin=False, split=True)
    _, kim = _dft_fwd(cs_ref, kdif_s, want_cos=False, split=True)
    wf = 2.0 / (2 * SEQ)
    for part, spec in enumerate((kre, kim)):
        lo = spec[0] * wf
        kf_ref[part, :HALF, :] = lo
        kf_ref[part, 0:1, :] = lo[0:1] * 0.5
        kf_ref[part, HALF:, :] = spec[1] * wf
    kny = jnp.sum(ksum * sgn, axis=0, keepdims=True) * (1.0 / (2 * SEQ))
    kny_ref[...] = jnp.broadcast_to(kny, (8, D_MIX))


def _hyena_filter(zfeat, window, w1, b1, freq, w2, b2, w3, table):
    consts = (zfeat, window, w1, b1, freq, w2, b2)
    return pl.pallas_call(
        _hyena_filter_kernel,
        grid=(HY_ORDER,),
        in_specs=[_const_spec(a.shape) for a in consts]
        + [pl.BlockSpec((HY_FFN, 2 * D_MIX), lambda o: (0, o)),
           _const_spec(table.shape)],
        out_specs=[pl.BlockSpec((None, 2, SEQ, D_MIX), lambda o: (o, 0, 0, 0)),
                   pl.BlockSpec((None, 8, D_MIX), lambda o: (o, 0, 0))],
        out_shape=[jax.ShapeDtypeStruct((HY_ORDER, 2, SEQ, D_MIX), F32),
                   jax.ShapeDtypeStruct((HY_ORDER, 8, D_MIX), F32)],
        scratch_shapes=[pltpu.VMEM((D_MIX // LANES, SEQ, LANES), F32)] * 2,
        compiler_params=_params("arbitrary"),
        name="hyena_filter",
    )(*consts, w3, table)


def _short_conv(u, w, b):
    t = lax.broadcasted_iota(jnp.int32, u.shape, 0)
    prev = jnp.where(t == 0, 0.0, pltpu.roll(u, 1, 0))
    nxt = jnp.where(t == SEQ - 1, 0.0, pltpu.roll(u, SEQ - 1, 0))
    return prev * w[0:1] + u * w[1:2] + nxt * w[2:3] + b


def _hyena_stage_kernel(gate_ref, src_ref, wg_ref, bg_ref, ws_ref, bs_ref, cs_ref,
                        kf_ref, kny_ref, skip_ref, out_ref, z_s, yre_s, yim_s, gate_s, rest_s,
                        *, conv_src):
    for q in range(HY_SEQS):
        seq = slice(q * SEQ, (q + 1) * SEQ)
        zq, yre_q, yim_q = z_s.at[q], yre_s.at[q], yim_s.at[q]
        z = src_ref[seq, :]
        if conv_src:
            z = _short_conv(z, ws_ref[...], bs_ref[...])
        _store_slabs(zq, z)
        t = lax.broadcasted_iota(jnp.int32, z.shape, 0)
        sgn = (1 - 2 * (t & 1)).astype(F32)
        nyq = jnp.sum(z * sgn, axis=0, keepdims=True) * kny_ref[0:1]
        gate = _short_conv(gate_ref[seq, :], wg_ref[...], bg_ref[...])
        gate_s[seq, :] = gate
        rest_s[seq, :] = gate * (sgn * nyq + z * skip_ref[...])
        a, b = _dft_fwd(cs_ref, zq)
        for half in range(2):
            rows = slice(half * HALF, (half + 1) * HALF)
            kre = kf_ref[0, rows, :]
            kim = kf_ref[1, rows, :]
            _store_slabs(yre_q, a[half] * kre + b[half] * kim, rows)
            _store_slabs(yim_q, a[half] * kim - b[half] * kre, rows)
        for half, y in enumerate(_dft_inv(cs_ref, yre_q, yim_q)):
            rows = slice(q * SEQ + half * HALF, q * SEQ + (half + 1) * HALF)
            out_ref[rows, :] = (gate_s[rows, :] * y + rest_s[rows, :]).astype(out_ref.dtype)


def _hyena_stage(u_hy, gate_blk, src, src_blk, conv_w, conv_b, table, kf, kny, skip,
                 conv_src, out_dtype):
    rows = HY_SEQS * SEQ
    col = lambda blk: pl.BlockSpec((rows, D_MIX), lambda b: (b, blk))
    wcol = lambda blk, r: pl.BlockSpec((r, D_MIX), lambda b: (0, blk))
    ws_blk = src_blk if conv_src else 0
    return pl.pallas_call(
        functools.partial(_hyena_stage_kernel, conv_src=conv_src),
        grid=(BATCH // HY_SEQS,),
        in_specs=[col(gate_blk), col(src_blk), wcol(gate_blk, 3), wcol(gate_blk, 1),
                  wcol(ws_blk, 3), wcol(ws_blk, 1), _const_spec(table.shape),
                  _const_spec(kf.shape), _const_spec(kny.shape),
                  _const_spec(skip.shape)],
        out_specs=pl.BlockSpec((rows, D_MIX), lambda b: (b, 0)),
        out_shape=jax.ShapeDtypeStruct((TOKENS, D_MIX), out_dtype),
        scratch_shapes=[pltpu.VMEM((HY_SEQS, D_MIX // LANES, SEQ, LANES), F32)] * 3
        + [pltpu.VMEM((rows, D_MIX), F32)] * 2,
        compiler_params=_params("parallel"),
        name="hyena_stage",
    )(u_hy, src, conv_w, conv_b, conv_w, conv_b, table, kf, kny, skip)


def _fnet_kernel(xa_ref, xb_ref, cs_ref, cg_ref, sg_ref, out_ref):
    parts = []
    for r in range(2):
        xb = _residue_rows((xa_ref, xb_ref), r, 2).astype(BF16)
        xc = _dot(xb, cg_ref[...]).astype(BF16)
        xs = _dot(xb, sg_ref[...]).astype(BF16)
        parts.append(_dot(_table_cols(cs_ref, r, 2), jnp.concatenate([xc, -xs], axis=0)))
    scale = (SEQ * D_MIX // FN_GROUPS) ** -0.5
    out_ref[:HALF, :] = ((parts[0] + parts[1]) * scale).astype(out_ref.dtype)
    out_ref[HALF:, :] = ((parts[0] - parts[1]) * scale).astype(out_ref.dtype)


def _fnet(u_fn, table, cg, sg):
    return pl.pallas_call(
        _fnet_kernel,
        grid=(BATCH,),
        in_specs=[pl.BlockSpec((SEQ, LANES), lambda b: (b, 0)),
                  pl.BlockSpec((SEQ, LANES), lambda b: (b, 1)), _const_spec(table.shape),
                  _const_spec(cg.shape), _const_spec(sg.shape)],
        out_specs=pl.BlockSpec((SEQ, D_MIX), lambda b: (b, 0)),
        out_shape=jax.ShapeDtypeStruct((TOKENS, D_MIX), BF16),
        compiler_params=_params("parallel"),
        name="fnet",
    )(u_fn, u_fn, table, cg, sg)


def _softmax2_pv(s2, v):
    m = jnp.max(s2, axis=-1, keepdims=True)
    p = jnp.exp2(s2 - m)
    l = jnp.sum(p, axis=-1, keepdims=True)
    return _dot(p.astype(BF16), v) / l


def _mla_kernel(q_ref, k_ref, v_ref, out_ref):
    v = v_ref[...]
    head = lax.broadcasted_iota(jnp.int32, out_ref.shape, 1) // MLA_V
    acc = jnp.zeros(out_ref.shape, F32)
    for h in range(MLA_HEADS):
        sl = slice(h * MLA_HEAD_PAD, (h + 1) * MLA_HEAD_PAD)
        s2 = _dot_nt(q_ref[:, sl], k_ref[:, sl])
        acc = jnp.where(head == h, _softmax2_pv(s2, v), acc)
    out_ref[...] = acc.astype(out_ref.dtype)


def _mla(q, k, v):
    tm = TILE["mla"]
    nt = SEQ // tm
    return pl.pallas_call(
        _mla_kernel,
        grid=(BATCH, nt),
        in_specs=[pl.BlockSpec((tm, MLA_QK), lambda b, i: (b * nt + i, 0)),
                  pl.BlockSpec((SEQ, MLA_QK), lambda b, i: (b, 0)),
                  pl.BlockSpec((SEQ, D_MIX), lambda b, i: (b, 0))],
        out_specs=pl.BlockSpec((tm, D_MIX), lambda b, i: (b * nt + i, 0)),
        out_shape=jax.ShapeDtypeStruct((TOKENS, D_MIX), BF16),
        compiler_params=_params("parallel", "parallel"),
        name="mla_attention",
    )(q, k, v)


def _na_key_row0(j):
    return jnp.clip(j * NA_QROWS - NA_WIN_R // 2, 0, GRID_R - NA_KROWS)


NA_PAIRS = 2 * NA_WIN_R


def _na_kernel(q_ref, k_ref, v_ref, tile_ref, out_ref, bias_s):
    j = pl.program_id(0)
    krow0 = _na_key_row0(j)

    @pl.when(pl.program_id(1) == 0)
    def _():
        rq = j * NA_QROWS + lax.broadcasted_iota(jnp.int32, (NA_QBLK, NA_KBLK), 0) // GRID_W
        rk = krow0 + lax.broadcasted_iota(jnp.int32, (NA_QBLK, NA_KBLK), 1) // GRID_W
        rs = jnp.clip(rq - NA_WIN_R // 2, 0, GRID_R - NA_WIN_R)
        rowmask = jnp.where(rk < rs, NEG_BIG, jnp.where(rk >= rs + NA_WIN_R, NEG_BIG, 0.0))
        base = krow0 - j * NA_QROWS + NA_WIN_R
        for h in range(NA_HEADS):
            bias = jnp.concatenate(
                [jnp.concatenate(
                    [tile_ref[h, jnp.clip(base + 2 * kp - r, 0, NA_PAIRS - 1)]
                     for kp in range(NA_KROWS // 2)], axis=1)
                 for r in range(NA_QROWS)], axis=0)
            bias_s[h] = bias + rowmask

    off = pl.multiple_of(krow0 * GRID_W, GRID_W)
    q = q_ref[...]
    k = k_ref[pl.ds(off, NA_KBLK), :]
    v = v_ref[pl.ds(off, NA_KBLK), :]
    head = lax.broadcasted_iota(jnp.int32, (NA_QBLK, D_MIX), 1) // NA_HEAD_DIM
    acc = jnp.zeros((NA_QBLK, D_MIX), F32)
    for h in range(NA_HEADS):
        qh = jnp.where(head == h, q, jnp.zeros_like(q))
        s2 = _dot_nt(qh, k) + bias_s[h]
        acc = jnp.where(head == h, _softmax2_pv(s2, v), acc)
    out_ref[...] = acc.astype(out_ref.dtype)


def _neighborhood(q, k, v, tiles):
    nj = SEQ // NA_QBLK
    return pl.pallas_call(
        _na_kernel,
        grid=(nj, BATCH),
        in_specs=[pl.BlockSpec((NA_QBLK, D_MIX), lambda j, b: (b * nj + j, 0)),
                  pl.BlockSpec((SEQ, D_MIX), lambda j, b: (b, 0)),
                  pl.BlockSpec((SEQ, D_MIX), lambda j, b: (b, 0)),
                  _const_spec(tiles.shape)],
        out_specs=pl.BlockSpec((NA_QBLK, D_MIX), lambda j, b: (b * nj + j, 0)),
        out_shape=jax.ShapeDtypeStruct((TOKENS, D_MIX), BF16),
        scratch_shapes=[pltpu.VMEM((NA_HEADS, NA_QBLK, NA_KBLK), F32)],
        compiler_params=_params("parallel", "arbitrary"),
        name="neighborhood_attention",
    )(q, k, v, tiles)


def _na_bias_tiles(rpb):
    c = jnp.arange(GRID_W)
    cs = jnp.clip(c - NA_WIN_C // 2, 0, GRID_W - NA_WIN_C)
    col_ok = (c[None, :] >= cs[:, None]) & (c[None, :] < cs[:, None] + NA_WIN_C)
    dc = jnp.clip(c[None, :] - c[:, None] + (NA_WIN_C - 1), 0, 2 * NA_WIN_C - 2)
    pick = (dc[None] == jnp.arange(2 * NA_WIN_C - 1)[:, None, None]).astype(F32)
    t = jnp.einsum('hrd,dqk->hrqk', rpb.astype(F32), pick, precision=lax.Precision.HIGHEST)
    t = jnp.where(col_ok, t * LOG2E, NEG_BIG)
    t = jnp.pad(t, ((0, 0), (1, 1), (0, 0), (0, 0)))
    return jnp.concatenate([t[:, :-1], t[:, 1:]], axis=-1)


def _merge_kernel(x_ref, g1_ref, yhy_ref, yfn_ref, ymla_ref, yna_ref, wg_ref, bg_ref,
                  wbr_ref, wout_ref, g2_ref, wr_ref, out_ref, hb_ref, logit_ref):
    x = x_ref[...]
    h = _rms(x, g1_ref[...]).astype(BF16)
    merged = jnp.zeros(x.shape, F32)
    for n, y_ref in enumerate((yhy_ref, yfn_ref, ymla_ref, yna_ref)):
        sl = slice(n * D_MODEL, (n + 1) * D_MODEL)
        gate = jax.nn.sigmoid(_dot(h, wg_ref[:, sl]) + bg_ref[:, sl])
        merged = merged + gate * _dot(y_ref[...], wbr_ref[n])
    x1 = x + _dot(merged.astype(BF16), wout_ref[...])
    out_ref[...] = x1
    h_hi, h_lo = _split2(_rms(x1, g2_ref[...]))
    hb_ref[...] = h_hi
    w_hi, w_lo = _split2(wr_ref[...])
    logits = _dot_nt(w_hi, h_hi) + _dot_nt(w_hi, h_lo) + _dot_nt(w_lo, h_hi)
    logit_ref[...] = logits[:N_EXPERTS]


def _merge(x, g1, yhy, yfn, ymla, yna, wg, bg, wbr, wout, g2, wr_t):
    tm = TILE["merge"]
    nt = SEQ // tm
    row = lambda n: pl.BlockSpec((tm, n), lambda i: (i, 0))
    return pl.pallas_call(
        _merge_kernel,
        grid=(TOKENS // tm,),
        in_specs=[row(D_MODEL), _const_spec(g1.shape), row(D_MIX), row(D_MIX), row(D_MIX),
                  row(D_MIX), _const_spec(wg.shape), _const_spec(bg.shape),
                  _const_spec(wbr.shape), _const_spec(wout.shape), _const_spec(g2.shape),
                  _const_spec(wr_t.shape)],
        out_specs=[row(D_MODEL), row(D_MODEL),
                   pl.BlockSpec((None, N_EXPERTS, tm), lambda i: (i // nt, 0, i % nt))],
        out_shape=[jax.ShapeDtypeStruct((TOKENS, D_MODEL), F32),
                   jax.ShapeDtypeStruct((TOKENS, D_MODEL), BF16),
                   jax.ShapeDtypeStruct((BATCH, N_EXPERTS, SEQ), F32)],
        compiler_params=_params("parallel"),
        name="merge",
    )(x, g1, yhy, yfn, ymla, yna, wg, bg, wbr, wout, g2, wr_t)


def _prefix_count(m):
    r = lax.broadcasted_iota(jnp.int32, (LANES, LANES), 0)
    c = lax.broadcasted_iota(jnp.int32, (LANES, LANES), 1)
    upper = jnp.where(r < c, 1.0, 0.0).astype(BF16)
    run = jnp.zeros((m.shape[0], 1), F32)
    parts = []
    for i in range(SEQ // LANES):
        chunk = m[:, i * LANES:(i + 1) * LANES]
        parts.append(_dot(chunk.astype(BF16), upper) + run)
        run = run + jnp.sum(chunk, axis=1, keepdims=True)
    return jnp.concatenate(parts, axis=1)


SELECT_MAX_ITERS = 192


def _select_kernel(logit_ref, slot_row_ref, slot_col_ref, w_row_ref, start_ref, slot_s, w_s):
    b = pl.program_id(0)
    rows = BATCH * N_EXPERTS

    @pl.when(b == 0)
    def _():
        logits = logit_ref[...]
        ex = jnp.exp(logits - jnp.max(logits, axis=1, keepdims=True))
        aff = (ex / jnp.sum(ex, axis=1, keepdims=True)).reshape(rows, SEQ)

        def bisect(c):
            it, lo, hi, _ = c
            mid = 0.5 * (lo + hi)
            cnt = jnp.sum(jnp.where(aff >= mid, 1.0, 0.0), axis=1, keepdims=True)
            moving = jnp.where(mid == lo, 0.0, jnp.where(mid == hi, 0.0, 1.0))
            enough = cnt >= CAPACITY
            return (it + 1, jnp.where(enough, mid, lo), jnp.where(enough, hi, mid),
                    (jnp.max(moving) > 0).astype(jnp.int32))

        _, lo, hi, _ = lax.while_loop(
            lambda c: (c[0] < SELECT_MAX_ITERS) & (c[3] > 0), bisect,
            (jnp.int32(0), jnp.zeros((rows, 1), F32), jnp.full((rows, 1), 2.0, F32),
             jnp.int32(1)))
        above = jnp.where(aff >= hi, 1.0, 0.0)
        band = jnp.where(aff >= lo, 1.0, 0.0) - above
        need = CAPACITY - jnp.sum(above, axis=1, keepdims=True)
        sel = above + band * jnp.where(_prefix_count(band) < need, 1.0, 0.0)
        slot_s[...] = jnp.where(sel > 0, _prefix_count(sel), -1.0)
        w_s[...] = sel * aff

    r0 = pl.multiple_of(b * N_EXPERTS, N_EXPERTS)
    slot = slot_s[pl.ds(r0, N_EXPERTS), :]
    slot_row_ref[...] = slot.astype(jnp.int32)
    pad = jnp.full((LANES - N_EXPERTS, SEQ), -1.0, F32)
    slot_col_ref[...] = jnp.concatenate([slot, pad], axis=0).T.astype(jnp.int32)
    w_row_ref[...] = w_s[pl.ds(r0, N_EXPERTS), :]
    tm = TILE["combine"]
    token = lax.broadcasted_iota(jnp.int32, slot.shape, 1)
    lane = lax.broadcasted_iota(jnp.int32, (N_EXPERTS, LANES), 1)
    starts = jnp.where(lane == SEQ // tm, float(CAPACITY), 0.0)
    for i in range(1, SEQ // tm):
        before = jnp.where(slot >= 0, jnp.where(token < i * tm, 1.0, 0.0), 0.0)
        starts = starts + jnp.where(lane == i, jnp.sum(before, axis=1, keepdims=True), 0.0)
    start_ref[...] = starts.astype(jnp.int32)


def _select(logits):
    return pl.pallas_call(
        _select_kernel,
        grid=(BATCH,),
        in_specs=[_const_spec(logits.shape)],
        out_specs=[pl.BlockSpec((None, N_EXPERTS, SEQ), lambda b: (b, 0, 0)),
                   pl.BlockSpec((None, SEQ, LANES), lambda b: (b, 0, 0)),
                   pl.BlockSpec((None, N_EXPERTS, SEQ), lambda b: (b, 0, 0)),
                   pl.BlockSpec((None, N_EXPERTS, LANES), lambda b: (b, 0, 0))],
        out_shape=[jax.ShapeDtypeStruct((BATCH, N_EXPERTS, SEQ), jnp.int32),
                   jax.ShapeDtypeStruct((BATCH, SEQ, LANES), jnp.int32),
                   jax.ShapeDtypeStruct((BATCH, N_EXPERTS, SEQ), F32),
                   jax.ShapeDtypeStruct((BATCH, N_EXPERTS, LANES), jnp.int32)],
        scratch_shapes=[pltpu.VMEM((BATCH * N_EXPERTS, SEQ), F32),
                        pltpu.VMEM((BATCH * N_EXPERTS, SEQ), F32)],
        compiler_params=_params("arbitrary"),
        name="expert_select",
    )(logits)


GATHER_EXPERTS = 8

def _gather_kernel(slot_ref, w_ref, hb_ref, xe_ref, wsl_ref):
    ge = GATHER_EXPERTS
    c = lax.broadcasted_iota(jnp.int32, (CAPACITY, SEQ), 0)
    hits = [slot_ref[e:e + 1, :] == c for e in range(ge)]
    onehot = jnp.concatenate([jnp.where(m, 1.0, 0.0).astype(BF16) for m in hits], axis=0)
    rows = _dot(onehot, hb_ref[...]).astype(BF16)
    for e in range(ge):
        xe_ref[e] = rows[e * CAPACITY:(e + 1) * CAPACITY]
        wslot = jnp.sum(jnp.where(hits[e], w_ref[e:e + 1, :], 0.0), axis=1, keepdims=True)
        wsl_ref[e] = jnp.broadcast_to(wslot, (CAPACITY, LANES))


def _gather(slot_row, w_row, hb):
    ge = GATHER_EXPERTS
    return pl.pallas_call(
        _gather_kernel,
        grid=(BATCH, N_EXPERTS // ge),
        in_specs=[pl.BlockSpec((None, ge, SEQ), lambda b, g: (b, g, 0)),
                  pl.BlockSpec((None, ge, SEQ), lambda b, g: (b, g, 0)),
                  pl.BlockSpec((SEQ, D_MODEL), lambda b, g: (b, 0))],
        out_specs=[pl.BlockSpec((ge, None, CAPACITY, D_MODEL), lambda b, g: (g, b, 0, 0)),
                   pl.BlockSpec((ge, None, CAPACITY, LANES), lambda b, g: (g, b, 0, 0))],
        out_shape=[jax.ShapeDtypeStruct((N_EXPERTS, BATCH, CAPACITY, D_MODEL), BF16),
                   jax.ShapeDtypeStruct((N_EXPERTS, BATCH, CAPACITY, LANES), F32)],
        compiler_params=_params("parallel", "parallel"),
        name="expert_gather",
    )(slot_row, w_row, hb)


def _expert_kernel(xe_ref, wsl_ref, wg_ref, wu_ref, wd_ref, ye_ref, wg_s, wu_s, wd_s, *, span):
    ph = pl.program_id(0)
    i = pl.program_id(1)
    rows = pl.ds(pl.multiple_of(i * span, span), span)
    nxt = ph % 2
    wg_s[nxt, rows, :] = wg_ref[rows, :].astype(BF16)
    wu_s[nxt, rows, :] = wu_ref[rows, :].astype(BF16)
    wd_s[nxt, rows, :] = wd_ref[rows, :].astype(BF16)

    @pl.when(ph > 0)
    def _():
        cur = (ph + 1) % 2
        xe = xe_ref[...]
        g = _dot(xe, wg_s[cur])
        u = _dot(xe, wu_s[cur])
        act = (g * jax.nn.sigmoid(g) * u).astype(BF16)
        ye_ref[...] = (_dot(act, wd_s[cur]) * wsl_ref[:, 0:1]).astype(BF16)


def _experts(xe, wsl, wg, wu, wd, layer):
    rows = BATCH * CAPACITY
    tm = TILE["expert"]
    nt = rows // tm
    last = N_EXPERTS - 1
    wspec = lambda a: pl.BlockSpec((None, None) + a.shape[2:],
                                   lambda ph, i: (layer, jnp.minimum(ph, last), 0, 0))
    data = lambda n: pl.BlockSpec(
        (None, tm, n), lambda ph, i: (jnp.maximum(ph - 1, 0), jnp.where(ph > 0, i, 0), 0))
    return pl.pallas_call(
        functools.partial(_expert_kernel, span=D_MODEL // nt),
        grid=(N_EXPERTS + 1, nt),
        in_specs=[data(D_MODEL), data(LANES), wspec(wg), wspec(wu), wspec(wd)],
        out_specs=data(D_MODEL),
        out_shape=jax.ShapeDtypeStruct((N_EXPERTS, rows, D_MODEL), BF16),
        scratch_shapes=[pltpu.VMEM((2, D_MODEL, D_FF), BF16), pltpu.VMEM((2, D_MODEL, D_FF), BF16),
                        pltpu.VMEM((2, D_FF, D_MODEL), BF16)],
        compiler_params=_params("arbitrary", "arbitrary"),
        name="expert_ffn",
    )(xe.reshape(N_EXPERTS, rows, D_MODEL), wsl.reshape(N_EXPERTS, rows, LANES), wg, wu, wd)


SCATTER_WINDOW = 128
BF16_ROWS = 16


def _combine_kernel(start_ref, x_ref, ye_ref, slot_ref, p_ref, g3_ref, wpg_ref, wpp_ref,
                    gf_ref, out_ref, moe_s, *, final_norm):
    b = pl.program_id(0)
    i = pl.program_id(1)
    slot = slot_ref[...]
    rows = slot.shape[0]
    stride = start_ref.shape[0] // (BATCH * N_EXPERTS)
    base, fits = [], None
    for e in range(N_EXPERTS):
        at = (b * N_EXPERTS + e) * stride + i
        first = start_ref[at]
        lo = jnp.minimum(first // BF16_ROWS * BF16_ROWS, CAPACITY - SCATTER_WINDOW)
        ok = start_ref[at + 1] - lo <= SCATTER_WINDOW
        fits = ok if fits is None else fits & ok
        base.append(lo)

    @pl.when(fits)
    def _():
        c = lax.broadcasted_iota(jnp.int32, (rows, SCATTER_WINDOW), 1)
        parts = []
        for e0 in range(0, N_EXPERTS, 2):
            hot, win = [], []
            for e in (e0, e0 + 1):
                hot.append(jnp.where(slot[:, e:e + 1] - base[e] == c, 1.0, 0.0).astype(BF16))
                lo = pl.multiple_of(base[e], BF16_ROWS)
                win.append(ye_ref[e, pl.ds(lo, SCATTER_WINDOW), :])
            parts.append(_dot(jnp.concatenate(hot, axis=1), jnp.concatenate(win, axis=0)))
        while len(parts) > 1:
            parts = [u + v for u, v in zip(parts[::2], parts[1::2])]
        moe_s[...] = parts[0]

    @pl.when(jnp.logical_not(fits))
    def _():
        c = lax.broadcasted_iota(jnp.int32, (rows, CAPACITY), 1)
        onehot = jnp.concatenate(
            [jnp.where(slot[:, e:e + 1] == c, 1.0, 0.0).astype(BF16) for e in range(N_EXPERTS)],
            axis=1)
        moe_s[...] = _dot(onehot, ye_ref[...].reshape(N_EXPERTS * CAPACITY, D_MODEL))

    acc = x_ref[...] + moe_s[...]
    h = _rms(acc, g3_ref[...]).astype(BF16)
    gate = jax.nn.sigmoid(_dot(h, wpg_ref[...]))
    y = acc + gate * _dot(p_ref[...].astype(BF16), wpp_ref[...])
    if final_norm:
        y = _rms(y, gf_ref[...])
    out_ref[...] = y


def _combine(x, ye, slot_col, starts, p, layer, g3, wpg, wpp, gf, final_norm):
    tm = TILE["combine"]
    nt = SEQ // tm
    p0 = layer * (TOKENS // tm)
    const = lambda a: pl.BlockSpec(a.shape, lambda b, i, s: (0,) * a.ndim,
                                   pipeline_mode=pl.Buffered(1))
    grid_spec = pltpu.PrefetchScalarGridSpec(
        num_scalar_prefetch=1,
        grid=(BATCH, nt),
        in_specs=[pl.BlockSpec((tm, D_MODEL), lambda b, i, s: (b * nt + i, 0)),
                  pl.BlockSpec((N_EXPERTS, None, CAPACITY, D_MODEL), lambda b, i, s: (0, b, 0, 0)),
                  pl.BlockSpec((None, tm, LANES), lambda b, i, s: (b, i, 0)),
                  pl.BlockSpec((tm, PLE_DIM), lambda b, i, s: (p0 + b * nt + i, 0)),
                  const(g3), const(wpg), const(wpp), const(gf)],
        out_specs=pl.BlockSpec((tm, D_MODEL), lambda b, i, s: (b * nt + i, 0)),
        scratch_shapes=[pltpu.VMEM((tm, D_MODEL), F32)])
    return pl.pallas_call(
        functools.partial(_combine_kernel, final_norm=final_norm),
        grid_spec=grid_spec,
        out_shape=jax.ShapeDtypeStruct((TOKENS, D_MODEL), F32),
        compiler_params=_params("parallel", "parallel"),
        name="combine",
    )(starts[:, :, :nt + 1].reshape(-1), x, ye.reshape(N_EXPERTS, BATCH, CAPACITY, D_MODEL),
      slot_col, p, g3, wpg, wpp, gf)


DFT_FINE = 64
DFT_STEP = 4


def _dft_kernel(ca_ref, sa_ref, cb_ref, sb_ref, out_ref):
    cb = cb_ref[...]
    sb = sb_ref[...]
    for r in range(DFT_STEP):
        rows = slice(r * DFT_FINE, (r + 1) * DFT_FINE)
        out_ref[rows, :] = (ca_ref[r] * cb - sa_ref[r] * sb).astype(BF16)


def _dft_tables(n_points, nres):
    blk = SEQ // nres
    t = jnp.arange(SEQ, dtype=jnp.int32).reshape(blk, nres).T
    t = jnp.concatenate([t, t], axis=1).reshape(1, 2 * SEQ)
    is_sin = (jnp.arange(2 * SEQ) // blk % 2 == 1)[None, :]
    coarse = jnp.arange(HALF // DFT_FINE, dtype=jnp.int32)[:, None] * DFT_FINE
    fine = jnp.arange(DFT_FINE, dtype=jnp.int32)[:, None]
    ang = lambda f: ((f * t) % n_points).astype(F32) * (2.0 * math.pi / n_points)
    ca = jnp.cos(ang(coarse))[:, None, :]
    sa = jnp.sin(ang(coarse))[:, None, :]
    cb = jnp.where(is_sin, jnp.sin(ang(fine)), jnp.cos(ang(fine)))
    sb = jnp.where(is_sin, -jnp.cos(ang(fine)), jnp.sin(ang(fine)))
    rows = DFT_STEP * DFT_FINE
    return pl.pallas_call(
        _dft_kernel,
        grid=(HALF // rows,),
        in_specs=[pl.BlockSpec((DFT_STEP, 1, 2 * SEQ), lambda i: (i, 0, 0)),
                  pl.BlockSpec((DFT_STEP, 1, 2 * SEQ), lambda i: (i, 0, 0)),
                  _const_spec(cb.shape), _const_spec(sb.shape)],
        out_specs=pl.BlockSpec((rows, 2 * SEQ), lambda i: (i, 0)),
        out_shape=jax.ShapeDtypeStruct((HALF, 2 * SEQ), BF16),
        compiler_params=_params("parallel"),
        name="dft_tables",
    )(ca, sa, cb, sb)


def _fnet_group_tables():
    gc = D_MIX // FN_GROUPS
    i = lax.broadcasted_iota(jnp.int32, (D_MIX, D_MIX), 0)
    j = lax.broadcasted_iota(jnp.int32, (D_MIX, D_MIX), 1)
    same = (i // gc) == (j // gc)
    ang = (((i % gc) * (j % gc)) % gc).astype(F32) * (2.0 * math.pi / gc)
    return (jnp.where(same, jnp.cos(ang), 0.0).astype(BF16),
            jnp.where(same, jnp.sin(ang), 0.0).astype(BF16))


def _hyena_features():
    t01 = jnp.linspace(0.0, 1.0, SEQ, dtype=F32)[:, None]
    bands = jnp.linspace(1e-4, HY_BANDS - 1, HY_BANDS, dtype=F32)
    ang = 2.0 * math.pi * jnp.arange(SEQ, dtype=F32)[:, None] * bands / SEQ
    z = jnp.concatenate([t01, jnp.cos(ang), -jnp.sin(ang)], axis=-1)
    z = jnp.pad(z, ((0, 0), (0, LANES - HY_EMB)))
    max_decay = math.log(HY_TARGET) / HY_FAST_DECAY
    min_decay = math.log(HY_TARGET) / HY_SLOW_DECAY
    deltas = jnp.linspace(min_decay, max_decay, D_MIX, dtype=F32)
    window = jnp.exp(-t01 * jnp.abs(deltas))
    return z, window


def _rot_cols(w):
    half = w.shape[-1] // 2
    return jnp.concatenate([-w[..., half:], w[..., :half]], axis=-1)


def _rope_tables():
    inv = ROPE_THETA ** (-jnp.arange(0, MLA_ROPE, 2, dtype=F32) / MLA_ROPE)
    ang = jnp.arange(SEQ, dtype=F32)[:, None] * inv
    cos = jnp.concatenate([jnp.cos(ang), jnp.cos(ang)], axis=-1)
    sin = jnp.concatenate([jnp.sin(ang), jnp.sin(ang)], axis=-1)
    scale = (MLA_NOPE + MLA_ROPE) ** -0.5 * LOG2E
    pad = MLA_HEAD_PAD - MLA_NOPE - MLA_ROPE
    one = jnp.ones((SEQ, MLA_NOPE), F32)
    zero = jnp.zeros((SEQ, MLA_NOPE), F32)
    zpad = jnp.zeros((SEQ, pad), F32)
    cosq = jnp.tile(jnp.concatenate([one, cos, zpad], axis=-1) * scale, (1, MLA_HEADS))
    sinq = jnp.tile(jnp.concatenate([zero, sin, zpad], axis=-1) * scale, (1, MLA_HEADS))
    csk = jnp.concatenate([cos, sin, jnp.zeros((SEQ, LANES - 2 * MLA_ROPE), F32)], axis=-1)
    return cosq, sinq, csk


def _mla_weights(w_uq, w_ukv):
    pad = MLA_HEAD_PAD - MLA_NOPE - MLA_ROPE
    wq = w_uq.reshape(MLA_Q_RANK, MLA_HEADS, MLA_NOPE + MLA_ROPE)
    nope, pe = wq[..., :MLA_NOPE], wq[..., MLA_NOPE:]
    zp = jnp.zeros((MLA_Q_RANK, MLA_HEADS, pad), F32)
    wqa = jnp.concatenate([nope, pe, zp], axis=-1).reshape(MLA_Q_RANK, MLA_QK)
    wqb = jnp.concatenate([jnp.zeros_like(nope), _rot_cols(pe), zp], axis=-1)
    wqb = wqb.reshape(MLA_Q_RANK, MLA_QK)
    wkv = w_ukv.reshape(MLA_KV_RANK, MLA_HEADS, MLA_NOPE + MLA_V)
    knope, v = wkv[..., :MLA_NOPE], wkv[..., MLA_NOPE:]
    wk = jnp.concatenate(
        [knope, jnp.zeros((MLA_KV_RANK, MLA_HEADS, MLA_HEAD_PAD - MLA_NOPE), F32)], axis=-1)
    wk = wk.reshape(MLA_KV_RANK, MLA_QK)
    wv = v.reshape(MLA_KV_RANK, MLA_HEADS * MLA_V)
    r = lax.broadcasted_iota(jnp.int32, (LANES, MLA_QK), 0)
    c = lax.broadcasted_iota(jnp.int32, (LANES, MLA_QK), 1)
    epe = jnp.where((r < 2 * MLA_ROPE) & (c % MLA_HEAD_PAD == MLA_NOPE + r % MLA_ROPE), 1.0, 0.0)
    return (wqa.astype(BF16), wqb.astype(BF16), wk.astype(BF16), wv.astype(BF16),
            epe.astype(BF16))


def kernel(x, p, norm1_g, w_in, b_gate, hy_conv_w, hy_conv_b, hf_w1, hf_b1, hf_freq, hf_w2,
           hf_b2, hf_w3, hy_skip, q_norm_g, w_uq, kv_norm_g, w_ukv, rpb, w_br, w_out, norm2_g,
           w_router, w_e_gate, w_e_up, w_e_down, norm3_g, w_ple_gate, w_ple_proj, final_g):
    conv_tab = _dft_tables(2 * SEQ, 4)
    fnet_tab = _dft_tables(SEQ, 2)
    fnet_cg, fnet_sg = _fnet_group_tables()
    zfeat, window = _hyena_features()
    cosq, sinq, csk = _rope_tables()
    row = lambda a: a.reshape(1, -1)

    xt = x.reshape(TOKENS, D_MODEL)
    for i in range(DEPTH):
        g1 = row(norm1_g[i])
        wa, wna, wgate = _inproj_weights(w_in, i)
        wqa, wqb, wk, wv, epe = _mla_weights(w_uq[i], w_ukv[i])

        u_hy, u_fn, q, k, v, naq, nak, nav = _inproj(
            xt, g1, wa, wna, row(q_norm_g[i]), wqa, wqb, row(kv_norm_g[i]), wk, wv, epe,
            cosq, sinq, csk)

        w1 = jnp.pad(hf_w1[i], ((0, LANES - HY_EMB), (0, 0)))
        kf, kny = _hyena_filter(zfeat, window, w1, row(hf_b1[i]), hf_freq[i], hf_w2[i],
                                row(hf_b2[i]), hf_w3[i], conv_tab)
        conv_b = row(hy_conv_b[i])
        z1 = _hyena_stage(u_hy, 0, u_hy, 2, hy_conv_w[i], conv_b, conv_tab, kf[0],
                          kny[0], row(hy_skip[i, 0]), True, F32)
        y_hy = _hyena_stage(u_hy, 1, z1, 0, hy_conv_w[i], conv_b, conv_tab, kf[1],
                            kny[1], row(hy_skip[i, 1]), False, BF16)
        y_fn = _fnet(u_fn, fnet_tab, fnet_cg, fnet_sg)
        y_mla = _mla(q, k, v)
        y_na = _neighborhood(naq, nak, nav, _na_bias_tiles(rpb[i]))
        wr_t = jnp.pad(w_router[i].T, ((0, LANES - N_EXPERTS), (0, 0)))
        xt, hb, logits = _merge(xt, g1, y_hy, y_fn, y_mla, y_na, wgate, row(b_gate[i]),
                                w_br[i].astype(BF16), w_out[i].astype(BF16),
                                row(norm2_g[i]), wr_t)
        slot_row, slot_col, w_row, starts = _select(logits)
        xe, wsl = _gather(slot_row, w_row, hb)
        ye = _experts(xe, wsl, w_e_gate, w_e_up, w_e_down, i)
        xt = _combine(xt, ye, slot_col, starts, p.reshape(DEPTH * TOKENS, PLE_DIM), i,
                      row(norm3_g[i]), w_ple_gate[i].astype(BF16),
                      w_ple_proj[i].astype(BF16), row(final_g), i == DEPTH - 1)
    return xt.reshape(BATCH, SEQ, D_MODEL)
```

```python
import functools
import math

import jax
import jax.numpy as jnp
from jax import lax
from jax.experimental import pallas as pl
from jax.experimental.pallas import tpu as pltpu

F32 = jnp.float32
BF16 = jnp.bfloat16

D_MODEL = 1024
BATCH = 8
SEQ = 2048
DEPTH = 2
TOKENS = BATCH * SEQ

GRID_W = 64
GRID_R = SEQ // GRID_W
D_MIX = 256
N_BRANCH = 4
EPS = 1e-6
HY_ORDER = 2
HY_BANDS = 16
HY_EMB = 2 * HY_BANDS + 1
HY_FFN = 64
HY_TARGET = 1e-2
HY_FAST_DECAY = 0.3
HY_SLOW_DECAY = 1.5
FN_GROUPS = 4
MLA_HEADS = 4
MLA_NOPE = 64
MLA_ROPE = 32
MLA_V = 64
MLA_Q_RANK = 256
MLA_KV_RANK = 128
ROPE_THETA = 10000.0
NA_HEADS = 4
NA_HEAD_DIM = D_MIX // NA_HEADS
NA_WIN_R = 8
NA_WIN_C = 16
N_EXPERTS = 16
CAPACITY = 2 * SEQ // N_EXPERTS
D_FF = 1024
PLE_DIM = 256

HY_COLS = 3 * D_MIX
OFF_FN = HY_COLS
OFF_CQ = OFF_FN + D_MIX
OFF_CKV = OFF_CQ + MLA_Q_RANK
OFF_KPE = OFF_CKV + MLA_KV_RANK
OFF_NA = OFF_KPE + MLA_ROPE
OFF_GATE = OFF_NA + 3 * D_MIX

LANES = 128
MLA_HEAD_PAD = 128
MLA_QK = MLA_HEADS * MLA_HEAD_PAD
WA_COLS = 1536
NEG_BIG = -1e30
LOG2E = math.log2(math.e)

TILE = dict(inproj=512, mla=512, merge=512, expert=512, combine=512)
NA_QROWS = 8
NA_KROWS = 16
NA_QBLK = NA_QROWS * GRID_W
NA_KBLK = NA_KROWS * GRID_W
VMEM_LIMIT = 56 * 1024 * 1024


def _params(*sem):
    return pltpu.CompilerParams(dimension_semantics=sem, vmem_limit_bytes=VMEM_LIMIT)


def _const_spec(shape):
    nd = len(shape)
    return pl.BlockSpec(shape, lambda *_: (0,) * nd, pipeline_mode=pl.Buffered(1))


def _rms(x, g):
    return x * lax.rsqrt(jnp.mean(x * x, axis=-1, keepdims=True) + EPS) * g


def _dot(a, b):
    return jnp.dot(a, b, preferred_element_type=F32)


def _dot_nt(a, b):
    return lax.dot_general(a, b, (((1,), (1,)), ((), ())), preferred_element_type=F32)


def _split2(x):
    hi = x.astype(BF16)
    lo = (x - hi.astype(F32)).astype(BF16)
    return hi, lo


def _inproj_kernel(x_ref, g1_ref, wa_ref, wna_ref, qg_ref, wqa_ref, wqb_ref, kvg_ref,
                   wk_ref, wv_ref, epe_ref, cosq_ref, sinq_ref, csk_ref,
                   uhy_ref, ufn_ref, q_ref, k_ref, v_ref, naq_ref, nak_ref, nav_ref):
    h = _rms(x_ref[...], g1_ref[...]).astype(BF16)
    ua = _dot_nt(h, wa_ref[...])
    uhy_ref[...] = ua[:, :HY_COLS]
    ufn_ref[...] = ua[:, OFF_FN:OFF_CQ]
    cqn = _rms(ua[:, OFF_CQ:OFF_CKV], qg_ref[...]).astype(BF16)
    q = _dot(cqn, wqa_ref[...]) * cosq_ref[...] + _dot(cqn, wqb_ref[...]) * sinq_ref[...]
    q_ref[...] = q.astype(BF16)
    kvn = _rms(ua[:, OFF_CKV:OFF_KPE], kvg_ref[...]).astype(BF16)
    kpe = ua[:, OFF_KPE:WA_COLS] * csk_ref[...]
    k = _dot(kvn, wk_ref[...]) + _dot(kpe.astype(BF16), epe_ref[...])
    k_ref[...] = k.astype(BF16)
    v_ref[...] = _dot(kvn, wv_ref[...]).astype(BF16)
    una = _dot_nt(h, wna_ref[...])
    naq_ref[...] = (una[:, :D_MIX] * (NA_HEAD_DIM ** -0.5 * LOG2E)).astype(BF16)
    nak_ref[...] = una[:, D_MIX:2 * D_MIX].astype(BF16)
    nav_ref[...] = una[:, 2 * D_MIX:].astype(BF16)


def _inproj(x, g1, wa, wna, qg, wqa, wqb, kvg, wk, wv, epe, cosq, sinq, csk):
    tm = TILE["inproj"]
    nt = SEQ // tm
    row = lambda n: pl.BlockSpec((tm, n), lambda i: (i, 0))
    pos = lambda n: pl.BlockSpec((tm, n), lambda i: (i % nt, 0))
    outs = [(HY_COLS, F32), (D_MIX, F32), (MLA_QK, BF16), (MLA_QK, BF16), (D_MIX, BF16),
            (D_MIX, BF16), (D_MIX, BF16), (D_MIX, BF16)]
    return pl.pallas_call(
        _inproj_kernel,
        grid=(TOKENS // tm,),
        in_specs=[row(D_MODEL), _const_spec(g1.shape), _const_spec(wa.shape),
                  _const_spec(wna.shape), _const_spec(qg.shape), _const_spec(wqa.shape),
                  _const_spec(wqb.shape), _const_spec(kvg.shape), _const_spec(wk.shape),
                  _const_spec(wv.shape), _const_spec(epe.shape),
                  pos(MLA_QK), pos(MLA_QK), pos(LANES)],
        out_specs=[row(n) for n, _ in outs],
        out_shape=[jax.ShapeDtypeStruct((TOKENS, n), dt) for n, dt in outs],
        compiler_params=_params("parallel"),
        name="inproj",
    )(x, g1, wa, wna, qg, wqa, wqb, kvg, wk, wv, epe, cosq, sinq, csk)


WPREP_ROWS = 512


def _inproj_weights_kernel(w_ref, wa_ref, wna_ref, wg_ref):
    i = pl.program_id(0)

    @pl.when(i == 0)
    def _():
        half = MLA_ROPE // 2
        wa_ref[:OFF_NA, :] = w_ref[:OFF_NA, :].astype(BF16)
        wa_ref[OFF_NA:OFF_NA + half, :] = (-w_ref[OFF_KPE + half:OFF_NA, :]).astype(BF16)
        wa_ref[OFF_NA + half:OFF_NA + MLA_ROPE, :] = w_ref[OFF_KPE:OFF_KPE + half, :].astype(BF16)
        wa_ref[OFF_NA + MLA_ROPE:, :] = jnp.zeros((WA_COLS - OFF_NA - MLA_ROPE, D_MODEL), BF16)
        wna_ref[...] = w_ref[OFF_NA:OFF_GATE, :].astype(BF16)

    start = pl.multiple_of(OFF_GATE + i * WPREP_ROWS, MLA_ROPE)
    wg_ref[...] = w_ref[pl.ds(start, WPREP_ROWS), :].astype(BF16)


def _inproj_weights(w_in_t, layer):
    n_in = w_in_t.shape[1]
    rows = (WA_COLS, OFF_GATE - OFF_NA, n_in - OFF_GATE)
    whole = lambda n: pl.BlockSpec((n, D_MODEL), lambda i: (0, 0))
    return pl.pallas_call(
        _inproj_weights_kernel,
        grid=(rows[2] // WPREP_ROWS,),
        in_specs=[pl.BlockSpec((None, n_in, D_MODEL), lambda i: (layer, 0, 0),
                               pipeline_mode=pl.Buffered(1))],
        out_specs=[whole(rows[0]), whole(rows[1]),
                   pl.BlockSpec((WPREP_ROWS, D_MODEL), lambda i: (i, 0))],
        out_shape=[jax.ShapeDtypeStruct((n, D_MODEL), BF16) for n in rows],
        compiler_params=_params("arbitrary"),
        name="inproj_weights",
    )(w_in_t)


HALF = SEQ // 2
HY_SEQS = 1


def _residue_rows(ref, r, nres):
    rows = pl.ds(r, SEQ // nres, stride=nres)
    if isinstance(ref, tuple):
        return jnp.concatenate([h[rows, :] for h in ref], axis=1)
    return jnp.concatenate([ref[j, rows, :] for j in range(ref.shape[0])], axis=1)


def _store_slabs(ref, value, rows=slice(None)):
    for j in range(ref.shape[0]):
        ref[j, rows, :] = value[:, j * LANES:(j + 1) * LANES]


def _table_cols(cs_ref, r, nres, part=None):
    blk = SEQ // nres
    lo = 2 * blk * r
    if part is None:
        return cs_ref[:, lo:lo + 2 * blk]
    return cs_ref[:, lo + part * blk:lo + (part + 1) * blk]


def _dft_fwd(cs_ref, src_ref, want_cos=True, want_sin=True, split=False):
    def prod(part, z, r):
        tab = _table_cols(cs_ref, r, 4, part)
        if split:
            hi, lo = _split2(z)
            return _dot(tab, hi) + _dot(tab, lo)
        return _dot(tab, z.astype(BF16))

    pc, ps = [None] * 4, [None] * 4
    for r in range(4):
        z = _residue_rows(src_ref, r, 4)
        odd = r % 2 == 1
        if want_cos or odd:
            pc[r] = prod(0, z, r)
        if want_sin or odd:
            ps[r] = prod(1, z, r)
    a = b = None
    if want_cos:
        a = ((pc[0] + pc[2]) + (pc[1] + pc[3]), (pc[0] - pc[2]) + (ps[3] - ps[1]))
    if want_sin:
        b = ((ps[0] + ps[2]) + (ps[1] + ps[3]), (ps[0] - ps[2]) + (pc[1] - pc[3]))
    return a, b


def _dft_inv(cs_ref, yre_ref, yim_ref):
    g, h = [], {}
    for r in range(4):
        yr = _residue_rows(yre_ref, r, 4).astype(BF16)
        yi = _residue_rows(yim_ref, r, 4).astype(BF16)
        tab = _table_cols(cs_ref, r, 4)
        g.append(_dot(tab, jnp.concatenate([yr, -yi], axis=0)))
        if r % 2 == 1:
            h[r] = _dot(tab, jnp.concatenate([yi, yr], axis=0))
    return (g[0] + g[2]) + (g[1] + g[3]), (g[0] - g[2]) + (h[3] - h[1])


def _hyena_filter_kernel(z_ref, win_ref, w1_ref, b1_ref, freq_ref, w2_ref, b2_ref, w3_ref,
                         cs_ref, kf_ref, kny_ref, ksum_s, kdif_s):
    hp = lax.Precision.HIGHEST
    freq = freq_ref[...]
    hf = jnp.sin(freq[0:1] * (jnp.dot(z_ref[...], w1_ref[...], precision=hp,
                                      preferred_element_type=F32) + b1_ref[...]))
    hf = jnp.sin(freq[1:2] * (jnp.dot(hf, w2_ref[...], precision=hp,
                                      preferred_element_type=F32) + b2_ref[...]))
    hf = jnp.dot(hf, w3_ref[...], precision=hp, preferred_element_type=F32)
    win = win_ref[...]
    t = lax.broadcasted_iota(jnp.int32, (SEQ, D_MIX), 0)
    sgn = (1 - 2 * (t & 1)).astype(F32)
    fwd = hf[:, :D_MIX] * win
    bwd = jnp.where(t == 0, 0.0, hf[:, D_MIX:] * win)
    nrm = lax.rsqrt(jnp.sum(fwd * fwd + bwd * bwd, axis=0, keepdims=True) + EPS)
    ksum = (fwd + bwd) * nrm
    _store_slabs(ksum_s, ksum)
    _store_slabs(kdif_s, (bwd - fwd) * nrm)
    kre, _ = _dft_fwd(cs_ref, ksum_s, want_sin=False, split=True)
    _, kim = _dft_fwd(cs_ref, kdif_s, want_cos=False, split=True)
    wf = 2.0 / (2 * SEQ)
    for part, spec in enumerate((kre, kim)):
        lo = spec[0] * wf
        kf_ref[part, :HALF, :] = lo
        kf_ref[part, 0:1, :] = lo[0:1] * 0.5
        kf_ref[part, HALF:, :] = spec[1] * wf
    kny = jnp.sum(ksum * sgn, axis=0, keepdims=True) * (1.0 / (2 * SEQ))
    kny_ref[...] = jnp.broadcast_to(kny, (8, D_MIX))


def _hyena_filter(zfeat, window, w1, b1, freq, w2, b2, w3, table):
    consts = (zfeat, window, w1, b1, freq, w2, b2)
    return pl.pallas_call(
        _hyena_filter_kernel,
        grid=(HY_ORDER,),
        in_specs=[_const_spec(a.shape) for a in consts]
        + [pl.BlockSpec((HY_FFN, 2 * D_MIX), lambda o: (0, o)),
           _const_spec(table.shape)],
        out_specs=[pl.BlockSpec((None, 2, SEQ, D_MIX), lambda o: (o, 0, 0, 0)),
                   pl.BlockSpec((None, 8, D_MIX), lambda o: (o, 0, 0))],
        out_shape=[jax.ShapeDtypeStruct((HY_ORDER, 2, SEQ, D_MIX), F32),
                   jax.ShapeDtypeStruct((HY_ORDER, 8, D_MIX), F32)],
        scratch_shapes=[pltpu.VMEM((D_MIX // LANES, SEQ, LANES), F32)] * 2,
        compiler_params=_params("arbitrary"),
        name="hyena_filter",
    )(*consts, w3, table)


def _short_conv(u, w, b):
    t = lax.broadcasted_iota(jnp.int32, u.shape, 0)
    prev = jnp.where(t == 0, 0.0, pltpu.roll(u, 1, 0))
    nxt = jnp.where(t == SEQ - 1, 0.0, pltpu.roll(u, SEQ - 1, 0))
    return prev * w[0:1] + u * w[1:2] + nxt * w[2:3] + b


def _hyena_stage_kernel(gate_ref, src_ref, wg_ref, bg_ref, ws_ref, bs_ref, cs_ref,
                        kf_ref, kny_ref, skip_ref, out_ref, z_s, yre_s, yim_s, gate_s, rest_s,
                        *, conv_src):
    for q in range(HY_SEQS):
        seq = slice(q * SEQ, (q + 1) * SEQ)
        zq, yre_q, yim_q = z_s.at[q], yre_s.at[q], yim_s.at[q]
        z = src_ref[seq, :]
        if conv_src:
            z = _short_conv(z, ws_ref[...], bs_ref[...])
        _store_slabs(zq, z)
        t = lax.broadcasted_iota(jnp.int32, z.shape, 0)
        sgn = (1 - 2 * (t & 1)).astype(F32)
        nyq = jnp.sum(z * sgn, axis=0, keepdims=True) * kny_ref[0:1]
        gate = _short_conv(gate_ref[seq, :], wg_ref[...], bg_ref[...])
        gate_s[seq, :] = gate
        rest_s[seq, :] = gate * (sgn * nyq + z * skip_ref[...])
        a, b = _dft_fwd(cs_ref, zq)
        for half in range(2):
            rows = slice(half * HALF, (half + 1) * HALF)
            kre = kf_ref[0, rows, :]
            kim = kf_ref[1, rows, :]
            _store_slabs(yre_q, a[half] * kre + b[half] * kim, rows)
            _store_slabs(yim_q, a[half] * kim - b[half] * kre, rows)
        for half, y in enumerate(_dft_inv(cs_ref, yre_q, yim_q)):
            rows = slice(q * SEQ + half * HALF, q * SEQ + (half + 1) * HALF)
            out_ref[rows, :] = (gate_s[rows, :] * y + rest_s[rows, :]).astype(out_ref.dtype)


def _hyena_stage(u_hy, gate_blk, src, src_blk, conv_w, conv_b, table, kf, kny, skip,
                 conv_src, out_dtype):
    rows = HY_SEQS * SEQ
    col = lambda blk: pl.BlockSpec((rows, D_MIX), lambda b: (b, blk))
    wcol = lambda blk, r: pl.BlockSpec((r, D_MIX), lambda b: (0, blk))
    ws_blk = src_blk if conv_src else 0
    return pl.pallas_call(
        functools.partial(_hyena_stage_kernel, conv_src=conv_src),
        grid=(BATCH // HY_SEQS,),
        in_specs=[col(gate_blk), col(src_blk), wcol(gate_blk, 3), wcol(gate_blk, 1),
                  wcol(ws_blk, 3), wcol(ws_blk, 1), _const_spec(table.shape),
                  _const_spec(kf.shape), _const_spec(kny.shape),
                  _const_spec(skip.shape)],
        out_specs=pl.BlockSpec((rows, D_MIX), lambda b: (b, 0)),
        out_shape=jax.ShapeDtypeStruct((TOKENS, D_MIX), out_dtype),
        scratch_shapes=[pltpu.VMEM((HY_SEQS, D_MIX // LANES, SEQ, LANES), F32)] * 3
        + [pltpu.VMEM((rows, D_MIX), F32)] * 2,
        compiler_params=_params("parallel"),
        name="hyena_stage",
    )(u_hy, src, conv_w, conv_b, conv_w, conv_b, table, kf, kny, skip)


def _fnet_kernel(xa_ref, xb_ref, cs_ref, cg_ref, sg_ref, out_ref):
    parts = []
    for r in range(2):
        xb = _residue_rows((xa_ref, xb_ref), r, 2).astype(BF16)
        xc = _dot(xb, cg_ref[...]).astype(BF16)
        xs = _dot(xb, sg_ref[...]).astype(BF16)
        parts.append(_dot(_table_cols(cs_ref, r, 2), jnp.concatenate([xc, -xs], axis=0)))
    scale = (SEQ * D_MIX // FN_GROUPS) ** -0.5
    out_ref[:HALF, :] = ((parts[0] + parts[1]) * scale).astype(out_ref.dtype)
    out_ref[HALF:, :] = ((parts[0] - parts[1]) * scale).astype(out_ref.dtype)


def _fnet(u_fn, table, cg, sg):
    return pl.pallas_call(
        _fnet_kernel,
        grid=(BATCH,),
        in_specs=[pl.BlockSpec((SEQ, LANES), lambda b: (b, 0)),
                  pl.BlockSpec((SEQ, LANES), lambda b: (b, 1)), _const_spec(table.shape),
                  _const_spec(cg.shape), _const_spec(sg.shape)],
        out_specs=pl.BlockSpec((SEQ, D_MIX), lambda b: (b, 0)),
        out_shape=jax.ShapeDtypeStruct((TOKENS, D_MIX), BF16),
        compiler_params=_params("parallel"),
        name="fnet",
    )(u_fn, u_fn, table, cg, sg)


def _softmax2_pv(s2, v):
    m = jnp.max(s2, axis=-1, keepdims=True)
    p = jnp.exp2(s2 - m)
    l = jnp.sum(p, axis=-1, keepdims=True)
    return _dot(p.astype(BF16), v) / l


def _mla_kernel(q_ref, k_ref, v_ref, out_ref):
    v = v_ref[...]
    head = lax.broadcasted_iota(jnp.int32, out_ref.shape, 1) // MLA_V
    acc = jnp.zeros(out_ref.shape, F32)
    for h in range(MLA_HEADS):
        sl = slice(h * MLA_HEAD_PAD, (h + 1) * MLA_HEAD_PAD)
        s2 = _dot_nt(q_ref[:, sl], k_ref[:, sl])
        acc = jnp.where(head == h, _softmax2_pv(s2, v), acc)
    out_ref[...] = acc.astype(out_ref.dtype)


def _mla(q, k, v):
    tm = TILE["mla"]
    nt = SEQ // tm
    return pl.pallas_call(
        _mla_kernel,
        grid=(BATCH, nt),
        in_specs=[pl.BlockSpec((tm, MLA_QK), lambda b, i: (b * nt + i, 0)),
                  pl.BlockSpec((SEQ, MLA_QK), lambda b, i: (b, 0)),
                  pl.BlockSpec((SEQ, D_MIX), lambda b, i: (b, 0))],
        out_specs=pl.BlockSpec((tm, D_MIX), lambda b, i: (b * nt + i, 0)),
        out_shape=jax.ShapeDtypeStruct((TOKENS, D_MIX), BF16),
        compiler_params=_params("parallel", "parallel"),
        name="mla_attention",
    )(q, k, v)


def _na_key_row0(j):
    return jnp.clip(j * NA_QROWS - NA_WIN_R // 2, 0, GRID_R - NA_KROWS)


NA_PAIRS = 2 * NA_WIN_R


def _na_kernel(q_ref, k_ref, v_ref, tile_ref, out_ref, bias_s):
    j = pl.program_id(0)
    krow0 = _na_key_row0(j)

    @pl.when(pl.program_id(1) == 0)
    def _():
        rq = j * NA_QROWS + lax.broadcasted_iota(jnp.int32, (NA_QBLK, NA_KBLK), 0) // GRID_W
        rk = krow0 + lax.broadcasted_iota(jnp.int32, (NA_QBLK, NA_KBLK), 1) // GRID_W
        rs = jnp.clip(rq - NA_WIN_R // 2, 0, GRID_R - NA_WIN_R)
        rowmask = jnp.where(rk < rs, NEG_BIG, jnp.where(rk >= rs + NA_WIN_R, NEG_BIG, 0.0))
        base = krow0 - j * NA_QROWS + NA_WIN_R
        for h in range(NA_HEADS):
            bias = jnp.concatenate(
                [jnp.concatenate(
                    [tile_ref[h, jnp.clip(base + 2 * kp - r, 0, NA_PAIRS - 1)]
                     for kp in range(NA_KROWS // 2)], axis=1)
                 for r in range(NA_QROWS)], axis=0)
            bias_s[h] = bias + rowmask

    off = pl.multiple_of(krow0 * GRID_W, GRID_W)
    q = q_ref[...]
    k = k_ref[pl.ds(off, NA_KBLK), :]
    v = v_ref[pl.ds(off, NA_KBLK), :]
    head = lax.broadcasted_iota(jnp.int32, (NA_QBLK, D_MIX), 1) // NA_HEAD_DIM
    acc = jnp.zeros((NA_QBLK, D_MIX), F32)
    for h in range(NA_HEADS):
        qh = jnp.where(head == h, q, jnp.zeros_like(q))
        s2 = _dot_nt(qh, k) + bias_s[h]
        acc = jnp.where(head == h, _softmax2_pv(s2, v), acc)
    out_ref[...] = acc.astype(out_ref.dtype)


def _neighborhood(q, k, v, tiles):
    nj = SEQ // NA_QBLK
    return pl.pallas_call(
        _na_kernel,
        grid=(nj, BATCH),
        in_specs=[pl.BlockSpec((NA_QBLK, D_MIX), lambda j, b: (b * nj + j, 0)),
                  pl.BlockSpec((SEQ, D_MIX), lambda j, b: (b, 0)),
                  pl.BlockSpec((SEQ, D_MIX), lambda j, b: (b, 0)),
                  _const_spec(tiles.shape)],
        out_specs=pl.BlockSpec((NA_QBLK, D_MIX), lambda j, b: (b * nj + j, 0)),
        out_shape=jax.ShapeDtypeStruct((TOKENS, D_MIX), BF16),
        scratch_shapes=[pltpu.VMEM((NA_HEADS, NA_QBLK, NA_KBLK), F32)],
        compiler_params=_params("parallel", "arbitrary"),
        name="neighborhood_attention",
    )(q, k, v, tiles)


def _na_bias_tiles(rpb):
    c = jnp.arange(GRID_W)
    cs = jnp.clip(c - NA_WIN_C // 2, 0, GRID_W - NA_WIN_C)
    col_ok = (c[None, :] >= cs[:, None]) & (c[None, :] < cs[:, None] + NA_WIN_C)
    dc = jnp.clip(c[None, :] - c[:, None] + (NA_WIN_C - 1), 0, 2 * NA_WIN_C - 2)
    pick = (dc[None] == jnp.arange(2 * NA_WIN_C - 1)[:, None, None]).astype(F32)
    t = jnp.einsum('hrd,dqk->hrqk', rpb.astype(F32), pick, precision=lax.Precision.HIGHEST)
    t = jnp.where(col_ok, t * LOG2E, NEG_BIG)
    t = jnp.pad(t, ((0, 0), (1, 1), (0, 0), (0, 0)))
    return jnp.concatenate([t[:, :-1], t[:, 1:]], axis=-1)


def _merge_kernel(x_ref, g1_ref, yhy_ref, yfn_ref, ymla_ref, yna_ref, wg_ref, bg_ref,
                  wbr_ref, wout_ref, g2_ref, wr_ref, out_ref, hb_ref, logit_ref):
    x = x_ref[...]
    h = _rms(x, g1_ref[...]).astype(BF16)
    merged = jnp.zeros(x.shape, F32)
    for n, y_ref in enumerate((yhy_ref, yfn_ref, ymla_ref, yna_ref)):
        sl = slice(n * D_MODEL, (n + 1) * D_MODEL)
        gate = jax.nn.sigmoid(_dot_nt(h, wg_ref[sl, :]) + bg_ref[:, sl])
        merged = merged + gate * _dot(y_ref[...], wbr_ref[n])
    x1 = x + _dot(merged.astype(BF16), wout_ref[...])
    out_ref[...] = x1
    h_hi, h_lo = _split2(_rms(x1, g2_ref[...]))
    hb_ref[...] = h_hi
    w_hi, w_lo = _split2(wr_ref[...])
    logits = _dot_nt(w_hi, h_hi) + _dot_nt(w_hi, h_lo) + _dot_nt(w_lo, h_hi)
    logit_ref[...] = logits[:N_EXPERTS]


def _merge(x, g1, yhy, yfn, ymla, yna, wg, bg, wbr, wout, g2, wr_t):
    tm = TILE["merge"]
    nt = SEQ // tm
    row = lambda n: pl.BlockSpec((tm, n), lambda i: (i, 0))
    return pl.pallas_call(
        _merge_kernel,
        grid=(TOKENS // tm,),
        in_specs=[row(D_MODEL), _const_spec(g1.shape), row(D_MIX), row(D_MIX), row(D_MIX),
                  row(D_MIX), _const_spec(wg.shape), _const_spec(bg.shape),
                  _const_spec(wbr.shape), _const_spec(wout.shape), _const_spec(g2.shape),
                  _const_spec(wr_t.shape)],
        out_specs=[row(D_MODEL), row(D_MODEL),
                   pl.BlockSpec((None, N_EXPERTS, tm), lambda i: (i // nt, 0, i % nt))],
        out_shape=[jax.ShapeDtypeStruct((TOKENS, D_MODEL), F32),
                   jax.ShapeDtypeStruct((TOKENS, D_MODEL), BF16),
                   jax.ShapeDtypeStruct((BATCH, N_EXPERTS, SEQ), F32)],
        compiler_params=_params("parallel"),
        name="merge",
    )(x, g1, yhy, yfn, ymla, yna, wg, bg, wbr, wout, g2, wr_t)


def _prefix_count(m):
    r = lax.broadcasted_iota(jnp.int32, (LANES, LANES), 0)
    c = lax.broadcasted_iota(jnp.int32, (LANES, LANES), 1)
    upper = jnp.where(r < c, 1.0, 0.0).astype(BF16)
    run = jnp.zeros((m.shape[0], 1), F32)
    parts = []
    for i in range(SEQ // LANES):
        chunk = m[:, i * LANES:(i + 1) * LANES]
        parts.append(_dot(chunk.astype(BF16), upper) + run)
        run = run + jnp.sum(chunk, axis=1, keepdims=True)
    return jnp.concatenate(parts, axis=1)


SELECT_MAX_ITERS = 192


def _select_kernel(logit_ref, slot_row_ref, slot_col_ref, w_row_ref, start_ref, slot_s, w_s):
    b = pl.program_id(0)
    rows = BATCH * N_EXPERTS

    @pl.when(b == 0)
    def _():
        logits = logit_ref[...]
        ex = jnp.exp(logits - jnp.max(logits, axis=1, keepdims=True))
        aff = (ex / jnp.sum(ex, axis=1, keepdims=True)).reshape(rows, SEQ)

        def bisect(c):
            it, lo, hi, _ = c
            mid = 0.5 * (lo + hi)
            cnt = jnp.sum(jnp.where(aff >= mid, 1.0, 0.0), axis=1, keepdims=True)
            moving = jnp.where(mid == lo, 0.0, jnp.where(mid == hi, 0.0, 1.0))
            enough = cnt >= CAPACITY
            return (it + 1, jnp.where(enough, mid, lo), jnp.where(enough, hi, mid),
                    (jnp.max(moving) > 0).astype(jnp.int32))

        _, lo, hi, _ = lax.while_loop(
            lambda c: (c[0] < SELECT_MAX_ITERS) & (c[3] > 0), bisect,
            (jnp.int32(0), jnp.zeros((rows, 1), F32), jnp.full((rows, 1), 2.0, F32),
             jnp.int32(1)))
        above = jnp.where(aff >= hi, 1.0, 0.0)
        band = jnp.where(aff >= lo, 1.0, 0.0) - above
        need = CAPACITY - jnp.sum(above, axis=1, keepdims=True)
        sel = above + band * jnp.where(_prefix_count(band) < need, 1.0, 0.0)
        slot_s[...] = jnp.where(sel > 0, _prefix_count(sel), -1.0)
        w_s[...] = sel * aff

    r0 = pl.multiple_of(b * N_EXPERTS, N_EXPERTS)
    slot = slot_s[pl.ds(r0, N_EXPERTS), :]
    slot_row_ref[...] = slot.astype(jnp.int32)
    pad = jnp.full((LANES - N_EXPERTS, SEQ), -1.0, F32)
    slot_col_ref[...] = jnp.concatenate([slot, pad], axis=0).T.astype(jnp.int32)
    w_row_ref[...] = w_s[pl.ds(r0, N_EXPERTS), :]
    tm = TILE["combine"]
    token = lax.broadcasted_iota(jnp.int32, slot.shape, 1)
    lane = lax.broadcasted_iota(jnp.int32, (N_EXPERTS, LANES), 1)
    starts = jnp.where(lane == SEQ // tm, float(CAPACITY), 0.0)
    for i in range(1, SEQ // tm):
        before = jnp.where(slot >= 0, jnp.where(token < i * tm, 1.0, 0.0), 0.0)
        starts = starts + jnp.where(lane == i, jnp.sum(before, axis=1, keepdims=True), 0.0)
    start_ref[...] = starts.astype(jnp.int32)


def _select(logits):
    return pl.pallas_call(
        _select_kernel,
        grid=(BATCH,),
        in_specs=[_const_spec(logits.shape)],
        out_specs=[pl.BlockSpec((None, N_EXPERTS, SEQ), lambda b: (b, 0, 0)),
                   pl.BlockSpec((None, SEQ, LANES), lambda b: (b, 0, 0)),
                   pl.BlockSpec((None, N_EXPERTS, SEQ), lambda b: (b, 0, 0)),
                   pl.BlockSpec((None, N_EXPERTS, LANES), lambda b: (b, 0, 0))],
        out_shape=[jax.ShapeDtypeStruct((BATCH, N_EXPERTS, SEQ), jnp.int32),
                   jax.ShapeDtypeStruct((BATCH, SEQ, LANES), jnp.int32),
                   jax.ShapeDtypeStruct((BATCH, N_EXPERTS, SEQ), F32),
                   jax.ShapeDtypeStruct((BATCH, N_EXPERTS, LANES), jnp.int32)],
        scratch_shapes=[pltpu.VMEM((BATCH * N_EXPERTS, SEQ), F32),
                        pltpu.VMEM((BATCH * N_EXPERTS, SEQ), F32)],
        compiler_params=_params("arbitrary"),
        name="expert_select",
    )(logits)


GATHER_EXPERTS = 8

def _gather_kernel(slot_ref, w_ref, hb_ref, xe_ref, wsl_ref):
    ge = GATHER_EXPERTS
    c = lax.broadcasted_iota(jnp.int32, (CAPACITY, SEQ), 0)
    hits = [slot_ref[e:e + 1, :] == c for e in range(ge)]
    onehot = jnp.concatenate([jnp.where(m, 1.0, 0.0).astype(BF16) for m in hits], axis=0)
    rows = _dot(onehot, hb_ref[...]).astype(BF16)
    for e in range(ge):
        xe_ref[e] = rows[e * CAPACITY:(e + 1) * CAPACITY]
        wslot = jnp.sum(jnp.where(hits[e], w_ref[e:e + 1, :], 0.0), axis=1, keepdims=True)
        wsl_ref[e] = jnp.broadcast_to(wslot, (CAPACITY, LANES))


def _gather(slot_row, w_row, hb):
    ge = GATHER_EXPERTS
    return pl.pallas_call(
        _gather_kernel,
        grid=(BATCH, N_EXPERTS // ge),
        in_specs=[pl.BlockSpec((None, ge, SEQ), lambda b, g: (b, g, 0)),
                  pl.BlockSpec((None, ge, SEQ), lambda b, g: (b, g, 0)),
                  pl.BlockSpec((SEQ, D_MODEL), lambda b, g: (b, 0))],
        out_specs=[pl.BlockSpec((ge, None, CAPACITY, D_MODEL), lambda b, g: (g, b, 0, 0)),
                   pl.BlockSpec((ge, None, CAPACITY, LANES), lambda b, g: (g, b, 0, 0))],
        out_shape=[jax.ShapeDtypeStruct((N_EXPERTS, BATCH, CAPACITY, D_MODEL), BF16),
                   jax.ShapeDtypeStruct((N_EXPERTS, BATCH, CAPACITY, LANES), F32)],
        compiler_params=_params("parallel", "parallel"),
        name="expert_gather",
    )(slot_row, w_row, hb)


def _expert_kernel(xe_ref, wsl_ref, wg_ref, wu_ref, wd_ref, ye_ref, wg_s, wu_s, wd_s, *, span):
    ph = pl.program_id(0)
    i = pl.program_id(1)
    rows = pl.ds(pl.multiple_of(i * span, span), span)
    nxt = ph % 2
    wg_s[nxt, rows, :] = wg_ref[rows, :].astype(BF16)
    wu_s[nxt, rows, :] = wu_ref[rows, :].astype(BF16)
    wd_s[nxt, rows, :] = wd_ref[rows, :].astype(BF16)

    @pl.when(ph > 0)
    def _():
        cur = (ph + 1) % 2
        xe = xe_ref[...]
        g = _dot(xe, wg_s[cur])
        u = _dot(xe, wu_s[cur])
        act = (g * jax.nn.sigmoid(g) * u).astype(BF16)
        ye_ref[...] = (_dot(act, wd_s[cur]) * wsl_ref[:, 0:1]).astype(BF16)


def _experts(xe, wsl, wg, wu, wd, layer):
    rows = BATCH * CAPACITY
    tm = TILE["expert"]
    nt = rows // tm
    last = N_EXPERTS - 1
    wspec = lambda a: pl.BlockSpec((None, None) + a.shape[2:],
                                   lambda ph, i: (layer, jnp.minimum(ph, last), 0, 0))
    data = lambda n: pl.BlockSpec(
        (None, tm, n), lambda ph, i: (jnp.maximum(ph - 1, 0), jnp.where(ph > 0, i, 0), 0))
    return pl.pallas_call(
        functools.partial(_expert_kernel, span=D_MODEL // nt),
        grid=(N_EXPERTS + 1, nt),
        in_specs=[data(D_MODEL), data(LANES), wspec(wg), wspec(wu), wspec(wd)],
        out_specs=data(D_MODEL),
        out_shape=jax.ShapeDtypeStruct((N_EXPERTS, rows, D_MODEL), BF16),
        scratch_shapes=[pltpu.VMEM((2, D_MODEL, D_FF), BF16), pltpu.VMEM((2, D_MODEL, D_FF), BF16),
                        pltpu.VMEM((2, D_FF, D_MODEL), BF16)],
        compiler_params=_params("arbitrary", "arbitrary"),
        name="expert_ffn",
    )(xe.reshape(N_EXPERTS, rows, D_MODEL), wsl.reshape(N_EXPERTS, rows, LANES), wg, wu, wd)


SCATTER_WINDOW = 128
BF16_ROWS = 16


def _combine_kernel(start_ref, x_ref, ye_ref, slot_ref, p_ref, g3_ref, wpg_ref, wpp_ref,
                    gf_ref, out_ref, moe_s, *, final_norm):
    b = pl.program_id(0)
    i = pl.program_id(1)
    slot = slot_ref[...]
    rows = slot.shape[0]
    stride = start_ref.shape[0] // (BATCH * N_EXPERTS)
    base, fits = [], None
    for e in range(N_EXPERTS):
        at = (b * N_EXPERTS + e) * stride + i
        first = start_ref[at]
        lo = jnp.minimum(first // BF16_ROWS * BF16_ROWS, CAPACITY - SCATTER_WINDOW)
        ok = start_ref[at + 1] - lo <= SCATTER_WINDOW
        fits = ok if fits is None else fits & ok
        base.append(lo)

    @pl.when(fits)
    def _():
        c = lax.broadcasted_iota(jnp.int32, (rows, SCATTER_WINDOW), 1)
        parts = []
        for e0 in range(0, N_EXPERTS, 2):
            hot, win = [], []
            for e in (e0, e0 + 1):
                hot.append(jnp.where(slot[:, e:e + 1] - base[e] == c, 1.0, 0.0).astype(BF16))
                lo = pl.multiple_of(base[e], BF16_ROWS)
                win.append(ye_ref[e, pl.ds(lo, SCATTER_WINDOW), :])
            parts.append(_dot(jnp.concatenate(hot, axis=1), jnp.concatenate(win, axis=0)))
        while len(parts) > 1:
            parts = [u + v for u, v in zip(parts[::2], parts[1::2])]
        moe_s[...] = parts[0]

    @pl.when(jnp.logical_not(fits))
    def _():
        c = lax.broadcasted_iota(jnp.int32, (rows, CAPACITY), 1)
        onehot = jnp.concatenate(
            [jnp.where(slot[:, e:e + 1] == c, 1.0, 0.0).astype(BF16) for e in range(N_EXPERTS)],
            axis=1)
        moe_s[...] = _dot(onehot, ye_ref[...].reshape(N_EXPERTS * CAPACITY, D_MODEL))

    acc = x_ref[...] + moe_s[...]
    h = _rms(acc, g3_ref[...]).astype(BF16)
    gate = jax.nn.sigmoid(_dot(h, wpg_ref[...]))
    y = acc + gate * _dot(p_ref[...].astype(BF16), wpp_ref[...])
    if final_norm:
        y = _rms(y, gf_ref[...])
    out_ref[...] = y


def _combine(x, ye, slot_col, starts, p, layer, g3, wpg, wpp, gf, final_norm):
    tm = TILE["combine"]
    nt = SEQ // tm
    p0 = layer * (TOKENS // tm)
    const = lambda a: pl.BlockSpec(a.shape, lambda b, i, s: (0,) * a.ndim,
                                   pipeline_mode=pl.Buffered(1))
    grid_spec = pltpu.PrefetchScalarGridSpec(
        num_scalar_prefetch=1,
        grid=(BATCH, nt),
        in_specs=[pl.BlockSpec((tm, D_MODEL), lambda b, i, s: (b * nt + i, 0)),
                  pl.BlockSpec((N_EXPERTS, None, CAPACITY, D_MODEL), lambda b, i, s: (0, b, 0, 0)),
                  pl.BlockSpec((None, tm, LANES), lambda b, i, s: (b, i, 0)),
                  pl.BlockSpec((tm, PLE_DIM), lambda b, i, s: (p0 + b * nt + i, 0)),
                  const(g3), const(wpg), const(wpp), const(gf)],
        out_specs=pl.BlockSpec((tm, D_MODEL), lambda b, i, s: (b * nt + i, 0)),
        scratch_shapes=[pltpu.VMEM((tm, D_MODEL), F32)])
    return pl.pallas_call(
        functools.partial(_combine_kernel, final_norm=final_norm),
        grid_spec=grid_spec,
        out_shape=jax.ShapeDtypeStruct((TOKENS, D_MODEL), F32),
        compiler_params=_params("parallel", "parallel"),
        name="combine",
    )(starts[:, :, :nt + 1].reshape(-1), x, ye.reshape(N_EXPERTS, BATCH, CAPACITY, D_MODEL),
      slot_col, p, g3, wpg, wpp, gf)


DFT_FINE = 64
DFT_STEP = 4


def _dft_kernel(ca_ref, sa_ref, cb_ref, sb_ref, out_ref):
    cb = cb_ref[...]
    sb = sb_ref[...]
    for r in range(DFT_STEP):
        rows = slice(r * DFT_FINE, (r + 1) * DFT_FINE)
        out_ref[rows, :] = (ca_ref[r] * cb - sa_ref[r] * sb).astype(BF16)


def _dft_tables(n_points, nres):
    blk = SEQ // nres
    t = jnp.arange(SEQ, dtype=jnp.int32).reshape(blk, nres).T
    t = jnp.concatenate([t, t], axis=1).reshape(1, 2 * SEQ)
    is_sin = (jnp.arange(2 * SEQ) // blk % 2 == 1)[None, :]
    coarse = jnp.arange(HALF // DFT_FINE, dtype=jnp.int32)[:, None] * DFT_FINE
    fine = jnp.arange(DFT_FINE, dtype=jnp.int32)[:, None]
    ang = lambda f: ((f * t) % n_points).astype(F32) * (2.0 * math.pi / n_points)
    ca = jnp.cos(ang(coarse))[:, None, :]
    sa = jnp.sin(ang(coarse))[:, None, :]
    cb = jnp.where(is_sin, jnp.sin(ang(fine)), jnp.cos(ang(fine)))
    sb = jnp.where(is_sin, -jnp.cos(ang(fine)), jnp.sin(ang(fine)))
    rows = DFT_STEP * DFT_FINE
    return pl.pallas_call(
        _dft_kernel,
        grid=(HALF // rows,),
        in_specs=[pl.BlockSpec((DFT_STEP, 1, 2 * SEQ), lambda i: (i, 0, 0)),
                  pl.BlockSpec((DFT_STEP, 1, 2 * SEQ), lambda i: (i, 0, 0)),
                  _const_spec(cb.shape), _const_spec(sb.shape)],
        out_specs=pl.BlockSpec((rows, 2 * SEQ), lambda i: (i, 0)),
        out_shape=jax.ShapeDtypeStruct((HALF, 2 * SEQ), BF16),
        compiler_params=_params("parallel"),
        name="dft_tables",
    )(ca, sa, cb, sb)


def _fnet_group_tables():
    gc = D_MIX // FN_GROUPS
    i = lax.broadcasted_iota(jnp.int32, (D_MIX, D_MIX), 0)
    j = lax.broadcasted_iota(jnp.int32, (D_MIX, D_MIX), 1)
    same = (i // gc) == (j // gc)
    ang = (((i % gc) * (j % gc)) % gc).astype(F32) * (2.0 * math.pi / gc)
    return (jnp.where(same, jnp.cos(ang), 0.0).astype(BF16),
            jnp.where(same, jnp.sin(ang), 0.0).astype(BF16))


def _hyena_features():
    t01 = jnp.linspace(0.0, 1.0, SEQ, dtype=F32)[:, None]
    bands = jnp.linspace(1e-4, HY_BANDS - 1, HY_BANDS, dtype=F32)
    ang = 2.0 * math.pi * jnp.arange(SEQ, dtype=F32)[:, None] * bands / SEQ
    z = jnp.concatenate([t01, jnp.cos(ang), -jnp.sin(ang)], axis=-1)
    z = jnp.pad(z, ((0, 0), (0, LANES - HY_EMB)))
    max_decay = math.log(HY_TARGET) / HY_FAST_DECAY
    min_decay = math.log(HY_TARGET) / HY_SLOW_DECAY
    deltas = jnp.linspace(min_decay, max_decay, D_MIX, dtype=F32)
    window = jnp.exp(-t01 * jnp.abs(deltas))
    return z, window


def _rot_cols(w):
    half = w.shape[-1] // 2
    return jnp.concatenate([-w[..., half:], w[..., :half]], axis=-1)


def _rope_tables():
    inv = ROPE_THETA ** (-jnp.arange(0, MLA_ROPE, 2, dtype=F32) / MLA_ROPE)
    ang = jnp.arange(SEQ, dtype=F32)[:, None] * inv
    cos = jnp.concatenate([jnp.cos(ang), jnp.cos(ang)], axis=-1)
    sin = jnp.concatenate([jnp.sin(ang), jnp.sin(ang)], axis=-1)
    scale = (MLA_NOPE + MLA_ROPE) ** -0.5 * LOG2E
    pad = MLA_HEAD_PAD - MLA_NOPE - MLA_ROPE
    one = jnp.ones((SEQ, MLA_NOPE), F32)
    zero = jnp.zeros((SEQ, MLA_NOPE), F32)
    zpad = jnp.zeros((SEQ, pad), F32)
    cosq = jnp.tile(jnp.concatenate([one, cos, zpad], axis=-1) * scale, (1, MLA_HEADS))
    sinq = jnp.tile(jnp.concatenate([zero, sin, zpad], axis=-1) * scale, (1, MLA_HEADS))
    csk = jnp.concatenate([cos, sin, jnp.zeros((SEQ, LANES - 2 * MLA_ROPE), F32)], axis=-1)
    return cosq, sinq, csk


def _mla_weights(w_uq, w_ukv):
    pad = MLA_HEAD_PAD - MLA_NOPE - MLA_ROPE
    wq = w_uq.reshape(MLA_Q_RANK, MLA_HEADS, MLA_NOPE + MLA_ROPE)
    nope, pe = wq[..., :MLA_NOPE], wq[..., MLA_NOPE:]
    zp = jnp.zeros((MLA_Q_RANK, MLA_HEADS, pad), F32)
    wqa = jnp.concatenate([nope, pe, zp], axis=-1).reshape(MLA_Q_RANK, MLA_QK)
    wqb = jnp.concatenate([jnp.zeros_like(nope), _rot_cols(pe), zp], axis=-1)
    wqb = wqb.reshape(MLA_Q_RANK, MLA_QK)
    wkv = w_ukv.reshape(MLA_KV_RANK, MLA_HEADS, MLA_NOPE + MLA_V)
    knope, v = wkv[..., :MLA_NOPE], wkv[..., MLA_NOPE:]
    wk = jnp.concatenate(
        [knope, jnp.zeros((MLA_KV_RANK, MLA_HEADS, MLA_HEAD_PAD - MLA_NOPE), F32)], axis=-1)
    wk = wk.reshape(MLA_KV_RANK, MLA_QK)
    wv = v.reshape(MLA_KV_RANK, MLA_HEADS * MLA_V)
    r = lax.broadcasted_iota(jnp.int32, (LANES, MLA_QK), 0)
    c = lax.broadcasted_iota(jnp.int32, (LANES, MLA_QK), 1)
    epe = jnp.where((r < 2 * MLA_ROPE) & (c % MLA_HEAD_PAD == MLA_NOPE + r % MLA_ROPE), 1.0, 0.0)
    return (wqa.astype(BF16), wqb.astype(BF16), wk.astype(BF16), wv.astype(BF16),
            epe.astype(BF16))


def kernel(x, p, norm1_g, w_in, b_gate, hy_conv_w, hy_conv_b, hf_w1, hf_b1, hf_freq, hf_w2,
           hf_b2, hf_w3, hy_skip, q_norm_g, w_uq, kv_norm_g, w_ukv, rpb, w_br, w_out, norm2_g,
           w_router, w_e_gate, w_e_up, w_e_down, norm3_g, w_ple_gate, w_ple_proj, final_g):
    conv_tab = _dft_tables(2 * SEQ, 4)
    fnet_tab = _dft_tables(SEQ, 2)
    fnet_cg, fnet_sg = _fnet_group_tables()
    zfeat, window = _hyena_features()
    cosq, sinq, csk = _rope_tables()
    row = lambda a: a.reshape(1, -1)

    xt = x.reshape(TOKENS, D_MODEL)
    w_in_t = jnp.swapaxes(w_in, 1, 2)
    for i in range(DEPTH):
        g1 = row(norm1_g[i])
        wa, wna, wgate = _inproj_weights(w_in_t, i)
        wqa, wqb, wk, wv, epe = _mla_weights(w_uq[i], w_ukv[i])

        u_hy, u_fn, q, k, v, naq, nak, nav = _inproj(
            xt, g1, wa, wna, row(q_norm_g[i]), wqa, wqb, row(kv_norm_g[i]), wk, wv, epe,
            cosq, sinq, csk)

        w1 = jnp.pad(hf_w1[i], ((0, LANES - HY_EMB), (0, 0)))
        kf, kny = _hyena_filter(zfeat, window, w1, row(hf_b1[i]), hf_freq[i], hf_w2[i],
                                row(hf_b2[i]), hf_w3[i], conv_tab)
        conv_b = row(hy_conv_b[i])
        z1 = _hyena_stage(u_hy, 0, u_hy, 2, hy_conv_w[i], conv_b, conv_tab, kf[0],
                          kny[0], row(hy_skip[i, 0]), True, F32)
        y_hy = _hyena_stage(u_hy, 1, z1, 0, hy_conv_w[i], conv_b, conv_tab, kf[1],
                            kny[1], row(hy_skip[i, 1]), False, BF16)
        y_fn = _fnet(u_fn, fnet_tab, fnet_cg, fnet_sg)
        y_mla = _mla(q, k, v)
        y_na = _neighborhood(naq, nak, nav, _na_bias_tiles(rpb[i]))
        wr_t = jnp.pad(w_router[i].T, ((0, LANES - N_EXPERTS), (0, 0)))
        xt, hb, logits = _merge(xt, g1, y_hy, y_fn, y_mla, y_na, wgate, row(b_gate[i]),
                                w_br[i].astype(BF16), w_out[i].astype(BF16),
                                row(norm2_g[i]), wr_t)
        slot_row, slot_col, w_row, starts = _select(logits)
        xe, wsl = _gather(slot_row, w_row, hb)
        ye = _experts(xe, wsl, w_e_gate, w_e_up, w_e_down, i)
        xt = _combine(xt, ye, slot_col, starts, p.reshape(DEPTH * TOKENS, PLE_DIM), i,
                      row(norm3_g[i]), w_ple_gate[i].astype(BF16),
                      w_ple_proj[i].astype(BF16), row(final_g), i == DEPTH - 1)
    return xt.reshape(BATCH, SEQ, D_MODEL)
```

```python
import functools
import math

import jax
import jax.numpy as jnp
from jax import lax
from jax.experimental import pallas as pl
from jax.experimental.pallas import tpu as pltpu

F32 = jnp.float32
BF16 = jnp.bfloat16

D_MODEL = 1024
BATCH = 8
SEQ = 2048
DEPTH = 2
TOKENS = BATCH * SEQ

GRID_W = 64
GRID_R = SEQ // GRID_W
D_MIX = 256
N_BRANCH = 4
EPS = 1e-6
HY_ORDER = 2
HY_BANDS = 16
HY_EMB = 2 * HY_BANDS + 1
HY_FFN = 64
HY_TARGET = 1e-2
HY_FAST_DECAY = 0.3
HY_SLOW_DECAY = 1.5
FN_GROUPS = 4
MLA_HEADS = 4
MLA_NOPE = 64
MLA_ROPE = 32
MLA_V = 64
MLA_Q_RANK = 256
MLA_KV_RANK = 128
ROPE_THETA = 10000.0
NA_HEADS = 4
NA_HEAD_DIM = D_MIX // NA_HEADS
NA_WIN_R = 8
NA_WIN_C = 16
N_EXPERTS = 16
CAPACITY = 2 * SEQ // N_EXPERTS
D_FF = 1024
PLE_DIM = 256

HY_COLS = 3 * D_MIX
OFF_FN = HY_COLS
OFF_CQ = OFF_FN + D_MIX
OFF_CKV = OFF_CQ + MLA_Q_RANK
OFF_KPE = OFF_CKV + MLA_KV_RANK
OFF_NA = OFF_KPE + MLA_ROPE
OFF_GATE = OFF_NA + 3 * D_MIX

LANES = 128
MLA_HEAD_PAD = 128
MLA_QK = MLA_HEADS * MLA_HEAD_PAD
WA_COLS = 1536
NEG_BIG = -1e30
LOG2E = math.log2(math.e)

TILE = dict(inproj=512, mla=512, merge=512, expert=512, combine=512)
NA_QROWS = 8
NA_KROWS = 16
NA_QBLK = NA_QROWS * GRID_W
NA_KBLK = NA_KROWS * GRID_W
VMEM_LIMIT = 56 * 1024 * 1024


def _params(*sem):
    return pltpu.CompilerParams(dimension_semantics=sem, vmem_limit_bytes=VMEM_LIMIT)


def _const_spec(shape):
    nd = len(shape)
    return pl.BlockSpec(shape, lambda *_: (0,) * nd, pipeline_mode=pl.Buffered(1))


def _rms(x, g):
    return x * lax.rsqrt(jnp.mean(x * x, axis=-1, keepdims=True) + EPS) * g


def _dot(a, b):
    return jnp.dot(a, b, preferred_element_type=F32)


def _dot_nt(a, b):
    return lax.dot_general(a, b, (((1,), (1,)), ((), ())), preferred_element_type=F32)


def _split2(x):
    hi = x.astype(BF16)
    lo = (x - hi.astype(F32)).astype(BF16)
    return hi, lo


def _inproj_kernel(x_ref, g1_ref, wa_ref, wna_ref, qg_ref, wqa_ref, wqb_ref, kvg_ref,
                   wk_ref, wv_ref, epe_ref, cosq_ref, sinq_ref, csk_ref,
                   uhy_ref, ufn_ref, q_ref, k_ref, v_ref, naq_ref, nak_ref, nav_ref):
    h = _rms(x_ref[...], g1_ref[...]).astype(BF16)
    ua = _dot_nt(h, wa_ref[...])
    uhy_ref[...] = ua[:, :HY_COLS]
    ufn_ref[...] = ua[:, OFF_FN:OFF_CQ]
    cqn = _rms(ua[:, OFF_CQ:OFF_CKV], qg_ref[...]).astype(BF16)
    q = _dot(cqn, wqa_ref[...]) * cosq_ref[...] + _dot(cqn, wqb_ref[...]) * sinq_ref[...]
    q_ref[...] = q.astype(BF16)
    kvn = _rms(ua[:, OFF_CKV:OFF_KPE], kvg_ref[...]).astype(BF16)
    kpe = ua[:, OFF_KPE:WA_COLS] * csk_ref[...]
    k = _dot(kvn, wk_ref[...]) + _dot(kpe.astype(BF16), epe_ref[...])
    k_ref[...] = k.astype(BF16)
    v_ref[...] = _dot(kvn, wv_ref[...]).astype(BF16)
    una = _dot_nt(h, wna_ref[...])
    naq_ref[...] = (una[:, :D_MIX] * (NA_HEAD_DIM ** -0.5 * LOG2E)).astype(BF16)
    nak_ref[...] = una[:, D_MIX:2 * D_MIX].astype(BF16)
    nav_ref[...] = una[:, 2 * D_MIX:].astype(BF16)


def _inproj(x, g1, wa, wna, qg, wqa, wqb, kvg, wk, wv, epe, cosq, sinq, csk):
    tm = TILE["inproj"]
    nt = SEQ // tm
    row = lambda n: pl.BlockSpec((tm, n), lambda i: (i, 0))
    pos = lambda n: pl.BlockSpec((tm, n), lambda i: (i % nt, 0))
    outs = [(HY_COLS, F32), (D_MIX, F32), (MLA_QK, BF16), (MLA_QK, BF16), (D_MIX, BF16),
            (D_MIX, BF16), (D_MIX, BF16), (D_MIX, BF16)]
    return pl.pallas_call(
        _inproj_kernel,
        grid=(TOKENS // tm,),
        in_specs=[row(D_MODEL), _const_spec(g1.shape), _const_spec(wa.shape),
                  _const_spec(wna.shape), _const_spec(qg.shape), _const_spec(wqa.shape),
                  _const_spec(wqb.shape), _const_spec(kvg.shape), _const_spec(wk.shape),
                  _const_spec(wv.shape), _const_spec(epe.shape),
                  pos(MLA_QK), pos(MLA_QK), pos(LANES)],
        out_specs=[row(n) for n, _ in outs],
        out_shape=[jax.ShapeDtypeStruct((TOKENS, n), dt) for n, dt in outs],
        compiler_params=_params("parallel"),
        name="inproj",
    )(x, g1, wa, wna, qg, wqa, wqb, kvg, wk, wv, epe, cosq, sinq, csk)


WPREP_ROWS = 512


def _inproj_weights_kernel(w_ref, wa_ref, wna_ref, wg_ref):
    i = pl.program_id(0)

    @pl.when(i == 0)
    def _():
        half = MLA_ROPE // 2
        wa_ref[:OFF_NA, :] = w_ref[:OFF_NA, :].astype(BF16)
        wa_ref[OFF_NA:OFF_NA + half, :] = (-w_ref[OFF_KPE + half:OFF_NA, :]).astype(BF16)
        wa_ref[OFF_NA + half:OFF_NA + MLA_ROPE, :] = w_ref[OFF_KPE:OFF_KPE + half, :].astype(BF16)
        wa_ref[OFF_NA + MLA_ROPE:, :] = jnp.zeros((WA_COLS - OFF_NA - MLA_ROPE, D_MODEL), BF16)
        wna_ref[...] = w_ref[OFF_NA:OFF_GATE, :].astype(BF16)

    start = pl.multiple_of(OFF_GATE + i * WPREP_ROWS, MLA_ROPE)
    wg_ref[...] = w_ref[pl.ds(start, WPREP_ROWS), :].astype(BF16)


def _inproj_weights(w_in_t, layer):
    n_in = w_in_t.shape[1]
    rows = (WA_COLS, OFF_GATE - OFF_NA, n_in - OFF_GATE)
    whole = lambda n: pl.BlockSpec((n, D_MODEL), lambda i: (0, 0))
    return pl.pallas_call(
        _inproj_weights_kernel,
        grid=(rows[2] // WPREP_ROWS,),
        in_specs=[pl.BlockSpec((None, n_in, D_MODEL), lambda i: (layer, 0, 0),
                               pipeline_mode=pl.Buffered(1))],
        out_specs=[whole(rows[0]), whole(rows[1]),
                   pl.BlockSpec((WPREP_ROWS, D_MODEL), lambda i: (i, 0))],
        out_shape=[jax.ShapeDtypeStruct((n, D_MODEL), BF16) for n in rows],
        compiler_params=_params("arbitrary"),
        name="inproj_weights",
    )(w_in_t)


HALF = SEQ // 2
HY_SEQS = 1


def _residue_rows(ref, r, nres):
    rows = pl.ds(r, SEQ // nres, stride=nres)
    if isinstance(ref, tuple):
        return jnp.concatenate([h[rows, :] for h in ref], axis=1)
    return jnp.concatenate([ref[j, rows, :] for j in range(ref.shape[0])], axis=1)


def _store_slabs(ref, value, rows=slice(None)):
    for j in range(ref.shape[0]):
        ref[j, rows, :] = value[:, j * LANES:(j + 1) * LANES]


def _table_cols(cs_ref, r, nres, part=None):
    blk = SEQ // nres
    lo = 2 * blk * r
    if part is None:
        return cs_ref[:, lo:lo + 2 * blk]
    return cs_ref[:, lo + part * blk:lo + (part + 1) * blk]


def _dft_fwd(cs_ref, src_ref, want_cos=True, want_sin=True, split=False):
    def prod(part, z, r):
        tab = _table_cols(cs_ref, r, 4, part)
        if split:
            hi, lo = _split2(z)
            return _dot(tab, hi) + _dot(tab, lo)
        return _dot(tab, z.astype(BF16))

    pc, ps = [None] * 4, [None] * 4
    for r in range(4):
        z = _residue_rows(src_ref, r, 4)
        odd = r % 2 == 1
        if want_cos or odd:
            pc[r] = prod(0, z, r)
        if want_sin or odd:
            ps[r] = prod(1, z, r)
    a = b = None
    if want_cos:
        a = ((pc[0] + pc[2]) + (pc[1] + pc[3]), (pc[0] - pc[2]) + (ps[3] - ps[1]))
    if want_sin:
        b = ((ps[0] + ps[2]) + (ps[1] + ps[3]), (ps[0] - ps[2]) + (pc[1] - pc[3]))
    return a, b


def _dft_inv(cs_ref, yre_ref, yim_ref):
    g, h = [], {}
    for r in range(4):
        yr = _residue_rows(yre_ref, r, 4).astype(BF16)
        yi = _residue_rows(yim_ref, r, 4).astype(BF16)
        tab = _table_cols(cs_ref, r, 4)
        g.append(_dot(tab, jnp.concatenate([yr, -yi], axis=0)))
        if r % 2 == 1:
            h[r] = _dot(tab, jnp.concatenate([yi, yr], axis=0))
    return (g[0] + g[2]) + (g[1] + g[3]), (g[0] - g[2]) + (h[3] - h[1])


def _hyena_filter_kernel(z_ref, win_ref, w1_ref, b1_ref, freq_ref, w2_ref, b2_ref, w3_ref,
                         cs_ref, kf_ref, kny_ref, ksum_s, kdif_s):
    hp = lax.Precision.HIGHEST
    freq = freq_ref[...]
    hf = jnp.sin(freq[0:1] * (jnp.dot(z_ref[...], w1_ref[...], precision=hp,
                                      preferred_element_type=F32) + b1_ref[...]))
    hf = jnp.sin(freq[1:2] * (jnp.dot(hf, w2_ref[...], precision=hp,
                                      preferred_element_type=F32) + b2_ref[...]))
    hf = jnp.dot(hf, w3_ref[...], precision=hp, preferred_element_type=F32)
    win = win_ref[...]
    t = lax.broadcasted_iota(jnp.int32, (SEQ, D_MIX), 0)
    sgn = (1 - 2 * (t & 1)).astype(F32)
    fwd = hf[:, :D_MIX] * win
    bwd = jnp.where(t == 0, 0.0, hf[:, D_MIX:] * win)
    nrm = lax.rsqrt(jnp.sum(fwd * fwd + bwd * bwd, axis=0, keepdims=True) + EPS)
    ksum = (fwd + bwd) * nrm
    _store_slabs(ksum_s, ksum)
    _store_slabs(kdif_s, (bwd - fwd) * nrm)
    kre, _ = _dft_fwd(cs_ref, ksum_s, want_sin=False, split=True)
    _, kim = _dft_fwd(cs_ref, kdif_s, want_cos=False, split=True)
    wf = 2.0 / (2 * SEQ)
    for part, spec in enumerate((kre, kim)):
        lo = spec[0] * wf
        kf_ref[part, :HALF, :] = lo
        kf_ref[part, 0:1, :] = lo[0:1] * 0.5
        kf_ref[part, HALF:, :] = spec[1] * wf
    kny = jnp.sum(ksum * sgn, axis=0, keepdims=True) * (1.0 / (2 * SEQ))
    kny_ref[...] = jnp.broadcast_to(kny, (8, D_MIX))


def _hyena_filter(zfeat, window, w1, b1, freq, w2, b2, w3, table):
    consts = (zfeat, window, w1, b1, freq, w2, b2)
    return pl.pallas_call(
        _hyena_filter_kernel,
        grid=(HY_ORDER,),
        in_specs=[_const_spec(a.shape) for a in consts]
        + [pl.BlockSpec((HY_FFN, 2 * D_MIX), lambda o: (0, o)),
           _const_spec(table.shape)],
        out_specs=[pl.BlockSpec((None, 2, SEQ, D_MIX), lambda o: (o, 0, 0, 0)),
                   pl.BlockSpec((None, 8, D_MIX), lambda o: (o, 0, 0))],
        out_shape=[jax.ShapeDtypeStruct((HY_ORDER, 2, SEQ, D_MIX), F32),
                   jax.ShapeDtypeStruct((HY_ORDER, 8, D_MIX), F32)],
        scratch_shapes=[pltpu.VMEM((D_MIX // LANES, SEQ, LANES), F32)] * 2,
        compiler_params=_params("arbitrary"),
        name="hyena_filter",
    )(*consts, w3, table)


def _short_conv(u, w, b):
    t = lax.broadcasted_iota(jnp.int32, u.shape, 0)
    prev = jnp.where(t == 0, 0.0, pltpu.roll(u, 1, 0))
    nxt = jnp.where(t == SEQ - 1, 0.0, pltpu.roll(u, SEQ - 1, 0))
    return prev * w[0:1] + u * w[1:2] + nxt * w[2:3] + b


def _hyena_stage_kernel(gate_ref, src_ref, wg_ref, bg_ref, ws_ref, bs_ref, cs_ref,
                        kf_ref, kny_ref, skip_ref, out_ref, z_s, yre_s, yim_s, gate_s, rest_s,
                        *, conv_src):
    for q in range(HY_SEQS):
        seq = slice(q * SEQ, (q + 1) * SEQ)
        zq, yre_q, yim_q = z_s.at[q], yre_s.at[q], yim_s.at[q]
        z = src_ref[seq, :]
        if conv_src:
            z = _short_conv(z, ws_ref[...], bs_ref[...])
        _store_slabs(zq, z)
        t = lax.broadcasted_iota(jnp.int32, z.shape, 0)
        sgn = (1 - 2 * (t & 1)).astype(F32)
        nyq = jnp.sum(z * sgn, axis=0, keepdims=True) * kny_ref[0:1]
        gate = _short_conv(gate_ref[seq, :], wg_ref[...], bg_ref[...])
        gate_s[seq, :] = gate
        rest_s[seq, :] = gate * (sgn * nyq + z * skip_ref[...])
        a, b = _dft_fwd(cs_ref, zq)
        for half in range(2):
            rows = slice(half * HALF, (half + 1) * HALF)
            kre = kf_ref[0, rows, :]
            kim = kf_ref[1, rows, :]
            _store_slabs(yre_q, a[half] * kre + b[half] * kim, rows)
            _store_slabs(yim_q, a[half] * kim - b[half] * kre, rows)
        for half, y in enumerate(_dft_inv(cs_ref, yre_q, yim_q)):
            rows = slice(q * SEQ + half * HALF, q * SEQ + (half + 1) * HALF)
            out_ref[rows, :] = (gate_s[rows, :] * y + rest_s[rows, :]).astype(out_ref.dtype)


def _hyena_stage(u_hy, gate_blk, src, src_blk, conv_w, conv_b, table, kf, kny, skip,
                 conv_src, out_dtype):
    rows = HY_SEQS * SEQ
    col = lambda blk: pl.BlockSpec((rows, D_MIX), lambda b: (b, blk))
    wcol = lambda blk, r: pl.BlockSpec((r, D_MIX), lambda b: (0, blk))
    ws_blk = src_blk if conv_src else 0
    return pl.pallas_call(
        functools.partial(_hyena_stage_kernel, conv_src=conv_src),
        grid=(BATCH // HY_SEQS,),
        in_specs=[col(gate_blk), col(src_blk), wcol(gate_blk, 3), wcol(gate_blk, 1),
                  wcol(ws_blk, 3), wcol(ws_blk, 1), _const_spec(table.shape),
                  _const_spec(kf.shape), _const_spec(kny.shape),
                  _const_spec(skip.shape)],
        out_specs=pl.BlockSpec((rows, D_MIX), lambda b: (b, 0)),
        out_shape=jax.ShapeDtypeStruct((TOKENS, D_MIX), out_dtype),
        scratch_shapes=[pltpu.VMEM((HY_SEQS, D_MIX // LANES, SEQ, LANES), F32)] * 3
        + [pltpu.VMEM((rows, D_MIX), F32)] * 2,
        compiler_params=_params("parallel"),
        name="hyena_stage",
    )(u_hy, src, conv_w, conv_b, conv_w, conv_b, table, kf, kny, skip)


def _fnet_kernel(xa_ref, xb_ref, cs_ref, cg_ref, sg_ref, out_ref):
    parts = []
    for r in range(2):
        xb = _residue_rows((xa_ref, xb_ref), r, 2).astype(BF16)
        xc = _dot(xb, cg_ref[...]).astype(BF16)
        xs = _dot(xb, sg_ref[...]).astype(BF16)
        parts.append(_dot(_table_cols(cs_ref, r, 2), jnp.concatenate([xc, -xs], axis=0)))
    scale = (SEQ * D_MIX // FN_GROUPS) ** -0.5
    out_ref[:HALF, :] = ((parts[0] + parts[1]) * scale).astype(out_ref.dtype)
    out_ref[HALF:, :] = ((parts[0] - parts[1]) * scale).astype(out_ref.dtype)


def _fnet(u_fn, table, cg, sg):
    return pl.pallas_call(
        _fnet_kernel,
        grid=(BATCH,),
        in_specs=[pl.BlockSpec((SEQ, LANES), lambda b: (b, 0)),
                  pl.BlockSpec((SEQ, LANES), lambda b: (b, 1)), _const_spec(table.shape),
                  _const_spec(cg.shape), _const_spec(sg.shape)],
        out_specs=pl.BlockSpec((SEQ, D_MIX), lambda b: (b, 0)),
        out_shape=jax.ShapeDtypeStruct((TOKENS, D_MIX), BF16),
        compiler_params=_params("parallel"),
        name="fnet",
    )(u_fn, u_fn, table, cg, sg)


def _softmax2_pv(s2, v):
    m = jnp.max(s2, axis=-1, keepdims=True)
    p = jnp.exp2(s2 - m)
    l = jnp.sum(p, axis=-1, keepdims=True)
    return _dot(p.astype(BF16), v) / l


def _mla_kernel(q_ref, k_ref, v_ref, out_ref):
    v = v_ref[...]
    head = lax.broadcasted_iota(jnp.int32, out_ref.shape, 1) // MLA_V
    acc = jnp.zeros(out_ref.shape, F32)
    for h in range(MLA_HEADS):
        sl = slice(h * MLA_HEAD_PAD, (h + 1) * MLA_HEAD_PAD)
        s2 = _dot_nt(q_ref[:, sl], k_ref[:, sl])
        acc = jnp.where(head == h, _softmax2_pv(s2, v), acc)
    out_ref[...] = acc.astype(out_ref.dtype)


def _mla(q, k, v):
    tm = TILE["mla"]
    nt = SEQ // tm
    return pl.pallas_call(
        _mla_kernel,
        grid=(BATCH, nt),
        in_specs=[pl.BlockSpec((tm, MLA_QK), lambda b, i: (b * nt + i, 0)),
                  pl.BlockSpec((SEQ, MLA_QK), lambda b, i: (b, 0)),
                  pl.BlockSpec((SEQ, D_MIX), lambda b, i: (b, 0))],
        out_specs=pl.BlockSpec((tm, D_MIX), lambda b, i: (b * nt + i, 0)),
        out_shape=jax.ShapeDtypeStruct((TOKENS, D_MIX), BF16),
        compiler_params=_params("parallel", "parallel"),
        name="mla_attention",
    )(q, k, v)


def _na_key_row0(j):
    return jnp.clip(j * NA_QROWS - NA_WIN_R // 2, 0, GRID_R - NA_KROWS)


NA_PAIRS = 2 * NA_WIN_R


def _na_kernel(q_ref, k_ref, v_ref, tile_ref, out_ref, bias_s):
    j = pl.program_id(0)
    krow0 = _na_key_row0(j)

    @pl.when(pl.program_id(1) == 0)
    def _():
        rq = j * NA_QROWS + lax.broadcasted_iota(jnp.int32, (NA_QBLK, NA_KBLK), 0) // GRID_W
        rk = krow0 + lax.broadcasted_iota(jnp.int32, (NA_QBLK, NA_KBLK), 1) // GRID_W
        rs = jnp.clip(rq - NA_WIN_R // 2, 0, GRID_R - NA_WIN_R)
        rowmask = jnp.where(rk < rs, NEG_BIG, jnp.where(rk >= rs + NA_WIN_R, NEG_BIG, 0.0))
        base = krow0 - j * NA_QROWS + NA_WIN_R
        for h in range(NA_HEADS):
            bias = jnp.concatenate(
                [jnp.concatenate(
                    [tile_ref[h, jnp.clip(base + 2 * kp - r, 0, NA_PAIRS - 1)]
                     for kp in range(NA_KROWS // 2)], axis=1)
                 for r in range(NA_QROWS)], axis=0)
            bias_s[h] = bias + rowmask

    off = pl.multiple_of(krow0 * GRID_W, GRID_W)
    q = q_ref[...]
    k = k_ref[pl.ds(off, NA_KBLK), :]
    v = v_ref[pl.ds(off, NA_KBLK), :]
    head = lax.broadcasted_iota(jnp.int32, (NA_QBLK, D_MIX), 1) // NA_HEAD_DIM
    acc = jnp.zeros((NA_QBLK, D_MIX), F32)
    for h in range(NA_HEADS):
        qh = jnp.where(head == h, q, jnp.zeros_like(q))
        s2 = _dot_nt(qh, k) + bias_s[h]
        acc = jnp.where(head == h, _softmax2_pv(s2, v), acc)
    out_ref[...] = acc.astype(out_ref.dtype)


def _neighborhood(q, k, v, tiles):
    nj = SEQ // NA_QBLK
    return pl.pallas_call(
        _na_kernel,
        grid=(nj, BATCH),
        in_specs=[pl.BlockSpec((NA_QBLK, D_MIX), lambda j, b: (b * nj + j, 0)),
                  pl.BlockSpec((SEQ, D_MIX), lambda j, b: (b, 0)),
                  pl.BlockSpec((SEQ, D_MIX), lambda j, b: (b, 0)),
                  _const_spec(tiles.shape)],
        out_specs=pl.BlockSpec((NA_QBLK, D_MIX), lambda j, b: (b * nj + j, 0)),
        out_shape=jax.ShapeDtypeStruct((TOKENS, D_MIX), BF16),
        scratch_shapes=[pltpu.VMEM((NA_HEADS, NA_QBLK, NA_KBLK), F32)],
        compiler_params=_params("parallel", "arbitrary"),
        name="neighborhood_attention",
    )(q, k, v, tiles)


def _na_bias_tiles(rpb):
    c = jnp.arange(GRID_W)
    cs = jnp.clip(c - NA_WIN_C // 2, 0, GRID_W - NA_WIN_C)
    col_ok = (c[None, :] >= cs[:, None]) & (c[None, :] < cs[:, None] + NA_WIN_C)
    dc = jnp.clip(c[None, :] - c[:, None] + (NA_WIN_C - 1), 0, 2 * NA_WIN_C - 2)
    pick = (dc[None] == jnp.arange(2 * NA_WIN_C - 1)[:, None, None]).astype(F32)
    t = jnp.einsum('hrd,dqk->hrqk', rpb.astype(F32), pick, precision=lax.Precision.HIGHEST)
    t = jnp.where(col_ok, t * LOG2E, NEG_BIG)
    t = jnp.pad(t, ((0, 0), (1, 1), (0, 0), (0, 0)))
    return jnp.concatenate([t[:, :-1], t[:, 1:]], axis=-1)


def _merge_kernel(x_ref, g1_ref, yhy_ref, yfn_ref, ymla_ref, yna_ref, wg_ref, bg_ref,
                  wbr_ref, wout_ref, g2_ref, wr_ref, out_ref, hb_ref, logit_ref):
    x = x_ref[...]
    h = _rms(x, g1_ref[...]).astype(BF16)
    merged = jnp.zeros(x.shape, F32)
    for n, y_ref in enumerate((yhy_ref, yfn_ref, ymla_ref, yna_ref)):
        sl = slice(n * D_MODEL, (n + 1) * D_MODEL)
        gate = jax.nn.sigmoid(_dot_nt(h, wg_ref[sl, :]) + bg_ref[:, sl])
        merged = merged + gate * _dot(y_ref[...], wbr_ref[n])
    x1 = x + _dot(merged.astype(BF16), wout_ref[...])
    out_ref[...] = x1
    h_hi, h_lo = _split2(_rms(x1, g2_ref[...]))
    hb_ref[...] = h_hi
    w_hi, w_lo = _split2(wr_ref[...])
    logits = _dot_nt(w_hi, h_hi) + _dot_nt(w_hi, h_lo) + _dot_nt(w_lo, h_hi)
    logit_ref[...] = logits[:N_EXPERTS]


def _merge(x, g1, yhy, yfn, ymla, yna, wg, bg, wbr, wout, g2, wr_t):
    tm = TILE["merge"]
    nt = SEQ // tm
    row = lambda n: pl.BlockSpec((tm, n), lambda i: (i, 0))
    return pl.pallas_call(
        _merge_kernel,
        grid=(TOKENS // tm,),
        in_specs=[row(D_MODEL), _const_spec(g1.shape), row(D_MIX), row(D_MIX), row(D_MIX),
                  row(D_MIX), _const_spec(wg.shape), _const_spec(bg.shape),
                  _const_spec(wbr.shape), _const_spec(wout.shape), _const_spec(g2.shape),
                  _const_spec(wr_t.shape)],
        out_specs=[row(D_MODEL), row(D_MODEL),
                   pl.BlockSpec((None, N_EXPERTS, tm), lambda i: (i // nt, 0, i % nt))],
        out_shape=[jax.ShapeDtypeStruct((TOKENS, D_MODEL), F32),
                   jax.ShapeDtypeStruct((TOKENS, D_MODEL), BF16),
                   jax.ShapeDtypeStruct((BATCH, N_EXPERTS, SEQ), F32)],
        compiler_params=_params("parallel"),
        name="merge",
    )(x, g1, yhy, yfn, ymla, yna, wg, bg, wbr, wout, g2, wr_t)


def _prefix_count(m):
    r = lax.broadcasted_iota(jnp.int32, (LANES, LANES), 0)
    c = lax.broadcasted_iota(jnp.int32, (LANES, LANES), 1)
    upper = jnp.where(r < c, 1.0, 0.0).astype(BF16)
    run = jnp.zeros((m.shape[0], 1), F32)
    parts = []
    for i in range(SEQ // LANES):
        chunk = m[:, i * LANES:(i + 1) * LANES]
        parts.append(_dot(chunk.astype(BF16), upper) + run)
        run = run + jnp.sum(chunk, axis=1, keepdims=True)
    return jnp.concatenate(parts, axis=1)


SELECT_MAX_ITERS = 192


def _select_kernel(logit_ref, slot_row_ref, slot_col_ref, w_row_ref, start_ref, slot_s, w_s):
    b = pl.program_id(0)
    rows = BATCH * N_EXPERTS

    @pl.when(b == 0)
    def _():
        logits = logit_ref[...]
        ex = jnp.exp(logits - jnp.max(logits, axis=1, keepdims=True))
        aff = (ex / jnp.sum(ex, axis=1, keepdims=True)).reshape(rows, SEQ)

        def bisect(c):
            it, lo, hi, _ = c
            mid = 0.5 * (lo + hi)
            cnt = jnp.sum(jnp.where(aff >= mid, 1.0, 0.0), axis=1, keepdims=True)
            moving = jnp.where(mid == lo, 0.0, jnp.where(mid == hi, 0.0, 1.0))
            enough = cnt >= CAPACITY
            return (it + 1, jnp.where(enough, mid, lo), jnp.where(enough, hi, mid),
                    (jnp.max(moving) > 0).astype(jnp.int32))

        _, lo, hi, _ = lax.while_loop(
            lambda c: (c[0] < SELECT_MAX_ITERS) & (c[3] > 0), bisect,
            (jnp.int32(0), jnp.zeros((rows, 1), F32), jnp.full((rows, 1), 2.0, F32),
             jnp.int32(1)))
        above = jnp.where(aff >= hi, 1.0, 0.0)
        band = jnp.where(aff >= lo, 1.0, 0.0) - above
        need = CAPACITY - jnp.sum(above, axis=1, keepdims=True)
        sel = above + band * jnp.where(_prefix_count(band) < need, 1.0, 0.0)
        slot_s[...] = jnp.where(sel > 0, _prefix_count(sel), -1.0)
        w_s[...] = sel * aff

    r0 = pl.multiple_of(b * N_EXPERTS, N_EXPERTS)
    slot = slot_s[pl.ds(r0, N_EXPERTS), :]
    slot_row_ref[...] = slot.astype(jnp.int32)
    pad = jnp.full((LANES - N_EXPERTS, SEQ), -1.0, F32)
    slot_col_ref[...] = jnp.concatenate([slot, pad], axis=0).T.astype(jnp.int32)
    w_row_ref[...] = w_s[pl.ds(r0, N_EXPERTS), :]
    tm = TILE["combine"]
    token = lax.broadcasted_iota(jnp.int32, slot.shape, 1)
    lane = lax.broadcasted_iota(jnp.int32, (N_EXPERTS, LANES), 1)
    starts = jnp.where(lane == SEQ // tm, float(CAPACITY), 0.0)
    for i in range(1, SEQ // tm):
        before = jnp.where(slot >= 0, jnp.where(token < i * tm, 1.0, 0.0), 0.0)
        starts = starts + jnp.where(lane == i, jnp.sum(before, axis=1, keepdims=True), 0.0)
    start_ref[...] = starts.astype(jnp.int32)


def _select(logits):
    return pl.pallas_call(
        _select_kernel,
        grid=(BATCH,),
        in_specs=[_const_spec(logits.shape)],
        out_specs=[pl.BlockSpec((None, N_EXPERTS, SEQ), lambda b: (b, 0, 0)),
                   pl.BlockSpec((None, SEQ, LANES), lambda b: (b, 0, 0)),
                   pl.BlockSpec((None, N_EXPERTS, SEQ), lambda b: (b, 0, 0)),
                   pl.BlockSpec((None, N_EXPERTS, LANES), lambda b: (b, 0, 0))],
        out_shape=[jax.ShapeDtypeStruct((BATCH, N_EXPERTS, SEQ), jnp.int32),
                   jax.ShapeDtypeStruct((BATCH, SEQ, LANES), jnp.int32),
                   jax.ShapeDtypeStruct((BATCH, N_EXPERTS, SEQ), F32),
                   jax.ShapeDtypeStruct((BATCH, N_EXPERTS, LANES), jnp.int32)],
        scratch_shapes=[pltpu.VMEM((BATCH * N_EXPERTS, SEQ), F32),
                        pltpu.VMEM((BATCH * N_EXPERTS, SEQ), F32)],
        compiler_params=_params("arbitrary"),
        name="expert_select",
    )(logits)


SLOT_WINDOW = 128
BF16_ROWS = 16


def _tile_windows(start_ref, b, i):
    stride = start_ref.shape[0] // (BATCH * N_EXPERTS)
    base, fits = [], None
    for e in range(N_EXPERTS):
        at = (b * N_EXPERTS + e) * stride + i
        lo = jnp.minimum(start_ref[at] // BF16_ROWS * BF16_ROWS, CAPACITY - SLOT_WINDOW)
        ok = start_ref[at + 1] - lo <= SLOT_WINDOW
        fits = ok if fits is None else fits & ok
        base.append(pl.multiple_of(lo, BF16_ROWS))
    return base, fits


def _gather_kernel(start_ref, slot_ref, w_ref, hb_ref, xe_ref, wsl_ref):
    b = pl.program_id(0)
    i = pl.program_id(1)
    tokens = slot_ref.shape[1]

    @pl.when(i == 0)
    def _():
        xe_ref[...] = jnp.zeros(xe_ref.shape, BF16)
        wsl_ref[...] = jnp.zeros(wsl_ref.shape, F32)

    def add_rows(e, rows, hit, picked):
        xe_ref[e, rows, :] = xe_ref[e, rows, :] + picked.astype(BF16)
        wslot = jnp.sum(jnp.where(hit, w_ref[e:e + 1, :], 0.0), axis=1, keepdims=True)
        wsl_ref[e, rows, :] = wsl_ref[e, rows, :] + jnp.broadcast_to(wslot, (hit.shape[0], LANES))

    base, fits = _tile_windows(start_ref, b, i)

    @pl.when(fits)
    def _():
        c = lax.broadcasted_iota(jnp.int32, (SLOT_WINDOW, tokens), 0)
        hits = [slot_ref[e:e + 1, :] - base[e] == c for e in range(N_EXPERTS)]
        onehot = jnp.concatenate([jnp.where(m, 1.0, 0.0).astype(BF16) for m in hits], axis=0)
        picked = _dot(onehot, hb_ref[...])
        for e in range(N_EXPERTS):
            add_rows(e, pl.ds(base[e], SLOT_WINDOW), hits[e],
                     picked[e * SLOT_WINDOW:(e + 1) * SLOT_WINDOW])

    @pl.when(jnp.logical_not(fits))
    def _():
        c = lax.broadcasted_iota(jnp.int32, (CAPACITY, tokens), 0)
        for e in range(N_EXPERTS):
            hit = slot_ref[e:e + 1, :] == c
            picked = _dot(jnp.where(hit, 1.0, 0.0).astype(BF16), hb_ref[...])
            add_rows(e, slice(None), hit, picked)


def _gather(slot_row, w_row, starts, hb):
    tm = TILE["combine"]
    nt = SEQ // tm
    grid_spec = pltpu.PrefetchScalarGridSpec(
        num_scalar_prefetch=1,
        grid=(BATCH, nt),
        in_specs=[pl.BlockSpec((None, N_EXPERTS, tm), lambda b, i, s: (b, 0, i)),
                  pl.BlockSpec((None, N_EXPERTS, tm), lambda b, i, s: (b, 0, i)),
                  pl.BlockSpec((tm, D_MODEL), lambda b, i, s: (b * nt + i, 0))],
        out_specs=[pl.BlockSpec((N_EXPERTS, None, CAPACITY, D_MODEL), lambda b, i, s: (0, b, 0, 0)),
                   pl.BlockSpec((N_EXPERTS, None, CAPACITY, LANES), lambda b, i, s: (0, b, 0, 0))])
    return pl.pallas_call(
        _gather_kernel,
        grid_spec=grid_spec,
        out_shape=[jax.ShapeDtypeStruct((N_EXPERTS, BATCH, CAPACITY, D_MODEL), BF16),
                   jax.ShapeDtypeStruct((N_EXPERTS, BATCH, CAPACITY, LANES), F32)],
        compiler_params=_params("parallel", "arbitrary"),
        name="expert_gather",
    )(starts[:, :, :nt + 1].reshape(-1), slot_row, w_row, hb)


def _expert_kernel(xe_ref, wsl_ref, wg_ref, wu_ref, wd_ref, ye_ref, wg_s, wu_s, wd_s, *, span):
    ph = pl.program_id(0)
    i = pl.program_id(1)
    rows = pl.ds(pl.multiple_of(i * span, span), span)
    nxt = ph % 2
    wg_s[nxt, rows, :] = wg_ref[rows, :].astype(BF16)
    wu_s[nxt, rows, :] = wu_ref[rows, :].astype(BF16)
    wd_s[nxt, rows, :] = wd_ref[rows, :].astype(BF16)

    @pl.when(ph > 0)
    def _():
        cur = (ph + 1) % 2
        xe = xe_ref[...]
        g = _dot(xe, wg_s[cur])
        u = _dot(xe, wu_s[cur])
        act = (g * jax.nn.sigmoid(g) * u).astype(BF16)
        ye_ref[...] = (_dot(act, wd_s[cur]) * wsl_ref[:, 0:1]).astype(BF16)


def _experts(xe, wsl, wg, wu, wd, layer):
    rows = BATCH * CAPACITY
    tm = TILE["expert"]
    nt = rows // tm
    last = N_EXPERTS - 1
    wspec = lambda a: pl.BlockSpec((None, None) + a.shape[2:],
                                   lambda ph, i: (layer, jnp.minimum(ph, last), 0, 0))
    data = lambda n: pl.BlockSpec(
        (None, tm, n), lambda ph, i: (jnp.maximum(ph - 1, 0), jnp.where(ph > 0, i, 0), 0))
    return pl.pallas_call(
        functools.partial(_expert_kernel, span=D_MODEL // nt),
        grid=(N_EXPERTS + 1, nt),
        in_specs=[data(D_MODEL), data(LANES), wspec(wg), wspec(wu), wspec(wd)],
        out_specs=data(D_MODEL),
        out_shape=jax.ShapeDtypeStruct((N_EXPERTS, rows, D_MODEL), BF16),
        scratch_shapes=[pltpu.VMEM((2, D_MODEL, D_FF), BF16), pltpu.VMEM((2, D_MODEL, D_FF), BF16),
                        pltpu.VMEM((2, D_FF, D_MODEL), BF16)],
        compiler_params=_params("arbitrary", "arbitrary"),
        name="expert_ffn",
    )(xe.reshape(N_EXPERTS, rows, D_MODEL), wsl.reshape(N_EXPERTS, rows, LANES), wg, wu, wd)


def _combine_kernel(start_ref, x_ref, ye_ref, slot_ref, p_ref, g3_ref, wpg_ref, wpp_ref,
                    gf_ref, out_ref, moe_s, *, final_norm):
    slot = slot_ref[...]
    rows = slot.shape[0]
    base, fits = _tile_windows(start_ref, pl.program_id(0), pl.program_id(1))

    @pl.when(fits)
    def _():
        c = lax.broadcasted_iota(jnp.int32, (rows, SLOT_WINDOW), 1)
        parts = []
        for e0 in range(0, N_EXPERTS, 2):
            hot, win = [], []
            for e in (e0, e0 + 1):
                hot.append(jnp.where(slot[:, e:e + 1] - base[e] == c, 1.0, 0.0).astype(BF16))
                win.append(ye_ref[e, pl.ds(base[e], SLOT_WINDOW), :])
            parts.append(_dot(jnp.concatenate(hot, axis=1), jnp.concatenate(win, axis=0)))
        while len(parts) > 1:
            parts = [u + v for u, v in zip(parts[::2], parts[1::2])]
        moe_s[...] = parts[0]

    @pl.when(jnp.logical_not(fits))
    def _():
        c = lax.broadcasted_iota(jnp.int32, (rows, CAPACITY), 1)
        onehot = jnp.concatenate(
            [jnp.where(slot[:, e:e + 1] == c, 1.0, 0.0).astype(BF16) for e in range(N_EXPERTS)],
            axis=1)
        moe_s[...] = _dot(onehot, ye_ref[...].reshape(N_EXPERTS * CAPACITY, D_MODEL))

    acc = x_ref[...] + moe_s[...]
    h = _rms(acc, g3_ref[...]).astype(BF16)
    gate = jax.nn.sigmoid(_dot(h, wpg_ref[...]))
    y = acc + gate * _dot(p_ref[...].astype(BF16), wpp_ref[...])
    if final_norm:
        y = _rms(y, gf_ref[...])
    out_ref[...] = y


def _combine(x, ye, slot_col, starts, p, layer, g3, wpg, wpp, gf, final_norm):
    tm = TILE["combine"]
    nt = SEQ // tm
    p0 = layer * (TOKENS // tm)
    const = lambda a: pl.BlockSpec(a.shape, lambda b, i, s: (0,) * a.ndim,
                                   pipeline_mode=pl.Buffered(1))
    grid_spec = pltpu.PrefetchScalarGridSpec(
        num_scalar_prefetch=1,
        grid=(BATCH, nt),
        in_specs=[pl.BlockSpec((tm, D_MODEL), lambda b, i, s: (b * nt + i, 0)),
                  pl.BlockSpec((N_EXPERTS, None, CAPACITY, D_MODEL), lambda b, i, s: (0, b, 0, 0)),
                  pl.BlockSpec((None, tm, LANES), lambda b, i, s: (b, i, 0)),
                  pl.BlockSpec((tm, PLE_DIM), lambda b, i, s: (p0 + b * nt + i, 0)),
                  const(g3), const(wpg), const(wpp), const(gf)],
        out_specs=pl.BlockSpec((tm, D_MODEL), lambda b, i, s: (b * nt + i, 0)),
        scratch_shapes=[pltpu.VMEM((tm, D_MODEL), F32)])
    return pl.pallas_call(
        functools.partial(_combine_kernel, final_norm=final_norm),
        grid_spec=grid_spec,
        out_shape=jax.ShapeDtypeStruct((TOKENS, D_MODEL), F32),
        compiler_params=_params("parallel", "parallel"),
        name="combine",
    )(starts[:, :, :nt + 1].reshape(-1), x, ye.reshape(N_EXPERTS, BATCH, CAPACITY, D_MODEL),
      slot_col, p, g3, wpg, wpp, gf)


DFT_FINE = 64
DFT_STEP = 4


def _dft_kernel(ca_ref, sa_ref, cb_ref, sb_ref, out_ref):
    cb = cb_ref[...]
    sb = sb_ref[...]
    for r in range(DFT_STEP):
        rows = slice(r * DFT_FINE, (r + 1) * DFT_FINE)
        out_ref[rows, :] = (ca_ref[r] * cb - sa_ref[r] * sb).astype(BF16)


def _dft_tables(n_points, nres):
    blk = SEQ // nres
    t = jnp.arange(SEQ, dtype=jnp.int32).reshape(blk, nres).T
    t = jnp.concatenate([t, t], axis=1).reshape(1, 2 * SEQ)
    is_sin = (jnp.arange(2 * SEQ) // blk % 2 == 1)[None, :]
    coarse = jnp.arange(HALF // DFT_FINE, dtype=jnp.int32)[:, None] * DFT_FINE
    fine = jnp.arange(DFT_FINE, dtype=jnp.int32)[:, None]
    ang = lambda f: ((f * t) % n_points).astype(F32) * (2.0 * math.pi / n_points)
    ca = jnp.cos(ang(coarse))[:, None, :]
    sa = jnp.sin(ang(coarse))[:, None, :]
    cb = jnp.where(is_sin, jnp.sin(ang(fine)), jnp.cos(ang(fine)))
    sb = jnp.where(is_sin, -jnp.cos(ang(fine)), jnp.sin(ang(fine)))
    rows = DFT_STEP * DFT_FINE
    return pl.pallas_call(
        _dft_kernel,
        grid=(HALF // rows,),
        in_specs=[pl.BlockSpec((DFT_STEP, 1, 2 * SEQ), lambda i: (i, 0, 0)),
                  pl.BlockSpec((DFT_STEP, 1, 2 * SEQ), lambda i: (i, 0, 0)),
                  _const_spec(cb.shape), _const_spec(sb.shape)],
        out_specs=pl.BlockSpec((rows, 2 * SEQ), lambda i: (i, 0)),
        out_shape=jax.ShapeDtypeStruct((HALF, 2 * SEQ), BF16),
        compiler_params=_params("parallel"),
        name="dft_tables",
    )(ca, sa, cb, sb)


def _fnet_group_tables():
    gc = D_MIX // FN_GROUPS
    i = lax.broadcasted_iota(jnp.int32, (D_MIX, D_MIX), 0)
    j = lax.broadcasted_iota(jnp.int32, (D_MIX, D_MIX), 1)
    same = (i // gc) == (j // gc)
    ang = (((i % gc) * (j % gc)) % gc).astype(F32) * (2.0 * math.pi / gc)
    return (jnp.where(same, jnp.cos(ang), 0.0).astype(BF16),
            jnp.where(same, jnp.sin(ang), 0.0).astype(BF16))


def _hyena_features():
    t01 = jnp.linspace(0.0, 1.0, SEQ, dtype=F32)[:, None]
    bands = jnp.linspace(1e-4, HY_BANDS - 1, HY_BANDS, dtype=F32)
    ang = 2.0 * math.pi * jnp.arange(SEQ, dtype=F32)[:, None] * bands / SEQ
    z = jnp.concatenate([t01, jnp.cos(ang), -jnp.sin(ang)], axis=-1)
    z = jnp.pad(z, ((0, 0), (0, LANES - HY_EMB)))
    max_decay = math.log(HY_TARGET) / HY_FAST_DECAY
    min_decay = math.log(HY_TARGET) / HY_SLOW_DECAY
    deltas = jnp.linspace(min_decay, max_decay, D_MIX, dtype=F32)
    window = jnp.exp(-t01 * jnp.abs(deltas))
    return z, window


def _rot_cols(w):
    half = w.shape[-1] // 2
    return jnp.concatenate([-w[..., half:], w[..., :half]], axis=-1)


def _rope_tables():
    inv = ROPE_THETA ** (-jnp.arange(0, MLA_ROPE, 2, dtype=F32) / MLA_ROPE)
    ang = jnp.arange(SEQ, dtype=F32)[:, None] * inv
    cos = jnp.concatenate([jnp.cos(ang), jnp.cos(ang)], axis=-1)
    sin = jnp.concatenate([jnp.sin(ang), jnp.sin(ang)], axis=-1)
    scale = (MLA_NOPE + MLA_ROPE) ** -0.5 * LOG2E
    pad = MLA_HEAD_PAD - MLA_NOPE - MLA_ROPE
    one = jnp.ones((SEQ, MLA_NOPE), F32)
    zero = jnp.zeros((SEQ, MLA_NOPE), F32)
    zpad = jnp.zeros((SEQ, pad), F32)
    cosq = jnp.tile(jnp.concatenate([one, cos, zpad], axis=-1) * scale, (1, MLA_HEADS))
    sinq = jnp.tile(jnp.concatenate([zero, sin, zpad], axis=-1) * scale, (1, MLA_HEADS))
    csk = jnp.concatenate([cos, sin, jnp.zeros((SEQ, LANES - 2 * MLA_ROPE), F32)], axis=-1)
    return cosq, sinq, csk


def _mla_weights(w_uq, w_ukv):
    pad = MLA_HEAD_PAD - MLA_NOPE - MLA_ROPE
    wq = w_uq.reshape(MLA_Q_RANK, MLA_HEADS, MLA_NOPE + MLA_ROPE)
    nope, pe = wq[..., :MLA_NOPE], wq[..., MLA_NOPE:]
    zp = jnp.zeros((MLA_Q_RANK, MLA_HEADS, pad), F32)
    wqa = jnp.concatenate([nope, pe, zp], axis=-1).reshape(MLA_Q_RANK, MLA_QK)
    wqb = jnp.concatenate([jnp.zeros_like(nope), _rot_cols(pe), zp], axis=-1)
    wqb = wqb.reshape(MLA_Q_RANK, MLA_QK)
    wkv = w_ukv.reshape(MLA_KV_RANK, MLA_HEADS, MLA_NOPE + MLA_V)
    knope, v = wkv[..., :MLA_NOPE], wkv[..., MLA_NOPE:]
    wk = jnp.concatenate(
        [knope, jnp.zeros((MLA_KV_RANK, MLA_HEADS, MLA_HEAD_PAD - MLA_NOPE), F32)], axis=-1)
    wk = wk.reshape(MLA_KV_RANK, MLA_QK)
    wv = v.reshape(MLA_KV_RANK, MLA_HEADS * MLA_V)
    r = lax.broadcasted_iota(jnp.int32, (LANES, MLA_QK), 0)
    c = lax.broadcasted_iota(jnp.int32, (LANES, MLA_QK), 1)
    epe = jnp.where((r < 2 * MLA_ROPE) & (c % MLA_HEAD_PAD == MLA_NOPE + r % MLA_ROPE), 1.0, 0.0)
    return (wqa.astype(BF16), wqb.astype(BF16), wk.astype(BF16), wv.astype(BF16),
            epe.astype(BF16))


def kernel(x, p, norm1_g, w_in, b_gate, hy_conv_w, hy_conv_b, hf_w1, hf_b1, hf_freq, hf_w2,
           hf_b2, hf_w3, hy_skip, q_norm_g, w_uq, kv_norm_g, w_ukv, rpb, w_br, w_out, norm2_g,
           w_router, w_e_gate, w_e_up, w_e_down, norm3_g, w_ple_gate, w_ple_proj, final_g):
    conv_tab = _dft_tables(2 * SEQ, 4)
    fnet_tab = _dft_tables(SEQ, 2)
    fnet_cg, fnet_sg = _fnet_group_tables()
    zfeat, window = _hyena_features()
    cosq, sinq, csk = _rope_tables()
    row = lambda a: a.reshape(1, -1)

    xt = x.reshape(TOKENS, D_MODEL)
    w_in_t = jnp.swapaxes(w_in, 1, 2)
    for i in range(DEPTH):
        g1 = row(norm1_g[i])
        wa, wna, wgate = _inproj_weights(w_in_t, i)
        wqa, wqb, wk, wv, epe = _mla_weights(w_uq[i], w_ukv[i])

        u_hy, u_fn, q, k, v, naq, nak, nav = _inproj(
            xt, g1, wa, wna, row(q_norm_g[i]), wqa, wqb, row(kv_norm_g[i]), wk, wv, epe,
            cosq, sinq, csk)

        w1 = jnp.pad(hf_w1[i], ((0, LANES - HY_EMB), (0, 0)))
        kf, kny = _hyena_filter(zfeat, window, w1, row(hf_b1[i]), hf_freq[i], hf_w2[i],
                                row(hf_b2[i]), hf_w3[i], conv_tab)
        conv_b = row(hy_conv_b[i])
        z1 = _hyena_stage(u_hy, 0, u_hy, 2, hy_conv_w[i], conv_b, conv_tab, kf[0],
                          kny[0], row(hy_skip[i, 0]), True, F32)
        y_hy = _hyena_stage(u_hy, 1, z1, 0, hy_conv_w[i], conv_b, conv_tab, kf[1],
                            kny[1], row(hy_skip[i, 1]), False, BF16)
        y_fn = _fnet(u_fn, fnet_tab, fnet_cg, fnet_sg)
        y_mla = _mla(q, k, v)
        y_na = _neighborhood(naq, nak, nav, _na_bias_tiles(rpb[i]))
        wr_t = jnp.pad(w_router[i].T, ((0, LANES - N_EXPERTS), (0, 0)))
        xt, hb, logits = _merge(xt, g1, y_hy, y_fn, y_mla, y_na, wgate, row(b_gate[i]),
                                w_br[i].astype(BF16), w_out[i].astype(BF16),
                                row(norm2_g[i]), wr_t)
        slot_row, slot_col, w_row, starts = _select(logits)
        xe, wsl = _gather(slot_row, w_row, starts, hb)
        ye = _experts(xe, wsl, w_e_gate, w_e_up, w_e_down, i)
        xt = _combine(xt, ye, slot_col, starts, p.reshape(DEPTH * TOKENS, PLE_DIM), i,
                      row(norm3_g[i]), w_ple_gate[i].astype(BF16),
                      w_ple_proj[i].astype(BF16), row(final_g), i == DEPTH - 1)
    return xt.reshape(BATCH, SEQ, D_MODEL)
```

```python
import functools
import math

import jax
import jax.numpy as jnp
from jax import lax
from jax.experimental import pallas as pl
from jax.experimental.pallas import tpu as pltpu

F32 = jnp.float32
BF16 = jnp.bfloat16

D_MODEL = 1024
BATCH = 8
SEQ = 2048
DEPTH = 2
TOKENS = BATCH * SEQ

GRID_W = 64
GRID_R = SEQ // GRID_W
D_MIX = 256
N_BRANCH = 4
EPS = 1e-6
HY_ORDER = 2
HY_BANDS = 16
HY_EMB = 2 * HY_BANDS + 1
HY_FFN = 64
HY_TARGET = 1e-2
HY_FAST_DECAY = 0.3
HY_SLOW_DECAY = 1.5
FN_GROUPS = 4
MLA_HEADS = 4
MLA_NOPE = 64
MLA_ROPE = 32
MLA_V = 64
MLA_Q_RANK = 256
MLA_KV_RANK = 128
ROPE_THETA = 10000.0
NA_HEADS = 4
NA_HEAD_DIM = D_MIX // NA_HEADS
NA_WIN_R = 8
NA_WIN_C = 16
N_EXPERTS = 16
CAPACITY = 2 * SEQ // N_EXPERTS
D_FF = 1024
PLE_DIM = 256

HY_COLS = 3 * D_MIX
OFF_FN = HY_COLS
OFF_CQ = OFF_FN + D_MIX
OFF_CKV = OFF_CQ + MLA_Q_RANK
OFF_KPE = OFF_CKV + MLA_KV_RANK
OFF_NA = OFF_KPE + MLA_ROPE
OFF_GATE = OFF_NA + 3 * D_MIX

LANES = 128
MLA_HEAD_PAD = 128
MLA_QK = MLA_HEADS * MLA_HEAD_PAD
WA_COLS = 1536
NEG_BIG = -1e30
LOG2E = math.log2(math.e)

TILE = dict(inproj=512, mla=512, merge=512, expert=512, combine=256)
NA_QROWS = 8
NA_KROWS = 16
NA_QBLK = NA_QROWS * GRID_W
NA_KBLK = NA_KROWS * GRID_W
VMEM_LIMIT = 56 * 1024 * 1024


def _params(*sem):
    return pltpu.CompilerParams(dimension_semantics=sem, vmem_limit_bytes=VMEM_LIMIT)


def _const_spec(shape):
    nd = len(shape)
    return pl.BlockSpec(shape, lambda *_: (0,) * nd, pipeline_mode=pl.Buffered(1))


def _rms(x, g):
    return x * lax.rsqrt(jnp.mean(x * x, axis=-1, keepdims=True) + EPS) * g


def _dot(a, b):
    return jnp.dot(a, b, preferred_element_type=F32)


def _dot_nt(a, b):
    return lax.dot_general(a, b, (((1,), (1,)), ((), ())), preferred_element_type=F32)


def _split2(x):
    hi = x.astype(BF16)
    lo = (x - hi.astype(F32)).astype(BF16)
    return hi, lo


def _inproj_kernel(x_ref, g1_ref, wa_ref, wna_ref, qg_ref, wqa_ref, wqb_ref, kvg_ref,
                   wk_ref, wv_ref, epe_ref, cosq_ref, sinq_ref, csk_ref,
                   uhy_ref, ufn_ref, q_ref, k_ref, v_ref, naq_ref, nak_ref, nav_ref):
    h = _rms(x_ref[...], g1_ref[...]).astype(BF16)
    ua = _dot_nt(h, wa_ref[...])
    uhy_ref[...] = ua[:, :HY_COLS]
    ufn_ref[...] = ua[:, OFF_FN:OFF_CQ]
    cqn = _rms(ua[:, OFF_CQ:OFF_CKV], qg_ref[...]).astype(BF16)
    q = _dot(cqn, wqa_ref[...]) * cosq_ref[...] + _dot(cqn, wqb_ref[...]) * sinq_ref[...]
    q_ref[...] = q.astype(BF16)
    kvn = _rms(ua[:, OFF_CKV:OFF_KPE], kvg_ref[...]).astype(BF16)
    kpe = ua[:, OFF_KPE:WA_COLS] * csk_ref[...]
    k = _dot(kvn, wk_ref[...]) + _dot(kpe.astype(BF16), epe_ref[...])
    k_ref[...] = k.astype(BF16)
    v_ref[...] = _dot(kvn, wv_ref[...]).astype(BF16)
    una = _dot_nt(h, wna_ref[...])
    naq_ref[...] = (una[:, :D_MIX] * (NA_HEAD_DIM ** -0.5 * LOG2E)).astype(BF16)
    nak_ref[...] = una[:, D_MIX:2 * D_MIX].astype(BF16)
    nav_ref[...] = una[:, 2 * D_MIX:].astype(BF16)


def _inproj(x, g1, wa, wna, qg, wqa, wqb, kvg, wk, wv, epe, cosq, sinq, csk):
    tm = TILE["inproj"]
    nt = SEQ // tm
    row = lambda n: pl.BlockSpec((tm, n), lambda i: (i, 0))
    pos = lambda n: pl.BlockSpec((tm, n), lambda i: (i % nt, 0))
    outs = [(HY_COLS, F32), (D_MIX, F32), (MLA_QK, BF16), (MLA_QK, BF16), (D_MIX, BF16),
            (D_MIX, BF16), (D_MIX, BF16), (D_MIX, BF16)]
    return pl.pallas_call(
        _inproj_kernel,
        grid=(TOKENS // tm,),
        in_specs=[row(D_MODEL), _const_spec(g1.shape), _const_spec(wa.shape),
                  _const_spec(wna.shape), _const_spec(qg.shape), _const_spec(wqa.shape),
                  _const_spec(wqb.shape), _const_spec(kvg.shape), _const_spec(wk.shape),
                  _const_spec(wv.shape), _const_spec(epe.shape),
                  pos(MLA_QK), pos(MLA_QK), pos(LANES)],
        out_specs=[row(n) for n, _ in outs],
        out_shape=[jax.ShapeDtypeStruct((TOKENS, n), dt) for n, dt in outs],
        compiler_params=_params("parallel"),
        name="inproj",
    )(x, g1, wa, wna, qg, wqa, wqb, kvg, wk, wv, epe, cosq, sinq, csk)


WPREP_ROWS = 512


def _inproj_weights_kernel(w_ref, wa_ref, wna_ref, wg_ref):
    i = pl.program_id(0)

    @pl.when(i == 0)
    def _():
        half = MLA_ROPE // 2
        wa_ref[:OFF_NA, :] = w_ref[:OFF_NA, :].astype(BF16)
        wa_ref[OFF_NA:OFF_NA + half, :] = (-w_ref[OFF_KPE + half:OFF_NA, :]).astype(BF16)
        wa_ref[OFF_NA + half:OFF_NA + MLA_ROPE, :] = w_ref[OFF_KPE:OFF_KPE + half, :].astype(BF16)
        wa_ref[OFF_NA + MLA_ROPE:, :] = jnp.zeros((WA_COLS - OFF_NA - MLA_ROPE, D_MODEL), BF16)
        wna_ref[...] = w_ref[OFF_NA:OFF_GATE, :].astype(BF16)

    start = pl.multiple_of(OFF_GATE + i * WPREP_ROWS, MLA_ROPE)
    wg_ref[...] = w_ref[pl.ds(start, WPREP_ROWS), :].astype(BF16)


def _inproj_weights(w_in_t, layer):
    n_in = w_in_t.shape[1]
    rows = (WA_COLS, OFF_GATE - OFF_NA, n_in - OFF_GATE)
    whole = lambda n: pl.BlockSpec((n, D_MODEL), lambda i: (0, 0))
    return pl.pallas_call(
        _inproj_weights_kernel,
        grid=(rows[2] // WPREP_ROWS,),
        in_specs=[pl.BlockSpec((None, n_in, D_MODEL), lambda i: (layer, 0, 0),
                               pipeline_mode=pl.Buffered(1))],
        out_specs=[whole(rows[0]), whole(rows[1]),
                   pl.BlockSpec((WPREP_ROWS, D_MODEL), lambda i: (i, 0))],
        out_shape=[jax.ShapeDtypeStruct((n, D_MODEL), BF16) for n in rows],
        compiler_params=_params("arbitrary"),
        name="inproj_weights",
    )(w_in_t)


HALF = SEQ // 2
HY_SEQS = 1


def _residue_rows(ref, r, nres):
    rows = pl.ds(r, SEQ // nres, stride=nres)
    if isinstance(ref, tuple):
        return jnp.concatenate([h[rows, :] for h in ref], axis=1)
    return jnp.concatenate([ref[j, rows, :] for j in range(ref.shape[0])], axis=1)


def _store_slabs(ref, value, rows=slice(None)):
    for j in range(ref.shape[0]):
        ref[j, rows, :] = value[:, j * LANES:(j + 1) * LANES]


def _table_cols(cs_ref, r, nres, part=None):
    blk = SEQ // nres
    lo = 2 * blk * r
    if part is None:
        return cs_ref[:, lo:lo + 2 * blk]
    return cs_ref[:, lo + part * blk:lo + (part + 1) * blk]


def _dft_fwd(cs_ref, src_ref, want_cos=True, want_sin=True, split=False):
    def prod(part, z, r):
        tab = _table_cols(cs_ref, r, 4, part)
        if split:
            hi, lo = _split2(z)
            return _dot(tab, hi) + _dot(tab, lo)
        return _dot(tab, z.astype(BF16))

    pc, ps = [None] * 4, [None] * 4
    for r in range(4):
        z = _residue_rows(src_ref, r, 4)
        odd = r % 2 == 1
        if want_cos or odd:
            pc[r] = prod(0, z, r)
        if want_sin or odd:
            ps[r] = prod(1, z, r)
    a = b = None
    if want_cos:
        a = ((pc[0] + pc[2]) + (pc[1] + pc[3]), (pc[0] - pc[2]) + (ps[3] - ps[1]))
    if want_sin:
        b = ((ps[0] + ps[2]) + (ps[1] + ps[3]), (ps[0] - ps[2]) + (pc[1] - pc[3]))
    return a, b


def _dft_inv(cs_ref, yre_ref, yim_ref):
    g, h = [], {}
    for r in range(4):
        yr = _residue_rows(yre_ref, r, 4).astype(BF16)
        yi = _residue_rows(yim_ref, r, 4).astype(BF16)
        tab = _table_cols(cs_ref, r, 4)
        g.append(_dot(tab, jnp.concatenate([yr, -yi], axis=0)))
        if r % 2 == 1:
            h[r] = _dot(tab, jnp.concatenate([yi, yr], axis=0))
    return (g[0] + g[2]) + (g[1] + g[3]), (g[0] - g[2]) + (h[3] - h[1])


def _hyena_filter_kernel(z_ref, win_ref, w1_ref, b1_ref, freq_ref, w2_ref, b2_ref, w3_ref,
                         cs_ref, kf_ref, kny_ref, ksum_s, kdif_s):
    hp = lax.Precision.HIGHEST
    freq = freq_ref[...]
    hf = jnp.sin(freq[0:1] * (jnp.dot(z_ref[...], w1_ref[...], precision=hp,
                                      preferred_element_type=F32) + b1_ref[...]))
    hf = jnp.sin(freq[1:2] * (jnp.dot(hf, w2_ref[...], precision=hp,
                                      preferred_element_type=F32) + b2_ref[...]))
    hf = jnp.dot(hf, w3_ref[...], precision=hp, preferred_element_type=F32)
    win = win_ref[...]
    t = lax.broadcasted_iota(jnp.int32, (SEQ, D_MIX), 0)
    sgn = (1 - 2 * (t & 1)).astype(F32)
    fwd = hf[:, :D_MIX] * win
    bwd = jnp.where(t == 0, 0.0, hf[:, D_MIX:] * win)
    nrm = lax.rsqrt(jnp.sum(fwd * fwd + bwd * bwd, axis=0, keepdims=True) + EPS)
    ksum = (fwd + bwd) * nrm
    _store_slabs(ksum_s, ksum)
    _store_slabs(kdif_s, (bwd - fwd) * nrm)
    kre, _ = _dft_fwd(cs_ref, ksum_s, want_sin=False, split=True)
    _, kim = _dft_fwd(cs_ref, kdif_s, want_cos=False, split=True)
    wf = 2.0 / (2 * SEQ)
    for part, spec in enumerate((kre, kim)):
        lo = spec[0] * wf
        kf_ref[part, :HALF, :] = lo
        kf_ref[part, 0:1, :] = lo[0:1] * 0.5
        kf_ref[part, HALF:, :] = spec[1] * wf
    kny = jnp.sum(ksum * sgn, axis=0, keepdims=True) * (1.0 / (2 * SEQ))
    kny_ref[...] = jnp.broadcast_to(kny, (8, D_MIX))


def _hyena_filter(zfeat, window, w1, b1, freq, w2, b2, w3, table):
    consts = (zfeat, window, w1, b1, freq, w2, b2)
    return pl.pallas_call(
        _hyena_filter_kernel,
        grid=(HY_ORDER,),
        in_specs=[_const_spec(a.shape) for a in consts]
        + [pl.BlockSpec((HY_FFN, 2 * D_MIX), lambda o: (0, o)),
           _const_spec(table.shape)],
        out_specs=[pl.BlockSpec((None, 2, SEQ, D_MIX), lambda o: (o, 0, 0, 0)),
                   pl.BlockSpec((None, 8, D_MIX), lambda o: (o, 0, 0))],
        out_shape=[jax.ShapeDtypeStruct((HY_ORDER, 2, SEQ, D_MIX), F32),
                   jax.ShapeDtypeStruct((HY_ORDER, 8, D_MIX), F32)],
        scratch_shapes=[pltpu.VMEM((D_MIX // LANES, SEQ, LANES), F32)] * 2,
        compiler_params=_params("arbitrary"),
        name="hyena_filter",
    )(*consts, w3, table)


def _short_conv(u, w, b):
    t = lax.broadcasted_iota(jnp.int32, u.shape, 0)
    prev = jnp.where(t == 0, 0.0, pltpu.roll(u, 1, 0))
    nxt = jnp.where(t == SEQ - 1, 0.0, pltpu.roll(u, SEQ - 1, 0))
    return prev * w[0:1] + u * w[1:2] + nxt * w[2:3] + b


def _hyena_stage_kernel(gate_ref, src_ref, wg_ref, bg_ref, ws_ref, bs_ref, cs_ref,
                        kf_ref, kny_ref, skip_ref, out_ref, z_s, yre_s, yim_s, gate_s, rest_s,
                        *, conv_src):
    for q in range(HY_SEQS):
        seq = slice(q * SEQ, (q + 1) * SEQ)
        zq, yre_q, yim_q = z_s.at[q], yre_s.at[q], yim_s.at[q]
        z = src_ref[seq, :]
        if conv_src:
            z = _short_conv(z, ws_ref[...], bs_ref[...])
        _store_slabs(zq, z)
        t = lax.broadcasted_iota(jnp.int32, z.shape, 0)
        sgn = (1 - 2 * (t & 1)).astype(F32)
        nyq = jnp.sum(z * sgn, axis=0, keepdims=True) * kny_ref[0:1]
        gate = _short_conv(gate_ref[seq, :], wg_ref[...], bg_ref[...])
        gate_s[seq, :] = gate
        rest_s[seq, :] = gate * (sgn * nyq + z * skip_ref[...])
        a, b = _dft_fwd(cs_ref, zq)
        for half in range(2):
            rows = slice(half * HALF, (half + 1) * HALF)
            kre = kf_ref[0, rows, :]
            kim = kf_ref[1, rows, :]
            _store_slabs(yre_q, a[half] * kre + b[half] * kim, rows)
            _store_slabs(yim_q, a[half] * kim - b[half] * kre, rows)
        for half, y in enumerate(_dft_inv(cs_ref, yre_q, yim_q)):
            rows = slice(q * SEQ + half * HALF, q * SEQ + (half + 1) * HALF)
            out_ref[rows, :] = (gate_s[rows, :] * y + rest_s[rows, :]).astype(out_ref.dtype)


def _hyena_stage(u_hy, gate_blk, src, src_blk, conv_w, conv_b, table, kf, kny, skip,
                 conv_src, out_dtype):
    rows = HY_SEQS * SEQ
    col = lambda blk: pl.BlockSpec((rows, D_MIX), lambda b: (b, blk))
    wcol = lambda blk, r: pl.BlockSpec((r, D_MIX), lambda b: (0, blk))
    ws_blk = src_blk if conv_src else 0
    return pl.pallas_call(
        functools.partial(_hyena_stage_kernel, conv_src=conv_src),
        grid=(BATCH // HY_SEQS,),
        in_specs=[col(gate_blk), col(src_blk), wcol(gate_blk, 3), wcol(gate_blk, 1),
                  wcol(ws_blk, 3), wcol(ws_blk, 1), _const_spec(table.shape),
                  _const_spec(kf.shape), _const_spec(kny.shape),
                  _const_spec(skip.shape)],
        out_specs=pl.BlockSpec((rows, D_MIX), lambda b: (b, 0)),
        out_shape=jax.ShapeDtypeStruct((TOKENS, D_MIX), out_dtype),
        scratch_shapes=[pltpu.VMEM((HY_SEQS, D_MIX // LANES, SEQ, LANES), F32)] * 3
        + [pltpu.VMEM((rows, D_MIX), F32)] * 2,
        compiler_params=_params("parallel"),
        name="hyena_stage",
    )(u_hy, src, conv_w, conv_b, conv_w, conv_b, table, kf, kny, skip)


def _fnet_kernel(xa_ref, xb_ref, cs_ref, cg_ref, sg_ref, out_ref):
    parts = []
    for r in range(2):
        xb = _residue_rows((xa_ref, xb_ref), r, 2).astype(BF16)
        xc = _dot(xb, cg_ref[...]).astype(BF16)
        xs = _dot(xb, sg_ref[...]).astype(BF16)
        parts.append(_dot(_table_cols(cs_ref, r, 2), jnp.concatenate([xc, -xs], axis=0)))
    scale = (SEQ * D_MIX // FN_GROUPS) ** -0.5
    out_ref[:HALF, :] = ((parts[0] + parts[1]) * scale).astype(out_ref.dtype)
    out_ref[HALF:, :] = ((parts[0] - parts[1]) * scale).astype(out_ref.dtype)


def _fnet(u_fn, table, cg, sg):
    return pl.pallas_call(
        _fnet_kernel,
        grid=(BATCH,),
        in_specs=[pl.BlockSpec((SEQ, LANES), lambda b: (b, 0)),
                  pl.BlockSpec((SEQ, LANES), lambda b: (b, 1)), _const_spec(table.shape),
                  _const_spec(cg.shape), _const_spec(sg.shape)],
        out_specs=pl.BlockSpec((SEQ, D_MIX), lambda b: (b, 0)),
        out_shape=jax.ShapeDtypeStruct((TOKENS, D_MIX), BF16),
        compiler_params=_params("parallel"),
        name="fnet",
    )(u_fn, u_fn, table, cg, sg)


def _softmax2_pv(s2, v):
    m = jnp.max(s2, axis=-1, keepdims=True)
    p = jnp.exp2(s2 - m)
    l = jnp.sum(p, axis=-1, keepdims=True)
    return _dot(p.astype(BF16), v) / l


def _mla_kernel(q_ref, k_ref, v_ref, out_ref):
    v = v_ref[...]
    head = lax.broadcasted_iota(jnp.int32, out_ref.shape, 1) // MLA_V
    acc = jnp.zeros(out_ref.shape, F32)
    for h in range(MLA_HEADS):
        sl = slice(h * MLA_HEAD_PAD, (h + 1) * MLA_HEAD_PAD)
        s2 = _dot_nt(q_ref[:, sl], k_ref[:, sl])
        acc = jnp.where(head == h, _softmax2_pv(s2, v), acc)
    out_ref[...] = acc.astype(out_ref.dtype)


def _mla(q, k, v):
    tm = TILE["mla"]
    nt = SEQ // tm
    return pl.pallas_call(
        _mla_kernel,
        grid=(BATCH, nt),
        in_specs=[pl.BlockSpec((tm, MLA_QK), lambda b, i: (b * nt + i, 0)),
                  pl.BlockSpec((SEQ, MLA_QK), lambda b, i: (b, 0)),
                  pl.BlockSpec((SEQ, D_MIX), lambda b, i: (b, 0))],
        out_specs=pl.BlockSpec((tm, D_MIX), lambda b, i: (b * nt + i, 0)),
        out_shape=jax.ShapeDtypeStruct((TOKENS, D_MIX), BF16),
        compiler_params=_params("parallel", "parallel"),
        name="mla_attention",
    )(q, k, v)


def _na_key_row0(j):
    return jnp.clip(j * NA_QROWS - NA_WIN_R // 2, 0, GRID_R - NA_KROWS)


NA_PAIRS = 2 * NA_WIN_R


def _na_kernel(q_ref, k_ref, v_ref, tile_ref, out_ref, bias_s):
    j = pl.program_id(0)
    krow0 = _na_key_row0(j)

    @pl.when(pl.program_id(1) == 0)
    def _():
        rq = j * NA_QROWS + lax.broadcasted_iota(jnp.int32, (NA_QBLK, NA_KBLK), 0) // GRID_W
        rk = krow0 + lax.broadcasted_iota(jnp.int32, (NA_QBLK, NA_KBLK), 1) // GRID_W
        rs = jnp.clip(rq - NA_WIN_R // 2, 0, GRID_R - NA_WIN_R)
        rowmask = jnp.where(rk < rs, NEG_BIG, jnp.where(rk >= rs + NA_WIN_R, NEG_BIG, 0.0))
        base = krow0 - j * NA_QROWS + NA_WIN_R
        for h in range(NA_HEADS):
            bias = jnp.concatenate(
                [jnp.concatenate(
                    [tile_ref[h, jnp.clip(base + 2 * kp - r, 0, NA_PAIRS - 1)]
                     for kp in range(NA_KROWS // 2)], axis=1)
                 for r in range(NA_QROWS)], axis=0)
            bias_s[h] = bias + rowmask

    off = pl.multiple_of(krow0 * GRID_W, GRID_W)
    q = q_ref[...]
    k = k_ref[pl.ds(off, NA_KBLK), :]
    v = v_ref[pl.ds(off, NA_KBLK), :]
    head = lax.broadcasted_iota(jnp.int32, (NA_QBLK, D_MIX), 1) // NA_HEAD_DIM
    acc = jnp.zeros((NA_QBLK, D_MIX), F32)
    for h in range(NA_HEADS):
        qh = jnp.where(head == h, q, jnp.zeros_like(q))
        s2 = _dot_nt(qh, k) + bias_s[h]
        acc = jnp.where(head == h, _softmax2_pv(s2, v), acc)
    out_ref[...] = acc.astype(out_ref.dtype)


def _neighborhood(q, k, v, tiles):
    nj = SEQ // NA_QBLK
    return pl.pallas_call(
        _na_kernel,
        grid=(nj, BATCH),
        in_specs=[pl.BlockSpec((NA_QBLK, D_MIX), lambda j, b: (b * nj + j, 0)),
                  pl.BlockSpec((SEQ, D_MIX), lambda j, b: (b, 0)),
                  pl.BlockSpec((SEQ, D_MIX), lambda j, b: (b, 0)),
                  _const_spec(tiles.shape)],
        out_specs=pl.BlockSpec((NA_QBLK, D_MIX), lambda j, b: (b * nj + j, 0)),
        out_shape=jax.ShapeDtypeStruct((TOKENS, D_MIX), BF16),
        scratch_shapes=[pltpu.VMEM((NA_HEADS, NA_QBLK, NA_KBLK), F32)],
        compiler_params=_params("parallel", "arbitrary"),
        name="neighborhood_attention",
    )(q, k, v, tiles)


def _na_bias_tiles(rpb):
    c = jnp.arange(GRID_W)
    cs = jnp.clip(c - NA_WIN_C // 2, 0, GRID_W - NA_WIN_C)
    col_ok = (c[None, :] >= cs[:, None]) & (c[None, :] < cs[:, None] + NA_WIN_C)
    dc = jnp.clip(c[None, :] - c[:, None] + (NA_WIN_C - 1), 0, 2 * NA_WIN_C - 2)
    pick = (dc[None] == jnp.arange(2 * NA_WIN_C - 1)[:, None, None]).astype(F32)
    t = jnp.einsum('hrd,dqk->hrqk', rpb.astype(F32), pick, precision=lax.Precision.HIGHEST)
    t = jnp.where(col_ok, t * LOG2E, NEG_BIG)
    t = jnp.pad(t, ((0, 0), (1, 1), (0, 0), (0, 0)))
    return jnp.concatenate([t[:, :-1], t[:, 1:]], axis=-1)


def _merge_kernel(x_ref, g1_ref, yhy_ref, yfn_ref, ymla_ref, yna_ref, wg_ref, bg_ref,
                  wbr_ref, wout_ref, g2_ref, wr_ref, out_ref, hb_ref, logit_ref):
    x = x_ref[...]
    h = _rms(x, g1_ref[...]).astype(BF16)
    merged = jnp.zeros(x.shape, F32)
    for n, y_ref in enumerate((yhy_ref, yfn_ref, ymla_ref, yna_ref)):
        sl = slice(n * D_MODEL, (n + 1) * D_MODEL)
        gate = jax.nn.sigmoid(_dot_nt(h, wg_ref[sl, :]) + bg_ref[:, sl])
        merged = merged + gate * _dot(y_ref[...], wbr_ref[n])
    x1 = x + _dot(merged.astype(BF16), wout_ref[...])
    out_ref[...] = x1
    h_hi, h_lo = _split2(_rms(x1, g2_ref[...]))
    hb_ref[...] = h_hi
    w_hi, w_lo = _split2(wr_ref[...])
    logits = _dot_nt(w_hi, h_hi) + _dot_nt(w_hi, h_lo) + _dot_nt(w_lo, h_hi)
    logit_ref[...] = logits[:N_EXPERTS]


def _merge(x, g1, yhy, yfn, ymla, yna, wg, bg, wbr, wout, g2, wr_t):
    tm = TILE["merge"]
    nt = SEQ // tm
    row = lambda n: pl.BlockSpec((tm, n), lambda i: (i, 0))
    return pl.pallas_call(
        _merge_kernel,
        grid=(TOKENS // tm,),
        in_specs=[row(D_MODEL), _const_spec(g1.shape), row(D_MIX), row(D_MIX), row(D_MIX),
                  row(D_MIX), _const_spec(wg.shape), _const_spec(bg.shape),
                  _const_spec(wbr.shape), _const_spec(wout.shape), _const_spec(g2.shape),
                  _const_spec(wr_t.shape)],
        out_specs=[row(D_MODEL), row(D_MODEL),
                   pl.BlockSpec((None, N_EXPERTS, tm), lambda i: (i // nt, 0, i % nt))],
        out_shape=[jax.ShapeDtypeStruct((TOKENS, D_MODEL), F32),
                   jax.ShapeDtypeStruct((TOKENS, D_MODEL), BF16),
                   jax.ShapeDtypeStruct((BATCH, N_EXPERTS, SEQ), F32)],
        compiler_params=_params("parallel"),
        name="merge",
    )(x, g1, yhy, yfn, ymla, yna, wg, bg, wbr, wout, g2, wr_t)


def _prefix_count(m):
    r = lax.broadcasted_iota(jnp.int32, (LANES, LANES), 0)
    c = lax.broadcasted_iota(jnp.int32, (LANES, LANES), 1)
    upper = jnp.where(r < c, 1.0, 0.0).astype(BF16)
    run = jnp.zeros((m.shape[0], 1), F32)
    parts = []
    for i in range(SEQ // LANES):
        chunk = m[:, i * LANES:(i + 1) * LANES]
        parts.append(_dot(chunk.astype(BF16), upper) + run)
        run = run + jnp.sum(chunk, axis=1, keepdims=True)
    return jnp.concatenate(parts, axis=1)


SELECT_MAX_ITERS = 192


def _select_kernel(logit_ref, slot_row_ref, slot_col_ref, w_row_ref, start_ref, slot_s, w_s):
    b = pl.program_id(0)
    rows = BATCH * N_EXPERTS

    @pl.when(b == 0)
    def _():
        logits = logit_ref[...]
        ex = jnp.exp(logits - jnp.max(logits, axis=1, keepdims=True))
        aff = (ex / jnp.sum(ex, axis=1, keepdims=True)).reshape(rows, SEQ)

        def bisect(c):
            it, lo, hi, _ = c
            mid = 0.5 * (lo + hi)
            cnt = jnp.sum(jnp.where(aff >= mid, 1.0, 0.0), axis=1, keepdims=True)
            moving = jnp.where(mid == lo, 0.0, jnp.where(mid == hi, 0.0, 1.0))
            enough = cnt >= CAPACITY
            return (it + 1, jnp.where(enough, mid, lo), jnp.where(enough, hi, mid),
                    (jnp.max(moving) > 0).astype(jnp.int32))

        _, lo, hi, _ = lax.while_loop(
            lambda c: (c[0] < SELECT_MAX_ITERS) & (c[3] > 0), bisect,
            (jnp.int32(0), jnp.zeros((rows, 1), F32), jnp.full((rows, 1), 2.0, F32),
             jnp.int32(1)))
        above = jnp.where(aff >= hi, 1.0, 0.0)
        band = jnp.where(aff >= lo, 1.0, 0.0) - above
        need = CAPACITY - jnp.sum(above, axis=1, keepdims=True)
        sel = above + band * jnp.where(_prefix_count(band) < need, 1.0, 0.0)
        slot_s[...] = jnp.where(sel > 0, _prefix_count(sel), -1.0)
        w_s[...] = sel * aff

    r0 = pl.multiple_of(b * N_EXPERTS, N_EXPERTS)
    slot = slot_s[pl.ds(r0, N_EXPERTS), :]
    slot_row_ref[...] = slot.astype(jnp.int32)
    pad = jnp.full((LANES - N_EXPERTS, SEQ), -1.0, F32)
    slot_col_ref[...] = jnp.concatenate([slot, pad], axis=0).T.astype(jnp.int32)
    w_row_ref[...] = w_s[pl.ds(r0, N_EXPERTS), :]
    tm = TILE["combine"]
    token = lax.broadcasted_iota(jnp.int32, slot.shape, 1)
    lane = lax.broadcasted_iota(jnp.int32, (N_EXPERTS, LANES), 1)
    starts = jnp.where(lane == SEQ // tm, float(CAPACITY), 0.0)
    for i in range(1, SEQ // tm):
        before = jnp.where(slot >= 0, jnp.where(token < i * tm, 1.0, 0.0), 0.0)
        starts = starts + jnp.where(lane == i, jnp.sum(before, axis=1, keepdims=True), 0.0)
    start_ref[...] = starts.astype(jnp.int32)


def _select(logits):
    return pl.pallas_call(
        _select_kernel,
        grid=(BATCH,),
        in_specs=[_const_spec(logits.shape)],
        out_specs=[pl.BlockSpec((None, N_EXPERTS, SEQ), lambda b: (b, 0, 0)),
                   pl.BlockSpec((None, SEQ, LANES), lambda b: (b, 0, 0)),
                   pl.BlockSpec((None, N_EXPERTS, SEQ), lambda b: (b, 0, 0)),
                   pl.BlockSpec((None, N_EXPERTS, LANES), lambda b: (b, 0, 0))],
        out_shape=[jax.ShapeDtypeStruct((BATCH, N_EXPERTS, SEQ), jnp.int32),
                   jax.ShapeDtypeStruct((BATCH, SEQ, LANES), jnp.int32),
                   jax.ShapeDtypeStruct((BATCH, N_EXPERTS, SEQ), F32),
                   jax.ShapeDtypeStruct((BATCH, N_EXPERTS, LANES), jnp.int32)],
        scratch_shapes=[pltpu.VMEM((BATCH * N_EXPERTS, SEQ), F32),
                        pltpu.VMEM((BATCH * N_EXPERTS, SEQ), F32)],
        compiler_params=_params("arbitrary"),
        name="expert_select",
    )(logits)


SLOT_WINDOW = 64
MXU_DEPTH = 256
BF16_ROWS = 16


def _tile_windows(start_ref, b, i):
    stride = start_ref.shape[0] // (BATCH * N_EXPERTS)
    base, fits = [], None
    for e in range(N_EXPERTS):
        at = (b * N_EXPERTS + e) * stride + i
        lo = jnp.minimum(start_ref[at] // BF16_ROWS * BF16_ROWS, CAPACITY - SLOT_WINDOW)
        ok = start_ref[at + 1] - lo <= SLOT_WINDOW
        fits = ok if fits is None else fits & ok
        base.append(pl.multiple_of(lo, BF16_ROWS))
    return base, fits


def _gather_kernel(start_ref, slot_ref, w_ref, hb_ref, xe_ref, wsl_ref):
    b = pl.program_id(0)
    i = pl.program_id(1)
    tokens = slot_ref.shape[1]

    @pl.when(i == 0)
    def _():
        xe_ref[...] = jnp.zeros(xe_ref.shape, BF16)
        wsl_ref[...] = jnp.zeros(wsl_ref.shape, F32)

    def add_rows(e, rows, hit, picked):
        xe_ref[e, rows, :] = xe_ref[e, rows, :] + picked.astype(BF16)
        wslot = jnp.sum(jnp.where(hit, w_ref[e:e + 1, :], 0.0), axis=1, keepdims=True)
        wsl_ref[e, rows, :] = wsl_ref[e, rows, :] + jnp.broadcast_to(wslot, (hit.shape[0], LANES))

    base, fits = _tile_windows(start_ref, b, i)

    @pl.when(fits)
    def _():
        c = lax.broadcasted_iota(jnp.int32, (SLOT_WINDOW, tokens), 0)
        hits = [slot_ref[e:e + 1, :] - base[e] == c for e in range(N_EXPERTS)]
        onehot = jnp.concatenate([jnp.where(m, 1.0, 0.0).astype(BF16) for m in hits], axis=0)
        picked = _dot(onehot, hb_ref[...])
        for e in range(N_EXPERTS):
            add_rows(e, pl.ds(base[e], SLOT_WINDOW), hits[e],
                     picked[e * SLOT_WINDOW:(e + 1) * SLOT_WINDOW])

    @pl.when(jnp.logical_not(fits))
    def _():
        c = lax.broadcasted_iota(jnp.int32, (CAPACITY, tokens), 0)
        for e in range(N_EXPERTS):
            hit = slot_ref[e:e + 1, :] == c
            picked = _dot(jnp.where(hit, 1.0, 0.0).astype(BF16), hb_ref[...])
            add_rows(e, slice(None), hit, picked)


def _gather(slot_row, w_row, starts, hb):
    tm = TILE["combine"]
    nt = SEQ // tm
    grid_spec = pltpu.PrefetchScalarGridSpec(
        num_scalar_prefetch=1,
        grid=(BATCH, nt),
        in_specs=[pl.BlockSpec((None, N_EXPERTS, tm), lambda b, i, s: (b, 0, i)),
                  pl.BlockSpec((None, N_EXPERTS, tm), lambda b, i, s: (b, 0, i)),
                  pl.BlockSpec((tm, D_MODEL), lambda b, i, s: (b * nt + i, 0))],
        out_specs=[pl.BlockSpec((N_EXPERTS, None, CAPACITY, D_MODEL), lambda b, i, s: (0, b, 0, 0)),
                   pl.BlockSpec((N_EXPERTS, None, CAPACITY, LANES), lambda b, i, s: (0, b, 0, 0))])
    return pl.pallas_call(
        _gather_kernel,
        grid_spec=grid_spec,
        out_shape=[jax.ShapeDtypeStruct((N_EXPERTS, BATCH, CAPACITY, D_MODEL), BF16),
                   jax.ShapeDtypeStruct((N_EXPERTS, BATCH, CAPACITY, LANES), F32)],
        compiler_params=_params("parallel", "arbitrary"),
        name="expert_gather",
    )(starts[:, :, :nt + 1].reshape(-1), slot_row, w_row, hb)


def _expert_kernel(xe_ref, wsl_ref, wg_ref, wu_ref, wd_ref, ye_ref, wg_s, wu_s, wd_s, *, span):
    ph = pl.program_id(0)
    i = pl.program_id(1)
    rows = pl.ds(pl.multiple_of(i * span, span), span)
    nxt = ph % 2
    wg_s[nxt, rows, :] = wg_ref[rows, :].astype(BF16)
    wu_s[nxt, rows, :] = wu_ref[rows, :].astype(BF16)
    wd_s[nxt, rows, :] = wd_ref[rows, :].astype(BF16)

    @pl.when(ph > 0)
    def _():
        cur = (ph + 1) % 2
        xe = xe_ref[...]
        g = _dot(xe, wg_s[cur])
        u = _dot(xe, wu_s[cur])
        act = (g * jax.nn.sigmoid(g) * u).astype(BF16)
        ye_ref[...] = (_dot(act, wd_s[cur]) * wsl_ref[:, 0:1]).astype(BF16)


def _experts(xe, wsl, wg, wu, wd, layer):
    rows = BATCH * CAPACITY
    tm = TILE["expert"]
    nt = rows // tm
    last = N_EXPERTS - 1
    wspec = lambda a: pl.BlockSpec((None, None) + a.shape[2:],
                                   lambda ph, i: (layer, jnp.minimum(ph, last), 0, 0))
    data = lambda n: pl.BlockSpec(
        (None, tm, n), lambda ph, i: (jnp.maximum(ph - 1, 0), jnp.where(ph > 0, i, 0), 0))
    return pl.pallas_call(
        functools.partial(_expert_kernel, span=D_MODEL // nt),
        grid=(N_EXPERTS + 1, nt),
        in_specs=[data(D_MODEL), data(LANES), wspec(wg), wspec(wu), wspec(wd)],
        out_specs=data(D_MODEL),
        out_shape=jax.ShapeDtypeStruct((N_EXPERTS, rows, D_MODEL), BF16),
        scratch_shapes=[pltpu.VMEM((2, D_MODEL, D_FF), BF16), pltpu.VMEM((2, D_MODEL, D_FF), BF16),
                        pltpu.VMEM((2, D_FF, D_MODEL), BF16)],
        compiler_params=_params("arbitrary", "arbitrary"),
        name="expert_ffn",
    )(xe.reshape(N_EXPERTS, rows, D_MODEL), wsl.reshape(N_EXPERTS, rows, LANES), wg, wu, wd)


def _combine_kernel(start_ref, x_ref, ye_ref, slot_ref, p_ref, g3_ref, wpg_ref, wpp_ref,
                    gf_ref, out_ref, moe_s, *, final_norm):
    slot = slot_ref[...]
    rows = slot.shape[0]
    base, fits = _tile_windows(start_ref, pl.program_id(0), pl.program_id(1))

    @pl.when(fits)
    def _():
        c = lax.broadcasted_iota(jnp.int32, (rows, SLOT_WINDOW), 1)
        parts = []
        group = MXU_DEPTH // SLOT_WINDOW
        for e0 in range(0, N_EXPERTS, group):
            hot, win = [], []
            for e in range(e0, e0 + group):
                hot.append(jnp.where(slot[:, e:e + 1] - base[e] == c, 1.0, 0.0).astype(BF16))
                win.append(ye_ref[e, pl.ds(base[e], SLOT_WINDOW), :])
            parts.append(_dot(jnp.concatenate(hot, axis=1), jnp.concatenate(win, axis=0)))
        while len(parts) > 1:
            parts = [u + v for u, v in zip(parts[::2], parts[1::2])]
        moe_s[...] = parts[0]

    @pl.when(jnp.logical_not(fits))
    def _():
        c = lax.broadcasted_iota(jnp.int32, (rows, CAPACITY), 1)
        onehot = jnp.concatenate(
            [jnp.where(slot[:, e:e + 1] == c, 1.0, 0.0).astype(BF16) for e in range(N_EXPERTS)],
            axis=1)
        moe_s[...] = _dot(onehot, ye_ref[...].reshape(N_EXPERTS * CAPACITY, D_MODEL))

    acc = x_ref[...] + moe_s[...]
    h = _rms(acc, g3_ref[...]).astype(BF16)
    gate = jax.nn.sigmoid(_dot(h, wpg_ref[...]))
    y = acc + gate * _dot(p_ref[...].astype(BF16), wpp_ref[...])
    if final_norm:
        y = _rms(y, gf_ref[...])
    out_ref[...] = y


def _combine(x, ye, slot_col, starts, p, layer, g3, wpg, wpp, gf, final_norm):
    tm = TILE["combine"]
    nt = SEQ // tm
    p0 = layer * (TOKENS // tm)
    const = lambda a: pl.BlockSpec(a.shape, lambda b, i, s: (0,) * a.ndim,
                                   pipeline_mode=pl.Buffered(1))
    grid_spec = pltpu.PrefetchScalarGridSpec(
        num_scalar_prefetch=1,
        grid=(BATCH, nt),
        in_specs=[pl.BlockSpec((tm, D_MODEL), lambda b, i, s: (b * nt + i, 0)),
                  pl.BlockSpec((N_EXPERTS, None, CAPACITY, D_MODEL), lambda b, i, s: (0, b, 0, 0)),
                  pl.BlockSpec((None, tm, LANES), lambda b, i, s: (b, i, 0)),
                  pl.BlockSpec((tm, PLE_DIM), lambda b, i, s: (p0 + b * nt + i, 0)),
                  const(g3), const(wpg), const(wpp), const(gf)],
        out_specs=pl.BlockSpec((tm, D_MODEL), lambda b, i, s: (b * nt + i, 0)),
        scratch_shapes=[pltpu.VMEM((tm, D_MODEL), F32)])
    return pl.pallas_call(
        functools.partial(_combine_kernel, final_norm=final_norm),
        grid_spec=grid_spec,
        out_shape=jax.ShapeDtypeStruct((TOKENS, D_MODEL), F32),
        compiler_params=_params("parallel", "parallel"),
        name="combine",
    )(starts[:, :, :nt + 1].reshape(-1), x, ye.reshape(N_EXPERTS, BATCH, CAPACITY, D_MODEL),
      slot_col, p, g3, wpg, wpp, gf)


DFT_FINE = 64
DFT_STEP = 4


def _dft_kernel(ca_ref, sa_ref, cb_ref, sb_ref, out_ref):
    cb = cb_ref[...]
    sb = sb_ref[...]
    for r in range(DFT_STEP):
        rows = slice(r * DFT_FINE, (r + 1) * DFT_FINE)
        out_ref[rows, :] = (ca_ref[r] * cb - sa_ref[r] * sb).astype(BF16)


def _dft_tables(n_points, nres):
    blk = SEQ // nres
    t = jnp.arange(SEQ, dtype=jnp.int32).reshape(blk, nres).T
    t = jnp.concatenate([t, t], axis=1).reshape(1, 2 * SEQ)
    is_sin = (jnp.arange(2 * SEQ) // blk % 2 == 1)[None, :]
    coarse = jnp.arange(HALF // DFT_FINE, dtype=jnp.int32)[:, None] * DFT_FINE
    fine = jnp.arange(DFT_FINE, dtype=jnp.int32)[:, None]
    ang = lambda f: ((f * t) % n_points).astype(F32) * (2.0 * math.pi / n_points)
    ca = jnp.cos(ang(coarse))[:, None, :]
    sa = jnp.sin(ang(coarse))[:, None, :]
    cb = jnp.where(is_sin, jnp.sin(ang(fine)), jnp.cos(ang(fine)))
    sb = jnp.where(is_sin, -jnp.cos(ang(fine)), jnp.sin(ang(fine)))
    rows = DFT_STEP * DFT_FINE
    return pl.pallas_call(
        _dft_kernel,
        grid=(HALF // rows,),
        in_specs=[pl.BlockSpec((DFT_STEP, 1, 2 * SEQ), lambda i: (i, 0, 0)),
                  pl.BlockSpec((DFT_STEP, 1, 2 * SEQ), lambda i: (i, 0, 0)),
                  _const_spec(cb.shape), _const_spec(sb.shape)],
        out_specs=pl.BlockSpec((rows, 2 * SEQ), lambda i: (i, 0)),
        out_shape=jax.ShapeDtypeStruct((HALF, 2 * SEQ), BF16),
        compiler_params=_params("parallel"),
        name="dft_tables",
    )(ca, sa, cb, sb)


def _fnet_group_tables():
    gc = D_MIX // FN_GROUPS
    i = lax.broadcasted_iota(jnp.int32, (D_MIX, D_MIX), 0)
    j = lax.broadcasted_iota(jnp.int32, (D_MIX, D_MIX), 1)
    same = (i // gc) == (j // gc)
    ang = (((i % gc) * (j % gc)) % gc).astype(F32) * (2.0 * math.pi / gc)
    return (jnp.where(same, jnp.cos(ang), 0.0).astype(BF16),
            jnp.where(same, jnp.sin(ang), 0.0).astype(BF16))


def _hyena_features():
    t01 = jnp.linspace(0.0, 1.0, SEQ, dtype=F32)[:, None]
    bands = jnp.linspace(1e-4, HY_BANDS - 1, HY_BANDS, dtype=F32)
    ang = 2.0 * math.pi * jnp.arange(SEQ, dtype=F32)[:, None] * bands / SEQ
    z = jnp.concatenate([t01, jnp.cos(ang), -jnp.sin(ang)], axis=-1)
    z = jnp.pad(z, ((0, 0), (0, LANES - HY_EMB)))
    max_decay = math.log(HY_TARGET) / HY_FAST_DECAY
    min_decay = math.log(HY_TARGET) / HY_SLOW_DECAY
    deltas = jnp.linspace(min_decay, max_decay, D_MIX, dtype=F32)
    window = jnp.exp(-t01 * jnp.abs(deltas))
    return z, window


def _rot_cols(w):
    half = w.shape[-1] // 2
    return jnp.concatenate([-w[..., half:], w[..., :half]], axis=-1)


def _rope_tables():
    inv = ROPE_THETA ** (-jnp.arange(0, MLA_ROPE, 2, dtype=F32) / MLA_ROPE)
    ang = jnp.arange(SEQ, dtype=F32)[:, None] * inv
    cos = jnp.concatenate([jnp.cos(ang), jnp.cos(ang)], axis=-1)
    sin = jnp.concatenate([jnp.sin(ang), jnp.sin(ang)], axis=-1)
    scale = (MLA_NOPE + MLA_ROPE) ** -0.5 * LOG2E
    pad = MLA_HEAD_PAD - MLA_NOPE - MLA_ROPE
    one = jnp.ones((SEQ, MLA_NOPE), F32)
    zero = jnp.zeros((SEQ, MLA_NOPE), F32)
    zpad = jnp.zeros((SEQ, pad), F32)
    cosq = jnp.tile(jnp.concatenate([one, cos, zpad], axis=-1) * scale, (1, MLA_HEADS))
    sinq = jnp.tile(jnp.concatenate([zero, sin, zpad], axis=-1) * scale, (1, MLA_HEADS))
    csk = jnp.concatenate([cos, sin, jnp.zeros((SEQ, LANES - 2 * MLA_ROPE), F32)], axis=-1)
    return cosq, sinq, csk


def _mla_weights(w_uq, w_ukv):
    pad = MLA_HEAD_PAD - MLA_NOPE - MLA_ROPE
    wq = w_uq.reshape(MLA_Q_RANK, MLA_HEADS, MLA_NOPE + MLA_ROPE)
    nope, pe = wq[..., :MLA_NOPE], wq[..., MLA_NOPE:]
    zp = jnp.zeros((MLA_Q_RANK, MLA_HEADS, pad), F32)
    wqa = jnp.concatenate([nope, pe, zp], axis=-1).reshape(MLA_Q_RANK, MLA_QK)
    wqb = jnp.concatenate([jnp.zeros_like(nope), _rot_cols(pe), zp], axis=-1)
    wqb = wqb.reshape(MLA_Q_RANK, MLA_QK)
    wkv = w_ukv.reshape(MLA_KV_RANK, MLA_HEADS, MLA_NOPE + MLA_V)
    knope, v = wkv[..., :MLA_NOPE], wkv[..., MLA_NOPE:]
    wk = jnp.concatenate(
        [knope, jnp.zeros((MLA_KV_RANK, MLA_HEADS, MLA_HEAD_PAD - MLA_NOPE), F32)], axis=-1)
    wk = wk.reshape(MLA_KV_RANK, MLA_QK)
    wv = v.reshape(MLA_KV_RANK, MLA_HEADS * MLA_V)
    r = lax.broadcasted_iota(jnp.int32, (LANES, MLA_QK), 0)
    c = lax.broadcasted_iota(jnp.int32, (LANES, MLA_QK), 1)
    epe = jnp.where((r < 2 * MLA_ROPE) & (c % MLA_HEAD_PAD == MLA_NOPE + r % MLA_ROPE), 1.0, 0.0)
    return (wqa.astype(BF16), wqb.astype(BF16), wk.astype(BF16), wv.astype(BF16),
            epe.astype(BF16))


def kernel(x, p, norm1_g, w_in, b_gate, hy_conv_w, hy_conv_b, hf_w1, hf_b1, hf_freq, hf_w2,
           hf_b2, hf_w3, hy_skip, q_norm_g, w_uq, kv_norm_g, w_ukv, rpb, w_br, w_out, norm2_g,
           w_router, w_e_gate, w_e_up, w_e_down, norm3_g, w_ple_gate, w_ple_proj, final_g):
    conv_tab = _dft_tables(2 * SEQ, 4)
    fnet_tab = _dft_tables(SEQ, 2)
    fnet_cg, fnet_sg = _fnet_group_tables()
    zfeat, window = _hyena_features()
    cosq, sinq, csk = _rope_tables()
    row = lambda a: a.reshape(1, -1)

    xt = x.reshape(TOKENS, D_MODEL)
    w_in_t = jnp.swapaxes(w_in, 1, 2)
    for i in range(DEPTH):
        g1 = row(norm1_g[i])
        wa, wna, wgate = _inproj_weights(w_in_t, i)
        wqa, wqb, wk, wv, epe = _mla_weights(w_uq[i], w_ukv[i])

        u_hy, u_fn, q, k, v, naq, nak, nav = _inproj(
            xt, g1, wa, wna, row(q_norm_g[i]), wqa, wqb, row(kv_norm_g[i]), wk, wv, epe,
            cosq, sinq, csk)

        w1 = jnp.pad(hf_w1[i], ((0, LANES - HY_EMB), (0, 0)))
        kf, kny = _hyena_filter(zfeat, window, w1, row(hf_b1[i]), hf_freq[i], hf_w2[i],
                                row(hf_b2[i]), hf_w3[i], conv_tab)
        conv_b = row(hy_conv_b[i])
        z1 = _hyena_stage(u_hy, 0, u_hy, 2, hy_conv_w[i], conv_b, conv_tab, kf[0],
                          kny[0], row(hy_skip[i, 0]), True, F32)
        y_hy = _hyena_stage(u_hy, 1, z1, 0, hy_conv_w[i], conv_b, conv_tab, kf[1],
                            kny[1], row(hy_skip[i, 1]), False, BF16)
        y_fn = _fnet(u_fn, fnet_tab, fnet_cg, fnet_sg)
        y_mla = _mla(q, k, v)
        y_na = _neighborhood(naq, nak, nav, _na_bias_tiles(rpb[i]))
        wr_t = jnp.pad(w_router[i].T, ((0, LANES - N_EXPERTS), (0, 0)))
        xt, hb, logits = _merge(xt, g1, y_hy, y_fn, y_mla, y_na, wgate, row(b_gate[i]),
                                w_br[i].astype(BF16), w_out[i].astype(BF16),
                                row(norm2_g[i]), wr_t)
        slot_row, slot_col, w_row, starts = _select(logits)
        xe, wsl = _gather(slot_row, w_row, starts, hb)
        ye = _experts(xe, wsl, w_e_gate, w_e_up, w_e_down, i)
        xt = _combine(xt, ye, slot_col, starts, p.reshape(DEPTH * TOKENS, PLE_DIM), i,
                      row(norm3_g[i]), w_ple_gate[i].astype(BF16),
                      w_ple_proj[i].astype(BF16), row(final_g), i == DEPTH - 1)
    return xt.reshape(BATCH, SEQ, D_MODEL)
```

```python
import functools
import math

import jax
import jax.numpy as jnp
from jax import lax
from jax.experimental import pallas as pl
from jax.experimental.pallas import tpu as pltpu

F32 = jnp.float32
BF16 = jnp.bfloat16

D_MODEL = 1024
BATCH = 8
SEQ = 2048
DEPTH = 2
TOKENS = BATCH * SEQ

GRID_W = 64
GRID_R = SEQ // GRID_W
D_MIX = 256
N_BRANCH = 4
EPS = 1e-6
HY_ORDER = 2
HY_BANDS = 16
HY_EMB = 2 * HY_BANDS + 1
HY_FFN = 64
HY_TARGET = 1e-2
HY_FAST_DECAY = 0.3
HY_SLOW_DECAY = 1.5
FN_GROUPS = 4
MLA_HEADS = 4
MLA_NOPE = 64
MLA_ROPE = 32
MLA_V = 64
MLA_Q_RANK = 256
MLA_KV_RANK = 128
ROPE_THETA = 10000.0
NA_HEADS = 4
NA_HEAD_DIM = D_MIX // NA_HEADS
NA_WIN_R = 8
NA_WIN_C = 16
N_EXPERTS = 16
CAPACITY = 2 * SEQ // N_EXPERTS
D_FF = 1024
PLE_DIM = 256

HY_COLS = 3 * D_MIX
OFF_FN = HY_COLS
OFF_CQ = OFF_FN + D_MIX
OFF_CKV = OFF_CQ + MLA_Q_RANK
OFF_KPE = OFF_CKV + MLA_KV_RANK
OFF_NA = OFF_KPE + MLA_ROPE
OFF_GATE = OFF_NA + 3 * D_MIX

LANES = 128
MLA_HEAD_PAD = 128
MLA_QK = MLA_HEADS * MLA_HEAD_PAD
WA_COLS = 1536
NEG_BIG = -1e30
LOG2E = math.log2(math.e)

TILE = dict(inproj=512, mla=512, merge=512, gather=256, expert=512, combine=512)
MOE_FINE = 256
NA_QROWS = 8
NA_KROWS = 16
NA_QBLK = NA_QROWS * GRID_W
NA_KBLK = NA_KROWS * GRID_W
VMEM_LIMIT = 56 * 1024 * 1024


def _params(*sem):
    return pltpu.CompilerParams(dimension_semantics=sem, vmem_limit_bytes=VMEM_LIMIT)


def _const_spec(shape):
    nd = len(shape)
    return pl.BlockSpec(shape, lambda *_: (0,) * nd, pipeline_mode=pl.Buffered(1))


def _rms(x, g):
    return x * lax.rsqrt(jnp.mean(x * x, axis=-1, keepdims=True) + EPS) * g


def _dot(a, b):
    return jnp.dot(a, b, preferred_element_type=F32)


def _dot_nt(a, b):
    return lax.dot_general(a, b, (((1,), (1,)), ((), ())), preferred_element_type=F32)


def _split2(x):
    hi = x.astype(BF16)
    lo = (x - hi.astype(F32)).astype(BF16)
    return hi, lo


def _inproj_kernel(x_ref, g1_ref, wa_ref, wna_ref, qg_ref, wqa_ref, wqb_ref, kvg_ref,
                   wk_ref, wv_ref, epe_ref, cosq_ref, sinq_ref, csk_ref,
                   uhy_ref, ufn_ref, q_ref, k_ref, v_ref, naq_ref, nak_ref, nav_ref):
    h = _rms(x_ref[...], g1_ref[...]).astype(BF16)
    ua = _dot_nt(h, wa_ref[...])
    uhy_ref[...] = ua[:, :HY_COLS]
    ufn_ref[...] = ua[:, OFF_FN:OFF_CQ]
    cqn = _rms(ua[:, OFF_CQ:OFF_CKV], qg_ref[...]).astype(BF16)
    q = _dot(cqn, wqa_ref[...]) * cosq_ref[...] + _dot(cqn, wqb_ref[...]) * sinq_ref[...]
    q_ref[...] = q.astype(BF16)
    kvn = _rms(ua[:, OFF_CKV:OFF_KPE], kvg_ref[...]).astype(BF16)
    kpe = ua[:, OFF_KPE:WA_COLS] * csk_ref[...]
    k = _dot(kvn, wk_ref[...]) + _dot(kpe.astype(BF16), epe_ref[...])
    k_ref[...] = k.astype(BF16)
    v_ref[...] = _dot(kvn, wv_ref[...]).astype(BF16)
    una = _dot_nt(h, wna_ref[...])
    naq_ref[...] = (una[:, :D_MIX] * (NA_HEAD_DIM ** -0.5 * LOG2E)).astype(BF16)
    nak_ref[...] = una[:, D_MIX:2 * D_MIX].astype(BF16)
    nav_ref[...] = una[:, 2 * D_MIX:].astype(BF16)


def _inproj(x, g1, wa, wna, qg, wqa, wqb, kvg, wk, wv, epe, cosq, sinq, csk):
    tm = TILE["inproj"]
    nt = SEQ // tm
    row = lambda n: pl.BlockSpec((tm, n), lambda i: (i, 0))
    pos = lambda n: pl.BlockSpec((tm, n), lambda i: (i % nt, 0))
    outs = [(HY_COLS, F32), (D_MIX, F32), (MLA_QK, BF16), (MLA_QK, BF16), (D_MIX, BF16),
            (D_MIX, BF16), (D_MIX, BF16), (D_MIX, BF16)]
    return pl.pallas_call(
        _inproj_kernel,
        grid=(TOKENS // tm,),
        in_specs=[row(D_MODEL), _const_spec(g1.shape), _const_spec(wa.shape),
                  _const_spec(wna.shape), _const_spec(qg.shape), _const_spec(wqa.shape),
                  _const_spec(wqb.shape), _const_spec(kvg.shape), _const_spec(wk.shape),
                  _const_spec(wv.shape), _const_spec(epe.shape),
                  pos(MLA_QK), pos(MLA_QK), pos(LANES)],
        out_specs=[row(n) for n, _ in outs],
        out_shape=[jax.ShapeDtypeStruct((TOKENS, n), dt) for n, dt in outs],
        compiler_params=_params("parallel"),
        name="inproj",
    )(x, g1, wa, wna, qg, wqa, wqb, kvg, wk, wv, epe, cosq, sinq, csk)


WPREP_ROWS = 512


def _inproj_weights_kernel(w_ref, wa_ref, wna_ref, wg_ref):
    i = pl.program_id(0)

    @pl.when(i == 0)
    def _():
        half = MLA_ROPE // 2
        wa_ref[:OFF_NA, :] = w_ref[:OFF_NA, :].astype(BF16)
        wa_ref[OFF_NA:OFF_NA + half, :] = (-w_ref[OFF_KPE + half:OFF_NA, :]).astype(BF16)
        wa_ref[OFF_NA + half:OFF_NA + MLA_ROPE, :] = w_ref[OFF_KPE:OFF_KPE + half, :].astype(BF16)
        wa_ref[OFF_NA + MLA_ROPE:, :] = jnp.zeros((WA_COLS - OFF_NA - MLA_ROPE, D_MODEL), BF16)
        wna_ref[...] = w_ref[OFF_NA:OFF_GATE, :].astype(BF16)

    start = pl.multiple_of(OFF_GATE + i * WPREP_ROWS, MLA_ROPE)
    wg_ref[...] = w_ref[pl.ds(start, WPREP_ROWS), :].astype(BF16)


def _inproj_weights(w_in_t, layer):
    n_in = w_in_t.shape[1]
    rows = (WA_COLS, OFF_GATE - OFF_NA, n_in - OFF_GATE)
    whole = lambda n: pl.BlockSpec((n, D_MODEL), lambda i: (0, 0))
    return pl.pallas_call(
        _inproj_weights_kernel,
        grid=(rows[2] // WPREP_ROWS,),
        in_specs=[pl.BlockSpec((None, n_in, D_MODEL), lambda i: (layer, 0, 0),
                               pipeline_mode=pl.Buffered(1))],
        out_specs=[whole(rows[0]), whole(rows[1]),
                   pl.BlockSpec((WPREP_ROWS, D_MODEL), lambda i: (i, 0))],
        out_shape=[jax.ShapeDtypeStruct((n, D_MODEL), BF16) for n in rows],
        compiler_params=_params("arbitrary"),
        name="inproj_weights",
    )(w_in_t)


HALF = SEQ // 2
HY_SEQS = 1


def _residue_rows(ref, r, nres):
    rows = pl.ds(r, SEQ // nres, stride=nres)
    if isinstance(ref, tuple):
        return jnp.concatenate([h[rows, :] for h in ref], axis=1)
    return jnp.concatenate([ref[j, rows, :] for j in range(ref.shape[0])], axis=1)


def _store_slabs(ref, value, rows=slice(None)):
    for j in range(ref.shape[0]):
        ref[j, rows, :] = value[:, j * LANES:(j + 1) * LANES]


def _table_cols(cs_ref, r, nres, part=None):
    blk = SEQ // nres
    lo = 2 * blk * r
    if part is None:
        return cs_ref[:, lo:lo + 2 * blk]
    return cs_ref[:, lo + part * blk:lo + (part + 1) * blk]


def _dft_fwd(cs_ref, src_ref, want_cos=True, want_sin=True, split=False):
    def prod(part, z, r):
        tab = _table_cols(cs_ref, r, 4, part)
        if split:
            hi, lo = _split2(z)
            return _dot(tab, hi) + _dot(tab, lo)
        return _dot(tab, z.astype(BF16))

    pc, ps = [None] * 4, [None] * 4
    for r in range(4):
        z = _residue_rows(src_ref, r, 4)
        odd = r % 2 == 1
        if want_cos or odd:
            pc[r] = prod(0, z, r)
        if want_sin or odd:
            ps[r] = prod(1, z, r)
    a = b = None
    if want_cos:
        a = ((pc[0] + pc[2]) + (pc[1] + pc[3]), (pc[0] - pc[2]) + (ps[3] - ps[1]))
    if want_sin:
        b = ((ps[0] + ps[2]) + (ps[1] + ps[3]), (ps[0] - ps[2]) + (pc[1] - pc[3]))
    return a, b


def _dft_inv(cs_ref, yre_ref, yim_ref):
    g, h = [], {}
    for r in range(4):
        yr = _residue_rows(yre_ref, r, 4).astype(BF16)
        yi = _residue_rows(yim_ref, r, 4).astype(BF16)
        tab = _table_cols(cs_ref, r, 4)
        g.append(_dot(tab, jnp.concatenate([yr, -yi], axis=0)))
        if r % 2 == 1:
            h[r] = _dot(tab, jnp.concatenate([yi, yr], axis=0))
    return (g[0] + g[2]) + (g[1] + g[3]), (g[0] - g[2]) + (h[3] - h[1])


def _hyena_filter_kernel(z_ref, win_ref, w1_ref, b1_ref, freq_ref, w2_ref, b2_ref, w3_ref,
                         cs_ref, kf_ref, kny_ref, ksum_s, kdif_s):
    hp = lax.Precision.HIGHEST
    freq = freq_ref[...]
    hf = jnp.sin(freq[0:1] * (jnp.dot(z_ref[...], w1_ref[...], precision=hp,
                                      preferred_element_type=F32) + b1_ref[...]))
    hf = jnp.sin(freq[1:2] * (jnp.dot(hf, w2_ref[...], precision=hp,
                                      preferred_element_type=F32) + b2_ref[...]))
    hf = jnp.dot(hf, w3_ref[...], precision=hp, preferred_element_type=F32)
    win = win_ref[...]
    t = lax.broadcasted_iota(jnp.int32, (SEQ, D_MIX), 0)
    sgn = (1 - 2 * (t & 1)).astype(F32)
    fwd = hf[:, :D_MIX] * win
    bwd = jnp.where(t == 0, 0.0, hf[:, D_MIX:] * win)
    nrm = lax.rsqrt(jnp.sum(fwd * fwd + bwd * bwd, axis=0, keepdims=True) + EPS)
    ksum = (fwd + bwd) * nrm
    _store_slabs(ksum_s, ksum)
    _store_slabs(kdif_s, (bwd - fwd) * nrm)
    kre, _ = _dft_fwd(cs_ref, ksum_s, want_sin=False, split=True)
    _, kim = _dft_fwd(cs_ref, kdif_s, want_cos=False, split=True)
    wf = 2.0 / (2 * SEQ)
    for part, spec in enumerate((kre, kim)):
        lo = spec[0] * wf
        kf_ref[part, :HALF, :] = lo
        kf_ref[part, 0:1, :] = lo[0:1] * 0.5
        kf_ref[part, HALF:, :] = spec[1] * wf
    kny = jnp.sum(ksum * sgn, axis=0, keepdims=True) * (1.0 / (2 * SEQ))
    kny_ref[...] = jnp.broadcast_to(kny, (8, D_MIX))


def _hyena_filter(zfeat, window, w1, b1, freq, w2, b2, w3, table):
    consts = (zfeat, window, w1, b1, freq, w2, b2)
    return pl.pallas_call(
        _hyena_filter_kernel,
        grid=(HY_ORDER,),
        in_specs=[_const_spec(a.shape) for a in consts]
        + [pl.BlockSpec((HY_FFN, 2 * D_MIX), lambda o: (0, o)),
           _const_spec(table.shape)],
        out_specs=[pl.BlockSpec((None, 2, SEQ, D_MIX), lambda o: (o, 0, 0, 0)),
                   pl.BlockSpec((None, 8, D_MIX), lambda o: (o, 0, 0))],
        out_shape=[jax.ShapeDtypeStruct((HY_ORDER, 2, SEQ, D_MIX), F32),
                   jax.ShapeDtypeStruct((HY_ORDER, 8, D_MIX), F32)],
        scratch_shapes=[pltpu.VMEM((D_MIX // LANES, SEQ, LANES), F32)] * 2,
        compiler_params=_params("arbitrary"),
        name="hyena_filter",
    )(*consts, w3, table)


def _short_conv(u, w, b):
    t = lax.broadcasted_iota(jnp.int32, u.shape, 0)
    prev = jnp.where(t == 0, 0.0, pltpu.roll(u, 1, 0))
    nxt = jnp.where(t == SEQ - 1, 0.0, pltpu.roll(u, SEQ - 1, 0))
    return prev * w[0:1] + u * w[1:2] + nxt * w[2:3] + b


def _hyena_stage_kernel(gate_ref, src_ref, wg_ref, bg_ref, ws_ref, bs_ref, cs_ref,
                        kf_ref, kny_ref, skip_ref, out_ref, z_s, yre_s, yim_s, gate_s, rest_s,
                        *, conv_src):
    for q in range(HY_SEQS):
        seq = slice(q * SEQ, (q + 1) * SEQ)
        zq, yre_q, yim_q = z_s.at[q], yre_s.at[q], yim_s.at[q]
        z = src_ref[seq, :]
        if conv_src:
            z = _short_conv(z, ws_ref[...], bs_ref[...])
        _store_slabs(zq, z)
        t = lax.broadcasted_iota(jnp.int32, z.shape, 0)
        sgn = (1 - 2 * (t & 1)).astype(F32)
        nyq = jnp.sum(z * sgn, axis=0, keepdims=True) * kny_ref[0:1]
        gate = _short_conv(gate_ref[seq, :], wg_ref[...], bg_ref[...])
        gate_s[seq, :] = gate
        rest_s[seq, :] = gate * (sgn * nyq + z * skip_ref[...])
        a, b = _dft_fwd(cs_ref, zq)
        for half in range(2):
            rows = slice(half * HALF, (half + 1) * HALF)
            kre = kf_ref[0, rows, :]
            kim = kf_ref[1, rows, :]
            _store_slabs(yre_q, a[half] * kre + b[half] * kim, rows)
            _store_slabs(yim_q, a[half] * kim - b[half] * kre, rows)
        for half, y in enumerate(_dft_inv(cs_ref, yre_q, yim_q)):
            rows = slice(q * SEQ + half * HALF, q * SEQ + (half + 1) * HALF)
            out_ref[rows, :] = (gate_s[rows, :] * y + rest_s[rows, :]).astype(out_ref.dtype)


def _hyena_stage(u_hy, gate_blk, src, src_blk, conv_w, conv_b, table, kf, kny, skip,
                 conv_src, out_dtype):
    rows = HY_SEQS * SEQ
    col = lambda blk: pl.BlockSpec((rows, D_MIX), lambda b: (b, blk))
    wcol = lambda blk, r: pl.BlockSpec((r, D_MIX), lambda b: (0, blk))
    ws_blk = src_blk if conv_src else 0
    return pl.pallas_call(
        functools.partial(_hyena_stage_kernel, conv_src=conv_src),
        grid=(BATCH // HY_SEQS,),
        in_specs=[col(gate_blk), col(src_blk), wcol(gate_blk, 3), wcol(gate_blk, 1),
                  wcol(ws_blk, 3), wcol(ws_blk, 1), _const_spec(table.shape),
                  _const_spec(kf.shape), _const_spec(kny.shape),
                  _const_spec(skip.shape)],
        out_specs=pl.BlockSpec((rows, D_MIX), lambda b: (b, 0)),
        out_shape=jax.ShapeDtypeStruct((TOKENS, D_MIX), out_dtype),
        scratch_shapes=[pltpu.VMEM((HY_SEQS, D_MIX // LANES, SEQ, LANES), F32)] * 3
        + [pltpu.VMEM((rows, D_MIX), F32)] * 2,
        compiler_params=_params("parallel"),
        name="hyena_stage",
    )(u_hy, src, conv_w, conv_b, conv_w, conv_b, table, kf, kny, skip)


def _fnet_kernel(xa_ref, xb_ref, cs_ref, cg_ref, sg_ref, out_ref):
    parts = []
    for r in range(2):
        xb = _residue_rows((xa_ref, xb_ref), r, 2).astype(BF16)
        xc = _dot(xb, cg_ref[...]).astype(BF16)
        xs = _dot(xb, sg_ref[...]).astype(BF16)
        parts.append(_dot(_table_cols(cs_ref, r, 2), jnp.concatenate([xc, -xs], axis=0)))
    scale = (SEQ * D_MIX // FN_GROUPS) ** -0.5
    out_ref[:HALF, :] = ((parts[0] + parts[1]) * scale).astype(out_ref.dtype)
    out_ref[HALF:, :] = ((parts[0] - parts[1]) * scale).astype(out_ref.dtype)


def _fnet(u_fn, table, cg, sg):
    return pl.pallas_call(
        _fnet_kernel,
        grid=(BATCH,),
        in_specs=[pl.BlockSpec((SEQ, LANES), lambda b: (b, 0)),
                  pl.BlockSpec((SEQ, LANES), lambda b: (b, 1)), _const_spec(table.shape),
                  _const_spec(cg.shape), _const_spec(sg.shape)],
        out_specs=pl.BlockSpec((SEQ, D_MIX), lambda b: (b, 0)),
        out_shape=jax.ShapeDtypeStruct((TOKENS, D_MIX), BF16),
        compiler_params=_params("parallel"),
        name="fnet",
    )(u_fn, u_fn, table, cg, sg)


def _softmax2_pv(s2, v):
    m = jnp.max(s2, axis=-1, keepdims=True)
    p = jnp.exp2(s2 - m)
    l = jnp.sum(p, axis=-1, keepdims=True)
    return _dot(p.astype(BF16), v) / l


def _mla_kernel(q_ref, k_ref, v_ref, out_ref):
    v = v_ref[...]
    head = lax.broadcasted_iota(jnp.int32, out_ref.shape, 1) // MLA_V
    acc = jnp.zeros(out_ref.shape, F32)
    for h in range(MLA_HEADS):
        sl = slice(h * MLA_HEAD_PAD, (h + 1) * MLA_HEAD_PAD)
        s2 = _dot_nt(q_ref[:, sl], k_ref[:, sl])
        acc = jnp.where(head == h, _softmax2_pv(s2, v), acc)
    out_ref[...] = acc.astype(out_ref.dtype)


def _mla(q, k, v):
    tm = TILE["mla"]
    nt = SEQ // tm
    return pl.pallas_call(
        _mla_kernel,
        grid=(BATCH, nt),
        in_specs=[pl.BlockSpec((tm, MLA_QK), lambda b, i: (b * nt + i, 0)),
                  pl.BlockSpec((SEQ, MLA_QK), lambda b, i: (b, 0)),
                  pl.BlockSpec((SEQ, D_MIX), lambda b, i: (b, 0))],
        out_specs=pl.BlockSpec((tm, D_MIX), lambda b, i: (b * nt + i, 0)),
        out_shape=jax.ShapeDtypeStruct((TOKENS, D_MIX), BF16),
        compiler_params=_params("parallel", "parallel"),
        name="mla_attention",
    )(q, k, v)


def _na_key_row0(j):
    return jnp.clip(j * NA_QROWS - NA_WIN_R // 2, 0, GRID_R - NA_KROWS)


NA_PAIRS = 2 * NA_WIN_R


def _na_kernel(q_ref, k_ref, v_ref, tile_ref, out_ref, bias_s):
    j = pl.program_id(0)
    krow0 = _na_key_row0(j)

    @pl.when(pl.program_id(1) == 0)
    def _():
        rq = j * NA_QROWS + lax.broadcasted_iota(jnp.int32, (NA_QBLK, NA_KBLK), 0) // GRID_W
        rk = krow0 + lax.broadcasted_iota(jnp.int32, (NA_QBLK, NA_KBLK), 1) // GRID_W
        rs = jnp.clip(rq - NA_WIN_R // 2, 0, GRID_R - NA_WIN_R)
        rowmask = jnp.where(rk < rs, NEG_BIG, jnp.where(rk >= rs + NA_WIN_R, NEG_BIG, 0.0))
        base = krow0 - j * NA_QROWS + NA_WIN_R
        for h in range(NA_HEADS):
            bias = jnp.concatenate(
                [jnp.concatenate(
                    [tile_ref[h, jnp.clip(base + 2 * kp - r, 0, NA_PAIRS - 1)]
                     for kp in range(NA_KROWS // 2)], axis=1)
                 for r in range(NA_QROWS)], axis=0)
            bias_s[h] = bias + rowmask

    off = pl.multiple_of(krow0 * GRID_W, GRID_W)
    q = q_ref[...]
    k = k_ref[pl.ds(off, NA_KBLK), :]
    v = v_ref[pl.ds(off, NA_KBLK), :]
    head = lax.broadcasted_iota(jnp.int32, (NA_QBLK, D_MIX), 1) // NA_HEAD_DIM
    acc = jnp.zeros((NA_QBLK, D_MIX), F32)
    for h in range(NA_HEADS):
        qh = jnp.where(head == h, q, jnp.zeros_like(q))
        s2 = _dot_nt(qh, k) + bias_s[h]
        acc = jnp.where(head == h, _softmax2_pv(s2, v), acc)
    out_ref[...] = acc.astype(out_ref.dtype)


def _neighborhood(q, k, v, tiles):
    nj = SEQ // NA_QBLK
    return pl.pallas_call(
        _na_kernel,
        grid=(nj, BATCH),
        in_specs=[pl.BlockSpec((NA_QBLK, D_MIX), lambda j, b: (b * nj + j, 0)),
                  pl.BlockSpec((SEQ, D_MIX), lambda j, b: (b, 0)),
                  pl.BlockSpec((SEQ, D_MIX), lambda j, b: (b, 0)),
                  _const_spec(tiles.shape)],
        out_specs=pl.BlockSpec((NA_QBLK, D_MIX), lambda j, b: (b * nj + j, 0)),
        out_shape=jax.ShapeDtypeStruct((TOKENS, D_MIX), BF16),
        scratch_shapes=[pltpu.VMEM((NA_HEADS, NA_QBLK, NA_KBLK), F32)],
        compiler_params=_params("parallel", "arbitrary"),
        name="neighborhood_attention",
    )(q, k, v, tiles)


def _na_bias_tiles(rpb):
    c = jnp.arange(GRID_W)
    cs = jnp.clip(c - NA_WIN_C // 2, 0, GRID_W - NA_WIN_C)
    col_ok = (c[None, :] >= cs[:, None]) & (c[None, :] < cs[:, None] + NA_WIN_C)
    dc = jnp.clip(c[None, :] - c[:, None] + (NA_WIN_C - 1), 0, 2 * NA_WIN_C - 2)
    pick = (dc[None] == jnp.arange(2 * NA_WIN_C - 1)[:, None, None]).astype(F32)
    t = jnp.einsum('hrd,dqk->hrqk', rpb.astype(F32), pick, precision=lax.Precision.HIGHEST)
    t = jnp.where(col_ok, t * LOG2E, NEG_BIG)
    t = jnp.pad(t, ((0, 0), (1, 1), (0, 0), (0, 0)))
    return jnp.concatenate([t[:, :-1], t[:, 1:]], axis=-1)


def _merge_kernel(x_ref, g1_ref, yhy_ref, yfn_ref, ymla_ref, yna_ref, wg_ref, bg_ref,
                  wbr_ref, wout_ref, g2_ref, wr_ref, out_ref, hb_ref, logit_ref):
    x = x_ref[...]
    h = _rms(x, g1_ref[...]).astype(BF16)
    merged = jnp.zeros(x.shape, F32)
    for n, y_ref in enumerate((yhy_ref, yfn_ref, ymla_ref, yna_ref)):
        sl = slice(n * D_MODEL, (n + 1) * D_MODEL)
        gate = jax.nn.sigmoid(_dot_nt(h, wg_ref[sl, :]) + bg_ref[:, sl])
        merged = merged + gate * _dot(y_ref[...], wbr_ref[n])
    x1 = x + _dot(merged.astype(BF16), wout_ref[...])
    out_ref[...] = x1
    h_hi, h_lo = _split2(_rms(x1, g2_ref[...]))
    hb_ref[...] = h_hi
    w_hi, w_lo = _split2(wr_ref[...])
    logits = _dot_nt(w_hi, h_hi) + _dot_nt(w_hi, h_lo) + _dot_nt(w_lo, h_hi)
    logit_ref[...] = logits[:N_EXPERTS]


def _merge(x, g1, yhy, yfn, ymla, yna, wg, bg, wbr, wout, g2, wr_t):
    tm = TILE["merge"]
    nt = SEQ // tm
    row = lambda n: pl.BlockSpec((tm, n), lambda i: (i, 0))
    return pl.pallas_call(
        _merge_kernel,
        grid=(TOKENS // tm,),
        in_specs=[row(D_MODEL), _const_spec(g1.shape), row(D_MIX), row(D_MIX), row(D_MIX),
                  row(D_MIX), _const_spec(wg.shape), _const_spec(bg.shape),
                  _const_spec(wbr.shape), _const_spec(wout.shape), _const_spec(g2.shape),
                  _const_spec(wr_t.shape)],
        out_specs=[row(D_MODEL), row(D_MODEL),
                   pl.BlockSpec((None, N_EXPERTS, tm), lambda i: (i // nt, 0, i % nt))],
        out_shape=[jax.ShapeDtypeStruct((TOKENS, D_MODEL), F32),
                   jax.ShapeDtypeStruct((TOKENS, D_MODEL), BF16),
                   jax.ShapeDtypeStruct((BATCH, N_EXPERTS, SEQ), F32)],
        compiler_params=_params("parallel"),
        name="merge",
    )(x, g1, yhy, yfn, ymla, yna, wg, bg, wbr, wout, g2, wr_t)


def _prefix_count(m):
    r = lax.broadcasted_iota(jnp.int32, (LANES, LANES), 0)
    c = lax.broadcasted_iota(jnp.int32, (LANES, LANES), 1)
    upper = jnp.where(r < c, 1.0, 0.0).astype(BF16)
    run = jnp.zeros((m.shape[0], 1), F32)
    parts = []
    for i in range(SEQ // LANES):
        chunk = m[:, i * LANES:(i + 1) * LANES]
        parts.append(_dot(chunk.astype(BF16), upper) + run)
        run = run + jnp.sum(chunk, axis=1, keepdims=True)
    return jnp.concatenate(parts, axis=1)


SELECT_MAX_ITERS = 192


def _select_kernel(logit_ref, slot_row_ref, slot_col_ref, w_row_ref, start_ref, slot_s, w_s):
    b = pl.program_id(0)
    rows = BATCH * N_EXPERTS

    @pl.when(b == 0)
    def _():
        logits = logit_ref[...]
        ex = jnp.exp(logits - jnp.max(logits, axis=1, keepdims=True))
        aff = (ex / jnp.sum(ex, axis=1, keepdims=True)).reshape(rows, SEQ)

        def bisect(c):
            it, lo, hi, _ = c
            mid = 0.5 * (lo + hi)
            cnt = jnp.sum(jnp.where(aff >= mid, 1.0, 0.0), axis=1, keepdims=True)
            moving = jnp.where(mid == lo, 0.0, jnp.where(mid == hi, 0.0, 1.0))
            enough = cnt >= CAPACITY
            return (it + 1, jnp.where(enough, mid, lo), jnp.where(enough, hi, mid),
                    (jnp.max(moving) > 0).astype(jnp.int32))

        _, lo, hi, _ = lax.while_loop(
            lambda c: (c[0] < SELECT_MAX_ITERS) & (c[3] > 0), bisect,
            (jnp.int32(0), jnp.zeros((rows, 1), F32), jnp.full((rows, 1), 2.0, F32),
             jnp.int32(1)))
        above = jnp.where(aff >= hi, 1.0, 0.0)
        band = jnp.where(aff >= lo, 1.0, 0.0) - above
        need = CAPACITY - jnp.sum(above, axis=1, keepdims=True)
        sel = above + band * jnp.where(_prefix_count(band) < need, 1.0, 0.0)
        slot_s[...] = jnp.where(sel > 0, _prefix_count(sel), -1.0)
        w_s[...] = sel * aff

    r0 = pl.multiple_of(b * N_EXPERTS, N_EXPERTS)
    slot = slot_s[pl.ds(r0, N_EXPERTS), :]
    slot_row_ref[...] = slot.astype(jnp.int32)
    pad = jnp.full((LANES - N_EXPERTS, SEQ), -1.0, F32)
    slot_col_ref[...] = jnp.concatenate([slot, pad], axis=0).T.astype(jnp.int32)
    w_row_ref[...] = w_s[pl.ds(r0, N_EXPERTS), :]
    tm = MOE_FINE
    token = lax.broadcasted_iota(jnp.int32, slot.shape, 1)
    lane = lax.broadcasted_iota(jnp.int32, (N_EXPERTS, LANES), 1)
    starts = jnp.where(lane == SEQ // tm, float(CAPACITY), 0.0)
    for i in range(1, SEQ // tm):
        before = jnp.where(slot >= 0, jnp.where(token < i * tm, 1.0, 0.0), 0.0)
        starts = starts + jnp.where(lane == i, jnp.sum(before, axis=1, keepdims=True), 0.0)
    start_ref[...] = starts.astype(jnp.int32)


def _select(logits):
    return pl.pallas_call(
        _select_kernel,
        grid=(BATCH,),
        in_specs=[_const_spec(logits.shape)],
        out_specs=[pl.BlockSpec((None, N_EXPERTS, SEQ), lambda b: (b, 0, 0)),
                   pl.BlockSpec((None, SEQ, LANES), lambda b: (b, 0, 0)),
                   pl.BlockSpec((None, N_EXPERTS, SEQ), lambda b: (b, 0, 0)),
                   pl.BlockSpec((None, N_EXPERTS, LANES), lambda b: (b, 0, 0))],
        out_shape=[jax.ShapeDtypeStruct((BATCH, N_EXPERTS, SEQ), jnp.int32),
                   jax.ShapeDtypeStruct((BATCH, SEQ, LANES), jnp.int32),
                   jax.ShapeDtypeStruct((BATCH, N_EXPERTS, SEQ), F32),
                   jax.ShapeDtypeStruct((BATCH, N_EXPERTS, LANES), jnp.int32)],
        scratch_shapes=[pltpu.VMEM((BATCH * N_EXPERTS, SEQ), F32),
                        pltpu.VMEM((BATCH * N_EXPERTS, SEQ), F32)],
        compiler_params=_params("arbitrary"),
        name="expert_select",
    )(logits)


MXU_DEPTH = 256
BF16_ROWS = 16


def _slot_window(tokens):
    return 2 * tokens * CAPACITY // SEQ


def _tile_windows(start_ref, b, i, tokens):
    stride = start_ref.shape[0] // (BATCH * N_EXPERTS)
    fine = tokens // MOE_FINE
    window = _slot_window(tokens)
    base, fits = [], None
    for e in range(N_EXPERTS):
        at = (b * N_EXPERTS + e) * stride + i * fine
        lo = jnp.minimum(start_ref[at] // BF16_ROWS * BF16_ROWS, CAPACITY - window)
        ok = start_ref[at + fine] - lo <= window
        fits = ok if fits is None else fits & ok
        base.append(pl.multiple_of(lo, BF16_ROWS))
    return base, fits


def _gather_kernel(start_ref, slot_ref, w_ref, hb_ref, xe_ref, wsl_ref):
    b = pl.program_id(0)
    i = pl.program_id(1)
    tokens = slot_ref.shape[1]

    @pl.when(i == 0)
    def _():
        xe_ref[...] = jnp.zeros(xe_ref.shape, BF16)
        wsl_ref[...] = jnp.zeros(wsl_ref.shape, F32)

    def add_rows(e, rows, hit, picked):
        xe_ref[e, rows, :] = xe_ref[e, rows, :] + picked.astype(BF16)
        wslot = jnp.sum(jnp.where(hit, w_ref[e:e + 1, :], 0.0), axis=1, keepdims=True)
        wsl_ref[e, rows, :] = wsl_ref[e, rows, :] + jnp.broadcast_to(wslot, (hit.shape[0], LANES))

    window = _slot_window(tokens)
    base, fits = _tile_windows(start_ref, b, i, tokens)

    @pl.when(fits)
    def _():
        c = lax.broadcasted_iota(jnp.int32, (window, tokens), 0)
        hits = [slot_ref[e:e + 1, :] - base[e] == c for e in range(N_EXPERTS)]
        onehot = jnp.concatenate([jnp.where(m, 1.0, 0.0).astype(BF16) for m in hits], axis=0)
        picked = _dot(onehot, hb_ref[...])
        for e in range(N_EXPERTS):
            add_rows(e, pl.ds(base[e], window), hits[e], picked[e * window:(e + 1) * window])

    @pl.when(jnp.logical_not(fits))
    def _():
        c = lax.broadcasted_iota(jnp.int32, (CAPACITY, tokens), 0)
        for e in range(N_EXPERTS):
            hit = slot_ref[e:e + 1, :] == c
            picked = _dot(jnp.where(hit, 1.0, 0.0).astype(BF16), hb_ref[...])
            add_rows(e, slice(None), hit, picked)


def _gather(slot_row, w_row, starts, hb):
    tm = TILE["gather"]
    nt = SEQ // tm
    grid_spec = pltpu.PrefetchScalarGridSpec(
        num_scalar_prefetch=1,
        grid=(BATCH, nt),
        in_specs=[pl.BlockSpec((None, N_EXPERTS, tm), lambda b, i, s: (b, 0, i)),
                  pl.BlockSpec((None, N_EXPERTS, tm), lambda b, i, s: (b, 0, i)),
                  pl.BlockSpec((tm, D_MODEL), lambda b, i, s: (b * nt + i, 0))],
        out_specs=[pl.BlockSpec((N_EXPERTS, None, CAPACITY, D_MODEL), lambda b, i, s: (0, b, 0, 0)),
                   pl.BlockSpec((N_EXPERTS, None, CAPACITY, LANES), lambda b, i, s: (0, b, 0, 0))])
    return pl.pallas_call(
        _gather_kernel,
        grid_spec=grid_spec,
        out_shape=[jax.ShapeDtypeStruct((N_EXPERTS, BATCH, CAPACITY, D_MODEL), BF16),
                   jax.ShapeDtypeStruct((N_EXPERTS, BATCH, CAPACITY, LANES), F32)],
        compiler_params=_params("parallel", "arbitrary"),
        name="expert_gather",
    )(starts[:, :, :SEQ // MOE_FINE + 1].reshape(-1), slot_row, w_row, hb)


def _expert_kernel(xe_ref, wsl_ref, wg_ref, wu_ref, wd_ref, ye_ref, wg_s, wu_s, wd_s, *, span):
    ph = pl.program_id(0)
    i = pl.program_id(1)
    rows = pl.ds(pl.multiple_of(i * span, span), span)
    nxt = ph % 2
    wg_s[nxt, rows, :] = wg_ref[rows, :].astype(BF16)
    wu_s[nxt, rows, :] = wu_ref[rows, :].astype(BF16)
    wd_s[nxt, rows, :] = wd_ref[rows, :].astype(BF16)

    @pl.when(ph > 0)
    def _():
        cur = (ph + 1) % 2
        xe = xe_ref[...]
        g = _dot(xe, wg_s[cur])
        u = _dot(xe, wu_s[cur])
        act = (g * jax.nn.sigmoid(g) * u).astype(BF16)
        ye_ref[...] = (_dot(act, wd_s[cur]) * wsl_ref[:, 0:1]).astype(BF16)


def _experts(xe, wsl, wg, wu, wd, layer):
    rows = BATCH * CAPACITY
    tm = TILE["expert"]
    nt = rows // tm
    last = N_EXPERTS - 1
    wspec = lambda a: pl.BlockSpec((None, None) + a.shape[2:],
                                   lambda ph, i: (layer, jnp.minimum(ph, last), 0, 0))
    data = lambda n: pl.BlockSpec(
        (None, tm, n), lambda ph, i: (jnp.maximum(ph - 1, 0), jnp.where(ph > 0, i, 0), 0))
    return pl.pallas_call(
        functools.partial(_expert_kernel, span=D_MODEL // nt),
        grid=(N_EXPERTS + 1, nt),
        in_specs=[data(D_MODEL), data(LANES), wspec(wg), wspec(wu), wspec(wd)],
        out_specs=data(D_MODEL),
        out_shape=jax.ShapeDtypeStruct((N_EXPERTS, rows, D_MODEL), BF16),
        scratch_shapes=[pltpu.VMEM((2, D_MODEL, D_FF), BF16), pltpu.VMEM((2, D_MODEL, D_FF), BF16),
                        pltpu.VMEM((2, D_FF, D_MODEL), BF16)],
        compiler_params=_params("arbitrary", "arbitrary"),
        name="expert_ffn",
    )(xe.reshape(N_EXPERTS, rows, D_MODEL), wsl.reshape(N_EXPERTS, rows, LANES), wg, wu, wd)


def _combine_kernel(start_ref, x_ref, ye_ref, slot_ref, p_ref, g3_ref, wpg_ref, wpp_ref,
                    gf_ref, out_ref, moe_s, *, final_norm):
    slot = slot_ref[...]
    rows = slot.shape[0]
    window = _slot_window(rows)
    base, fits = _tile_windows(start_ref, pl.program_id(0), pl.program_id(1), rows)

    @pl.when(fits)
    def _():
        c = lax.broadcasted_iota(jnp.int32, (rows, window), 1)
        parts = []
        group = MXU_DEPTH // window
        for e0 in range(0, N_EXPERTS, group):
            hot, win = [], []
            for e in range(e0, e0 + group):
                hot.append(jnp.where(slot[:, e:e + 1] - base[e] == c, 1.0, 0.0).astype(BF16))
                win.append(ye_ref[e, pl.ds(base[e], window), :])
            parts.append(_dot(jnp.concatenate(hot, axis=1), jnp.concatenate(win, axis=0)))
        while len(parts) > 1:
            parts = [u + v for u, v in zip(parts[::2], parts[1::2])]
        moe_s[...] = parts[0]

    @pl.when(jnp.logical_not(fits))
    def _():
        c = lax.broadcasted_iota(jnp.int32, (rows, CAPACITY), 1)
        onehot = jnp.concatenate(
            [jnp.where(slot[:, e:e + 1] == c, 1.0, 0.0).astype(BF16) for e in range(N_EXPERTS)],
            axis=1)
        moe_s[...] = _dot(onehot, ye_ref[...].reshape(N_EXPERTS * CAPACITY, D_MODEL))

    acc = x_ref[...] + moe_s[...]
    h = _rms(acc, g3_ref[...]).astype(BF16)
    gate = jax.nn.sigmoid(_dot(h, wpg_ref[...]))
    y = acc + gate * _dot(p_ref[...].astype(BF16), wpp_ref[...])
    if final_norm:
        y = _rms(y, gf_ref[...])
    out_ref[...] = y


def _combine(x, ye, slot_col, starts, p, layer, g3, wpg, wpp, gf, final_norm):
    tm = TILE["combine"]
    nt = SEQ // tm
    p0 = layer * (TOKENS // tm)
    const = lambda a: pl.BlockSpec(a.shape, lambda b, i, s: (0,) * a.ndim,
                                   pipeline_mode=pl.Buffered(1))
    grid_spec = pltpu.PrefetchScalarGridSpec(
        num_scalar_prefetch=1,
        grid=(BATCH, nt),
        in_specs=[pl.BlockSpec((tm, D_MODEL), lambda b, i, s: (b * nt + i, 0)),
                  pl.BlockSpec((N_EXPERTS, None, CAPACITY, D_MODEL), lambda b, i, s: (0, b, 0, 0)),
                  pl.BlockSpec((None, tm, LANES), lambda b, i, s: (b, i, 0)),
                  pl.BlockSpec((tm, PLE_DIM), lambda b, i, s: (p0 + b * nt + i, 0)),
                  const(g3), const(wpg), const(wpp), const(gf)],
        out_specs=pl.BlockSpec((tm, D_MODEL), lambda b, i, s: (b * nt + i, 0)),
        scratch_shapes=[pltpu.VMEM((tm, D_MODEL), F32)])
    return pl.pallas_call(
        functools.partial(_combine_kernel, final_norm=final_norm),
        grid_spec=grid_spec,
        out_shape=jax.ShapeDtypeStruct((TOKENS, D_MODEL), F32),
        compiler_params=_params("parallel", "parallel"),
        name="combine",
    )(starts[:, :, :SEQ // MOE_FINE + 1].reshape(-1), x, ye.reshape(N_EXPERTS, BATCH, CAPACITY, D_MODEL),
      slot_col, p, g3, wpg, wpp, gf)


DFT_FINE = 64
DFT_STEP = 4


def _dft_kernel(ca_ref, sa_ref, cb_ref, sb_ref, out_ref):
    cb = cb_ref[...]
    sb = sb_ref[...]
    for r in range(DFT_STEP):
        rows = slice(r * DFT_FINE, (r + 1) * DFT_FINE)
        out_ref[rows, :] = (ca_ref[r] * cb - sa_ref[r] * sb).astype(BF16)


def _dft_tables(n_points, nres):
    blk = SEQ // nres
    t = jnp.arange(SEQ, dtype=jnp.int32).reshape(blk, nres).T
    t = jnp.concatenate([t, t], axis=1).reshape(1, 2 * SEQ)
    is_sin = (jnp.arange(2 * SEQ) // blk % 2 == 1)[None, :]
    coarse = jnp.arange(HALF // DFT_FINE, dtype=jnp.int32)[:, None] * DFT_FINE
    fine = jnp.arange(DFT_FINE, dtype=jnp.int32)[:, None]
    ang = lambda f: ((f * t) % n_points).astype(F32) * (2.0 * math.pi / n_points)
    ca = jnp.cos(ang(coarse))[:, None, :]
    sa = jnp.sin(ang(coarse))[:, None, :]
    cb = jnp.where(is_sin, jnp.sin(ang(fine)), jnp.cos(ang(fine)))
    sb = jnp.where(is_sin, -jnp.cos(ang(fine)), jnp.sin(ang(fine)))
    rows = DFT_STEP * DFT_FINE
    return pl.pallas_call(
        _dft_kernel,
        grid=(HALF // rows,),
        in_specs=[pl.BlockSpec((DFT_STEP, 1, 2 * SEQ), lambda i: (i, 0, 0)),
                  pl.BlockSpec((DFT_STEP, 1, 2 * SEQ), lambda i: (i, 0, 0)),
                  _const_spec(cb.shape), _const_spec(sb.shape)],
        out_specs=pl.BlockSpec((rows, 2 * SEQ), lambda i: (i, 0)),
        out_shape=jax.ShapeDtypeStruct((HALF, 2 * SEQ), BF16),
        compiler_params=_params("parallel"),
        name="dft_tables",
    )(ca, sa, cb, sb)


def _fnet_group_tables():
    gc = D_MIX // FN_GROUPS
    i = lax.broadcasted_iota(jnp.int32, (D_MIX, D_MIX), 0)
    j = lax.broadcasted_iota(jnp.int32, (D_MIX, D_MIX), 1)
    same = (i // gc) == (j // gc)
    ang = (((i % gc) * (j % gc)) % gc).astype(F32) * (2.0 * math.pi / gc)
    return (jnp.where(same, jnp.cos(ang), 0.0).astype(BF16),
            jnp.where(same, jnp.sin(ang), 0.0).astype(BF16))


def _hyena_features():
    t01 = jnp.linspace(0.0, 1.0, SEQ, dtype=F32)[:, None]
    bands = jnp.linspace(1e-4, HY_BANDS - 1, HY_BANDS, dtype=F32)
    ang = 2.0 * math.pi * jnp.arange(SEQ, dtype=F32)[:, None] * bands / SEQ
    z = jnp.concatenate([t01, jnp.cos(ang), -jnp.sin(ang)], axis=-1)
    z = jnp.pad(z, ((0, 0), (0, LANES - HY_EMB)))
    max_decay = math.log(HY_TARGET) / HY_FAST_DECAY
    min_decay = math.log(HY_TARGET) / HY_SLOW_DECAY
    deltas = jnp.linspace(min_decay, max_decay, D_MIX, dtype=F32)
    window = jnp.exp(-t01 * jnp.abs(deltas))
    return z, window


def _rot_cols(w):
    half = w.shape[-1] // 2
    return jnp.concatenate([-w[..., half:], w[..., :half]], axis=-1)


def _rope_tables():
    inv = ROPE_THETA ** (-jnp.arange(0, MLA_ROPE, 2, dtype=F32) / MLA_ROPE)
    ang = jnp.arange(SEQ, dtype=F32)[:, None] * inv
    cos = jnp.concatenate([jnp.cos(ang), jnp.cos(ang)], axis=-1)
    sin = jnp.concatenate([jnp.sin(ang), jnp.sin(ang)], axis=-1)
    scale = (MLA_NOPE + MLA_ROPE) ** -0.5 * LOG2E
    pad = MLA_HEAD_PAD - MLA_NOPE - MLA_ROPE
    one = jnp.ones((SEQ, MLA_NOPE), F32)
    zero = jnp.zeros((SEQ, MLA_NOPE), F32)
    zpad = jnp.zeros((SEQ, pad), F32)
    cosq = jnp.tile(jnp.concatenate([one, cos, zpad], axis=-1) * scale, (1, MLA_HEADS))
    sinq = jnp.tile(jnp.concatenate([zero, sin, zpad], axis=-1) * scale, (1, MLA_HEADS))
    csk = jnp.concatenate([cos, sin, jnp.zeros((SEQ, LANES - 2 * MLA_ROPE), F32)], axis=-1)
    return cosq, sinq, csk


def _mla_weights(w_uq, w_ukv):
    pad = MLA_HEAD_PAD - MLA_NOPE - MLA_ROPE
    wq = w_uq.reshape(MLA_Q_RANK, MLA_HEADS, MLA_NOPE + MLA_ROPE)
    nope, pe = wq[..., :MLA_NOPE], wq[..., MLA_NOPE:]
    zp = jnp.zeros((MLA_Q_RANK, MLA_HEADS, pad), F32)
    wqa = jnp.concatenate([nope, pe, zp], axis=-1).reshape(MLA_Q_RANK, MLA_QK)
    wqb = jnp.concatenate([jnp.zeros_like(nope), _rot_cols(pe), zp], axis=-1)
    wqb = wqb.reshape(MLA_Q_RANK, MLA_QK)
    wkv = w_ukv.reshape(MLA_KV_RANK, MLA_HEADS, MLA_NOPE + MLA_V)
    knope, v = wkv[..., :MLA_NOPE], wkv[..., MLA_NOPE:]
    wk = jnp.concatenate(
        [knope, jnp.zeros((MLA_KV_RANK, MLA_HEADS, MLA_HEAD_PAD - MLA_NOPE), F32)], axis=-1)
    wk = wk.reshape(MLA_KV_RANK, MLA_QK)
    wv = v.reshape(MLA_KV_RANK, MLA_HEADS * MLA_V)
    r = lax.broadcasted_iota(jnp.int32, (LANES, MLA_QK), 0)
    c = lax.broadcasted_iota(jnp.int32, (LANES, MLA_QK), 1)
    epe = jnp.where((r < 2 * MLA_ROPE) & (c % MLA_HEAD_PAD == MLA_NOPE + r % MLA_ROPE), 1.0, 0.0)
    return (wqa.astype(BF16), wqb.astype(BF16), wk.astype(BF16), wv.astype(BF16),
            epe.astype(BF16))


def kernel(x, p, norm1_g, w_in, b_gate, hy_conv_w, hy_conv_b, hf_w1, hf_b1, hf_freq, hf_w2,
           hf_b2, hf_w3, hy_skip, q_norm_g, w_uq, kv_norm_g, w_ukv, rpb, w_br, w_out, norm2_g,
           w_router, w_e_gate, w_e_up, w_e_down, norm3_g, w_ple_gate, w_ple_proj, final_g):
    conv_tab = _dft_tables(2 * SEQ, 4)
    fnet_tab = _dft_tables(SEQ, 2)
    fnet_cg, fnet_sg = _fnet_group_tables()
    zfeat, window = _hyena_features()
    cosq, sinq, csk = _rope_tables()
    row = lambda a: a.reshape(1, -1)

    xt = x.reshape(TOKENS, D_MODEL)
    w_in_t = jnp.swapaxes(w_in, 1, 2)
    for i in range(DEPTH):
        g1 = row(norm1_g[i])
        wa, wna, wgate = _inproj_weights(w_in_t, i)
        wqa, wqb, wk, wv, epe = _mla_weights(w_uq[i], w_ukv[i])

        u_hy, u_fn, q, k, v, naq, nak, nav = _inproj(
            xt, g1, wa, wna, row(q_norm_g[i]), wqa, wqb, row(kv_norm_g[i]), wk, wv, epe,
            cosq, sinq, csk)

        w1 = jnp.pad(hf_w1[i], ((0, LANES - HY_EMB), (0, 0)))
        kf, kny = _hyena_filter(zfeat, window, w1, row(hf_b1[i]), hf_freq[i], hf_w2[i],
                                row(hf_b2[i]), hf_w3[i], conv_tab)
        conv_b = row(hy_conv_b[i])
        z1 = _hyena_stage(u_hy, 0, u_hy, 2, hy_conv_w[i], conv_b, conv_tab, kf[0],
                          kny[0], row(hy_skip[i, 0]), True, F32)
        y_hy = _hyena_stage(u_hy, 1, z1, 0, hy_conv_w[i], conv_b, conv_tab, kf[1],
                            kny[1], row(hy_skip[i, 1]), False, BF16)
        y_fn = _fnet(u_fn, fnet_tab, fnet_cg, fnet_sg)
        y_mla = _mla(q, k, v)
        y_na = _neighborhood(naq, nak, nav, _na_bias_tiles(rpb[i]))
        wr_t = jnp.pad(w_router[i].T, ((0, LANES - N_EXPERTS), (0, 0)))
        xt, hb, logits = _merge(xt, g1, y_hy, y_fn, y_mla, y_na, wgate, row(b_gate[i]),
                                w_br[i].astype(BF16), w_out[i].astype(BF16),
                                row(norm2_g[i]), wr_t)
        slot_row, slot_col, w_row, starts = _select(logits)
        xe, wsl = _gather(slot_row, w_row, starts, hb)
        ye = _experts(xe, wsl, w_e_gate, w_e_up, w_e_down, i)
        xt = _combine(xt, ye, slot_col, starts, p.reshape(DEPTH * TOKENS, PLE_DIM), i,
                      row(norm3_g[i]), w_ple_gate[i].astype(BF16),
                      w_ple_proj[i].astype(BF16), row(final_g), i == DEPTH - 1)
    return xt.reshape(BATCH, SEQ, D_MODEL)
```

```python
import functools
import math

import jax
import jax.numpy as jnp
from jax import lax
from jax.experimental import pallas as pl
from jax.experimental.pallas import tpu as pltpu

F32 = jnp.float32
BF16 = jnp.bfloat16

D_MODEL = 1024
BATCH = 8
SEQ = 2048
DEPTH = 2
TOKENS = BATCH * SEQ

GRID_W = 64
GRID_R = SEQ // GRID_W
D_MIX = 256
N_BRANCH = 4
EPS = 1e-6
HY_ORDER = 2
HY_BANDS = 16
HY_EMB = 2 * HY_BANDS + 1
HY_FFN = 64
HY_TARGET = 1e-2
HY_FAST_DECAY = 0.3
HY_SLOW_DECAY = 1.5
FN_GROUPS = 4
MLA_HEADS = 4
MLA_NOPE = 64
MLA_ROPE = 32
MLA_V = 64
MLA_Q_RANK = 256
MLA_KV_RANK = 128
ROPE_THETA = 10000.0
NA_HEADS = 4
NA_HEAD_DIM = D_MIX // NA_HEADS
NA_WIN_R = 8
NA_WIN_C = 16
N_EXPERTS = 16
CAPACITY = 2 * SEQ // N_EXPERTS
D_FF = 1024
PLE_DIM = 256

HY_COLS = 3 * D_MIX
OFF_FN = HY_COLS
OFF_CQ = OFF_FN + D_MIX
OFF_CKV = OFF_CQ + MLA_Q_RANK
OFF_KPE = OFF_CKV + MLA_KV_RANK
OFF_NA = OFF_KPE + MLA_ROPE
OFF_GATE = OFF_NA + 3 * D_MIX

LANES = 128
MLA_HEAD_PAD = 128
MLA_QK = MLA_HEADS * MLA_HEAD_PAD
WA_COLS = 1536
NEG_BIG = -1e30
LOG2E = math.log2(math.e)

TILE = dict(inproj=512, mla=512, merge=512, gather=256, expert=512, combine=512)
MOE_FINE = 256
NA_QROWS = 8
NA_KROWS = 16
NA_QBLK = NA_QROWS * GRID_W
NA_KBLK = NA_KROWS * GRID_W
VMEM_LIMIT = 56 * 1024 * 1024


def _params(*sem):
    return pltpu.CompilerParams(dimension_semantics=sem, vmem_limit_bytes=VMEM_LIMIT)


def _const_spec(shape):
    nd = len(shape)
    return pl.BlockSpec(shape, lambda *_: (0,) * nd, pipeline_mode=pl.Buffered(1))


def _rms(x, g):
    return x * lax.rsqrt(jnp.mean(x * x, axis=-1, keepdims=True) + EPS) * g


def _dot(a, b):
    return jnp.dot(a, b, preferred_element_type=F32)


def _dot_nt(a, b):
    return lax.dot_general(a, b, (((1,), (1,)), ((), ())), preferred_element_type=F32)


def _split2(x):
    hi = x.astype(BF16)
    lo = (x - hi.astype(F32)).astype(BF16)
    return hi, lo


def _inproj_kernel(x_ref, g1_ref, wa_ref, wna_ref, qg_ref, wqa_ref, wqb_ref, kvg_ref,
                   wk_ref, wv_ref, epe_ref, cosq_ref, sinq_ref, csk_ref,
                   uhy_ref, ufn_ref, q_ref, k_ref, v_ref, naq_ref, nak_ref, nav_ref):
    h = _rms(x_ref[...], g1_ref[...]).astype(BF16)
    ua = _dot_nt(h, wa_ref[...])
    uhy_ref[...] = ua[:, :HY_COLS]
    ufn_ref[...] = ua[:, OFF_FN:OFF_CQ]
    cqn = _rms(ua[:, OFF_CQ:OFF_CKV], qg_ref[...]).astype(BF16)
    q = _dot(cqn, wqa_ref[...]) * cosq_ref[...] + _dot(cqn, wqb_ref[...]) * sinq_ref[...]
    q_ref[...] = q.astype(BF16)
    kvn = _rms(ua[:, OFF_CKV:OFF_KPE], kvg_ref[...]).astype(BF16)
    kpe = ua[:, OFF_KPE:WA_COLS] * csk_ref[...]
    k = _dot(kvn, wk_ref[...]) + _dot(kpe.astype(BF16), epe_ref[...])
    k_ref[...] = k.astype(BF16)
    v_ref[...] = _dot(kvn, wv_ref[...]).astype(BF16)
    una = _dot_nt(h, wna_ref[...])
    naq_ref[...] = (una[:, :D_MIX] * (NA_HEAD_DIM ** -0.5 * LOG2E)).astype(BF16)
    nak_ref[...] = una[:, D_MIX:2 * D_MIX].astype(BF16)
    nav_ref[...] = una[:, 2 * D_MIX:].astype(BF16)


def _inproj(x, g1, wa, wna, qg, wqa, wqb, kvg, wk, wv, epe, cosq, sinq, csk):
    tm = TILE["inproj"]
    nt = SEQ // tm
    row = lambda n: pl.BlockSpec((tm, n), lambda i: (i, 0))
    pos = lambda n: pl.BlockSpec((tm, n), lambda i: (i % nt, 0))
    outs = [(HY_COLS, F32), (D_MIX, F32), (MLA_QK, BF16), (MLA_QK, BF16), (D_MIX, BF16),
            (D_MIX, BF16), (D_MIX, BF16), (D_MIX, BF16)]
    return pl.pallas_call(
        _inproj_kernel,
        grid=(TOKENS // tm,),
        in_specs=[row(D_MODEL), _const_spec(g1.shape), _const_spec(wa.shape),
                  _const_spec(wna.shape), _const_spec(qg.shape), _const_spec(wqa.shape),
                  _const_spec(wqb.shape), _const_spec(kvg.shape), _const_spec(wk.shape),
                  _const_spec(wv.shape), _const_spec(epe.shape),
                  pos(MLA_QK), pos(MLA_QK), pos(LANES)],
        out_specs=[row(n) for n, _ in outs],
        out_shape=[jax.ShapeDtypeStruct((TOKENS, n), dt) for n, dt in outs],
        compiler_params=_params("parallel"),
        name="inproj",
    )(x, g1, wa, wna, qg, wqa, wqb, kvg, wk, wv, epe, cosq, sinq, csk)


WPREP_ROWS = 512


def _inproj_weights_kernel(w_ref, wa_ref, wna_ref, wg_ref):
    i = pl.program_id(0)

    @pl.when(i == 0)
    def _():
        half = MLA_ROPE // 2
        wa_ref[:OFF_NA, :] = w_ref[:OFF_NA, :].astype(BF16)
        wa_ref[OFF_NA:OFF_NA + half, :] = (-w_ref[OFF_KPE + half:OFF_NA, :]).astype(BF16)
        wa_ref[OFF_NA + half:OFF_NA + MLA_ROPE, :] = w_ref[OFF_KPE:OFF_KPE + half, :].astype(BF16)
        wa_ref[OFF_NA + MLA_ROPE:, :] = jnp.zeros((WA_COLS - OFF_NA - MLA_ROPE, D_MODEL), BF16)
        wna_ref[...] = w_ref[OFF_NA:OFF_GATE, :].astype(BF16)

    start = pl.multiple_of(OFF_GATE + i * WPREP_ROWS, MLA_ROPE)
    wg_ref[...] = w_ref[pl.ds(start, WPREP_ROWS), :].astype(BF16)


def _inproj_weights(w_in_t, layer):
    n_in = w_in_t.shape[1]
    rows = (WA_COLS, OFF_GATE - OFF_NA, n_in - OFF_GATE)
    whole = lambda n: pl.BlockSpec((n, D_MODEL), lambda i: (0, 0))
    return pl.pallas_call(
        _inproj_weights_kernel,
        grid=(rows[2] // WPREP_ROWS,),
        in_specs=[pl.BlockSpec((None, n_in, D_MODEL), lambda i: (layer, 0, 0),
                               pipeline_mode=pl.Buffered(1))],
        out_specs=[whole(rows[0]), whole(rows[1]),
                   pl.BlockSpec((WPREP_ROWS, D_MODEL), lambda i: (i, 0))],
        out_shape=[jax.ShapeDtypeStruct((n, D_MODEL), BF16) for n in rows],
        compiler_params=_params("arbitrary"),
        name="inproj_weights",
    )(w_in_t)


HALF = SEQ // 2
HY_SEQS = 1


def _residue_rows(ref, r, nres):
    rows = pl.ds(r, SEQ // nres, stride=nres)
    if isinstance(ref, tuple):
        return jnp.concatenate([h[rows, :] for h in ref], axis=1)
    return jnp.concatenate([ref[j, rows, :] for j in range(ref.shape[0])], axis=1)


def _store_slabs(ref, value, rows=slice(None)):
    for j in range(ref.shape[0]):
        ref[j, rows, :] = value[:, j * LANES:(j + 1) * LANES]


def _table_cols(cs_ref, r, nres, part=None):
    blk = SEQ // nres
    lo = 2 * blk * r
    if part is None:
        return cs_ref[:, lo:lo + 2 * blk]
    return cs_ref[:, lo + part * blk:lo + (part + 1) * blk]


def _dft_fwd(cs_ref, src_ref, want_cos=True, want_sin=True, split=False):
    def prod(part, z, r):
        tab = _table_cols(cs_ref, r, 4, part)
        if split:
            hi, lo = _split2(z)
            return _dot(tab, hi) + _dot(tab, lo)
        return _dot(tab, z.astype(BF16))

    pc, ps = [None] * 4, [None] * 4
    for r in range(4):
        z = _residue_rows(src_ref, r, 4)
        odd = r % 2 == 1
        if want_cos or odd:
            pc[r] = prod(0, z, r)
        if want_sin or odd:
            ps[r] = prod(1, z, r)
    a = b = None
    if want_cos:
        a = ((pc[0] + pc[2]) + (pc[1] + pc[3]), (pc[0] - pc[2]) + (ps[3] - ps[1]))
    if want_sin:
        b = ((ps[0] + ps[2]) + (ps[1] + ps[3]), (ps[0] - ps[2]) + (pc[1] - pc[3]))
    return a, b


def _dft_inv(cs_ref, yre_ref, yim_ref):
    g, h = [], {}
    for r in range(4):
        yr = _residue_rows(yre_ref, r, 4).astype(BF16)
        yi = _residue_rows(yim_ref, r, 4).astype(BF16)
        tab = _table_cols(cs_ref, r, 4)
        g.append(_dot(tab, jnp.concatenate([yr, -yi], axis=0)))
        if r % 2 == 1:
            h[r] = _dot(tab, jnp.concatenate([yi, yr], axis=0))
    return (g[0] + g[2]) + (g[1] + g[3]), (g[0] - g[2]) + (h[3] - h[1])


def _hyena_filter_kernel(z_ref, win_ref, w1_ref, b1_ref, freq_ref, w2_ref, b2_ref, w3_ref,
                         cs_ref, kf_ref, kny_ref, ksum_s, kdif_s, trunk_s):
    hp = lax.Precision.HIGHEST

    @pl.when(pl.program_id(0) == 0)
    def _():
        freq = freq_ref[...]
        hf = jnp.sin(freq[0:1] * (jnp.dot(z_ref[...], w1_ref[...], precision=hp,
                                          preferred_element_type=F32) + b1_ref[...]))
        trunk_s[...] = jnp.sin(freq[1:2] * (jnp.dot(hf, w2_ref[...], precision=hp,
                                                    preferred_element_type=F32) + b2_ref[...]))

    hf = jnp.dot(trunk_s[...], w3_ref[...], precision=hp, preferred_element_type=F32)
    win = win_ref[...]
    t = lax.broadcasted_iota(jnp.int32, (SEQ, D_MIX), 0)
    sgn = (1 - 2 * (t & 1)).astype(F32)
    fwd = hf[:, :D_MIX] * win
    bwd = jnp.where(t == 0, 0.0, hf[:, D_MIX:] * win)
    nrm = lax.rsqrt(jnp.sum(fwd * fwd + bwd * bwd, axis=0, keepdims=True) + EPS)
    ksum = (fwd + bwd) * nrm
    _store_slabs(ksum_s, ksum)
    _store_slabs(kdif_s, (bwd - fwd) * nrm)
    kre, _ = _dft_fwd(cs_ref, ksum_s, want_sin=False, split=True)
    _, kim = _dft_fwd(cs_ref, kdif_s, want_cos=False, split=True)
    wf = 2.0 / (2 * SEQ)
    for part, spec in enumerate((kre, kim)):
        lo = spec[0] * wf
        kf_ref[part, :HALF, :] = lo
        kf_ref[part, 0:1, :] = lo[0:1] * 0.5
        kf_ref[part, HALF:, :] = spec[1] * wf
    kny = jnp.sum(ksum * sgn, axis=0, keepdims=True) * (1.0 / (2 * SEQ))
    kny_ref[...] = jnp.broadcast_to(kny, (8, D_MIX))


def _hyena_filter(zfeat, window, w1, b1, freq, w2, b2, w3, table):
    consts = (zfeat, window, w1, b1, freq, w2, b2)
    return pl.pallas_call(
        _hyena_filter_kernel,
        grid=(HY_ORDER,),
        in_specs=[_const_spec(a.shape) for a in consts]
        + [pl.BlockSpec((HY_FFN, 2 * D_MIX), lambda o: (0, o)),
           _const_spec(table.shape)],
        out_specs=[pl.BlockSpec((None, 2, SEQ, D_MIX), lambda o: (o, 0, 0, 0)),
                   pl.BlockSpec((None, 8, D_MIX), lambda o: (o, 0, 0))],
        out_shape=[jax.ShapeDtypeStruct((HY_ORDER, 2, SEQ, D_MIX), F32),
                   jax.ShapeDtypeStruct((HY_ORDER, 8, D_MIX), F32)],
        scratch_shapes=[pltpu.VMEM((D_MIX // LANES, SEQ, LANES), F32)] * 2
        + [pltpu.VMEM((SEQ, HY_FFN), F32)],
        compiler_params=_params("arbitrary"),
        name="hyena_filter",
    )(*consts, w3, table)


def _short_conv(u, w, b):
    t = lax.broadcasted_iota(jnp.int32, u.shape, 0)
    prev = jnp.where(t == 0, 0.0, pltpu.roll(u, 1, 0))
    nxt = jnp.where(t == SEQ - 1, 0.0, pltpu.roll(u, SEQ - 1, 0))
    return prev * w[0:1] + u * w[1:2] + nxt * w[2:3] + b


def _hyena_stage_kernel(gate_ref, src_ref, wg_ref, bg_ref, ws_ref, bs_ref, cs_ref,
                        kf_ref, kny_ref, skip_ref, out_ref, z_s, yre_s, yim_s, gate_s, rest_s,
                        *, conv_src):
    for q in range(HY_SEQS):
        seq = slice(q * SEQ, (q + 1) * SEQ)
        zq, yre_q, yim_q = z_s.at[q], yre_s.at[q], yim_s.at[q]
        z = src_ref[seq, :]
        if conv_src:
            z = _short_conv(z, ws_ref[...], bs_ref[...])
        _store_slabs(zq, z)
        t = lax.broadcasted_iota(jnp.int32, z.shape, 0)
        sgn = (1 - 2 * (t & 1)).astype(F32)
        nyq = jnp.sum(z * sgn, axis=0, keepdims=True) * kny_ref[0:1]
        gate = _short_conv(gate_ref[seq, :], wg_ref[...], bg_ref[...])
        gate_s[seq, :] = gate
        rest_s[seq, :] = gate * (sgn * nyq + z * skip_ref[...])
        a, b = _dft_fwd(cs_ref, zq)
        for half in range(2):
            rows = slice(half * HALF, (half + 1) * HALF)
            kre = kf_ref[0, rows, :]
            kim = kf_ref[1, rows, :]
            _store_slabs(yre_q, a[half] * kre + b[half] * kim, rows)
            _store_slabs(yim_q, a[half] * kim - b[half] * kre, rows)
        for half, y in enumerate(_dft_inv(cs_ref, yre_q, yim_q)):
            rows = slice(q * SEQ + half * HALF, q * SEQ + (half + 1) * HALF)
            out_ref[rows, :] = (gate_s[rows, :] * y + rest_s[rows, :]).astype(out_ref.dtype)


def _hyena_stage(u_hy, gate_blk, src, src_blk, conv_w, conv_b, table, kf, kny, order, skip,
                 conv_src, out_dtype):
    rows = HY_SEQS * SEQ
    of_order = lambda a: pl.BlockSpec(
        (None,) + a.shape[1:], lambda b: (order,) + (0,) * (a.ndim - 1),
        pipeline_mode=pl.Buffered(1))
    col = lambda blk: pl.BlockSpec((rows, D_MIX), lambda b: (b, blk))
    wcol = lambda blk, r: pl.BlockSpec((r, D_MIX), lambda b: (0, blk))
    ws_blk = src_blk if conv_src else 0
    return pl.pallas_call(
        functools.partial(_hyena_stage_kernel, conv_src=conv_src),
        grid=(BATCH // HY_SEQS,),
        in_specs=[col(gate_blk), col(src_blk), wcol(gate_blk, 3), wcol(gate_blk, 1),
                  wcol(ws_blk, 3), wcol(ws_blk, 1), _const_spec(table.shape),
                  of_order(kf), of_order(kny), _const_spec(skip.shape)],
        out_specs=pl.BlockSpec((rows, D_MIX), lambda b: (b, 0)),
        out_shape=jax.ShapeDtypeStruct((TOKENS, D_MIX), out_dtype),
        scratch_shapes=[pltpu.VMEM((HY_SEQS, D_MIX // LANES, SEQ, LANES), F32)] * 3
        + [pltpu.VMEM((rows, D_MIX), F32)] * 2,
        compiler_params=_params("parallel"),
        name="hyena_stage",
    )(u_hy, src, conv_w, conv_b, conv_w, conv_b, table, kf, kny, skip)


def _fnet_kernel(xa_ref, xb_ref, cs_ref, cg_ref, sg_ref, out_ref):
    parts = []
    for r in range(2):
        xb = _residue_rows((xa_ref, xb_ref), r, 2).astype(BF16)
        xc = _dot(xb, cg_ref[...]).astype(BF16)
        xs = _dot(xb, sg_ref[...]).astype(BF16)
        parts.append(_dot(_table_cols(cs_ref, r, 2), jnp.concatenate([xc, -xs], axis=0)))
    scale = (SEQ * D_MIX // FN_GROUPS) ** -0.5
    out_ref[:HALF, :] = ((parts[0] + parts[1]) * scale).astype(out_ref.dtype)
    out_ref[HALF:, :] = ((parts[0] - parts[1]) * scale).astype(out_ref.dtype)


def _fnet(u_fn, table, cg, sg):
    return pl.pallas_call(
        _fnet_kernel,
        grid=(BATCH,),
        in_specs=[pl.BlockSpec((SEQ, LANES), lambda b: (b, 0)),
                  pl.BlockSpec((SEQ, LANES), lambda b: (b, 1)), _const_spec(table.shape),
                  _const_spec(cg.shape), _const_spec(sg.shape)],
        out_specs=pl.BlockSpec((SEQ, D_MIX), lambda b: (b, 0)),
        out_shape=jax.ShapeDtypeStruct((TOKENS, D_MIX), BF16),
        compiler_params=_params("parallel"),
        name="fnet",
    )(u_fn, u_fn, table, cg, sg)


def _softmax2_pv(s2, v):
    m = jnp.max(s2, axis=-1, keepdims=True)
    p = jnp.exp2(s2 - m)
    l = jnp.sum(p, axis=-1, keepdims=True)
    return _dot(p.astype(BF16), v) / l


def _mla_kernel(q_ref, k_ref, v_ref, out_ref):
    v = v_ref[...]
    head = lax.broadcasted_iota(jnp.int32, out_ref.shape, 1) // MLA_V
    acc = jnp.zeros(out_ref.shape, F32)
    for h in range(MLA_HEADS):
        sl = slice(h * MLA_HEAD_PAD, (h + 1) * MLA_HEAD_PAD)
        s2 = _dot_nt(q_ref[:, sl], k_ref[:, sl])
        acc = jnp.where(head == h, _softmax2_pv(s2, v), acc)
    out_ref[...] = acc.astype(out_ref.dtype)


def _mla(q, k, v):
    tm = TILE["mla"]
    nt = SEQ // tm
    return pl.pallas_call(
        _mla_kernel,
        grid=(BATCH, nt),
        in_specs=[pl.BlockSpec((tm, MLA_QK), lambda b, i: (b * nt + i, 0)),
                  pl.BlockSpec((SEQ, MLA_QK), lambda b, i: (b, 0)),
                  pl.BlockSpec((SEQ, D_MIX), lambda b, i: (b, 0))],
        out_specs=pl.BlockSpec((tm, D_MIX), lambda b, i: (b * nt + i, 0)),
        out_shape=jax.ShapeDtypeStruct((TOKENS, D_MIX), BF16),
        compiler_params=_params("parallel", "parallel"),
        name="mla_attention",
    )(q, k, v)


def _na_key_row0(j):
    return jnp.clip(j * NA_QROWS - NA_WIN_R // 2, 0, GRID_R - NA_KROWS)


NA_PAIRS = 2 * NA_WIN_R


def _na_kernel(q_ref, k_ref, v_ref, tile_ref, out_ref, bias_s):
    j = pl.program_id(0)
    krow0 = _na_key_row0(j)

    @pl.when(pl.program_id(1) == 0)
    def _():
        rq = j * NA_QROWS + lax.broadcasted_iota(jnp.int32, (NA_QBLK, NA_KBLK), 0) // GRID_W
        rk = krow0 + lax.broadcasted_iota(jnp.int32, (NA_QBLK, NA_KBLK), 1) // GRID_W
        rs = jnp.clip(rq - NA_WIN_R // 2, 0, GRID_R - NA_WIN_R)
        rowmask = jnp.where(rk < rs, NEG_BIG, jnp.where(rk >= rs + NA_WIN_R, NEG_BIG, 0.0))
        base = krow0 - j * NA_QROWS + NA_WIN_R
        for h in range(NA_HEADS):
            bias = jnp.concatenate(
                [jnp.concatenate(
                    [tile_ref[h, jnp.clip(base + 2 * kp - r, 0, NA_PAIRS - 1)]
                     for kp in range(NA_KROWS // 2)], axis=1)
                 for r in range(NA_QROWS)], axis=0)
            bias_s[h] = bias + rowmask

    off = pl.multiple_of(krow0 * GRID_W, GRID_W)
    q = q_ref[...]
    k = k_ref[pl.ds(off, NA_KBLK), :]
    v = v_ref[pl.ds(off, NA_KBLK), :]
    head = lax.broadcasted_iota(jnp.int32, (NA_QBLK, D_MIX), 1) // NA_HEAD_DIM
    acc = jnp.zeros((NA_QBLK, D_MIX), F32)
    for h in range(NA_HEADS):
        qh = jnp.where(head == h, q, jnp.zeros_like(q))
        s2 = _dot_nt(qh, k) + bias_s[h]
        acc = jnp.where(head == h, _softmax2_pv(s2, v), acc)
    out_ref[...] = acc.astype(out_ref.dtype)


def _neighborhood(q, k, v, tiles):
    nj = SEQ // NA_QBLK
    return pl.pallas_call(
        _na_kernel,
        grid=(nj, BATCH),
        in_specs=[pl.BlockSpec((NA_QBLK, D_MIX), lambda j, b: (b * nj + j, 0)),
                  pl.BlockSpec((SEQ, D_MIX), lambda j, b: (b, 0)),
                  pl.BlockSpec((SEQ, D_MIX), lambda j, b: (b, 0)),
                  _const_spec(tiles.shape)],
        out_specs=pl.BlockSpec((NA_QBLK, D_MIX), lambda j, b: (b * nj + j, 0)),
        out_shape=jax.ShapeDtypeStruct((TOKENS, D_MIX), BF16),
        scratch_shapes=[pltpu.VMEM((NA_HEADS, NA_QBLK, NA_KBLK), F32)],
        compiler_params=_params("parallel", "arbitrary"),
        name="neighborhood_attention",
    )(q, k, v, tiles)


def _na_bias_tiles(rpb):
    c = jnp.arange(GRID_W)
    cs = jnp.clip(c - NA_WIN_C // 2, 0, GRID_W - NA_WIN_C)
    col_ok = (c[None, :] >= cs[:, None]) & (c[None, :] < cs[:, None] + NA_WIN_C)
    dc = jnp.clip(c[None, :] - c[:, None] + (NA_WIN_C - 1), 0, 2 * NA_WIN_C - 2)
    pick = (dc[None] == jnp.arange(2 * NA_WIN_C - 1)[:, None, None]).astype(F32)
    t = jnp.einsum('hrd,dqk->hrqk', rpb.astype(F32), pick, precision=lax.Precision.HIGHEST)
    t = jnp.where(col_ok, t * LOG2E, NEG_BIG)
    t = jnp.pad(t, ((0, 0), (1, 1), (0, 0), (0, 0)))
    return jnp.concatenate([t[:, :-1], t[:, 1:]], axis=-1)


def _merge_kernel(x_ref, g1_ref, yhy_ref, yfn_ref, ymla_ref, yna_ref, wg_ref, bg_ref,
                  wbr_ref, wout_ref, g2_ref, wr_ref, out_ref, hb_ref, logit_ref):
    x = x_ref[...]
    h = _rms(x, g1_ref[...]).astype(BF16)
    merged = jnp.zeros(x.shape, F32)
    for n, y_ref in enumerate((yhy_ref, yfn_ref, ymla_ref, yna_ref)):
        sl = slice(n * D_MODEL, (n + 1) * D_MODEL)
        gate = jax.nn.sigmoid(_dot_nt(h, wg_ref[sl, :]) + bg_ref[:, sl])
        merged = merged + gate * _dot(y_ref[...], wbr_ref[n])
    x1 = x + _dot(merged.astype(BF16), wout_ref[...])
    out_ref[...] = x1
    h_hi, h_lo = _split2(_rms(x1, g2_ref[...]))
    hb_ref[...] = h_hi
    w_hi, w_lo = _split2(wr_ref[...])
    logits = _dot_nt(w_hi, h_hi) + _dot_nt(w_hi, h_lo) + _dot_nt(w_lo, h_hi)
    logit_ref[...] = logits[:N_EXPERTS]


def _merge(x, g1, yhy, yfn, ymla, yna, wg, bg, wbr, wout, g2, wr_t):
    tm = TILE["merge"]
    nt = SEQ // tm
    row = lambda n: pl.BlockSpec((tm, n), lambda i: (i, 0))
    return pl.pallas_call(
        _merge_kernel,
        grid=(TOKENS // tm,),
        in_specs=[row(D_MODEL), _const_spec(g1.shape), row(D_MIX), row(D_MIX), row(D_MIX),
                  row(D_MIX), _const_spec(wg.shape), _const_spec(bg.shape),
                  _const_spec(wbr.shape), _const_spec(wout.shape), _const_spec(g2.shape),
                  _const_spec(wr_t.shape)],
        out_specs=[row(D_MODEL), row(D_MODEL),
                   pl.BlockSpec((None, N_EXPERTS, tm), lambda i: (i // nt, 0, i % nt))],
        out_shape=[jax.ShapeDtypeStruct((TOKENS, D_MODEL), F32),
                   jax.ShapeDtypeStruct((TOKENS, D_MODEL), BF16),
                   jax.ShapeDtypeStruct((BATCH, N_EXPERTS, SEQ), F32)],
        compiler_params=_params("parallel"),
        name="merge",
    )(x, g1, yhy, yfn, ymla, yna, wg, bg, wbr, wout, g2, wr_t)


def _prefix_count(m):
    r = lax.broadcasted_iota(jnp.int32, (LANES, LANES), 0)
    c = lax.broadcasted_iota(jnp.int32, (LANES, LANES), 1)
    upper = jnp.where(r < c, 1.0, 0.0).astype(BF16)
    run = jnp.zeros((m.shape[0], 1), F32)
    parts = []
    for i in range(SEQ // LANES):
        chunk = m[:, i * LANES:(i + 1) * LANES]
        parts.append(_dot(chunk.astype(BF16), upper) + run)
        run = run + jnp.sum(chunk, axis=1, keepdims=True)
    return jnp.concatenate(parts, axis=1)


SELECT_MAX_ITERS = 192


def _select_kernel(logit_ref, slot_row_ref, slot_col_ref, w_row_ref, start_ref, slot_s, w_s):
    b = pl.program_id(0)
    rows = BATCH * N_EXPERTS

    @pl.when(b == 0)
    def _():
        logits = logit_ref[...]
        ex = jnp.exp(logits - jnp.max(logits, axis=1, keepdims=True))
        aff = (ex / jnp.sum(ex, axis=1, keepdims=True)).reshape(rows, SEQ)

        def bisect(c):
            it, lo, hi, _ = c
            mid = 0.5 * (lo + hi)
            cnt = jnp.sum(jnp.where(aff >= mid, 1.0, 0.0), axis=1, keepdims=True)
            moving = jnp.where(mid == lo, 0.0, jnp.where(mid == hi, 0.0, 1.0))
            enough = cnt >= CAPACITY
            return (it + 1, jnp.where(enough, mid, lo), jnp.where(enough, hi, mid),
                    (jnp.max(moving) > 0).astype(jnp.int32))

        _, lo, hi, _ = lax.while_loop(
            lambda c: (c[0] < SELECT_MAX_ITERS) & (c[3] > 0), bisect,
            (jnp.int32(0), jnp.zeros((rows, 1), F32), jnp.full((rows, 1), 2.0, F32),
             jnp.int32(1)))
        above = jnp.where(aff >= hi, 1.0, 0.0)
        band = jnp.where(aff >= lo, 1.0, 0.0) - above
        need = CAPACITY - jnp.sum(above, axis=1, keepdims=True)
        sel = above + band * jnp.where(_prefix_count(band) < need, 1.0, 0.0)
        slot_s[...] = jnp.where(sel > 0, _prefix_count(sel), -1.0)
        w_s[...] = sel * aff

    r0 = pl.multiple_of(b * N_EXPERTS, N_EXPERTS)
    slot = slot_s[pl.ds(r0, N_EXPERTS), :]
    slot_row_ref[...] = slot.astype(jnp.int32)
    pad = jnp.full((LANES - N_EXPERTS, SEQ), -1.0, F32)
    slot_col_ref[...] = jnp.concatenate([slot, pad], axis=0).T.astype(jnp.int32)
    w_row_ref[...] = w_s[pl.ds(r0, N_EXPERTS), :]
    tm = MOE_FINE
    token = lax.broadcasted_iota(jnp.int32, slot.shape, 1)
    lane = lax.broadcasted_iota(jnp.int32, (N_EXPERTS, LANES), 1)
    starts = jnp.where(lane == SEQ // tm, float(CAPACITY), 0.0)
    for i in range(1, SEQ // tm):
        before = jnp.where(slot >= 0, jnp.where(token < i * tm, 1.0, 0.0), 0.0)
        starts = starts + jnp.where(lane == i, jnp.sum(before, axis=1, keepdims=True), 0.0)
    start_ref[...] = starts.astype(jnp.int32)


def _select(logits):
    return pl.pallas_call(
        _select_kernel,
        grid=(BATCH,),
        in_specs=[_const_spec(logits.shape)],
        out_specs=[pl.BlockSpec((None, N_EXPERTS, SEQ), lambda b: (b, 0, 0)),
                   pl.BlockSpec((None, SEQ, LANES), lambda b: (b, 0, 0)),
                   pl.BlockSpec((None, N_EXPERTS, SEQ), lambda b: (b, 0, 0)),
                   pl.BlockSpec((None, N_EXPERTS, LANES), lambda b: (b, 0, 0))],
        out_shape=[jax.ShapeDtypeStruct((BATCH, N_EXPERTS, SEQ), jnp.int32),
                   jax.ShapeDtypeStruct((BATCH, SEQ, LANES), jnp.int32),
                   jax.ShapeDtypeStruct((BATCH, N_EXPERTS, SEQ), F32),
                   jax.ShapeDtypeStruct((BATCH, N_EXPERTS, LANES), jnp.int32)],
        scratch_shapes=[pltpu.VMEM((BATCH * N_EXPERTS, SEQ), F32),
                        pltpu.VMEM((BATCH * N_EXPERTS, SEQ), F32)],
        compiler_params=_params("arbitrary"),
        name="expert_select",
    )(logits)


MXU_DEPTH = 256
BF16_ROWS = 16


def _slot_window(tokens):
    return 2 * tokens * CAPACITY // SEQ


def _tile_windows(start_ref, b, i, tokens):
    stride = start_ref.shape[0] // (BATCH * N_EXPERTS)
    fine = tokens // MOE_FINE
    window = _slot_window(tokens)
    base, fits = [], None
    for e in range(N_EXPERTS):
        at = (b * N_EXPERTS + e) * stride + i * fine
        lo = jnp.minimum(start_ref[at] // BF16_ROWS * BF16_ROWS, CAPACITY - window)
        ok = start_ref[at + fine] - lo <= window
        fits = ok if fits is None else fits & ok
        base.append(pl.multiple_of(lo, BF16_ROWS))
    return base, fits


def _gather_kernel(start_ref, slot_ref, w_ref, hb_ref, xe_ref, wsl_ref):
    b = pl.program_id(0)
    i = pl.program_id(1)
    tokens = slot_ref.shape[1]

    @pl.when(i == 0)
    def _():
        xe_ref[...] = jnp.zeros(xe_ref.shape, BF16)
        wsl_ref[...] = jnp.zeros(wsl_ref.shape, F32)

    def add_rows(e, rows, hit, picked):
        xe_ref[e, rows, :] = xe_ref[e, rows, :] + picked.astype(BF16)
        wslot = jnp.sum(jnp.where(hit, w_ref[e:e + 1, :], 0.0), axis=1, keepdims=True)
        wsl_ref[e, rows, :] = wsl_ref[e, rows, :] + jnp.broadcast_to(wslot, (hit.shape[0], LANES))

    window = _slot_window(tokens)
    base, fits = _tile_windows(start_ref, b, i, tokens)

    @pl.when(fits)
    def _():
        c = lax.broadcasted_iota(jnp.int32, (window, tokens), 0)
        hits = [slot_ref[e:e + 1, :] - base[e] == c for e in range(N_EXPERTS)]
        onehot = jnp.concatenate([jnp.where(m, 1.0, 0.0).astype(BF16) for m in hits], axis=0)
        picked = _dot(onehot, hb_ref[...])
        for e in range(N_EXPERTS):
            add_rows(e, pl.ds(base[e], window), hits[e], picked[e * window:(e + 1) * window])

    @pl.when(jnp.logical_not(fits))
    def _():
        c = lax.broadcasted_iota(jnp.int32, (CAPACITY, tokens), 0)
        for e in range(N_EXPERTS):
            hit = slot_ref[e:e + 1, :] == c
            picked = _dot(jnp.where(hit, 1.0, 0.0).astype(BF16), hb_ref[...])
            add_rows(e, slice(None), hit, picked)


def _gather(slot_row, w_row, starts, hb):
    tm = TILE["gather"]
    nt = SEQ // tm
    grid_spec = pltpu.PrefetchScalarGridSpec(
        num_scalar_prefetch=1,
        grid=(BATCH, nt),
        in_specs=[pl.BlockSpec((None, N_EXPERTS, tm), lambda b, i, s: (b, 0, i)),
                  pl.BlockSpec((None, N_EXPERTS, tm), lambda b, i, s: (b, 0, i)),
                  pl.BlockSpec((tm, D_MODEL), lambda b, i, s: (b * nt + i, 0))],
        out_specs=[pl.BlockSpec((N_EXPERTS, None, CAPACITY, D_MODEL), lambda b, i, s: (0, b, 0, 0)),
                   pl.BlockSpec((N_EXPERTS, None, CAPACITY, LANES), lambda b, i, s: (0, b, 0, 0))])
    return pl.pallas_call(
        _gather_kernel,
        grid_spec=grid_spec,
        out_shape=[jax.ShapeDtypeStruct((N_EXPERTS, BATCH, CAPACITY, D_MODEL), BF16),
                   jax.ShapeDtypeStruct((N_EXPERTS, BATCH, CAPACITY, LANES), F32)],
        compiler_params=_params("parallel", "arbitrary"),
        name="expert_gather",
    )(starts[:, :, :SEQ // MOE_FINE + 1].reshape(-1), slot_row, w_row, hb)


def _expert_kernel(xe_ref, wsl_ref, wg_ref, wu_ref, wd_ref, ye_ref, wg_s, wu_s, wd_s, *, span):
    ph = pl.program_id(0)
    i = pl.program_id(1)
    rows = pl.ds(pl.multiple_of(i * span, span), span)
    nxt = ph % 2
    wg_s[nxt, rows, :] = wg_ref[rows, :].astype(BF16)
    wu_s[nxt, rows, :] = wu_ref[rows, :].astype(BF16)
    wd_s[nxt, rows, :] = wd_ref[rows, :].astype(BF16)

    @pl.when(ph > 0)
    def _():
        cur = (ph + 1) % 2
        xe = xe_ref[...]
        g = _dot(xe, wg_s[cur])
        u = _dot(xe, wu_s[cur])
        act = (g * jax.nn.sigmoid(g) * u).astype(BF16)
        ye_ref[...] = (_dot(act, wd_s[cur]) * wsl_ref[:, 0:1]).astype(BF16)


def _experts(xe, wsl, wg, wu, wd, layer):
    rows = BATCH * CAPACITY
    tm = TILE["expert"]
    nt = rows // tm
    last = N_EXPERTS - 1
    wspec = lambda a: pl.BlockSpec((None, None) + a.shape[2:],
                                   lambda ph, i: (layer, jnp.minimum(ph, last), 0, 0))
    data = lambda n: pl.BlockSpec(
        (None, tm, n), lambda ph, i: (jnp.maximum(ph - 1, 0), jnp.where(ph > 0, i, 0), 0))
    return pl.pallas_call(
        functools.partial(_expert_kernel, span=D_MODEL // nt),
        grid=(N_EXPERTS + 1, nt),
        in_specs=[data(D_MODEL), data(LANES), wspec(wg), wspec(wu), wspec(wd)],
        out_specs=data(D_MODEL),
        out_shape=jax.ShapeDtypeStruct((N_EXPERTS, rows, D_MODEL), BF16),
        scratch_shapes=[pltpu.VMEM((2, D_MODEL, D_FF), BF16), pltpu.VMEM((2, D_MODEL, D_FF), BF16),
                        pltpu.VMEM((2, D_FF, D_MODEL), BF16)],
        compiler_params=_params("arbitrary", "arbitrary"),
        name="expert_ffn",
    )(xe.reshape(N_EXPERTS, rows, D_MODEL), wsl.reshape(N_EXPERTS, rows, LANES), wg, wu, wd)


def _combine_kernel(start_ref, x_ref, ye_ref, slot_ref, p_ref, g3_ref, wpg_ref, wpp_ref,
                    gf_ref, out_ref, moe_s, *, final_norm):
    slot = slot_ref[...]
    rows = slot.shape[0]
    window = _slot_window(rows)
    base, fits = _tile_windows(start_ref, pl.program_id(0), pl.program_id(1), rows)

    @pl.when(fits)
    def _():
        c = lax.broadcasted_iota(jnp.int32, (rows, window), 1)
        parts = []
        group = MXU_DEPTH // window
        for e0 in range(0, N_EXPERTS, group):
            hot, win = [], []
            for e in range(e0, e0 + group):
                hot.append(jnp.where(slot[:, e:e + 1] - base[e] == c, 1.0, 0.0).astype(BF16))
                win.append(ye_ref[e, pl.ds(base[e], window), :])
            parts.append(_dot(jnp.concatenate(hot, axis=1), jnp.concatenate(win, axis=0)))
        while len(parts) > 1:
            parts = [u + v for u, v in zip(parts[::2], parts[1::2])]
        moe_s[...] = parts[0]

    @pl.when(jnp.logical_not(fits))
    def _():
        c = lax.broadcasted_iota(jnp.int32, (rows, CAPACITY), 1)
        onehot = jnp.concatenate(
            [jnp.where(slot[:, e:e + 1] == c, 1.0, 0.0).astype(BF16) for e in range(N_EXPERTS)],
            axis=1)
        moe_s[...] = _dot(onehot, ye_ref[...].reshape(N_EXPERTS * CAPACITY, D_MODEL))

    acc = x_ref[...] + moe_s[...]
    h = _rms(acc, g3_ref[...]).astype(BF16)
    gate = jax.nn.sigmoid(_dot(h, wpg_ref[...]))
    y = acc + gate * _dot(p_ref[...].astype(BF16), wpp_ref[...])
    if final_norm:
        y = _rms(y, gf_ref[...])
    out_ref[...] = y


def _combine(x, ye, slot_col, starts, p, layer, g3, wpg, wpp, gf, final_norm):
    tm = TILE["combine"]
    nt = SEQ // tm
    p0 = layer * (TOKENS // tm)
    const = lambda a: pl.BlockSpec(a.shape, lambda b, i, s: (0,) * a.ndim,
                                   pipeline_mode=pl.Buffered(1))
    grid_spec = pltpu.PrefetchScalarGridSpec(
        num_scalar_prefetch=1,
        grid=(BATCH, nt),
        in_specs=[pl.BlockSpec((tm, D_MODEL), lambda b, i, s: (b * nt + i, 0)),
                  pl.BlockSpec((N_EXPERTS, None, CAPACITY, D_MODEL), lambda b, i, s: (0, b, 0, 0)),
                  pl.BlockSpec((None, tm, LANES), lambda b, i, s: (b, i, 0)),
                  pl.BlockSpec((tm, PLE_DIM), lambda b, i, s: (p0 + b * nt + i, 0)),
                  const(g3), const(wpg), const(wpp), const(gf)],
        out_specs=pl.BlockSpec((tm, D_MODEL), lambda b, i, s: (b * nt + i, 0)),
        scratch_shapes=[pltpu.VMEM((tm, D_MODEL), F32)])
    return pl.pallas_call(
        functools.partial(_combine_kernel, final_norm=final_norm),
        grid_spec=grid_spec,
        out_shape=jax.ShapeDtypeStruct((TOKENS, D_MODEL), F32),
        compiler_params=_params("parallel", "parallel"),
        name="combine",
    )(starts[:, :, :SEQ // MOE_FINE + 1].reshape(-1), x, ye.reshape(N_EXPERTS, BATCH, CAPACITY, D_MODEL),
      slot_col, p, g3, wpg, wpp, gf)


DFT_FINE = 64
DFT_STEP = 4


def _dft_kernel(ca_ref, sa_ref, cb_ref, sb_ref, out_ref):
    cb = cb_ref[...]
    sb = sb_ref[...]
    for r in range(DFT_STEP):
        rows = slice(r * DFT_FINE, (r + 1) * DFT_FINE)
        out_ref[rows, :] = (ca_ref[r] * cb - sa_ref[r] * sb).astype(BF16)


def _dft_tables(n_points, nres):
    blk = SEQ // nres
    t = jnp.arange(SEQ, dtype=jnp.int32).reshape(blk, nres).T
    t = jnp.concatenate([t, t], axis=1).reshape(1, 2 * SEQ)
    is_sin = (jnp.arange(2 * SEQ) // blk % 2 == 1)[None, :]
    coarse = jnp.arange(HALF // DFT_FINE, dtype=jnp.int32)[:, None] * DFT_FINE
    fine = jnp.arange(DFT_FINE, dtype=jnp.int32)[:, None]
    ang = lambda f: ((f * t) % n_points).astype(F32) * (2.0 * math.pi / n_points)
    ca = jnp.cos(ang(coarse))[:, None, :]
    sa = jnp.sin(ang(coarse))[:, None, :]
    cb = jnp.where(is_sin, jnp.sin(ang(fine)), jnp.cos(ang(fine)))
    sb = jnp.where(is_sin, -jnp.cos(ang(fine)), jnp.sin(ang(fine)))
    rows = DFT_STEP * DFT_FINE
    return pl.pallas_call(
        _dft_kernel,
        grid=(HALF // rows,),
        in_specs=[pl.BlockSpec((DFT_STEP, 1, 2 * SEQ), lambda i: (i, 0, 0)),
                  pl.BlockSpec((DFT_STEP, 1, 2 * SEQ), lambda i: (i, 0, 0)),
                  _const_spec(cb.shape), _const_spec(sb.shape)],
        out_specs=pl.BlockSpec((rows, 2 * SEQ), lambda i: (i, 0)),
        out_shape=jax.ShapeDtypeStruct((HALF, 2 * SEQ), BF16),
        compiler_params=_params("parallel"),
        name="dft_tables",
    )(ca, sa, cb, sb)


def _fnet_group_tables():
    gc = D_MIX // FN_GROUPS
    i = lax.broadcasted_iota(jnp.int32, (D_MIX, D_MIX), 0)
    j = lax.broadcasted_iota(jnp.int32, (D_MIX, D_MIX), 1)
    same = (i // gc) == (j // gc)
    ang = (((i % gc) * (j % gc)) % gc).astype(F32) * (2.0 * math.pi / gc)
    return (jnp.where(same, jnp.cos(ang), 0.0).astype(BF16),
            jnp.where(same, jnp.sin(ang), 0.0).astype(BF16))


def _hyena_features():
    t01 = jnp.linspace(0.0, 1.0, SEQ, dtype=F32)[:, None]
    bands = jnp.linspace(1e-4, HY_BANDS - 1, HY_BANDS, dtype=F32)
    ang = 2.0 * math.pi * jnp.arange(SEQ, dtype=F32)[:, None] * bands / SEQ
    z = jnp.concatenate([t01, jnp.cos(ang), -jnp.sin(ang)], axis=-1)
    z = jnp.pad(z, ((0, 0), (0, LANES - HY_EMB)))
    max_decay = math.log(HY_TARGET) / HY_FAST_DECAY
    min_decay = math.log(HY_TARGET) / HY_SLOW_DECAY
    deltas = jnp.linspace(min_decay, max_decay, D_MIX, dtype=F32)
    window = jnp.exp(-t01 * jnp.abs(deltas))
    return z, window


def _rot_cols(w):
    half = w.shape[-1] // 2
    return jnp.concatenate([-w[..., half:], w[..., :half]], axis=-1)


def _rope_tables():
    inv = ROPE_THETA ** (-jnp.arange(0, MLA_ROPE, 2, dtype=F32) / MLA_ROPE)
    ang = jnp.arange(SEQ, dtype=F32)[:, None] * inv
    cos = jnp.concatenate([jnp.cos(ang), jnp.cos(ang)], axis=-1)
    sin = jnp.concatenate([jnp.sin(ang), jnp.sin(ang)], axis=-1)
    scale = (MLA_NOPE + MLA_ROPE) ** -0.5 * LOG2E
    pad = MLA_HEAD_PAD - MLA_NOPE - MLA_ROPE
    one = jnp.ones((SEQ, MLA_NOPE), F32)
    zero = jnp.zeros((SEQ, MLA_NOPE), F32)
    zpad = jnp.zeros((SEQ, pad), F32)
    cosq = jnp.tile(jnp.concatenate([one, cos, zpad], axis=-1) * scale, (1, MLA_HEADS))
    sinq = jnp.tile(jnp.concatenate([zero, sin, zpad], axis=-1) * scale, (1, MLA_HEADS))
    csk = jnp.concatenate([cos, sin, jnp.zeros((SEQ, LANES - 2 * MLA_ROPE), F32)], axis=-1)
    return cosq, sinq, csk


def _mla_weights(w_uq, w_ukv):
    pad = MLA_HEAD_PAD - MLA_NOPE - MLA_ROPE
    wq = w_uq.reshape(MLA_Q_RANK, MLA_HEADS, MLA_NOPE + MLA_ROPE)
    nope, pe = wq[..., :MLA_NOPE], wq[..., MLA_NOPE:]
    zp = jnp.zeros((MLA_Q_RANK, MLA_HEADS, pad), F32)
    wqa = jnp.concatenate([nope, pe, zp], axis=-1).reshape(MLA_Q_RANK, MLA_QK)
    wqb = jnp.concatenate([jnp.zeros_like(nope), _rot_cols(pe), zp], axis=-1)
    wqb = wqb.reshape(MLA_Q_RANK, MLA_QK)
    wkv = w_ukv.reshape(MLA_KV_RANK, MLA_HEADS, MLA_NOPE + MLA_V)
    knope, v = wkv[..., :MLA_NOPE], wkv[..., MLA_NOPE:]
    wk = jnp.concatenate(
        [knope, jnp.zeros((MLA_KV_RANK, MLA_HEADS, MLA_HEAD_PAD - MLA_NOPE), F32)], axis=-1)
    wk = wk.reshape(MLA_KV_RANK, MLA_QK)
    wv = v.reshape(MLA_KV_RANK, MLA_HEADS * MLA_V)
    r = lax.broadcasted_iota(jnp.int32, (LANES, MLA_QK), 0)
    c = lax.broadcasted_iota(jnp.int32, (LANES, MLA_QK), 1)
    epe = jnp.where((r < 2 * MLA_ROPE) & (c % MLA_HEAD_PAD == MLA_NOPE + r % MLA_ROPE), 1.0, 0.0)
    return (wqa.astype(BF16), wqb.astype(BF16), wk.astype(BF16), wv.astype(BF16),
            epe.astype(BF16))


def kernel(x, p, norm1_g, w_in, b_gate, hy_conv_w, hy_conv_b, hf_w1, hf_b1, hf_freq, hf_w2,
           hf_b2, hf_w3, hy_skip, q_norm_g, w_uq, kv_norm_g, w_ukv, rpb, w_br, w_out, norm2_g,
           w_router, w_e_gate, w_e_up, w_e_down, norm3_g, w_ple_gate, w_ple_proj, final_g):
    conv_tab = _dft_tables(2 * SEQ, 4)
    fnet_tab = _dft_tables(SEQ, 2)
    fnet_cg, fnet_sg = _fnet_group_tables()
    zfeat, window = _hyena_features()
    cosq, sinq, csk = _rope_tables()
    row = lambda a: a.reshape(1, -1)

    xt = x.reshape(TOKENS, D_MODEL)
    w_in_t = jnp.swapaxes(w_in, 1, 2)
    for i in range(DEPTH):
        g1 = row(norm1_g[i])
        wa, wna, wgate = _inproj_weights(w_in_t, i)
        wqa, wqb, wk, wv, epe = _mla_weights(w_uq[i], w_ukv[i])

        u_hy, u_fn, q, k, v, naq, nak, nav = _inproj(
            xt, g1, wa, wna, row(q_norm_g[i]), wqa, wqb, row(kv_norm_g[i]), wk, wv, epe,
            cosq, sinq, csk)

        w1 = jnp.pad(hf_w1[i], ((0, LANES - HY_EMB), (0, 0)))
        kf, kny = _hyena_filter(zfeat, window, w1, row(hf_b1[i]), hf_freq[i], hf_w2[i],
                                row(hf_b2[i]), hf_w3[i], conv_tab)
        conv_b = row(hy_conv_b[i])
        z1 = _hyena_stage(u_hy, 0, u_hy, 2, hy_conv_w[i], conv_b, conv_tab, kf, kny, 0,
                          row(hy_skip[i, 0]), True, F32)
        y_hy = _hyena_stage(u_hy, 1, z1, 0, hy_conv_w[i], conv_b, conv_tab, kf, kny, 1,
                            row(hy_skip[i, 1]), False, BF16)
        y_fn = _fnet(u_fn, fnet_tab, fnet_cg, fnet_sg)
        y_mla = _mla(q, k, v)
        y_na = _neighborhood(naq, nak, nav, _na_bias_tiles(rpb[i]))
        wr_t = jnp.pad(w_router[i].T, ((0, LANES - N_EXPERTS), (0, 0)))
        xt, hb, logits = _merge(xt, g1, y_hy, y_fn, y_mla, y_na, wgate, row(b_gate[i]),
                                w_br[i].astype(BF16), w_out[i].astype(BF16),
                                row(norm2_g[i]), wr_t)
        slot_row, slot_col, w_row, starts = _select(logits)
        xe, wsl = _gather(slot_row, w_row, starts, hb)
        ye = _experts(xe, wsl, w_e_gate, w_e_up, w_e_down, i)
        xt = _combine(xt, ye, slot_col, starts, p.reshape(DEPTH * TOKENS, PLE_DIM), i,
                      row(norm3_g[i]), w_ple_gate[i].astype(BF16),
                      w_ple_proj[i].astype(BF16), row(final_g), i == DEPTH - 1)
    return xt.reshape(BATCH, SEQ, D_MODEL)
```

```python
import functools
import math

import jax
import jax.numpy as jnp
from jax import lax
from jax.experimental import pallas as pl
from jax.experimental.pallas import tpu as pltpu

F32 = jnp.float32
BF16 = jnp.bfloat16

D_MODEL = 1024
BATCH = 8
SEQ = 2048
DEPTH = 2
TOKENS = BATCH * SEQ

GRID_W = 64
GRID_R = SEQ // GRID_W
D_MIX = 256
N_BRANCH = 4
EPS = 1e-6
HY_ORDER = 2
HY_BANDS = 16
HY_EMB = 2 * HY_BANDS + 1
HY_FFN = 64
HY_TARGET = 1e-2
HY_FAST_DECAY = 0.3
HY_SLOW_DECAY = 1.5
FN_GROUPS = 4
MLA_HEADS = 4
MLA_NOPE = 64
MLA_ROPE = 32
MLA_V = 64
MLA_Q_RANK = 256
MLA_KV_RANK = 128
ROPE_THETA = 10000.0
NA_HEADS = 4
NA_HEAD_DIM = D_MIX // NA_HEADS
NA_WIN_R = 8
NA_WIN_C = 16
N_EXPERTS = 16
CAPACITY = 2 * SEQ // N_EXPERTS
D_FF = 1024
PLE_DIM = 256

HY_COLS = 3 * D_MIX
OFF_FN = HY_COLS
OFF_CQ = OFF_FN + D_MIX
OFF_CKV = OFF_CQ + MLA_Q_RANK
OFF_KPE = OFF_CKV + MLA_KV_RANK
OFF_NA = OFF_KPE + MLA_ROPE
OFF_GATE = OFF_NA + 3 * D_MIX

LANES = 128
MLA_HEAD_PAD = 128
MLA_QK = MLA_HEADS * MLA_HEAD_PAD
WA_COLS = 1536
NEG_BIG = -1e30
LOG2E = math.log2(math.e)

TILE = dict(inproj=512, mla=512, merge=512, gather=256, expert=512, combine=512)
MOE_FINE = 256
NA_QROWS = 8
NA_KROWS = 16
NA_QBLK = NA_QROWS * GRID_W
NA_KBLK = NA_KROWS * GRID_W
VMEM_LIMIT = 56 * 1024 * 1024


def _params(*sem):
    return pltpu.CompilerParams(dimension_semantics=sem, vmem_limit_bytes=VMEM_LIMIT)


def _const_spec(shape):
    nd = len(shape)
    return pl.BlockSpec(shape, lambda *_: (0,) * nd, pipeline_mode=pl.Buffered(1))


def _rms(x, g):
    return x * lax.rsqrt(jnp.mean(x * x, axis=-1, keepdims=True) + EPS) * g


def _dot(a, b):
    return jnp.dot(a, b, preferred_element_type=F32)


def _dot_nt(a, b):
    return lax.dot_general(a, b, (((1,), (1,)), ((), ())), preferred_element_type=F32)


def _split2(x):
    hi = x.astype(BF16)
    lo = (x - hi.astype(F32)).astype(BF16)
    return hi, lo


def _inproj_kernel(x_ref, g1_ref, wa_ref, wna_ref, qg_ref, wqa_ref, wqb_ref, kvg_ref,
                   wk_ref, wv_ref, epe_ref, cosq_ref, sinq_ref, csk_ref,
                   uhy_ref, ufn_ref, q_ref, k_ref, v_ref, naq_ref, nak_ref, nav_ref):
    h = _rms(x_ref[...], g1_ref[...]).astype(BF16)
    ua = _dot_nt(h, wa_ref[...])
    uhy_ref[...] = ua[:, :HY_COLS]
    ufn_ref[...] = ua[:, OFF_FN:OFF_CQ]
    cqn = _rms(ua[:, OFF_CQ:OFF_CKV], qg_ref[...]).astype(BF16)
    q = _dot(cqn, wqa_ref[...]) * cosq_ref[...] + _dot(cqn, wqb_ref[...]) * sinq_ref[...]
    q_ref[...] = q.astype(BF16)
    kvn = _rms(ua[:, OFF_CKV:OFF_KPE], kvg_ref[...]).astype(BF16)
    kpe = ua[:, OFF_KPE:WA_COLS] * csk_ref[...]
    k = _dot(kvn, wk_ref[...]) + _dot(kpe.astype(BF16), epe_ref[...])
    k_ref[...] = k.astype(BF16)
    v_ref[...] = _dot_nt(wv_ref[...], kvn).astype(BF16)
    una = _dot_nt(h, wna_ref[...])
    naq_ref[...] = (una[:, :D_MIX] * (NA_HEAD_DIM ** -0.5 * LOG2E)).astype(BF16)
    nak_ref[...] = una[:, D_MIX:2 * D_MIX].astype(BF16)
    nav_ref[...] = una[:, 2 * D_MIX:].astype(BF16)


def _inproj(x, g1, wa, wna, qg, wqa, wqb, kvg, wk, wv, epe, cosq, sinq, csk):
    tm = TILE["inproj"]
    nt = SEQ // tm
    row = lambda n: pl.BlockSpec((tm, n), lambda i: (i, 0))
    pos = lambda n: pl.BlockSpec((tm, n), lambda i: (i % nt, 0))
    outs = [(HY_COLS, F32), (D_MIX, F32), (MLA_QK, BF16), (MLA_QK, BF16), None,
            (D_MIX, BF16), (D_MIX, BF16), (D_MIX, BF16)]
    vt_spec = pl.BlockSpec((None, D_MIX, tm), lambda i: (i // nt, 0, i % nt))
    vt_shape = jax.ShapeDtypeStruct((BATCH, D_MIX, SEQ), BF16)
    return pl.pallas_call(
        _inproj_kernel,
        grid=(TOKENS // tm,),
        in_specs=[row(D_MODEL), _const_spec(g1.shape), _const_spec(wa.shape),
                  _const_spec(wna.shape), _const_spec(qg.shape), _const_spec(wqa.shape),
                  _const_spec(wqb.shape), _const_spec(kvg.shape), _const_spec(wk.shape),
                  _const_spec(wv.shape), _const_spec(epe.shape),
                  pos(MLA_QK), pos(MLA_QK), pos(LANES)],
        out_specs=[vt_spec if o is None else row(o[0]) for o in outs],
        out_shape=[vt_shape if o is None else jax.ShapeDtypeStruct((TOKENS, o[0]), o[1])
                   for o in outs],
        compiler_params=_params("parallel"),
        name="inproj",
    )(x, g1, wa, wna, qg, wqa, wqb, kvg, wk, wv, epe, cosq, sinq, csk)


WPREP_ROWS = 512


def _inproj_weights_kernel(w_ref, wa_ref, wna_ref, wg_ref):
    i = pl.program_id(0)

    @pl.when(i == 0)
    def _():
        half = MLA_ROPE // 2
        wa_ref[:OFF_NA, :] = w_ref[:OFF_NA, :].astype(BF16)
        wa_ref[OFF_NA:OFF_NA + half, :] = (-w_ref[OFF_KPE + half:OFF_NA, :]).astype(BF16)
        wa_ref[OFF_NA + half:OFF_NA + MLA_ROPE, :] = w_ref[OFF_KPE:OFF_KPE + half, :].astype(BF16)
        wa_ref[OFF_NA + MLA_ROPE:, :] = jnp.zeros((WA_COLS - OFF_NA - MLA_ROPE, D_MODEL), BF16)
        wna_ref[...] = w_ref[OFF_NA:OFF_GATE, :].astype(BF16)

    start = pl.multiple_of(OFF_GATE + i * WPREP_ROWS, MLA_ROPE)
    wg_ref[...] = w_ref[pl.ds(start, WPREP_ROWS), :].astype(BF16)


def _inproj_weights(w_in_t, layer):
    n_in = w_in_t.shape[1]
    rows = (WA_COLS, OFF_GATE - OFF_NA, n_in - OFF_GATE)
    whole = lambda n: pl.BlockSpec((n, D_MODEL), lambda i: (0, 0))
    return pl.pallas_call(
        _inproj_weights_kernel,
        grid=(rows[2] // WPREP_ROWS,),
        in_specs=[pl.BlockSpec((None, n_in, D_MODEL), lambda i: (layer, 0, 0),
                               pipeline_mode=pl.Buffered(1))],
        out_specs=[whole(rows[0]), whole(rows[1]),
                   pl.BlockSpec((WPREP_ROWS, D_MODEL), lambda i: (i, 0))],
        out_shape=[jax.ShapeDtypeStruct((n, D_MODEL), BF16) for n in rows],
        compiler_params=_params("arbitrary"),
        name="inproj_weights",
    )(w_in_t)


HALF = SEQ // 2
HY_SEQS = 1


def _residue_rows(ref, r, nres):
    rows = pl.ds(r, SEQ // nres, stride=nres)
    if isinstance(ref, tuple):
        return jnp.concatenate([h[rows, :] for h in ref], axis=1)
    return jnp.concatenate([ref[j, rows, :] for j in range(ref.shape[0])], axis=1)


def _store_slabs(ref, value, rows=slice(None)):
    for j in range(ref.shape[0]):
        ref[j, rows, :] = value[:, j * LANES:(j + 1) * LANES]


def _table_cols(cs_ref, r, nres, part=None):
    blk = SEQ // nres
    lo = 2 * blk * r
    if part is None:
        return cs_ref[:, lo:lo + 2 * blk]
    return cs_ref[:, lo + part * blk:lo + (part + 1) * blk]


def _dft_fwd(cs_ref, src_ref, want_cos=True, want_sin=True, split=False):
    def prod(part, z, r):
        tab = _table_cols(cs_ref, r, 4, part)
        if split:
            hi, lo = _split2(z)
            return _dot(tab, hi) + _dot(tab, lo)
        return _dot(tab, z.astype(BF16))

    pc, ps = [None] * 4, [None] * 4
    for r in range(4):
        z = _residue_rows(src_ref, r, 4)
        odd = r % 2 == 1
        if want_cos or odd:
            pc[r] = prod(0, z, r)
        if want_sin or odd:
            ps[r] = prod(1, z, r)
    a = b = None
    if want_cos:
        a = ((pc[0] + pc[2]) + (pc[1] + pc[3]), (pc[0] - pc[2]) + (ps[3] - ps[1]))
    if want_sin:
        b = ((ps[0] + ps[2]) + (ps[1] + ps[3]), (ps[0] - ps[2]) + (pc[1] - pc[3]))
    return a, b


def _dft_inv(cs_ref, yre_ref, yim_ref):
    g, h = [], {}
    for r in range(4):
        yr = _residue_rows(yre_ref, r, 4).astype(BF16)
        yi = _residue_rows(yim_ref, r, 4).astype(BF16)
        tab = _table_cols(cs_ref, r, 4)
        g.append(_dot(tab, jnp.concatenate([yr, -yi], axis=0)))
        if r % 2 == 1:
            h[r] = _dot(tab, jnp.concatenate([yi, yr], axis=0))
    return (g[0] + g[2]) + (g[1] + g[3]), (g[0] - g[2]) + (h[3] - h[1])


def _hyena_filter_kernel(z_ref, win_ref, w1_ref, b1_ref, freq_ref, w2_ref, b2_ref, w3_ref,
                         cs_ref, kf_ref, kny_ref, ksum_s, kdif_s, trunk_s):
    hp = lax.Precision.HIGHEST

    @pl.when(pl.program_id(0) == 0)
    def _():
        freq = freq_ref[...]
        hf = jnp.sin(freq[0:1] * (jnp.dot(z_ref[...], w1_ref[...], precision=hp,
                                          preferred_element_type=F32) + b1_ref[...]))
        trunk_s[...] = jnp.sin(freq[1:2] * (jnp.dot(hf, w2_ref[...], precision=hp,
                                                    preferred_element_type=F32) + b2_ref[...]))

    hf = jnp.dot(trunk_s[...], w3_ref[...], precision=hp, preferred_element_type=F32)
    win = win_ref[...]
    t = lax.broadcasted_iota(jnp.int32, (SEQ, D_MIX), 0)
    sgn = (1 - 2 * (t & 1)).astype(F32)
    fwd = hf[:, :D_MIX] * win
    bwd = jnp.where(t == 0, 0.0, hf[:, D_MIX:] * win)
    nrm = lax.rsqrt(jnp.sum(fwd * fwd + bwd * bwd, axis=0, keepdims=True) + EPS)
    ksum = (fwd + bwd) * nrm
    _store_slabs(ksum_s, ksum)
    _store_slabs(kdif_s, (bwd - fwd) * nrm)
    kre, _ = _dft_fwd(cs_ref, ksum_s, want_sin=False, split=True)
    _, kim = _dft_fwd(cs_ref, kdif_s, want_cos=False, split=True)
    wf = 2.0 / (2 * SEQ)
    for part, spec in enumerate((kre, kim)):
        lo = spec[0] * wf
        kf_ref[part, :HALF, :] = lo
        kf_ref[part, 0:1, :] = lo[0:1] * 0.5
        kf_ref[part, HALF:, :] = spec[1] * wf
    kny = jnp.sum(ksum * sgn, axis=0, keepdims=True) * (1.0 / (2 * SEQ))
    kny_ref[...] = jnp.broadcast_to(kny, (8, D_MIX))


def _hyena_filter(zfeat, window, w1, b1, freq, w2, b2, w3, table):
    consts = (zfeat, window, w1, b1, freq, w2, b2)
    return pl.pallas_call(
        _hyena_filter_kernel,
        grid=(HY_ORDER,),
        in_specs=[_const_spec(a.shape) for a in consts]
        + [pl.BlockSpec((HY_FFN, 2 * D_MIX), lambda o: (0, o)),
           _const_spec(table.shape)],
        out_specs=[pl.BlockSpec((None, 2, SEQ, D_MIX), lambda o: (o, 0, 0, 0)),
                   pl.BlockSpec((None, 8, D_MIX), lambda o: (o, 0, 0))],
        out_shape=[jax.ShapeDtypeStruct((HY_ORDER, 2, SEQ, D_MIX), F32),
                   jax.ShapeDtypeStruct((HY_ORDER, 8, D_MIX), F32)],
        scratch_shapes=[pltpu.VMEM((D_MIX // LANES, SEQ, LANES), F32)] * 2
        + [pltpu.VMEM((SEQ, HY_FFN), F32)],
        compiler_params=_params("arbitrary"),
        name="hyena_filter",
    )(*consts, w3, table)


def _short_conv(u, w, b):
    t = lax.broadcasted_iota(jnp.int32, u.shape, 0)
    prev = jnp.where(t == 0, 0.0, pltpu.roll(u, 1, 0))
    nxt = jnp.where(t == SEQ - 1, 0.0, pltpu.roll(u, SEQ - 1, 0))
    return prev * w[0:1] + u * w[1:2] + nxt * w[2:3] + b


def _hyena_stage_kernel(gate_ref, src_ref, wg_ref, bg_ref, ws_ref, bs_ref, cs_ref,
                        kf_ref, kny_ref, skip_ref, out_ref, z_s, yre_s, yim_s, gate_s, rest_s,
                        *, conv_src):
    for q in range(HY_SEQS):
        seq = slice(q * SEQ, (q + 1) * SEQ)
        zq, yre_q, yim_q = z_s.at[q], yre_s.at[q], yim_s.at[q]
        z = src_ref[seq, :]
        if conv_src:
            z = _short_conv(z, ws_ref[...], bs_ref[...])
        _store_slabs(zq, z)
        t = lax.broadcasted_iota(jnp.int32, z.shape, 0)
        sgn = (1 - 2 * (t & 1)).astype(F32)
        nyq = jnp.sum(z * sgn, axis=0, keepdims=True) * kny_ref[0:1]
        gate = _short_conv(gate_ref[seq, :], wg_ref[...], bg_ref[...])
        gate_s[seq, :] = gate
        rest_s[seq, :] = gate * (sgn * nyq + z * skip_ref[...])
        a, b = _dft_fwd(cs_ref, zq)
        for half in range(2):
            rows = slice(half * HALF, (half + 1) * HALF)
            kre = kf_ref[0, rows, :]
            kim = kf_ref[1, rows, :]
            _store_slabs(yre_q, a[half] * kre + b[half] * kim, rows)
            _store_slabs(yim_q, a[half] * kim - b[half] * kre, rows)
        for half, y in enumerate(_dft_inv(cs_ref, yre_q, yim_q)):
            rows = slice(q * SEQ + half * HALF, q * SEQ + (half + 1) * HALF)
            out_ref[rows, :] = (gate_s[rows, :] * y + rest_s[rows, :]).astype(out_ref.dtype)


def _hyena_stage(u_hy, gate_blk, src, src_blk, conv_w, conv_b, table, kf, kny, order, skip,
                 conv_src, out_dtype):
    rows = HY_SEQS * SEQ
    of_order = lambda a: pl.BlockSpec(
        (None,) + a.shape[1:], lambda b: (order,) + (0,) * (a.ndim - 1),
        pipeline_mode=pl.Buffered(1))
    col = lambda blk: pl.BlockSpec((rows, D_MIX), lambda b: (b, blk))
    wcol = lambda blk, r: pl.BlockSpec((r, D_MIX), lambda b: (0, blk))
    ws_blk = src_blk if conv_src else 0
    return pl.pallas_call(
        functools.partial(_hyena_stage_kernel, conv_src=conv_src),
        grid=(BATCH // HY_SEQS,),
        in_specs=[col(gate_blk), col(src_blk), wcol(gate_blk, 3), wcol(gate_blk, 1),
                  wcol(ws_blk, 3), wcol(ws_blk, 1), _const_spec(table.shape),
                  of_order(kf), of_order(kny), _const_spec(skip.shape)],
        out_specs=pl.BlockSpec((rows, D_MIX), lambda b: (b, 0)),
        out_shape=jax.ShapeDtypeStruct((TOKENS, D_MIX), out_dtype),
        scratch_shapes=[pltpu.VMEM((HY_SEQS, D_MIX // LANES, SEQ, LANES), F32)] * 3
        + [pltpu.VMEM((rows, D_MIX), F32)] * 2,
        compiler_params=_params("parallel"),
        name="hyena_stage",
    )(u_hy, src, conv_w, conv_b, conv_w, conv_b, table, kf, kny, skip)


def _fnet_kernel(xa_ref, xb_ref, cs_ref, cg_ref, sg_ref, out_ref):
    parts = []
    for r in range(2):
        xb = _residue_rows((xa_ref, xb_ref), r, 2).astype(BF16)
        xc = _dot(xb, cg_ref[...]).astype(BF16)
        xs = _dot(xb, sg_ref[...]).astype(BF16)
        parts.append(_dot(_table_cols(cs_ref, r, 2), jnp.concatenate([xc, -xs], axis=0)))
    scale = (SEQ * D_MIX // FN_GROUPS) ** -0.5
    out_ref[:HALF, :] = ((parts[0] + parts[1]) * scale).astype(out_ref.dtype)
    out_ref[HALF:, :] = ((parts[0] - parts[1]) * scale).astype(out_ref.dtype)


def _fnet(u_fn, table, cg, sg):
    return pl.pallas_call(
        _fnet_kernel,
        grid=(BATCH,),
        in_specs=[pl.BlockSpec((SEQ, LANES), lambda b: (b, 0)),
                  pl.BlockSpec((SEQ, LANES), lambda b: (b, 1)), _const_spec(table.shape),
                  _const_spec(cg.shape), _const_spec(sg.shape)],
        out_specs=pl.BlockSpec((SEQ, D_MIX), lambda b: (b, 0)),
        out_shape=jax.ShapeDtypeStruct((TOKENS, D_MIX), BF16),
        compiler_params=_params("parallel"),
        name="fnet",
    )(u_fn, u_fn, table, cg, sg)


def _softmax2_pv(s2, v):
    m = jnp.max(s2, axis=-1, keepdims=True)
    p = jnp.exp2(s2 - m)
    l = jnp.sum(p, axis=-1, keepdims=True)
    return _dot(p.astype(BF16), v) / l


def _mla_kernel(q_ref, k_ref, vt_ref, out_ref):
    outs = []
    for h in range(MLA_HEADS):
        sl = slice(h * MLA_HEAD_PAD, (h + 1) * MLA_HEAD_PAD)
        s2 = _dot_nt(k_ref[:, sl], q_ref[:, sl])
        p = jnp.exp2(s2 - jnp.max(s2, axis=0, keepdims=True))
        l = jnp.sum(p, axis=0, keepdims=True)
        o = _dot(vt_ref[h * MLA_V:(h + 1) * MLA_V, :], p.astype(BF16))
        outs.append(o / l)
    out_ref[...] = jnp.concatenate(outs, axis=0).T.astype(out_ref.dtype)


def _mla(q, k, vt):
    tm = TILE["mla"]
    nt = SEQ // tm
    return pl.pallas_call(
        _mla_kernel,
        grid=(BATCH, nt),
        in_specs=[pl.BlockSpec((tm, MLA_QK), lambda b, i: (b * nt + i, 0)),
                  pl.BlockSpec((SEQ, MLA_QK), lambda b, i: (b, 0)),
                  pl.BlockSpec((None, D_MIX, SEQ), lambda b, i: (b, 0, 0))],
        out_specs=pl.BlockSpec((tm, D_MIX), lambda b, i: (b * nt + i, 0)),
        out_shape=jax.ShapeDtypeStruct((TOKENS, D_MIX), BF16),
        compiler_params=_params("parallel", "parallel"),
        name="mla_attention",
    )(q, k, vt)


def _na_key_row0(j):
    return jnp.clip(j * NA_QROWS - NA_WIN_R // 2, 0, GRID_R - NA_KROWS)


NA_PAIRS = 2 * NA_WIN_R


def _na_kernel(q_ref, k_ref, v_ref, tile_ref, out_ref, bias_s):
    j = pl.program_id(0)
    krow0 = _na_key_row0(j)

    @pl.when(pl.program_id(1) == 0)
    def _():
        rq = j * NA_QROWS + lax.broadcasted_iota(jnp.int32, (NA_QBLK, NA_KBLK), 0) // GRID_W
        rk = krow0 + lax.broadcasted_iota(jnp.int32, (NA_QBLK, NA_KBLK), 1) // GRID_W
        rs = jnp.clip(rq - NA_WIN_R // 2, 0, GRID_R - NA_WIN_R)
        rowmask = jnp.where(rk < rs, NEG_BIG, jnp.where(rk >= rs + NA_WIN_R, NEG_BIG, 0.0))
        base = krow0 - j * NA_QROWS + NA_WIN_R
        for h in range(NA_HEADS):
            bias = jnp.concatenate(
                [jnp.concatenate(
                    [tile_ref[h, jnp.clip(base + 2 * kp - r, 0, NA_PAIRS - 1)]
                     for kp in range(NA_KROWS // 2)], axis=1)
                 for r in range(NA_QROWS)], axis=0)
            bias_s[h] = bias + rowmask

    off = pl.multiple_of(krow0 * GRID_W, GRID_W)
    q = q_ref[...]
    k = k_ref[pl.ds(off, NA_KBLK), :]
    v = v_ref[pl.ds(off, NA_KBLK), :]
    head = lax.broadcasted_iota(jnp.int32, (NA_QBLK, D_MIX), 1) // NA_HEAD_DIM
    acc = jnp.zeros((NA_QBLK, D_MIX), F32)
    for h in range(NA_HEADS):
        qh = jnp.where(head == h, q, jnp.zeros_like(q))
        s2 = _dot_nt(qh, k) + bias_s[h]
        acc = jnp.where(head == h, _softmax2_pv(s2, v), acc)
    out_ref[...] = acc.astype(out_ref.dtype)


def _neighborhood(q, k, v, tiles):
    nj = SEQ // NA_QBLK
    return pl.pallas_call(
        _na_kernel,
        grid=(nj, BATCH),
        in_specs=[pl.BlockSpec((NA_QBLK, D_MIX), lambda j, b: (b * nj + j, 0)),
                  pl.BlockSpec((SEQ, D_MIX), lambda j, b: (b, 0)),
                  pl.BlockSpec((SEQ, D_MIX), lambda j, b: (b, 0)),
                  _const_spec(tiles.shape)],
        out_specs=pl.BlockSpec((NA_QBLK, D_MIX), lambda j, b: (b * nj + j, 0)),
        out_shape=jax.ShapeDtypeStruct((TOKENS, D_MIX), BF16),
        scratch_shapes=[pltpu.VMEM((NA_HEADS, NA_QBLK, NA_KBLK), F32)],
        compiler_params=_params("parallel", "arbitrary"),
        name="neighborhood_attention",
    )(q, k, v, tiles)


def _na_bias_tiles(rpb):
    c = jnp.arange(GRID_W)
    cs = jnp.clip(c - NA_WIN_C // 2, 0, GRID_W - NA_WIN_C)
    col_ok = (c[None, :] >= cs[:, None]) & (c[None, :] < cs[:, None] + NA_WIN_C)
    dc = jnp.clip(c[None, :] - c[:, None] + (NA_WIN_C - 1), 0, 2 * NA_WIN_C - 2)
    pick = (dc[None] == jnp.arange(2 * NA_WIN_C - 1)[:, None, None]).astype(F32)
    t = jnp.einsum('hrd,dqk->hrqk', rpb.astype(F32), pick, precision=lax.Precision.HIGHEST)
    t = jnp.where(col_ok, t * LOG2E, NEG_BIG)
    t = jnp.pad(t, ((0, 0), (1, 1), (0, 0), (0, 0)))
    return jnp.concatenate([t[:, :-1], t[:, 1:]], axis=-1)


def _merge_kernel(x_ref, g1_ref, yhy_ref, yfn_ref, ymla_ref, yna_ref, wg_ref, bg_ref,
                  wbr_ref, wout_ref, g2_ref, wr_ref, out_ref, hb_ref, logit_ref):
    x = x_ref[...]
    h = _rms(x, g1_ref[...]).astype(BF16)
    merged = jnp.zeros(x.shape, F32)
    for n, y_ref in enumerate((yhy_ref, yfn_ref, ymla_ref, yna_ref)):
        sl = slice(n * D_MODEL, (n + 1) * D_MODEL)
        gate = jax.nn.sigmoid(_dot_nt(h, wg_ref[sl, :]) + bg_ref[:, sl])
        merged = merged + gate * _dot(y_ref[...], wbr_ref[n])
    x1 = x + _dot(merged.astype(BF16), wout_ref[...])
    out_ref[...] = x1
    h_hi, h_lo = _split2(_rms(x1, g2_ref[...]))
    hb_ref[...] = h_hi
    w_hi, w_lo = _split2(wr_ref[...])
    logits = _dot_nt(w_hi, h_hi) + _dot_nt(w_hi, h_lo) + _dot_nt(w_lo, h_hi)
    logit_ref[...] = logits[:N_EXPERTS]


def _merge(x, g1, yhy, yfn, ymla, yna, wg, bg, wbr, wout, g2, wr_t):
    tm = TILE["merge"]
    nt = SEQ // tm
    row = lambda n: pl.BlockSpec((tm, n), lambda i: (i, 0))
    return pl.pallas_call(
        _merge_kernel,
        grid=(TOKENS // tm,),
        in_specs=[row(D_MODEL), _const_spec(g1.shape), row(D_MIX), row(D_MIX), row(D_MIX),
                  row(D_MIX), _const_spec(wg.shape), _const_spec(bg.shape),
                  _const_spec(wbr.shape), _const_spec(wout.shape), _const_spec(g2.shape),
                  _const_spec(wr_t.shape)],
        out_specs=[row(D_MODEL), row(D_MODEL),
                   pl.BlockSpec((None, N_EXPERTS, tm), lambda i: (i // nt, 0, i % nt))],
        out_shape=[jax.ShapeDtypeStruct((TOKENS, D_MODEL), F32),
                   jax.ShapeDtypeStruct((TOKENS, D_MODEL), BF16),
                   jax.ShapeDtypeStruct((BATCH, N_EXPERTS, SEQ), F32)],
        compiler_params=_params("parallel"),
        name="merge",
    )(x, g1, yhy, yfn, ymla, yna, wg, bg, wbr, wout, g2, wr_t)


def _prefix_count(m):
    r = lax.broadcasted_iota(jnp.int32, (LANES, LANES), 0)
    c = lax.broadcasted_iota(jnp.int32, (LANES, LANES), 1)
    upper = jnp.where(r < c, 1.0, 0.0).astype(BF16)
    run = jnp.zeros((m.shape[0], 1), F32)
    parts = []
    for i in range(SEQ // LANES):
        chunk = m[:, i * LANES:(i + 1) * LANES]
        parts.append(_dot(chunk.astype(BF16), upper) + run)
        run = run + jnp.sum(chunk, axis=1, keepdims=True)
    return jnp.concatenate(parts, axis=1)


SELECT_MAX_ITERS = 192


def _select_kernel(logit_ref, slot_row_ref, slot_col_ref, w_row_ref, start_ref, slot_s, w_s):
    b = pl.program_id(0)
    rows = BATCH * N_EXPERTS

    @pl.when(b == 0)
    def _():
        logits = logit_ref[...]
        ex = jnp.exp(logits - jnp.max(logits, axis=1, keepdims=True))
        aff = (ex / jnp.sum(ex, axis=1, keepdims=True)).reshape(rows, SEQ)

        def bisect(c):
            it, lo, hi, _ = c
            mid = 0.5 * (lo + hi)
            cnt = jnp.sum(jnp.where(aff >= mid, 1.0, 0.0), axis=1, keepdims=True)
            moving = jnp.where(mid == lo, 0.0, jnp.where(mid == hi, 0.0, 1.0))
            enough = cnt >= CAPACITY
            return (it + 1, jnp.where(enough, mid, lo), jnp.where(enough, hi, mid),
                    (jnp.max(moving) > 0).astype(jnp.int32))

        _, lo, hi, _ = lax.while_loop(
            lambda c: (c[0] < SELECT_MAX_ITERS) & (c[3] > 0), bisect,
            (jnp.int32(0), jnp.zeros((rows, 1), F32), jnp.full((rows, 1), 2.0, F32),
             jnp.int32(1)))
        above = jnp.where(aff >= hi, 1.0, 0.0)
        band = jnp.where(aff >= lo, 1.0, 0.0) - above
        need = CAPACITY - jnp.sum(above, axis=1, keepdims=True)
        sel = above + band * jnp.where(_prefix_count(band) < need, 1.0, 0.0)
        slot_s[...] = jnp.where(sel > 0, _prefix_count(sel), -1.0)
        w_s[...] = sel * aff

    r0 = pl.multiple_of(b * N_EXPERTS, N_EXPERTS)
    slot = slot_s[pl.ds(r0, N_EXPERTS), :]
    slot_row_ref[...] = slot.astype(jnp.int32)
    pad = jnp.full((LANES - N_EXPERTS, SEQ), -1.0, F32)
    slot_col_ref[...] = jnp.concatenate([slot, pad], axis=0).T.astype(jnp.int32)
    w_row_ref[...] = w_s[pl.ds(r0, N_EXPERTS), :]
    tm = MOE_FINE
    token = lax.broadcasted_iota(jnp.int32, slot.shape, 1)
    lane = lax.broadcasted_iota(jnp.int32, (N_EXPERTS, LANES), 1)
    starts = jnp.where(lane == SEQ // tm, float(CAPACITY), 0.0)
    for i in range(1, SEQ // tm):
        before = jnp.where(slot >= 0, jnp.where(token < i * tm, 1.0, 0.0), 0.0)
        starts = starts + jnp.where(lane == i, jnp.sum(before, axis=1, keepdims=True), 0.0)
    start_ref[...] = starts.astype(jnp.int32)


def _select(logits):
    return pl.pallas_call(
        _select_kernel,
        grid=(BATCH,),
        in_specs=[_const_spec(logits.shape)],
        out_specs=[pl.BlockSpec((None, N_EXPERTS, SEQ), lambda b: (b, 0, 0)),
                   pl.BlockSpec((None, SEQ, LANES), lambda b: (b, 0, 0)),
                   pl.BlockSpec((None, N_EXPERTS, SEQ), lambda b: (b, 0, 0)),
                   pl.BlockSpec((None, N_EXPERTS, LANES), lambda b: (b, 0, 0))],
        out_shape=[jax.ShapeDtypeStruct((BATCH, N_EXPERTS, SEQ), jnp.int32),
                   jax.ShapeDtypeStruct((BATCH, SEQ, LANES), jnp.int32),
                   jax.ShapeDtypeStruct((BATCH, N_EXPERTS, SEQ), F32),
                   jax.ShapeDtypeStruct((BATCH, N_EXPERTS, LANES), jnp.int32)],
        scratch_shapes=[pltpu.VMEM((BATCH * N_EXPERTS, SEQ), F32),
                        pltpu.VMEM((BATCH * N_EXPERTS, SEQ), F32)],
        compiler_params=_params("arbitrary"),
        name="expert_select",
    )(logits)


MXU_DEPTH = 256
BF16_ROWS = 16


def _slot_window(tokens):
    return 2 * tokens * CAPACITY // SEQ


def _tile_windows(start_ref, b, i, tokens):
    stride = start_ref.shape[0] // (BATCH * N_EXPERTS)
    fine = tokens // MOE_FINE
    window = _slot_window(tokens)
    base, fits = [], None
    for e in range(N_EXPERTS):
        at = (b * N_EXPERTS + e) * stride + i * fine
        lo = jnp.minimum(start_ref[at] // BF16_ROWS * BF16_ROWS, CAPACITY - window)
        ok = start_ref[at + fine] - lo <= window
        fits = ok if fits is None else fits & ok
        base.append(pl.multiple_of(lo, BF16_ROWS))
    return base, fits


def _gather_kernel(start_ref, slot_ref, w_ref, hb_ref, xe_ref, wsl_ref):
    b = pl.program_id(0)
    i = pl.program_id(1)
    tokens = slot_ref.shape[1]

    @pl.when(i == 0)
    def _():
        xe_ref[...] = jnp.zeros(xe_ref.shape, BF16)
        wsl_ref[...] = jnp.zeros(wsl_ref.shape, F32)

    def add_rows(e, rows, hit, picked):
        xe_ref[e, rows, :] = xe_ref[e, rows, :] + picked.astype(BF16)
        wslot = jnp.sum(jnp.where(hit, w_ref[e:e + 1, :], 0.0), axis=1, keepdims=True)
        wsl_ref[e, rows, :] = wsl_ref[e, rows, :] + jnp.broadcast_to(wslot, (hit.shape[0], LANES))

    window = _slot_window(tokens)
    base, fits = _tile_windows(start_ref, b, i, tokens)

    @pl.when(fits)
    def _():
        c = lax.broadcasted_iota(jnp.int32, (window, tokens), 0)
        hits = [slot_ref[e:e + 1, :] - base[e] == c for e in range(N_EXPERTS)]
        onehot = jnp.concatenate([jnp.where(m, 1.0, 0.0).astype(BF16) for m in hits], axis=0)
        picked = _dot(onehot, hb_ref[...])
        for e in range(N_EXPERTS):
            add_rows(e, pl.ds(base[e], window), hits[e], picked[e * window:(e + 1) * window])

    @pl.when(jnp.logical_not(fits))
    def _():
        c = lax.broadcasted_iota(jnp.int32, (CAPACITY, tokens), 0)
        for e in range(N_EXPERTS):
            hit = slot_ref[e:e + 1, :] == c
            picked = _dot(jnp.where(hit, 1.0, 0.0).astype(BF16), hb_ref[...])
            add_rows(e, slice(None), hit, picked)


def _gather(slot_row, w_row, starts, hb):
    tm = TILE["gather"]
    nt = SEQ // tm
    grid_spec = pltpu.PrefetchScalarGridSpec(
        num_scalar_prefetch=1,
        grid=(BATCH, nt),
        in_specs=[pl.BlockSpec((None, N_EXPERTS, tm), lambda b, i, s: (b, 0, i)),
                  pl.BlockSpec((None, N_EXPERTS, tm), lambda b, i, s: (b, 0, i)),
                  pl.BlockSpec((tm, D_MODEL), lambda b, i, s: (b * nt + i, 0))],
        out_specs=[pl.BlockSpec((N_EXPERTS, None, CAPACITY, D_MODEL), lambda b, i, s: (0, b, 0, 0)),
                   pl.BlockSpec((N_EXPERTS, None, CAPACITY, LANES), lambda b, i, s: (0, b, 0, 0))])
    return pl.pallas_call(
        _gather_kernel,
        grid_spec=grid_spec,
        out_shape=[jax.ShapeDtypeStruct((N_EXPERTS, BATCH, CAPACITY, D_MODEL), BF16),
                   jax.ShapeDtypeStruct((N_EXPERTS, BATCH, CAPACITY, LANES), F32)],
        compiler_params=_params("parallel", "arbitrary"),
        name="expert_gather",
    )(starts[:, :, :SEQ // MOE_FINE + 1].reshape(-1), slot_row, w_row, hb)


def _expert_kernel(xe_ref, wsl_ref, wg_ref, wu_ref, wd_ref, ye_ref, wg_s, wu_s, wd_s, *, span):
    ph = pl.program_id(0)
    i = pl.program_id(1)
    rows = pl.ds(pl.multiple_of(i * span, span), span)
    nxt = ph % 2
    wg_s[nxt, rows, :] = wg_ref[rows, :].astype(BF16)
    wu_s[nxt, rows, :] = wu_ref[rows, :].astype(BF16)
    wd_s[nxt, rows, :] = wd_ref[rows, :].astype(BF16)

    @pl.when(ph > 0)
    def _():
        cur = (ph + 1) % 2
        xe = xe_ref[...]
        g = _dot(xe, wg_s[cur])
        u = _dot(xe, wu_s[cur])
        act = (g * jax.nn.sigmoid(g) * u).astype(BF16)
        ye_ref[...] = (_dot(act, wd_s[cur]) * wsl_ref[:, 0:1]).astype(BF16)


def _experts(xe, wsl, wg, wu, wd, layer):
    rows = BATCH * CAPACITY
    tm = TILE["expert"]
    nt = rows // tm
    last = N_EXPERTS - 1
    wspec = lambda a: pl.BlockSpec((None, None) + a.shape[2:],
                                   lambda ph, i: (layer, jnp.minimum(ph, last), 0, 0))
    data = lambda n: pl.BlockSpec(
        (None, tm, n), lambda ph, i: (jnp.maximum(ph - 1, 0), jnp.where(ph > 0, i, 0), 0))
    return pl.pallas_call(
        functools.partial(_expert_kernel, span=D_MODEL // nt),
        grid=(N_EXPERTS + 1, nt),
        in_specs=[data(D_MODEL), data(LANES), wspec(wg), wspec(wu), wspec(wd)],
        out_specs=data(D_MODEL),
        out_shape=jax.ShapeDtypeStruct((N_EXPERTS, rows, D_MODEL), BF16),
        scratch_shapes=[pltpu.VMEM((2, D_MODEL, D_FF), BF16), pltpu.VMEM((2, D_MODEL, D_FF), BF16),
                        pltpu.VMEM((2, D_FF, D_MODEL), BF16)],
        compiler_params=_params("arbitrary", "arbitrary"),
        name="expert_ffn",
    )(xe.reshape(N_EXPERTS, rows, D_MODEL), wsl.reshape(N_EXPERTS, rows, LANES), wg, wu, wd)


def _combine_kernel(start_ref, x_ref, ye_ref, slot_ref, p_ref, g3_ref, wpg_ref, wpp_ref,
                    gf_ref, out_ref, moe_s, *, final_norm):
    slot = slot_ref[...]
    rows = slot.shape[0]
    window = _slot_window(rows)
    base, fits = _tile_windows(start_ref, pl.program_id(0), pl.program_id(1), rows)

    @pl.when(fits)
    def _():
        c = lax.broadcasted_iota(jnp.int32, (rows, window), 1)
        parts = []
        group = MXU_DEPTH // window
        for e0 in range(0, N_EXPERTS, group):
            hot, win = [], []
            for e in range(e0, e0 + group):
                hot.append(jnp.where(slot[:, e:e + 1] - base[e] == c, 1.0, 0.0).astype(BF16))
                win.append(ye_ref[e, pl.ds(base[e], window), :])
            parts.append(_dot(jnp.concatenate(hot, axis=1), jnp.concatenate(win, axis=0)))
        while len(parts) > 1:
            parts = [u + v for u, v in zip(parts[::2], parts[1::2])]
        moe_s[...] = parts[0]

    @pl.when(jnp.logical_not(fits))
    def _():
        c = lax.broadcasted_iota(jnp.int32, (rows, CAPACITY), 1)
        onehot = jnp.concatenate(
            [jnp.where(slot[:, e:e + 1] == c, 1.0, 0.0).astype(BF16) for e in range(N_EXPERTS)],
            axis=1)
        moe_s[...] = _dot(onehot, ye_ref[...].reshape(N_EXPERTS * CAPACITY, D_MODEL))

    acc = x_ref[...] + moe_s[...]
    h = _rms(acc, g3_ref[...]).astype(BF16)
    gate = jax.nn.sigmoid(_dot(h, wpg_ref[...]))
    y = acc + gate * _dot(p_ref[...].astype(BF16), wpp_ref[...])
    if final_norm:
        y = _rms(y, gf_ref[...])
    out_ref[...] = y


def _combine(x, ye, slot_col, starts, p, layer, g3, wpg, wpp, gf, final_norm):
    tm = TILE["combine"]
    nt = SEQ // tm
    p0 = layer * (TOKENS // tm)
    const = lambda a: pl.BlockSpec(a.shape, lambda b, i, s: (0,) * a.ndim,
                                   pipeline_mode=pl.Buffered(1))
    grid_spec = pltpu.PrefetchScalarGridSpec(
        num_scalar_prefetch=1,
        grid=(BATCH, nt),
        in_specs=[pl.BlockSpec((tm, D_MODEL), lambda b, i, s: (b * nt + i, 0)),
                  pl.BlockSpec((N_EXPERTS, None, CAPACITY, D_MODEL), lambda b, i, s: (0, b, 0, 0)),
                  pl.BlockSpec((None, tm, LANES), lambda b, i, s: (b, i, 0)),
                  pl.BlockSpec((tm, PLE_DIM), lambda b, i, s: (p0 + b * nt + i, 0)),
                  const(g3), const(wpg), const(wpp), const(gf)],
        out_specs=pl.BlockSpec((tm, D_MODEL), lambda b, i, s: (b * nt + i, 0)),
        scratch_shapes=[pltpu.VMEM((tm, D_MODEL), F32)])
    return pl.pallas_call(
        functools.partial(_combine_kernel, final_norm=final_norm),
        grid_spec=grid_spec,
        out_shape=jax.ShapeDtypeStruct((TOKENS, D_MODEL), F32),
        compiler_params=_params("parallel", "parallel"),
        name="combine",
    )(starts[:, :, :SEQ // MOE_FINE + 1].reshape(-1), x,
      ye.reshape(N_EXPERTS, BATCH, CAPACITY, D_MODEL),
      slot_col, p, g3, wpg, wpp, gf)


DFT_FINE = 64
DFT_STEP = 4


def _dft_kernel(ca_ref, sa_ref, cb_ref, sb_ref, out_ref):
    cb = cb_ref[...]
    sb = sb_ref[...]
    for r in range(DFT_STEP):
        rows = slice(r * DFT_FINE, (r + 1) * DFT_FINE)
        out_ref[rows, :] = (ca_ref[r] * cb - sa_ref[r] * sb).astype(BF16)


def _dft_tables(n_points, nres):
    blk = SEQ // nres
    t = jnp.arange(SEQ, dtype=jnp.int32).reshape(blk, nres).T
    t = jnp.concatenate([t, t], axis=1).reshape(1, 2 * SEQ)
    is_sin = (jnp.arange(2 * SEQ) // blk % 2 == 1)[None, :]
    coarse = jnp.arange(HALF // DFT_FINE, dtype=jnp.int32)[:, None] * DFT_FINE
    fine = jnp.arange(DFT_FINE, dtype=jnp.int32)[:, None]
    ang = lambda f: ((f * t) % n_points).astype(F32) * (2.0 * math.pi / n_points)
    ca = jnp.cos(ang(coarse))[:, None, :]
    sa = jnp.sin(ang(coarse))[:, None, :]
    cb = jnp.where(is_sin, jnp.sin(ang(fine)), jnp.cos(ang(fine)))
    sb = jnp.where(is_sin, -jnp.cos(ang(fine)), jnp.sin(ang(fine)))
    rows = DFT_STEP * DFT_FINE
    return pl.pallas_call(
        _dft_kernel,
        grid=(HALF // rows,),
        in_specs=[pl.BlockSpec((DFT_STEP, 1, 2 * SEQ), lambda i: (i, 0, 0)),
                  pl.BlockSpec((DFT_STEP, 1, 2 * SEQ), lambda i: (i, 0, 0)),
                  _const_spec(cb.shape), _const_spec(sb.shape)],
        out_specs=pl.BlockSpec((rows, 2 * SEQ), lambda i: (i, 0)),
        out_shape=jax.ShapeDtypeStruct((HALF, 2 * SEQ), BF16),
        compiler_params=_params("parallel"),
        name="dft_tables",
    )(ca, sa, cb, sb)


def _fnet_group_tables():
    gc = D_MIX // FN_GROUPS
    i = lax.broadcasted_iota(jnp.int32, (D_MIX, D_MIX), 0)
    j = lax.broadcasted_iota(jnp.int32, (D_MIX, D_MIX), 1)
    same = (i // gc) == (j // gc)
    ang = (((i % gc) * (j % gc)) % gc).astype(F32) * (2.0 * math.pi / gc)
    return (jnp.where(same, jnp.cos(ang), 0.0).astype(BF16),
            jnp.where(same, jnp.sin(ang), 0.0).astype(BF16))


def _hyena_features():
    t01 = jnp.linspace(0.0, 1.0, SEQ, dtype=F32)[:, None]
    bands = jnp.linspace(1e-4, HY_BANDS - 1, HY_BANDS, dtype=F32)
    ang = 2.0 * math.pi * jnp.arange(SEQ, dtype=F32)[:, None] * bands / SEQ
    z = jnp.concatenate([t01, jnp.cos(ang), -jnp.sin(ang)], axis=-1)
    z = jnp.pad(z, ((0, 0), (0, LANES - HY_EMB)))
    max_decay = math.log(HY_TARGET) / HY_FAST_DECAY
    min_decay = math.log(HY_TARGET) / HY_SLOW_DECAY
    deltas = jnp.linspace(min_decay, max_decay, D_MIX, dtype=F32)
    window = jnp.exp(-t01 * jnp.abs(deltas))
    return z, window


def _rot_cols(w):
    half = w.shape[-1] // 2
    return jnp.concatenate([-w[..., half:], w[..., :half]], axis=-1)


def _rope_tables():
    inv = ROPE_THETA ** (-jnp.arange(0, MLA_ROPE, 2, dtype=F32) / MLA_ROPE)
    ang = jnp.arange(SEQ, dtype=F32)[:, None] * inv
    cos = jnp.concatenate([jnp.cos(ang), jnp.cos(ang)], axis=-1)
    sin = jnp.concatenate([jnp.sin(ang), jnp.sin(ang)], axis=-1)
    scale = (MLA_NOPE + MLA_ROPE) ** -0.5 * LOG2E
    pad = MLA_HEAD_PAD - MLA_NOPE - MLA_ROPE
    one = jnp.ones((SEQ, MLA_NOPE), F32)
    zero = jnp.zeros((SEQ, MLA_NOPE), F32)
    zpad = jnp.zeros((SEQ, pad), F32)
    cosq = jnp.tile(jnp.concatenate([one, cos, zpad], axis=-1) * scale, (1, MLA_HEADS))
    sinq = jnp.tile(jnp.concatenate([zero, sin, zpad], axis=-1) * scale, (1, MLA_HEADS))
    csk = jnp.concatenate([cos, sin, jnp.zeros((SEQ, LANES - 2 * MLA_ROPE), F32)], axis=-1)
    return cosq, sinq, csk


def _mla_weights(w_uq, w_ukv):
    pad = MLA_HEAD_PAD - MLA_NOPE - MLA_ROPE
    wq = w_uq.reshape(MLA_Q_RANK, MLA_HEADS, MLA_NOPE + MLA_ROPE)
    nope, pe = wq[..., :MLA_NOPE], wq[..., MLA_NOPE:]
    zp = jnp.zeros((MLA_Q_RANK, MLA_HEADS, pad), F32)
    wqa = jnp.concatenate([nope, pe, zp], axis=-1).reshape(MLA_Q_RANK, MLA_QK)
    wqb = jnp.concatenate([jnp.zeros_like(nope), _rot_cols(pe), zp], axis=-1)
    wqb = wqb.reshape(MLA_Q_RANK, MLA_QK)
    wkv = w_ukv.reshape(MLA_KV_RANK, MLA_HEADS, MLA_NOPE + MLA_V)
    knope, v = wkv[..., :MLA_NOPE], wkv[..., MLA_NOPE:]
    wk = jnp.concatenate(
        [knope, jnp.zeros((MLA_KV_RANK, MLA_HEADS, MLA_HEAD_PAD - MLA_NOPE), F32)], axis=-1)
    wk = wk.reshape(MLA_KV_RANK, MLA_QK)
    wv = v.reshape(MLA_KV_RANK, MLA_HEADS * MLA_V).T
    r = lax.broadcasted_iota(jnp.int32, (LANES, MLA_QK), 0)
    c = lax.broadcasted_iota(jnp.int32, (LANES, MLA_QK), 1)
    epe = jnp.where((r < 2 * MLA_ROPE) & (c % MLA_HEAD_PAD == MLA_NOPE + r % MLA_ROPE), 1.0, 0.0)
    return (wqa.astype(BF16), wqb.astype(BF16), wk.astype(BF16), wv.astype(BF16),
            epe.astype(BF16))


def kernel(x, p, norm1_g, w_in, b_gate, hy_conv_w, hy_conv_b, hf_w1, hf_b1, hf_freq, hf_w2,
           hf_b2, hf_w3, hy_skip, q_norm_g, w_uq, kv_norm_g, w_ukv, rpb, w_br, w_out, norm2_g,
           w_router, w_e_gate, w_e_up, w_e_down, norm3_g, w_ple_gate, w_ple_proj, final_g):
    conv_tab = _dft_tables(2 * SEQ, 4)
    fnet_tab = _dft_tables(SEQ, 2)
    fnet_cg, fnet_sg = _fnet_group_tables()
    zfeat, window = _hyena_features()
    cosq, sinq, csk = _rope_tables()
    row = lambda a: a.reshape(1, -1)

    xt = x.reshape(TOKENS, D_MODEL)
    w_in_t = jnp.swapaxes(w_in, 1, 2)
    for i in range(DEPTH):
        g1 = row(norm1_g[i])
        wa, wna, wgate = _inproj_weights(w_in_t, i)
        wqa, wqb, wk, wv, epe = _mla_weights(w_uq[i], w_ukv[i])

        u_hy, u_fn, q, k, v, naq, nak, nav = _inproj(
            xt, g1, wa, wna, row(q_norm_g[i]), wqa, wqb, row(kv_norm_g[i]), wk, wv, epe,
            cosq, sinq, csk)

        w1 = jnp.pad(hf_w1[i], ((0, LANES - HY_EMB), (0, 0)))
        kf, kny = _hyena_filter(zfeat, window, w1, row(hf_b1[i]), hf_freq[i], hf_w2[i],
                                row(hf_b2[i]), hf_w3[i], conv_tab)
        conv_b = row(hy_conv_b[i])
        z1 = _hyena_stage(u_hy, 0, u_hy, 2, hy_conv_w[i], conv_b, conv_tab, kf, kny, 0,
                          row(hy_skip[i, 0]), True, F32)
        y_hy = _hyena_stage(u_hy, 1, z1, 0, hy_conv_w[i], conv_b, conv_tab, kf, kny, 1,
                            row(hy_skip[i, 1]), False, BF16)
        y_fn = _fnet(u_fn, fnet_tab, fnet_cg, fnet_sg)
        y_mla = _mla(q, k, v)
        y_na = _neighborhood(naq, nak, nav, _na_bias_tiles(rpb[i]))
        wr_t = jnp.pad(w_router[i].T, ((0, LANES - N_EXPERTS), (0, 0)))
        xt, hb, logits = _merge(xt, g1, y_hy, y_fn, y_mla, y_na, wgate, row(b_gate[i]),
                                w_br[i].astype(BF16), w_out[i].astype(BF16),
                                row(norm2_g[i]), wr_t)
        slot_row, slot_col, w_row, starts = _select(logits)
        xe, wsl = _gather(slot_row, w_row, starts, hb)
        ye = _experts(xe, wsl, w_e_gate, w_e_up, w_e_down, i)
        xt = _combine(xt, ye, slot_col, starts, p.reshape(DEPTH * TOKENS, PLE_DIM), i,
                      row(norm3_g[i]), w_ple_gate[i].astype(BF16),
                      w_ple_proj[i].astype(BF16), row(final_g), i == DEPTH - 1)
    return xt.reshape(BATCH, SEQ, D_MODEL)
```

```python
import functools
import math

import jax
import jax.numpy as jnp
from jax import lax
from jax.experimental import pallas as pl
from jax.experimental.pallas import tpu as pltpu

F32 = jnp.float32
BF16 = jnp.bfloat16

D_MODEL = 1024
BATCH = 8
SEQ = 2048
DEPTH = 2
TOKENS = BATCH * SEQ

GRID_W = 64
GRID_R = SEQ // GRID_W
D_MIX = 256
N_BRANCH = 4
EPS = 1e-6
HY_ORDER = 2
HY_BANDS = 16
HY_EMB = 2 * HY_BANDS + 1
HY_FFN = 64
HY_TARGET = 1e-2
HY_FAST_DECAY = 0.3
HY_SLOW_DECAY = 1.5
FN_GROUPS = 4
MLA_HEADS = 4
MLA_NOPE = 64
MLA_ROPE = 32
MLA_V = 64
MLA_Q_RANK = 256
MLA_KV_RANK = 128
ROPE_THETA = 10000.0
NA_HEADS = 4
NA_HEAD_DIM = D_MIX // NA_HEADS
NA_WIN_R = 8
NA_WIN_C = 16
N_EXPERTS = 16
CAPACITY = 2 * SEQ // N_EXPERTS
D_FF = 1024
PLE_DIM = 256

HY_COLS = 3 * D_MIX
OFF_FN = HY_COLS
OFF_CQ = OFF_FN + D_MIX
OFF_CKV = OFF_CQ + MLA_Q_RANK
OFF_KPE = OFF_CKV + MLA_KV_RANK
OFF_NA = OFF_KPE + MLA_ROPE
OFF_GATE = OFF_NA + 3 * D_MIX

LANES = 128
MLA_HEAD_PAD = 128
MLA_QK = MLA_HEADS * MLA_HEAD_PAD
WA_COLS = 1536
NEG_BIG = -1e30
LOG2E = math.log2(math.e)

TILE = dict(inproj=512, mla=1024, merge=512, gather=256, expert=512, combine=512)
MOE_FINE = 256
NA_QROWS = 8
NA_KROWS = 16
NA_QBLK = NA_QROWS * GRID_W
NA_KBLK = NA_KROWS * GRID_W
VMEM_LIMIT = 56 * 1024 * 1024


def _params(*sem):
    return pltpu.CompilerParams(dimension_semantics=sem, vmem_limit_bytes=VMEM_LIMIT)


def _const_spec(shape):
    nd = len(shape)
    return pl.BlockSpec(shape, lambda *_: (0,) * nd, pipeline_mode=pl.Buffered(1))


def _rms(x, g):
    return x * lax.rsqrt(jnp.mean(x * x, axis=-1, keepdims=True) + EPS) * g


def _dot(a, b):
    return jnp.dot(a, b, preferred_element_type=F32)


def _dot_nt(a, b):
    return lax.dot_general(a, b, (((1,), (1,)), ((), ())), preferred_element_type=F32)


def _split2(x):
    hi = x.astype(BF16)
    lo = (x - hi.astype(F32)).astype(BF16)
    return hi, lo


def _inproj_kernel(x_ref, g1_ref, wa_ref, wna_ref, qg_ref, wqa_ref, wqb_ref, kvg_ref,
                   wk_ref, wv_ref, epe_ref, cosq_ref, sinq_ref, csk_ref,
                   uhy_ref, ufn_ref, q_ref, k_ref, v_ref, naq_ref, nak_ref, nav_ref):
    h = _rms(x_ref[...], g1_ref[...]).astype(BF16)
    ua = _dot_nt(h, wa_ref[...])
    uhy_ref[...] = ua[:, :HY_COLS]
    ufn_ref[...] = ua[:, OFF_FN:OFF_CQ]
    cqn = _rms(ua[:, OFF_CQ:OFF_CKV], qg_ref[...]).astype(BF16)
    q = _dot(cqn, wqa_ref[...]) * cosq_ref[...] + _dot(cqn, wqb_ref[...]) * sinq_ref[...]
    q_ref[...] = q.astype(BF16)
    kvn = _rms(ua[:, OFF_CKV:OFF_KPE], kvg_ref[...]).astype(BF16)
    kpe = ua[:, OFF_KPE:WA_COLS] * csk_ref[...]
    k = _dot(kvn, wk_ref[...]) + _dot(kpe.astype(BF16), epe_ref[...])
    k_ref[...] = k.astype(BF16)
    v_ref[...] = _dot_nt(wv_ref[...], kvn).astype(BF16)
    una = _dot_nt(h, wna_ref[...])
    naq_ref[...] = (una[:, :D_MIX] * (NA_HEAD_DIM ** -0.5 * LOG2E)).astype(BF16)
    nak_ref[...] = una[:, D_MIX:2 * D_MIX].astype(BF16)
    nav_ref[...] = una[:, 2 * D_MIX:].astype(BF16)


def _inproj(x, g1, wa, wna, qg, wqa, wqb, kvg, wk, wv, epe, cosq, sinq, csk):
    tm = TILE["inproj"]
    nt = SEQ // tm
    row = lambda n: pl.BlockSpec((tm, n), lambda i: (i, 0))
    pos = lambda n: pl.BlockSpec((tm, n), lambda i: (i % nt, 0))
    outs = [(HY_COLS, F32), (D_MIX, F32), (MLA_QK, BF16), (MLA_QK, BF16), None,
            (D_MIX, BF16), (D_MIX, BF16), (D_MIX, BF16)]
    vt_spec = pl.BlockSpec((None, D_MIX, tm), lambda i: (i // nt, 0, i % nt))
    vt_shape = jax.ShapeDtypeStruct((BATCH, D_MIX, SEQ), BF16)
    return pl.pallas_call(
        _inproj_kernel,
        grid=(TOKENS // tm,),
        in_specs=[row(D_MODEL), _const_spec(g1.shape), _const_spec(wa.shape),
                  _const_spec(wna.shape), _const_spec(qg.shape), _const_spec(wqa.shape),
                  _const_spec(wqb.shape), _const_spec(kvg.shape), _const_spec(wk.shape),
                  _const_spec(wv.shape), _const_spec(epe.shape),
                  pos(MLA_QK), pos(MLA_QK), pos(LANES)],
        out_specs=[vt_spec if o is None else row(o[0]) for o in outs],
        out_shape=[vt_shape if o is None else jax.ShapeDtypeStruct((TOKENS, o[0]), o[1])
                   for o in outs],
        compiler_params=_params("parallel"),
        name="inproj",
    )(x, g1, wa, wna, qg, wqa, wqb, kvg, wk, wv, epe, cosq, sinq, csk)


WPREP_ROWS = 512


def _inproj_weights_kernel(w_ref, wa_ref, wna_ref, wg_ref):
    i = pl.program_id(0)

    @pl.when(i == 0)
    def _():
        half = MLA_ROPE // 2
        wa_ref[:OFF_NA, :] = w_ref[:OFF_NA, :].astype(BF16)
        wa_ref[OFF_NA:OFF_NA + half, :] = (-w_ref[OFF_KPE + half:OFF_NA, :]).astype(BF16)
        wa_ref[OFF_NA + half:OFF_NA + MLA_ROPE, :] = w_ref[OFF_KPE:OFF_KPE + half, :].astype(BF16)
        wa_ref[OFF_NA + MLA_ROPE:, :] = jnp.zeros((WA_COLS - OFF_NA - MLA_ROPE, D_MODEL), BF16)
        wna_ref[...] = w_ref[OFF_NA:OFF_GATE, :].astype(BF16)

    start = pl.multiple_of(OFF_GATE + i * WPREP_ROWS, MLA_ROPE)
    wg_ref[...] = w_ref[pl.ds(start, WPREP_ROWS), :].astype(BF16)


def _inproj_weights(w_in_t, layer):
    n_in = w_in_t.shape[1]
    rows = (WA_COLS, OFF_GATE - OFF_NA, n_in - OFF_GATE)
    whole = lambda n: pl.BlockSpec((n, D_MODEL), lambda i: (0, 0))
    return pl.pallas_call(
        _inproj_weights_kernel,
        grid=(rows[2] // WPREP_ROWS,),
        in_specs=[pl.BlockSpec((None, n_in, D_MODEL), lambda i: (layer, 0, 0),
                               pipeline_mode=pl.Buffered(1))],
        out_specs=[whole(rows[0]), whole(rows[1]),
                   pl.BlockSpec((WPREP_ROWS, D_MODEL), lambda i: (i, 0))],
        out_shape=[jax.ShapeDtypeStruct((n, D_MODEL), BF16) for n in rows],
        compiler_params=_params("arbitrary"),
        name="inproj_weights",
    )(w_in_t)


HALF = SEQ // 2
HY_SEQS = 1


def _residue_rows(ref, r, nres):
    rows = pl.ds(r, SEQ // nres, stride=nres)
    if isinstance(ref, tuple):
        return jnp.concatenate([h[rows, :] for h in ref], axis=1)
    return jnp.concatenate([ref[j, rows, :] for j in range(ref.shape[0])], axis=1)


def _store_slabs(ref, value, rows=slice(None)):
    for j in range(ref.shape[0]):
        ref[j, rows, :] = value[:, j * LANES:(j + 1) * LANES]


def _table_cols(cs_ref, r, nres, part=None):
    blk = SEQ // nres
    lo = 2 * blk * r
    if part is None:
        return cs_ref[:, lo:lo + 2 * blk]
    return cs_ref[:, lo + part * blk:lo + (part + 1) * blk]


def _dft_fwd(cs_ref, src_ref, want_cos=True, want_sin=True, split=False):
    def prod(part, z, r):
        tab = _table_cols(cs_ref, r, 4, part)
        if split:
            hi, lo = _split2(z)
            return _dot(tab, hi) + _dot(tab, lo)
        return _dot(tab, z.astype(BF16))

    pc, ps = [None] * 4, [None] * 4
    for r in range(4):
        z = _residue_rows(src_ref, r, 4)
        odd = r % 2 == 1
        if want_cos or odd:
            pc[r] = prod(0, z, r)
        if want_sin or odd:
            ps[r] = prod(1, z, r)
    a = b = None
    if want_cos:
        a = ((pc[0] + pc[2]) + (pc[1] + pc[3]), (pc[0] - pc[2]) + (ps[3] - ps[1]))
    if want_sin:
        b = ((ps[0] + ps[2]) + (ps[1] + ps[3]), (ps[0] - ps[2]) + (pc[1] - pc[3]))
    return a, b


def _dft_inv(cs_ref, yre_ref, yim_ref):
    g, h = [], {}
    for r in range(4):
        yr = _residue_rows(yre_ref, r, 4).astype(BF16)
        yi = _residue_rows(yim_ref, r, 4).astype(BF16)
        tab = _table_cols(cs_ref, r, 4)
        g.append(_dot(tab, jnp.concatenate([yr, -yi], axis=0)))
        if r % 2 == 1:
            h[r] = _dot(tab, jnp.concatenate([yi, yr], axis=0))
    return (g[0] + g[2]) + (g[1] + g[3]), (g[0] - g[2]) + (h[3] - h[1])


def _hyena_filter_kernel(z_ref, win_ref, w1_ref, b1_ref, freq_ref, w2_ref, b2_ref, w3_ref,
                         cs_ref, kf_ref, kny_ref, ksum_s, kdif_s, trunk_s):
    hp = lax.Precision.HIGHEST

    @pl.when(pl.program_id(0) == 0)
    def _():
        freq = freq_ref[...]
        hf = jnp.sin(freq[0:1] * (jnp.dot(z_ref[...], w1_ref[...], precision=hp,
                                          preferred_element_type=F32) + b1_ref[...]))
        trunk_s[...] = jnp.sin(freq[1:2] * (jnp.dot(hf, w2_ref[...], precision=hp,
                                                    preferred_element_type=F32) + b2_ref[...]))

    hf = jnp.dot(trunk_s[...], w3_ref[...], precision=hp, preferred_element_type=F32)
    win = win_ref[...]
    t = lax.broadcasted_iota(jnp.int32, (SEQ, D_MIX), 0)
    sgn = (1 - 2 * (t & 1)).astype(F32)
    fwd = hf[:, :D_MIX] * win
    bwd = jnp.where(t == 0, 0.0, hf[:, D_MIX:] * win)
    nrm = lax.rsqrt(jnp.sum(fwd * fwd + bwd * bwd, axis=0, keepdims=True) + EPS)
    ksum = (fwd + bwd) * nrm
    _store_slabs(ksum_s, ksum)
    _store_slabs(kdif_s, (bwd - fwd) * nrm)
    kre, _ = _dft_fwd(cs_ref, ksum_s, want_sin=False, split=True)
    _, kim = _dft_fwd(cs_ref, kdif_s, want_cos=False, split=True)
    wf = 2.0 / (2 * SEQ)
    for part, spec in enumerate((kre, kim)):
        lo = spec[0] * wf
        kf_ref[part, :HALF, :] = lo
        kf_ref[part, 0:1, :] = lo[0:1] * 0.5
        kf_ref[part, HALF:, :] = spec[1] * wf
    kny = jnp.sum(ksum * sgn, axis=0, keepdims=True) * (1.0 / (2 * SEQ))
    kny_ref[...] = jnp.broadcast_to(kny, (8, D_MIX))


def _hyena_filter(zfeat, window, w1, b1, freq, w2, b2, w3, table):
    consts = (zfeat, window, w1, b1, freq, w2, b2)
    return pl.pallas_call(
        _hyena_filter_kernel,
        grid=(HY_ORDER,),
        in_specs=[_const_spec(a.shape) for a in consts]
        + [pl.BlockSpec((HY_FFN, 2 * D_MIX), lambda o: (0, o)),
           _const_spec(table.shape)],
        out_specs=[pl.BlockSpec((None, 2, SEQ, D_MIX), lambda o: (o, 0, 0, 0)),
                   pl.BlockSpec((None, 8, D_MIX), lambda o: (o, 0, 0))],
        out_shape=[jax.ShapeDtypeStruct((HY_ORDER, 2, SEQ, D_MIX), F32),
                   jax.ShapeDtypeStruct((HY_ORDER, 8, D_MIX), F32)],
        scratch_shapes=[pltpu.VMEM((D_MIX // LANES, SEQ, LANES), F32)] * 2
        + [pltpu.VMEM((SEQ, HY_FFN), F32)],
        compiler_params=_params("arbitrary"),
        name="hyena_filter",
    )(*consts, w3, table)


def _short_conv(u, w, b):
    t = lax.broadcasted_iota(jnp.int32, u.shape, 0)
    prev = jnp.where(t == 0, 0.0, pltpu.roll(u, 1, 0))
    nxt = jnp.where(t == SEQ - 1, 0.0, pltpu.roll(u, SEQ - 1, 0))
    return prev * w[0:1] + u * w[1:2] + nxt * w[2:3] + b


def _hyena_stage_kernel(gate_ref, src_ref, wg_ref, bg_ref, ws_ref, bs_ref, cs_ref,
                        kf_ref, kny_ref, skip_ref, out_ref, z_s, yre_s, yim_s, gate_s, rest_s,
                        *, conv_src):
    for q in range(HY_SEQS):
        seq = slice(q * SEQ, (q + 1) * SEQ)
        zq, yre_q, yim_q = z_s.at[q], yre_s.at[q], yim_s.at[q]
        z = src_ref[seq, :]
        if conv_src:
            z = _short_conv(z, ws_ref[...], bs_ref[...])
        _store_slabs(zq, z)
        t = lax.broadcasted_iota(jnp.int32, z.shape, 0)
        sgn = (1 - 2 * (t & 1)).astype(F32)
        nyq = jnp.sum(z * sgn, axis=0, keepdims=True) * kny_ref[0:1]
        gate = _short_conv(gate_ref[seq, :], wg_ref[...], bg_ref[...])
        gate_s[seq, :] = gate
        rest_s[seq, :] = gate * (sgn * nyq + z * skip_ref[...])
        a, b = _dft_fwd(cs_ref, zq)
        for half in range(2):
            rows = slice(half * HALF, (half + 1) * HALF)
            kre = kf_ref[0, rows, :]
            kim = kf_ref[1, rows, :]
            _store_slabs(yre_q, a[half] * kre + b[half] * kim, rows)
            _store_slabs(yim_q, a[half] * kim - b[half] * kre, rows)
        for half, y in enumerate(_dft_inv(cs_ref, yre_q, yim_q)):
            rows = slice(q * SEQ + half * HALF, q * SEQ + (half + 1) * HALF)
            out_ref[rows, :] = (gate_s[rows, :] * y + rest_s[rows, :]).astype(out_ref.dtype)


def _hyena_stage(u_hy, gate_blk, src, src_blk, conv_w, conv_b, table, kf, kny, order, skip,
                 conv_src, out_dtype):
    rows = HY_SEQS * SEQ
    of_order = lambda a: pl.BlockSpec(
        (None,) + a.shape[1:], lambda b: (order,) + (0,) * (a.ndim - 1),
        pipeline_mode=pl.Buffered(1))
    col = lambda blk: pl.BlockSpec((rows, D_MIX), lambda b: (b, blk))
    wcol = lambda blk, r: pl.BlockSpec((r, D_MIX), lambda b: (0, blk))
    ws_blk = src_blk if conv_src else 0
    return pl.pallas_call(
        functools.partial(_hyena_stage_kernel, conv_src=conv_src),
        grid=(BATCH // HY_SEQS,),
        in_specs=[col(gate_blk), col(src_blk), wcol(gate_blk, 3), wcol(gate_blk, 1),
                  wcol(ws_blk, 3), wcol(ws_blk, 1), _const_spec(table.shape),
                  of_order(kf), of_order(kny), _const_spec(skip.shape)],
        out_specs=pl.BlockSpec((rows, D_MIX), lambda b: (b, 0)),
        out_shape=jax.ShapeDtypeStruct((TOKENS, D_MIX), out_dtype),
        scratch_shapes=[pltpu.VMEM((HY_SEQS, D_MIX // LANES, SEQ, LANES), F32)] * 3
        + [pltpu.VMEM((rows, D_MIX), F32)] * 2,
        compiler_params=_params("parallel"),
        name="hyena_stage",
    )(u_hy, src, conv_w, conv_b, conv_w, conv_b, table, kf, kny, skip)


def _fnet_kernel(xa_ref, xb_ref, cs_ref, cg_ref, sg_ref, out_ref):
    parts = []
    for r in range(2):
        xb = _residue_rows((xa_ref, xb_ref), r, 2).astype(BF16)
        xc = _dot(xb, cg_ref[...]).astype(BF16)
        xs = _dot(xb, sg_ref[...]).astype(BF16)
        parts.append(_dot(_table_cols(cs_ref, r, 2), jnp.concatenate([xc, -xs], axis=0)))
    scale = (SEQ * D_MIX // FN_GROUPS) ** -0.5
    out_ref[:HALF, :] = ((parts[0] + parts[1]) * scale).astype(out_ref.dtype)
    out_ref[HALF:, :] = ((parts[0] - parts[1]) * scale).astype(out_ref.dtype)


def _fnet(u_fn, table, cg, sg):
    return pl.pallas_call(
        _fnet_kernel,
        grid=(BATCH,),
        in_specs=[pl.BlockSpec((SEQ, LANES), lambda b: (b, 0)),
                  pl.BlockSpec((SEQ, LANES), lambda b: (b, 1)), _const_spec(table.shape),
                  _const_spec(cg.shape), _const_spec(sg.shape)],
        out_specs=pl.BlockSpec((SEQ, D_MIX), lambda b: (b, 0)),
        out_shape=jax.ShapeDtypeStruct((TOKENS, D_MIX), BF16),
        compiler_params=_params("parallel"),
        name="fnet",
    )(u_fn, u_fn, table, cg, sg)


def _softmax2_pv(s2, v):
    m = jnp.max(s2, axis=-1, keepdims=True)
    p = jnp.exp2(s2 - m)
    l = jnp.sum(p, axis=-1, keepdims=True)
    return _dot(p.astype(BF16), v) / l


def _mla_kernel(q_ref, k_ref, vt_ref, out_ref):
    outs = []
    for h in range(MLA_HEADS):
        sl = slice(h * MLA_HEAD_PAD, (h + 1) * MLA_HEAD_PAD)
        s2 = _dot_nt(k_ref[:, sl], q_ref[:, sl])
        p = jnp.exp2(s2 - jnp.max(s2, axis=0, keepdims=True))
        l = jnp.sum(p, axis=0, keepdims=True)
        o = _dot(vt_ref[h * MLA_V:(h + 1) * MLA_V, :], p.astype(BF16))
        outs.append(o / l)
    out_ref[...] = jnp.concatenate(outs, axis=0).T.astype(out_ref.dtype)


def _mla(q, k, vt):
    tm = TILE["mla"]
    nt = SEQ // tm
    return pl.pallas_call(
        _mla_kernel,
        grid=(BATCH, nt),
        in_specs=[pl.BlockSpec((tm, MLA_QK), lambda b, i: (b * nt + i, 0)),
                  pl.BlockSpec((SEQ, MLA_QK), lambda b, i: (b, 0)),
                  pl.BlockSpec((None, D_MIX, SEQ), lambda b, i: (b, 0, 0))],
        out_specs=pl.BlockSpec((tm, D_MIX), lambda b, i: (b * nt + i, 0)),
        out_shape=jax.ShapeDtypeStruct((TOKENS, D_MIX), BF16),
        compiler_params=_params("parallel", "parallel"),
        name="mla_attention",
    )(q, k, vt)


def _na_key_row0(j):
    return jnp.clip(j * NA_QROWS - NA_WIN_R // 2, 0, GRID_R - NA_KROWS)


NA_PAIRS = 2 * NA_WIN_R


def _na_kernel(q_ref, k_ref, v_ref, tile_ref, out_ref, bias_s):
    j = pl.program_id(0)
    krow0 = _na_key_row0(j)

    @pl.when(pl.program_id(1) == 0)
    def _():
        rq = j * NA_QROWS + lax.broadcasted_iota(jnp.int32, (NA_QBLK, NA_KBLK), 0) // GRID_W
        rk = krow0 + lax.broadcasted_iota(jnp.int32, (NA_QBLK, NA_KBLK), 1) // GRID_W
        rs = jnp.clip(rq - NA_WIN_R // 2, 0, GRID_R - NA_WIN_R)
        rowmask = jnp.where(rk < rs, NEG_BIG, jnp.where(rk >= rs + NA_WIN_R, NEG_BIG, 0.0))
        base = krow0 - j * NA_QROWS + NA_WIN_R
        for h in range(NA_HEADS):
            bias = jnp.concatenate(
                [jnp.concatenate(
                    [tile_ref[h, jnp.clip(base + 2 * kp - r, 0, NA_PAIRS - 1)]
                     for kp in range(NA_KROWS // 2)], axis=1)
                 for r in range(NA_QROWS)], axis=0)
            bias_s[h] = bias + rowmask

    off = pl.multiple_of(krow0 * GRID_W, GRID_W)
    q = q_ref[...]
    k = k_ref[pl.ds(off, NA_KBLK), :]
    v = v_ref[pl.ds(off, NA_KBLK), :]
    head = lax.broadcasted_iota(jnp.int32, (NA_QBLK, D_MIX), 1) // NA_HEAD_DIM
    acc = jnp.zeros((NA_QBLK, D_MIX), F32)
    for h in range(NA_HEADS):
        qh = jnp.where(head == h, q, jnp.zeros_like(q))
        s2 = _dot_nt(qh, k) + bias_s[h]
        acc = jnp.where(head == h, _softmax2_pv(s2, v), acc)
    out_ref[...] = acc.astype(out_ref.dtype)


def _neighborhood(q, k, v, tiles):
    nj = SEQ // NA_QBLK
    return pl.pallas_call(
        _na_kernel,
        grid=(nj, BATCH),
        in_specs=[pl.BlockSpec((NA_QBLK, D_MIX), lambda j, b: (b * nj + j, 0)),
                  pl.BlockSpec((SEQ, D_MIX), lambda j, b: (b, 0)),
                  pl.BlockSpec((SEQ, D_MIX), lambda j, b: (b, 0)),
                  _const_spec(tiles.shape)],
        out_specs=pl.BlockSpec((NA_QBLK, D_MIX), lambda j, b: (b * nj + j, 0)),
        out_shape=jax.ShapeDtypeStruct((TOKENS, D_MIX), BF16),
        scratch_shapes=[pltpu.VMEM((NA_HEADS, NA_QBLK, NA_KBLK), F32)],
        compiler_params=_params("parallel", "arbitrary"),
        name="neighborhood_attention",
    )(q, k, v, tiles)


def _na_bias_tiles(rpb):
    c = jnp.arange(GRID_W)
    cs = jnp.clip(c - NA_WIN_C // 2, 0, GRID_W - NA_WIN_C)
    col_ok = (c[None, :] >= cs[:, None]) & (c[None, :] < cs[:, None] + NA_WIN_C)
    dc = jnp.clip(c[None, :] - c[:, None] + (NA_WIN_C - 1), 0, 2 * NA_WIN_C - 2)
    pick = (dc[None] == jnp.arange(2 * NA_WIN_C - 1)[:, None, None]).astype(F32)
    t = jnp.einsum('hrd,dqk->hrqk', rpb.astype(F32), pick, precision=lax.Precision.HIGHEST)
    t = jnp.where(col_ok, t * LOG2E, NEG_BIG)
    t = jnp.pad(t, ((0, 0), (1, 1), (0, 0), (0, 0)))
    return jnp.concatenate([t[:, :-1], t[:, 1:]], axis=-1)


def _merge_kernel(x_ref, g1_ref, yhy_ref, yfn_ref, ymla_ref, yna_ref, wg_ref, bg_ref,
                  wbr_ref, wout_ref, g2_ref, wr_ref, out_ref, hb_ref, logit_ref):
    x = x_ref[...]
    h = _rms(x, g1_ref[...]).astype(BF16)
    merged = jnp.zeros(x.shape, F32)
    for n, y_ref in enumerate((yhy_ref, yfn_ref, ymla_ref, yna_ref)):
        sl = slice(n * D_MODEL, (n + 1) * D_MODEL)
        gate = jax.nn.sigmoid(_dot_nt(h, wg_ref[sl, :]) + bg_ref[:, sl])
        merged = merged + gate * _dot(y_ref[...], wbr_ref[n])
    x1 = x + _dot(merged.astype(BF16), wout_ref[...])
    out_ref[...] = x1
    h_hi, h_lo = _split2(_rms(x1, g2_ref[...]))
    hb_ref[...] = h_hi
    w_hi, w_lo = _split2(wr_ref[...])
    logits = _dot_nt(w_hi, h_hi) + _dot_nt(w_hi, h_lo) + _dot_nt(w_lo, h_hi)
    logit_ref[...] = logits[:N_EXPERTS]


def _merge(x, g1, yhy, yfn, ymla, yna, wg, bg, wbr, wout, g2, wr_t):
    tm = TILE["merge"]
    nt = SEQ // tm
    row = lambda n: pl.BlockSpec((tm, n), lambda i: (i, 0))
    return pl.pallas_call(
        _merge_kernel,
        grid=(TOKENS // tm,),
        in_specs=[row(D_MODEL), _const_spec(g1.shape), row(D_MIX), row(D_MIX), row(D_MIX),
                  row(D_MIX), _const_spec(wg.shape), _const_spec(bg.shape),
                  _const_spec(wbr.shape), _const_spec(wout.shape), _const_spec(g2.shape),
                  _const_spec(wr_t.shape)],
        out_specs=[row(D_MODEL), row(D_MODEL),
                   pl.BlockSpec((None, N_EXPERTS, tm), lambda i: (i // nt, 0, i % nt))],
        out_shape=[jax.ShapeDtypeStruct((TOKENS, D_MODEL), F32),
                   jax.ShapeDtypeStruct((TOKENS, D_MODEL), BF16),
                   jax.ShapeDtypeStruct((BATCH, N_EXPERTS, SEQ), F32)],
        compiler_params=_params("parallel"),
        name="merge",
    )(x, g1, yhy, yfn, ymla, yna, wg, bg, wbr, wout, g2, wr_t)


def _prefix_count(m):
    r = lax.broadcasted_iota(jnp.int32, (LANES, LANES), 0)
    c = lax.broadcasted_iota(jnp.int32, (LANES, LANES), 1)
    upper = jnp.where(r < c, 1.0, 0.0).astype(BF16)
    run = jnp.zeros((m.shape[0], 1), F32)
    parts = []
    for i in range(SEQ // LANES):
        chunk = m[:, i * LANES:(i + 1) * LANES]
        parts.append(_dot(chunk.astype(BF16), upper) + run)
        run = run + jnp.sum(chunk, axis=1, keepdims=True)
    return jnp.concatenate(parts, axis=1)


SELECT_MAX_ITERS = 192


def _select_kernel(logit_ref, slot_row_ref, slot_col_ref, w_row_ref, start_ref, slot_s, w_s):
    b = pl.program_id(0)
    rows = BATCH * N_EXPERTS

    @pl.when(b == 0)
    def _():
        logits = logit_ref[...]
        ex = jnp.exp(logits - jnp.max(logits, axis=1, keepdims=True))
        aff = (ex / jnp.sum(ex, axis=1, keepdims=True)).reshape(rows, SEQ)

        def bisect(c):
            it, lo, hi, _ = c
            mid = 0.5 * (lo + hi)
            cnt = jnp.sum(jnp.where(aff >= mid, 1.0, 0.0), axis=1, keepdims=True)
            moving = jnp.where(mid == lo, 0.0, jnp.where(mid == hi, 0.0, 1.0))
            enough = cnt >= CAPACITY
            return (it + 1, jnp.where(enough, mid, lo), jnp.where(enough, hi, mid),
                    (jnp.max(moving) > 0).astype(jnp.int32))

        _, lo, hi, _ = lax.while_loop(
            lambda c: (c[0] < SELECT_MAX_ITERS) & (c[3] > 0), bisect,
            (jnp.int32(0), jnp.zeros((rows, 1), F32), jnp.full((rows, 1), 2.0, F32),
             jnp.int32(1)))
        above = jnp.where(aff >= hi, 1.0, 0.0)
        band = jnp.where(aff >= lo, 1.0, 0.0) - above
        need = CAPACITY - jnp.sum(above, axis=1, keepdims=True)
        sel = above + band * jnp.where(_prefix_count(band) < need, 1.0, 0.0)
        slot_s[...] = jnp.where(sel > 0, _prefix_count(sel), -1.0)
        w_s[...] = sel * aff

    r0 = pl.multiple_of(b * N_EXPERTS, N_EXPERTS)
    slot = slot_s[pl.ds(r0, N_EXPERTS), :]
    slot_row_ref[...] = slot.astype(jnp.int32)
    pad = jnp.full((LANES - N_EXPERTS, SEQ), -1.0, F32)
    slot_col_ref[...] = jnp.concatenate([slot, pad], axis=0).T.astype(jnp.int32)
    w_row_ref[...] = w_s[pl.ds(r0, N_EXPERTS), :]
    tm = MOE_FINE
    token = lax.broadcasted_iota(jnp.int32, slot.shape, 1)
    lane = lax.broadcasted_iota(jnp.int32, (N_EXPERTS, LANES), 1)
    starts = jnp.where(lane == SEQ // tm, float(CAPACITY), 0.0)
    for i in range(1, SEQ // tm):
        before = jnp.where(slot >= 0, jnp.where(token < i * tm, 1.0, 0.0), 0.0)
        starts = starts + jnp.where(lane == i, jnp.sum(before, axis=1, keepdims=True), 0.0)
    start_ref[...] = starts.astype(jnp.int32)


def _select(logits):
    return pl.pallas_call(
        _select_kernel,
        grid=(BATCH,),
        in_specs=[_const_spec(logits.shape)],
        out_specs=[pl.BlockSpec((None, N_EXPERTS, SEQ), lambda b: (b, 0, 0)),
                   pl.BlockSpec((None, SEQ, LANES), lambda b: (b, 0, 0)),
                   pl.BlockSpec((None, N_EXPERTS, SEQ), lambda b: (b, 0, 0)),
                   pl.BlockSpec((None, N_EXPERTS, LANES), lambda b: (b, 0, 0))],
        out_shape=[jax.ShapeDtypeStruct((BATCH, N_EXPERTS, SEQ), jnp.int32),
                   jax.ShapeDtypeStruct((BATCH, SEQ, LANES), jnp.int32),
                   jax.ShapeDtypeStruct((BATCH, N_EXPERTS, SEQ), F32),
                   jax.ShapeDtypeStruct((BATCH, N_EXPERTS, LANES), jnp.int32)],
        scratch_shapes=[pltpu.VMEM((BATCH * N_EXPERTS, SEQ), F32),
                        pltpu.VMEM((BATCH * N_EXPERTS, SEQ), F32)],
        compiler_params=_params("arbitrary"),
        name="expert_select",
    )(logits)


MXU_DEPTH = 256
BF16_ROWS = 16


def _slot_window(tokens):
    return 2 * tokens * CAPACITY // SEQ


def _tile_windows(start_ref, b, i, tokens):
    stride = start_ref.shape[0] // (BATCH * N_EXPERTS)
    fine = tokens // MOE_FINE
    window = _slot_window(tokens)
    base, fits = [], None
    for e in range(N_EXPERTS):
        at = (b * N_EXPERTS + e) * stride + i * fine
        lo = jnp.minimum(start_ref[at] // BF16_ROWS * BF16_ROWS, CAPACITY - window)
        ok = start_ref[at + fine] - lo <= window
        fits = ok if fits is None else fits & ok
        base.append(pl.multiple_of(lo, BF16_ROWS))
    return base, fits


def _gather_kernel(start_ref, slot_ref, w_ref, hb_ref, xe_ref, wsl_ref):
    b = pl.program_id(0)
    i = pl.program_id(1)
    tokens = slot_ref.shape[1]

    @pl.when(i == 0)
    def _():
        xe_ref[...] = jnp.zeros(xe_ref.shape, BF16)
        wsl_ref[...] = jnp.zeros(wsl_ref.shape, F32)

    def add_rows(e, rows, hit, picked):
        xe_ref[e, rows, :] = xe_ref[e, rows, :] + picked.astype(BF16)
        wslot = jnp.sum(jnp.where(hit, w_ref[e:e + 1, :], 0.0), axis=1, keepdims=True)
        wsl_ref[e, rows, :] = wsl_ref[e, rows, :] + jnp.broadcast_to(wslot, (hit.shape[0], LANES))

    window = _slot_window(tokens)
    base, fits = _tile_windows(start_ref, b, i, tokens)

    @pl.when(fits)
    def _():
        c = lax.broadcasted_iota(jnp.int32, (window, tokens), 0)
        hits = [slot_ref[e:e + 1, :] - base[e] == c for e in range(N_EXPERTS)]
        onehot = jnp.concatenate([jnp.where(m, 1.0, 0.0).astype(BF16) for m in hits], axis=0)
        picked = _dot(onehot, hb_ref[...])
        for e in range(N_EXPERTS):
            add_rows(e, pl.ds(base[e], window), hits[e], picked[e * window:(e + 1) * window])

    @pl.when(jnp.logical_not(fits))
    def _():
        c = lax.broadcasted_iota(jnp.int32, (CAPACITY, tokens), 0)
        for e in range(N_EXPERTS):
            hit = slot_ref[e:e + 1, :] == c
            picked = _dot(jnp.where(hit, 1.0, 0.0).astype(BF16), hb_ref[...])
            add_rows(e, slice(None), hit, picked)


def _gather(slot_row, w_row, starts, hb):
    tm = TILE["gather"]
    nt = SEQ // tm
    grid_spec = pltpu.PrefetchScalarGridSpec(
        num_scalar_prefetch=1,
        grid=(BATCH, nt),
        in_specs=[pl.BlockSpec((None, N_EXPERTS, tm), lambda b, i, s: (b, 0, i)),
                  pl.BlockSpec((None, N_EXPERTS, tm), lambda b, i, s: (b, 0, i)),
                  pl.BlockSpec((tm, D_MODEL), lambda b, i, s: (b * nt + i, 0))],
        out_specs=[pl.BlockSpec((N_EXPERTS, None, CAPACITY, D_MODEL), lambda b, i, s: (0, b, 0, 0)),
                   pl.BlockSpec((N_EXPERTS, None, CAPACITY, LANES), lambda b, i, s: (0, b, 0, 0))])
    return pl.pallas_call(
        _gather_kernel,
        grid_spec=grid_spec,
        out_shape=[jax.ShapeDtypeStruct((N_EXPERTS, BATCH, CAPACITY, D_MODEL), BF16),
                   jax.ShapeDtypeStruct((N_EXPERTS, BATCH, CAPACITY, LANES), F32)],
        compiler_params=_params("parallel", "arbitrary"),
        name="expert_gather",
    )(starts[:, :, :SEQ // MOE_FINE + 1].reshape(-1), slot_row, w_row, hb)


def _expert_kernel(xe_ref, wsl_ref, wg_ref, wu_ref, wd_ref, ye_ref, wg_s, wu_s, wd_s, *, span):
    ph = pl.program_id(0)
    i = pl.program_id(1)
    rows = pl.ds(pl.multiple_of(i * span, span), span)
    nxt = ph % 2
    wg_s[nxt, rows, :] = wg_ref[rows, :].astype(BF16)
    wu_s[nxt, rows, :] = wu_ref[rows, :].astype(BF16)
    wd_s[nxt, rows, :] = wd_ref[rows, :].astype(BF16)

    @pl.when(ph > 0)
    def _():
        cur = (ph + 1) % 2
        xe = xe_ref[...]
        g = _dot(xe, wg_s[cur])
        u = _dot(xe, wu_s[cur])
        act = (g * jax.nn.sigmoid(g) * u).astype(BF16)
        ye_ref[...] = (_dot(act, wd_s[cur]) * wsl_ref[:, 0:1]).astype(BF16)


def _experts(xe, wsl, wg, wu, wd, layer):
    rows = BATCH * CAPACITY
    tm = TILE["expert"]
    nt = rows // tm
    last = N_EXPERTS - 1
    wspec = lambda a: pl.BlockSpec((None, None) + a.shape[2:],
                                   lambda ph, i: (layer, jnp.minimum(ph, last), 0, 0))
    data = lambda n: pl.BlockSpec(
        (None, tm, n), lambda ph, i: (jnp.maximum(ph - 1, 0), jnp.where(ph > 0, i, 0), 0))
    return pl.pallas_call(
        functools.partial(_expert_kernel, span=D_MODEL // nt),
        grid=(N_EXPERTS + 1, nt),
        in_specs=[data(D_MODEL), data(LANES), wspec(wg), wspec(wu), wspec(wd)],
        out_specs=data(D_MODEL),
        out_shape=jax.ShapeDtypeStruct((N_EXPERTS, rows, D_MODEL), BF16),
        scratch_shapes=[pltpu.VMEM((2, D_MODEL, D_FF), BF16), pltpu.VMEM((2, D_MODEL, D_FF), BF16),
                        pltpu.VMEM((2, D_FF, D_MODEL), BF16)],
        compiler_params=_params("arbitrary", "arbitrary"),
        name="expert_ffn",
    )(xe.reshape(N_EXPERTS, rows, D_MODEL), wsl.reshape(N_EXPERTS, rows, LANES), wg, wu, wd)


def _combine_kernel(start_ref, x_ref, ye_ref, slot_ref, p_ref, g3_ref, wpg_ref, wpp_ref,
                    gf_ref, out_ref, moe_s, *, final_norm):
    slot = slot_ref[...]
    rows = slot.shape[0]
    window = _slot_window(rows)
    base, fits = _tile_windows(start_ref, pl.program_id(0), pl.program_id(1), rows)

    @pl.when(fits)
    def _():
        c = lax.broadcasted_iota(jnp.int32, (rows, window), 1)
        parts = []
        group = MXU_DEPTH // window
        for e0 in range(0, N_EXPERTS, group):
            hot, win = [], []
            for e in range(e0, e0 + group):
                hot.append(jnp.where(slot[:, e:e + 1] - base[e] == c, 1.0, 0.0).astype(BF16))
                win.append(ye_ref[e, pl.ds(base[e], window), :])
            parts.append(_dot(jnp.concatenate(hot, axis=1), jnp.concatenate(win, axis=0)))
        while len(parts) > 1:
            parts = [u + v for u, v in zip(parts[::2], parts[1::2])]
        moe_s[...] = parts[0]

    @pl.when(jnp.logical_not(fits))
    def _():
        c = lax.broadcasted_iota(jnp.int32, (rows, CAPACITY), 1)
        onehot = jnp.concatenate(
            [jnp.where(slot[:, e:e + 1] == c, 1.0, 0.0).astype(BF16) for e in range(N_EXPERTS)],
            axis=1)
        moe_s[...] = _dot(onehot, ye_ref[...].reshape(N_EXPERTS * CAPACITY, D_MODEL))

    acc = x_ref[...] + moe_s[...]
    h = _rms(acc, g3_ref[...]).astype(BF16)
    gate = jax.nn.sigmoid(_dot(h, wpg_ref[...]))
    y = acc + gate * _dot(p_ref[...].astype(BF16), wpp_ref[...])
    if final_norm:
        y = _rms(y, gf_ref[...])
    out_ref[...] = y


def _combine(x, ye, slot_col, starts, p, layer, g3, wpg, wpp, gf, final_norm):
    tm = TILE["combine"]
    nt = SEQ // tm
    p0 = layer * (TOKENS // tm)
    const = lambda a: pl.BlockSpec(a.shape, lambda b, i, s: (0,) * a.ndim,
                                   pipeline_mode=pl.Buffered(1))
    grid_spec = pltpu.PrefetchScalarGridSpec(
        num_scalar_prefetch=1,
        grid=(BATCH, nt),
        in_specs=[pl.BlockSpec((tm, D_MODEL), lambda b, i, s: (b * nt + i, 0)),
                  pl.BlockSpec((N_EXPERTS, None, CAPACITY, D_MODEL), lambda b, i, s: (0, b, 0, 0)),
                  pl.BlockSpec((None, tm, LANES), lambda b, i, s: (b, i, 0)),
                  pl.BlockSpec((tm, PLE_DIM), lambda b, i, s: (p0 + b * nt + i, 0)),
                  const(g3), const(wpg), const(wpp), const(gf)],
        out_specs=pl.BlockSpec((tm, D_MODEL), lambda b, i, s: (b * nt + i, 0)),
        scratch_shapes=[pltpu.VMEM((tm, D_MODEL), F32)])
    return pl.pallas_call(
        functools.partial(_combine_kernel, final_norm=final_norm),
        grid_spec=grid_spec,
        out_shape=jax.ShapeDtypeStruct((TOKENS, D_MODEL), F32),
        compiler_params=_params("parallel", "parallel"),
        name="combine",
    )(starts[:, :, :SEQ // MOE_FINE + 1].reshape(-1), x,
      ye.reshape(N_EXPERTS, BATCH, CAPACITY, D_MODEL),
      slot_col, p, g3, wpg, wpp, gf)


DFT_FINE = 64
DFT_STEP = 4


def _dft_kernel(ca_ref, sa_ref, cb_ref, sb_ref, out_ref):
    cb = cb_ref[...]
    sb = sb_ref[...]
    for r in range(DFT_STEP):
        rows = slice(r * DFT_FINE, (r + 1) * DFT_FINE)
        out_ref[rows, :] = (ca_ref[r] * cb - sa_ref[r] * sb).astype(BF16)


def _dft_tables(n_points, nres):
    blk = SEQ // nres
    t = jnp.arange(SEQ, dtype=jnp.int32).reshape(blk, nres).T
    t = jnp.concatenate([t, t], axis=1).reshape(1, 2 * SEQ)
    is_sin = (jnp.arange(2 * SEQ) // blk % 2 == 1)[None, :]
    coarse = jnp.arange(HALF // DFT_FINE, dtype=jnp.int32)[:, None] * DFT_FINE
    fine = jnp.arange(DFT_FINE, dtype=jnp.int32)[:, None]
    ang = lambda f: ((f * t) % n_points).astype(F32) * (2.0 * math.pi / n_points)
    ca = jnp.cos(ang(coarse))[:, None, :]
    sa = jnp.sin(ang(coarse))[:, None, :]
    cb = jnp.where(is_sin, jnp.sin(ang(fine)), jnp.cos(ang(fine)))
    sb = jnp.where(is_sin, -jnp.cos(ang(fine)), jnp.sin(ang(fine)))
    rows = DFT_STEP * DFT_FINE
    return pl.pallas_call(
        _dft_kernel,
        grid=(HALF // rows,),
        in_specs=[pl.BlockSpec((DFT_STEP, 1, 2 * SEQ), lambda i: (i, 0, 0)),
                  pl.BlockSpec((DFT_STEP, 1, 2 * SEQ), lambda i: (i, 0, 0)),
                  _const_spec(cb.shape), _const_spec(sb.shape)],
        out_specs=pl.BlockSpec((rows, 2 * SEQ), lambda i: (i, 0)),
        out_shape=jax.ShapeDtypeStruct((HALF, 2 * SEQ), BF16),
        compiler_params=_params("parallel"),
        name="dft_tables",
    )(ca, sa, cb, sb)


def _fnet_group_tables():
    gc = D_MIX // FN_GROUPS
    i = lax.broadcasted_iota(jnp.int32, (D_MIX, D_MIX), 0)
    j = lax.broadcasted_iota(jnp.int32, (D_MIX, D_MIX), 1)
    same = (i // gc) == (j // gc)
    ang = (((i % gc) * (j % gc)) % gc).astype(F32) * (2.0 * math.pi / gc)
    return (jnp.where(same, jnp.cos(ang), 0.0).astype(BF16),
            jnp.where(same, jnp.sin(ang), 0.0).astype(BF16))


def _hyena_features():
    t01 = jnp.linspace(0.0, 1.0, SEQ, dtype=F32)[:, None]
    bands = jnp.linspace(1e-4, HY_BANDS - 1, HY_BANDS, dtype=F32)
    ang = 2.0 * math.pi * jnp.arange(SEQ, dtype=F32)[:, None] * bands / SEQ
    z = jnp.concatenate([t01, jnp.cos(ang), -jnp.sin(ang)], axis=-1)
    z = jnp.pad(z, ((0, 0), (0, LANES - HY_EMB)))
    max_decay = math.log(HY_TARGET) / HY_FAST_DECAY
    min_decay = math.log(HY_TARGET) / HY_SLOW_DECAY
    deltas = jnp.linspace(min_decay, max_decay, D_MIX, dtype=F32)
    window = jnp.exp(-t01 * jnp.abs(deltas))
    return z, window


def _rot_cols(w):
    half = w.shape[-1] // 2
    return jnp.concatenate([-w[..., half:], w[..., :half]], axis=-1)


def _rope_tables():
    inv = ROPE_THETA ** (-jnp.arange(0, MLA_ROPE, 2, dtype=F32) / MLA_ROPE)
    ang = jnp.arange(SEQ, dtype=F32)[:, None] * inv
    cos = jnp.concatenate([jnp.cos(ang), jnp.cos(ang)], axis=-1)
    sin = jnp.concatenate([jnp.sin(ang), jnp.sin(ang)], axis=-1)
    scale = (MLA_NOPE + MLA_ROPE) ** -0.5 * LOG2E
    pad = MLA_HEAD_PAD - MLA_NOPE - MLA_ROPE
    one = jnp.ones((SEQ, MLA_NOPE), F32)
    zero = jnp.zeros((SEQ, MLA_NOPE), F32)
    zpad = jnp.zeros((SEQ, pad), F32)
    cosq = jnp.tile(jnp.concatenate([one, cos, zpad], axis=-1) * scale, (1, MLA_HEADS))
    sinq = jnp.tile(jnp.concatenate([zero, sin, zpad], axis=-1) * scale, (1, MLA_HEADS))
    csk = jnp.concatenate([cos, sin, jnp.zeros((SEQ, LANES - 2 * MLA_ROPE), F32)], axis=-1)
    return cosq, sinq, csk


def _mla_weights(w_uq, w_ukv):
    pad = MLA_HEAD_PAD - MLA_NOPE - MLA_ROPE
    wq = w_uq.reshape(MLA_Q_RANK, MLA_HEADS, MLA_NOPE + MLA_ROPE)
    nope, pe = wq[..., :MLA_NOPE], wq[..., MLA_NOPE:]
    zp = jnp.zeros((MLA_Q_RANK, MLA_HEADS, pad), F32)
    wqa = jnp.concatenate([nope, pe, zp], axis=-1).reshape(MLA_Q_RANK, MLA_QK)
    wqb = jnp.concatenate([jnp.zeros_like(nope), _rot_cols(pe), zp], axis=-1)
    wqb = wqb.reshape(MLA_Q_RANK, MLA_QK)
    wkv = w_ukv.reshape(MLA_KV_RANK, MLA_HEADS, MLA_NOPE + MLA_V)
    knope, v = wkv[..., :MLA_NOPE], wkv[..., MLA_NOPE:]
    wk = jnp.concatenate(
        [knope, jnp.zeros((MLA_KV_RANK, MLA_HEADS, MLA_HEAD_PAD - MLA_NOPE), F32)], axis=-1)
    wk = wk.reshape(MLA_KV_RANK, MLA_QK)
    wv = v.reshape(MLA_KV_RANK, MLA_HEADS * MLA_V).T
    r = lax.broadcasted_iota(jnp.int32, (LANES, MLA_QK), 0)
    c = lax.broadcasted_iota(jnp.int32, (LANES, MLA_QK), 1)
    epe = jnp.where((r < 2 * MLA_ROPE) & (c % MLA_HEAD_PAD == MLA_NOPE + r % MLA_ROPE), 1.0, 0.0)
    return (wqa.astype(BF16), wqb.astype(BF16), wk.astype(BF16), wv.astype(BF16),
            epe.astype(BF16))


def kernel(x, p, norm1_g, w_in, b_gate, hy_conv_w, hy_conv_b, hf_w1, hf_b1, hf_freq, hf_w2,
           hf_b2, hf_w3, hy_skip, q_norm_g, w_uq, kv_norm_g, w_ukv, rpb, w_br, w_out, norm2_g,
           w_router, w_e_gate, w_e_up, w_e_down, norm3_g, w_ple_gate, w_ple_proj, final_g):
    conv_tab = _dft_tables(2 * SEQ, 4)
    fnet_tab = _dft_tables(SEQ, 2)
    fnet_cg, fnet_sg = _fnet_group_tables()
    zfeat, window = _hyena_features()
    cosq, sinq, csk = _rope_tables()
    row = lambda a: a.reshape(1, -1)

    xt = x.reshape(TOKENS, D_MODEL)
    w_in_t = jnp.swapaxes(w_in, 1, 2)
    for i in range(DEPTH):
        g1 = row(norm1_g[i])
        wa, wna, wgate = _inproj_weights(w_in_t, i)
        wqa, wqb, wk, wv, epe = _mla_weights(w_uq[i], w_ukv[i])

        u_hy, u_fn, q, k, v, naq, nak, nav = _inproj(
            xt, g1, wa, wna, row(q_norm_g[i]), wqa, wqb, row(kv_norm_g[i]), wk, wv, epe,
            cosq, sinq, csk)

        w1 = jnp.pad(hf_w1[i], ((0, LANES - HY_EMB), (0, 0)))
        kf, kny = _hyena_filter(zfeat, window, w1, row(hf_b1[i]), hf_freq[i], hf_w2[i],
                                row(hf_b2[i]), hf_w3[i], conv_tab)
        conv_b = row(hy_conv_b[i])
        z1 = _hyena_stage(u_hy, 0, u_hy, 2, hy_conv_w[i], conv_b, conv_tab, kf, kny, 0,
                          row(hy_skip[i, 0]), True, F32)
        y_hy = _hyena_stage(u_hy, 1, z1, 0, hy_conv_w[i], conv_b, conv_tab, kf, kny, 1,
                            row(hy_skip[i, 1]), False, BF16)
        y_fn = _fnet(u_fn, fnet_tab, fnet_cg, fnet_sg)
        y_mla = _mla(q, k, v)
        y_na = _neighborhood(naq, nak, nav, _na_bias_tiles(rpb[i]))
        wr_t = jnp.pad(w_router[i].T, ((0, LANES - N_EXPERTS), (0, 0)))
        xt, hb, logits = _merge(xt, g1, y_hy, y_fn, y_mla, y_na, wgate, row(b_gate[i]),
                                w_br[i].astype(BF16), w_out[i].astype(BF16),
                                row(norm2_g[i]), wr_t)
        slot_row, slot_col, w_row, starts = _select(logits)
        xe, wsl = _gather(slot_row, w_row, starts, hb)
        ye = _experts(xe, wsl, w_e_gate, w_e_up, w_e_down, i)
        xt = _combine(xt, ye, slot_col, starts, p.reshape(DEPTH * TOKENS, PLE_DIM), i,
                      row(norm3_g[i]), w_ple_gate[i].astype(BF16),
                      w_ple_proj[i].astype(BF16), row(final_g), i == DEPTH - 1)
    return xt.reshape(BATCH, SEQ, D_MODEL)
```

```python
import functools
import math

import jax
import jax.numpy as jnp
from jax import lax
from jax.experimental import pallas as pl
from jax.experimental.pallas import tpu as pltpu

F32 = jnp.float32
BF16 = jnp.bfloat16

D_MODEL = 1024
BATCH = 8
SEQ = 2048
DEPTH = 2
TOKENS = BATCH * SEQ

GRID_W = 64
GRID_R = SEQ // GRID_W
D_MIX = 256
N_BRANCH = 4
EPS = 1e-6
HY_ORDER = 2
HY_BANDS = 16
HY_EMB = 2 * HY_BANDS + 1
HY_FFN = 64
HY_TARGET = 1e-2
HY_FAST_DECAY = 0.3
HY_SLOW_DECAY = 1.5
FN_GROUPS = 4
MLA_HEADS = 4
MLA_NOPE = 64
MLA_ROPE = 32
MLA_V = 64
MLA_Q_RANK = 256
MLA_KV_RANK = 128
ROPE_THETA = 10000.0
NA_HEADS = 4
NA_HEAD_DIM = D_MIX // NA_HEADS
NA_WIN_R = 8
NA_WIN_C = 16
N_EXPERTS = 16
CAPACITY = 2 * SEQ // N_EXPERTS
D_FF = 1024
PLE_DIM = 256

HY_COLS = 3 * D_MIX
OFF_FN = HY_COLS
OFF_CQ = OFF_FN + D_MIX
OFF_CKV = OFF_CQ + MLA_Q_RANK
OFF_KPE = OFF_CKV + MLA_KV_RANK
OFF_NA = OFF_KPE + MLA_ROPE
OFF_GATE = OFF_NA + 3 * D_MIX

LANES = 128
MLA_HEAD_PAD = 128
MLA_QK = MLA_HEADS * MLA_HEAD_PAD
WA_COLS = 1536
NEG_BIG = -1e30
LOG2E = math.log2(math.e)

TILE = dict(inproj=512, mla=1024, merge=512, gather=256, expert=512, combine=512)
MOE_FINE = 256
NA_QROWS = 4
NA_KROWS = 12
NA_QBLK = NA_QROWS * GRID_W
NA_KBLK = NA_KROWS * GRID_W
VMEM_LIMIT = 56 * 1024 * 1024


def _params(*sem):
    return pltpu.CompilerParams(dimension_semantics=sem, vmem_limit_bytes=VMEM_LIMIT)


def _const_spec(shape):
    nd = len(shape)
    return pl.BlockSpec(shape, lambda *_: (0,) * nd, pipeline_mode=pl.Buffered(1))


def _rms(x, g):
    return x * lax.rsqrt(jnp.mean(x * x, axis=-1, keepdims=True) + EPS) * g


def _dot(a, b):
    return jnp.dot(a, b, preferred_element_type=F32)


def _dot_nt(a, b):
    return lax.dot_general(a, b, (((1,), (1,)), ((), ())), preferred_element_type=F32)


def _split2(x):
    hi = x.astype(BF16)
    lo = (x - hi.astype(F32)).astype(BF16)
    return hi, lo


def _inproj_kernel(x_ref, g1_ref, wa_ref, wna_ref, qg_ref, wqa_ref, wqb_ref, kvg_ref,
                   wk_ref, wv_ref, epe_ref, cosq_ref, sinq_ref, csk_ref,
                   uhy_ref, ufn_ref, q_ref, k_ref, v_ref, naq_ref, nak_ref, nav_ref):
    h = _rms(x_ref[...], g1_ref[...]).astype(BF16)
    ua = _dot_nt(h, wa_ref[...])
    uhy_ref[...] = ua[:, :HY_COLS]
    ufn_ref[...] = ua[:, OFF_FN:OFF_CQ]
    cqn = _rms(ua[:, OFF_CQ:OFF_CKV], qg_ref[...]).astype(BF16)
    q = _dot(cqn, wqa_ref[...]) * cosq_ref[...] + _dot(cqn, wqb_ref[...]) * sinq_ref[...]
    q_ref[...] = q.astype(BF16)
    kvn = _rms(ua[:, OFF_CKV:OFF_KPE], kvg_ref[...]).astype(BF16)
    kpe = ua[:, OFF_KPE:WA_COLS] * csk_ref[...]
    k = _dot(kvn, wk_ref[...]) + _dot(kpe.astype(BF16), epe_ref[...])
    k_ref[...] = k.astype(BF16)
    v_ref[...] = _dot_nt(wv_ref[...], kvn).astype(BF16)
    una = _dot_nt(h, wna_ref[...])
    naq_ref[...] = (una[:, :D_MIX] * (NA_HEAD_DIM ** -0.5 * LOG2E)).astype(BF16)
    nak_ref[...] = una[:, D_MIX:2 * D_MIX].astype(BF16)
    nav_ref[...] = una[:, 2 * D_MIX:].astype(BF16)


def _inproj(x, g1, wa, wna, qg, wqa, wqb, kvg, wk, wv, epe, cosq, sinq, csk):
    tm = TILE["inproj"]
    nt = SEQ // tm
    row = lambda n: pl.BlockSpec((tm, n), lambda i: (i, 0))
    pos = lambda n: pl.BlockSpec((tm, n), lambda i: (i % nt, 0))
    outs = [(HY_COLS, F32), (D_MIX, F32), (MLA_QK, BF16), (MLA_QK, BF16), None,
            (D_MIX, BF16), (D_MIX, BF16), (D_MIX, BF16)]
    vt_spec = pl.BlockSpec((None, D_MIX, tm), lambda i: (i // nt, 0, i % nt))
    vt_shape = jax.ShapeDtypeStruct((BATCH, D_MIX, SEQ), BF16)
    return pl.pallas_call(
        _inproj_kernel,
        grid=(TOKENS // tm,),
        in_specs=[row(D_MODEL), _const_spec(g1.shape), _const_spec(wa.shape),
                  _const_spec(wna.shape), _const_spec(qg.shape), _const_spec(wqa.shape),
                  _const_spec(wqb.shape), _const_spec(kvg.shape), _const_spec(wk.shape),
                  _const_spec(wv.shape), _const_spec(epe.shape),
                  pos(MLA_QK), pos(MLA_QK), pos(LANES)],
        out_specs=[vt_spec if o is None else row(o[0]) for o in outs],
        out_shape=[vt_shape if o is None else jax.ShapeDtypeStruct((TOKENS, o[0]), o[1])
                   for o in outs],
        compiler_params=_params("parallel"),
        name="inproj",
    )(x, g1, wa, wna, qg, wqa, wqb, kvg, wk, wv, epe, cosq, sinq, csk)


WPREP_ROWS = 512


def _inproj_weights_kernel(w_ref, wa_ref, wna_ref, wg_ref):
    i = pl.program_id(0)

    @pl.when(i == 0)
    def _():
        half = MLA_ROPE // 2
        wa_ref[:OFF_NA, :] = w_ref[:OFF_NA, :].astype(BF16)
        wa_ref[OFF_NA:OFF_NA + half, :] = (-w_ref[OFF_KPE + half:OFF_NA, :]).astype(BF16)
        wa_ref[OFF_NA + half:OFF_NA + MLA_ROPE, :] = w_ref[OFF_KPE:OFF_KPE + half, :].astype(BF16)
        wa_ref[OFF_NA + MLA_ROPE:, :] = jnp.zeros((WA_COLS - OFF_NA - MLA_ROPE, D_MODEL), BF16)
        wna_ref[...] = w_ref[OFF_NA:OFF_GATE, :].astype(BF16)

    start = pl.multiple_of(OFF_GATE + i * WPREP_ROWS, MLA_ROPE)
    wg_ref[...] = w_ref[pl.ds(start, WPREP_ROWS), :].astype(BF16)


def _inproj_weights(w_in_t, layer):
    n_in = w_in_t.shape[1]
    rows = (WA_COLS, OFF_GATE - OFF_NA, n_in - OFF_GATE)
    whole = lambda n: pl.BlockSpec((n, D_MODEL), lambda i: (0, 0))
    return pl.pallas_call(
        _inproj_weights_kernel,
        grid=(rows[2] // WPREP_ROWS,),
        in_specs=[pl.BlockSpec((None, n_in, D_MODEL), lambda i: (layer, 0, 0),
                               pipeline_mode=pl.Buffered(1))],
        out_specs=[whole(rows[0]), whole(rows[1]),
                   pl.BlockSpec((WPREP_ROWS, D_MODEL), lambda i: (i, 0))],
        out_shape=[jax.ShapeDtypeStruct((n, D_MODEL), BF16) for n in rows],
        compiler_params=_params("arbitrary"),
        name="inproj_weights",
    )(w_in_t)


HALF = SEQ // 2
HY_SEQS = 1


def _residue_rows(ref, r, nres):
    rows = pl.ds(r, SEQ // nres, stride=nres)
    if isinstance(ref, tuple):
        return jnp.concatenate([h[rows, :] for h in ref], axis=1)
    return jnp.concatenate([ref[j, rows, :] for j in range(ref.shape[0])], axis=1)


def _store_slabs(ref, value, rows=slice(None)):
    for j in range(ref.shape[0]):
        ref[j, rows, :] = value[:, j * LANES:(j + 1) * LANES]


def _table_cols(cs_ref, r, nres, part=None):
    blk = SEQ // nres
    lo = 2 * blk * r
    if part is None:
        return cs_ref[:, lo:lo + 2 * blk]
    return cs_ref[:, lo + part * blk:lo + (part + 1) * blk]


def _dft_fwd(cs_ref, src_ref, want_cos=True, want_sin=True, split=False):
    def prod(part, z, r):
        tab = _table_cols(cs_ref, r, 4, part)
        if split:
            hi, lo = _split2(z)
            return _dot(tab, hi) + _dot(tab, lo)
        return _dot(tab, z.astype(BF16))

    pc, ps = [None] * 4, [None] * 4
    for r in range(4):
        z = _residue_rows(src_ref, r, 4)
        odd = r % 2 == 1
        if want_cos or odd:
            pc[r] = prod(0, z, r)
        if want_sin or odd:
            ps[r] = prod(1, z, r)
    a = b = None
    if want_cos:
        a = ((pc[0] + pc[2]) + (pc[1] + pc[3]), (pc[0] - pc[2]) + (ps[3] - ps[1]))
    if want_sin:
        b = ((ps[0] + ps[2]) + (ps[1] + ps[3]), (ps[0] - ps[2]) + (pc[1] - pc[3]))
    return a, b


def _dft_inv(cs_ref, yre_ref, yim_ref):
    g, h = [], {}
    for r in range(4):
        yr = _residue_rows(yre_ref, r, 4).astype(BF16)
        yi = _residue_rows(yim_ref, r, 4).astype(BF16)
        tab = _table_cols(cs_ref, r, 4)
        g.append(_dot(tab, jnp.concatenate([yr, -yi], axis=0)))
        if r % 2 == 1:
            h[r] = _dot(tab, jnp.concatenate([yi, yr], axis=0))
    return (g[0] + g[2]) + (g[1] + g[3]), (g[0] - g[2]) + (h[3] - h[1])


def _hyena_filter_kernel(z_ref, win_ref, w1_ref, b1_ref, freq_ref, w2_ref, b2_ref, w3_ref,
                         cs_ref, kf_ref, kny_ref, ksum_s, kdif_s, trunk_s):
    hp = lax.Precision.HIGHEST

    @pl.when(pl.program_id(0) == 0)
    def _():
        freq = freq_ref[...]
        hf = jnp.sin(freq[0:1] * (jnp.dot(z_ref[...], w1_ref[...], precision=hp,
                                          preferred_element_type=F32) + b1_ref[...]))
        trunk_s[...] = jnp.sin(freq[1:2] * (jnp.dot(hf, w2_ref[...], precision=hp,
                                                    preferred_element_type=F32) + b2_ref[...]))

    hf = jnp.dot(trunk_s[...], w3_ref[...], precision=hp, preferred_element_type=F32)
    win = win_ref[...]
    t = lax.broadcasted_iota(jnp.int32, (SEQ, D_MIX), 0)
    sgn = (1 - 2 * (t & 1)).astype(F32)
    fwd = hf[:, :D_MIX] * win
    bwd = jnp.where(t == 0, 0.0, hf[:, D_MIX:] * win)
    nrm = lax.rsqrt(jnp.sum(fwd * fwd + bwd * bwd, axis=0, keepdims=True) + EPS)
    ksum = (fwd + bwd) * nrm
    _store_slabs(ksum_s, ksum)
    _store_slabs(kdif_s, (bwd - fwd) * nrm)
    kre, _ = _dft_fwd(cs_ref, ksum_s, want_sin=False, split=True)
    _, kim = _dft_fwd(cs_ref, kdif_s, want_cos=False, split=True)
    wf = 2.0 / (2 * SEQ)
    for part, spec in enumerate((kre, kim)):
        lo = spec[0] * wf
        kf_ref[part, :HALF, :] = lo
        kf_ref[part, 0:1, :] = lo[0:1] * 0.5
        kf_ref[part, HALF:, :] = spec[1] * wf
    kny = jnp.sum(ksum * sgn, axis=0, keepdims=True) * (1.0 / (2 * SEQ))
    kny_ref[...] = jnp.broadcast_to(kny, (8, D_MIX))


def _hyena_filter(zfeat, window, w1, b1, freq, w2, b2, w3, table):
    consts = (zfeat, window, w1, b1, freq, w2, b2)
    return pl.pallas_call(
        _hyena_filter_kernel,
        grid=(HY_ORDER,),
        in_specs=[_const_spec(a.shape) for a in consts]
        + [pl.BlockSpec((HY_FFN, 2 * D_MIX), lambda o: (0, o)),
           _const_spec(table.shape)],
        out_specs=[pl.BlockSpec((None, 2, SEQ, D_MIX), lambda o: (o, 0, 0, 0)),
                   pl.BlockSpec((None, 8, D_MIX), lambda o: (o, 0, 0))],
        out_shape=[jax.ShapeDtypeStruct((HY_ORDER, 2, SEQ, D_MIX), F32),
                   jax.ShapeDtypeStruct((HY_ORDER, 8, D_MIX), F32)],
        scratch_shapes=[pltpu.VMEM((D_MIX // LANES, SEQ, LANES), F32)] * 2
        + [pltpu.VMEM((SEQ, HY_FFN), F32)],
        compiler_params=_params("arbitrary"),
        name="hyena_filter",
    )(*consts, w3, table)


def _short_conv(u, w, b):
    t = lax.broadcasted_iota(jnp.int32, u.shape, 0)
    prev = jnp.where(t == 0, 0.0, pltpu.roll(u, 1, 0))
    nxt = jnp.where(t == SEQ - 1, 0.0, pltpu.roll(u, SEQ - 1, 0))
    return prev * w[0:1] + u * w[1:2] + nxt * w[2:3] + b


def _hyena_stage_kernel(gate_ref, src_ref, wg_ref, bg_ref, ws_ref, bs_ref, cs_ref,
                        kf_ref, kny_ref, skip_ref, out_ref, z_s, yre_s, yim_s, gate_s, rest_s,
                        *, conv_src):
    for q in range(HY_SEQS):
        seq = slice(q * SEQ, (q + 1) * SEQ)
        zq, yre_q, yim_q = z_s.at[q], yre_s.at[q], yim_s.at[q]
        z = src_ref[seq, :]
        if conv_src:
            z = _short_conv(z, ws_ref[...], bs_ref[...])
        _store_slabs(zq, z)
        t = lax.broadcasted_iota(jnp.int32, z.shape, 0)
        sgn = (1 - 2 * (t & 1)).astype(F32)
        nyq = jnp.sum(z * sgn, axis=0, keepdims=True) * kny_ref[0:1]
        gate = _short_conv(gate_ref[seq, :], wg_ref[...], bg_ref[...])
        gate_s[seq, :] = gate
        rest_s[seq, :] = gate * (sgn * nyq + z * skip_ref[...])
        a, b = _dft_fwd(cs_ref, zq)
        for half in range(2):
            rows = slice(half * HALF, (half + 1) * HALF)
            kre = kf_ref[0, rows, :]
            kim = kf_ref[1, rows, :]
            _store_slabs(yre_q, a[half] * kre + b[half] * kim, rows)
            _store_slabs(yim_q, a[half] * kim - b[half] * kre, rows)
        for half, y in enumerate(_dft_inv(cs_ref, yre_q, yim_q)):
            rows = slice(q * SEQ + half * HALF, q * SEQ + (half + 1) * HALF)
            out_ref[rows, :] = (gate_s[rows, :] * y + rest_s[rows, :]).astype(out_ref.dtype)


def _hyena_stage(u_hy, gate_blk, src, src_blk, conv_w, conv_b, table, kf, kny, order, skip,
                 conv_src, out_dtype):
    rows = HY_SEQS * SEQ
    of_order = lambda a: pl.BlockSpec(
        (None,) + a.shape[1:], lambda b: (order,) + (0,) * (a.ndim - 1),
        pipeline_mode=pl.Buffered(1))
    col = lambda blk: pl.BlockSpec((rows, D_MIX), lambda b: (b, blk))
    wcol = lambda blk, r: pl.BlockSpec((r, D_MIX), lambda b: (0, blk))
    ws_blk = src_blk if conv_src else 0
    return pl.pallas_call(
        functools.partial(_hyena_stage_kernel, conv_src=conv_src),
        grid=(BATCH // HY_SEQS,),
        in_specs=[col(gate_blk), col(src_blk), wcol(gate_blk, 3), wcol(gate_blk, 1),
                  wcol(ws_blk, 3), wcol(ws_blk, 1), _const_spec(table.shape),
                  of_order(kf), of_order(kny), _const_spec(skip.shape)],
        out_specs=pl.BlockSpec((rows, D_MIX), lambda b: (b, 0)),
        out_shape=jax.ShapeDtypeStruct((TOKENS, D_MIX), out_dtype),
        scratch_shapes=[pltpu.VMEM((HY_SEQS, D_MIX // LANES, SEQ, LANES), F32)] * 3
        + [pltpu.VMEM((rows, D_MIX), F32)] * 2,
        compiler_params=_params("parallel"),
        name="hyena_stage",
    )(u_hy, src, conv_w, conv_b, conv_w, conv_b, table, kf, kny, skip)


def _fnet_kernel(xa_ref, xb_ref, cs_ref, cg_ref, sg_ref, out_ref):
    parts = []
    for r in range(2):
        xb = _residue_rows((xa_ref, xb_ref), r, 2).astype(BF16)
        xc = _dot(xb, cg_ref[...]).astype(BF16)
        xs = _dot(xb, sg_ref[...]).astype(BF16)
        parts.append(_dot(_table_cols(cs_ref, r, 2), jnp.concatenate([xc, -xs], axis=0)))
    scale = (SEQ * D_MIX // FN_GROUPS) ** -0.5
    out_ref[:HALF, :] = ((parts[0] + parts[1]) * scale).astype(out_ref.dtype)
    out_ref[HALF:, :] = ((parts[0] - parts[1]) * scale).astype(out_ref.dtype)


def _fnet(u_fn, table, cg, sg):
    return pl.pallas_call(
        _fnet_kernel,
        grid=(BATCH,),
        in_specs=[pl.BlockSpec((SEQ, LANES), lambda b: (b, 0)),
                  pl.BlockSpec((SEQ, LANES), lambda b: (b, 1)), _const_spec(table.shape),
                  _const_spec(cg.shape), _const_spec(sg.shape)],
        out_specs=pl.BlockSpec((SEQ, D_MIX), lambda b: (b, 0)),
        out_shape=jax.ShapeDtypeStruct((TOKENS, D_MIX), BF16),
        compiler_params=_params("parallel"),
        name="fnet",
    )(u_fn, u_fn, table, cg, sg)


def _softmax2_pv(s2, v):
    m = jnp.max(s2, axis=-1, keepdims=True)
    p = jnp.exp2(s2 - m)
    l = jnp.sum(p, axis=-1, keepdims=True)
    return _dot(p.astype(BF16), v) / l


def _mla_kernel(q_ref, k_ref, vt_ref, out_ref):
    outs = []
    for h in range(MLA_HEADS):
        sl = slice(h * MLA_HEAD_PAD, (h + 1) * MLA_HEAD_PAD)
        s2 = _dot_nt(k_ref[:, sl], q_ref[:, sl])
        p = jnp.exp2(s2 - jnp.max(s2, axis=0, keepdims=True))
        l = jnp.sum(p, axis=0, keepdims=True)
        o = _dot(vt_ref[h * MLA_V:(h + 1) * MLA_V, :], p.astype(BF16))
        outs.append(o / l)
    out_ref[...] = jnp.concatenate(outs, axis=0).T.astype(out_ref.dtype)


def _mla(q, k, vt):
    tm = TILE["mla"]
    nt = SEQ // tm
    return pl.pallas_call(
        _mla_kernel,
        grid=(BATCH, nt),
        in_specs=[pl.BlockSpec((tm, MLA_QK), lambda b, i: (b * nt + i, 0)),
                  pl.BlockSpec((SEQ, MLA_QK), lambda b, i: (b, 0)),
                  pl.BlockSpec((None, D_MIX, SEQ), lambda b, i: (b, 0, 0))],
        out_specs=pl.BlockSpec((tm, D_MIX), lambda b, i: (b * nt + i, 0)),
        out_shape=jax.ShapeDtypeStruct((TOKENS, D_MIX), BF16),
        compiler_params=_params("parallel", "parallel"),
        name="mla_attention",
    )(q, k, vt)


def _na_key_row0(j):
    return jnp.clip(j * NA_QROWS - NA_WIN_R // 2, 0, GRID_R - NA_KROWS)


NA_PAIRS = 2 * NA_WIN_R


def _na_kernel(q_ref, k_ref, v_ref, tile_ref, out_ref, bias_s):
    j = pl.program_id(0)
    krow0 = _na_key_row0(j)

    @pl.when(pl.program_id(1) == 0)
    def _():
        rq = j * NA_QROWS + lax.broadcasted_iota(jnp.int32, (NA_QBLK, NA_KBLK), 0) // GRID_W
        rk = krow0 + lax.broadcasted_iota(jnp.int32, (NA_QBLK, NA_KBLK), 1) // GRID_W
        rs = jnp.clip(rq - NA_WIN_R // 2, 0, GRID_R - NA_WIN_R)
        rowmask = jnp.where(rk < rs, NEG_BIG, jnp.where(rk >= rs + NA_WIN_R, NEG_BIG, 0.0))
        base = krow0 - j * NA_QROWS + NA_WIN_R
        for h in range(NA_HEADS):
            bias = jnp.concatenate(
                [jnp.concatenate(
                    [tile_ref[h, jnp.clip(base + 2 * kp - r, 0, NA_PAIRS - 1)]
                     for kp in range(NA_KROWS // 2)], axis=1)
                 for r in range(NA_QROWS)], axis=0)
            bias_s[h] = bias + rowmask

    off = pl.multiple_of(krow0 * GRID_W, GRID_W)
    q = q_ref[...]
    k = k_ref[pl.ds(off, NA_KBLK), :]
    v = v_ref[pl.ds(off, NA_KBLK), :]
    head = lax.broadcasted_iota(jnp.int32, (NA_QBLK, D_MIX), 1) // NA_HEAD_DIM
    acc = jnp.zeros((NA_QBLK, D_MIX), F32)
    for h in range(NA_HEADS):
        qh = jnp.where(head == h, q, jnp.zeros_like(q))
        s2 = _dot_nt(qh, k) + bias_s[h]
        acc = jnp.where(head == h, _softmax2_pv(s2, v), acc)
    out_ref[...] = acc.astype(out_ref.dtype)


def _neighborhood(q, k, v, tiles):
    nj = SEQ // NA_QBLK
    return pl.pallas_call(
        _na_kernel,
        grid=(nj, BATCH),
        in_specs=[pl.BlockSpec((NA_QBLK, D_MIX), lambda j, b: (b * nj + j, 0)),
                  pl.BlockSpec((SEQ, D_MIX), lambda j, b: (b, 0)),
                  pl.BlockSpec((SEQ, D_MIX), lambda j, b: (b, 0)),
                  _const_spec(tiles.shape)],
        out_specs=pl.BlockSpec((NA_QBLK, D_MIX), lambda j, b: (b * nj + j, 0)),
        out_shape=jax.ShapeDtypeStruct((TOKENS, D_MIX), BF16),
        scratch_shapes=[pltpu.VMEM((NA_HEADS, NA_QBLK, NA_KBLK), F32)],
        compiler_params=_params("parallel", "arbitrary"),
        name="neighborhood_attention",
    )(q, k, v, tiles)


def _na_bias_tiles(rpb):
    c = jnp.arange(GRID_W)
    cs = jnp.clip(c - NA_WIN_C // 2, 0, GRID_W - NA_WIN_C)
    col_ok = (c[None, :] >= cs[:, None]) & (c[None, :] < cs[:, None] + NA_WIN_C)
    dc = jnp.clip(c[None, :] - c[:, None] + (NA_WIN_C - 1), 0, 2 * NA_WIN_C - 2)
    pick = (dc[None] == jnp.arange(2 * NA_WIN_C - 1)[:, None, None]).astype(F32)
    t = jnp.einsum('hrd,dqk->hrqk', rpb.astype(F32), pick, precision=lax.Precision.HIGHEST)
    t = jnp.where(col_ok, t * LOG2E, NEG_BIG)
    t = jnp.pad(t, ((0, 0), (1, 1), (0, 0), (0, 0)))
    return jnp.concatenate([t[:, :-1], t[:, 1:]], axis=-1)


def _merge_kernel(x_ref, g1_ref, yhy_ref, yfn_ref, ymla_ref, yna_ref, wg_ref, bg_ref,
                  wbr_ref, wout_ref, g2_ref, wr_ref, out_ref, hb_ref, logit_ref):
    x = x_ref[...]
    h = _rms(x, g1_ref[...]).astype(BF16)
    merged = jnp.zeros(x.shape, F32)
    for n, y_ref in enumerate((yhy_ref, yfn_ref, ymla_ref, yna_ref)):
        sl = slice(n * D_MODEL, (n + 1) * D_MODEL)
        gate = jax.nn.sigmoid(_dot_nt(h, wg_ref[sl, :]) + bg_ref[:, sl])
        merged = merged + gate * _dot(y_ref[...], wbr_ref[n])
    x1 = x + _dot(merged.astype(BF16), wout_ref[...])
    out_ref[...] = x1
    h_hi, h_lo = _split2(_rms(x1, g2_ref[...]))
    hb_ref[...] = h_hi
    w_hi, w_lo = _split2(wr_ref[...])
    logits = _dot_nt(w_hi, h_hi) + _dot_nt(w_hi, h_lo) + _dot_nt(w_lo, h_hi)
    logit_ref[...] = logits[:N_EXPERTS]


def _merge(x, g1, yhy, yfn, ymla, yna, wg, bg, wbr, wout, g2, wr_t):
    tm = TILE["merge"]
    nt = SEQ // tm
    row = lambda n: pl.BlockSpec((tm, n), lambda i: (i, 0))
    return pl.pallas_call(
        _merge_kernel,
        grid=(TOKENS // tm,),
        in_specs=[row(D_MODEL), _const_spec(g1.shape), row(D_MIX), row(D_MIX), row(D_MIX),
                  row(D_MIX), _const_spec(wg.shape), _const_spec(bg.shape),
                  _const_spec(wbr.shape), _const_spec(wout.shape), _const_spec(g2.shape),
                  _const_spec(wr_t.shape)],
        out_specs=[row(D_MODEL), row(D_MODEL),
                   pl.BlockSpec((None, N_EXPERTS, tm), lambda i: (i // nt, 0, i % nt))],
        out_shape=[jax.ShapeDtypeStruct((TOKENS, D_MODEL), F32),
                   jax.ShapeDtypeStruct((TOKENS, D_MODEL), BF16),
                   jax.ShapeDtypeStruct((BATCH, N_EXPERTS, SEQ), F32)],
        compiler_params=_params("parallel"),
        name="merge",
    )(x, g1, yhy, yfn, ymla, yna, wg, bg, wbr, wout, g2, wr_t)


def _prefix_count(m):
    r = lax.broadcasted_iota(jnp.int32, (LANES, LANES), 0)
    c = lax.broadcasted_iota(jnp.int32, (LANES, LANES), 1)
    upper = jnp.where(r < c, 1.0, 0.0).astype(BF16)
    run = jnp.zeros((m.shape[0], 1), F32)
    parts = []
    for i in range(SEQ // LANES):
        chunk = m[:, i * LANES:(i + 1) * LANES]
        parts.append(_dot(chunk.astype(BF16), upper) + run)
        run = run + jnp.sum(chunk, axis=1, keepdims=True)
    return jnp.concatenate(parts, axis=1)


SELECT_MAX_ITERS = 192


def _select_kernel(logit_ref, slot_row_ref, slot_col_ref, w_row_ref, start_ref, slot_s, w_s):
    b = pl.program_id(0)
    rows = BATCH * N_EXPERTS

    @pl.when(b == 0)
    def _():
        logits = logit_ref[...]
        ex = jnp.exp(logits - jnp.max(logits, axis=1, keepdims=True))
        aff = (ex / jnp.sum(ex, axis=1, keepdims=True)).reshape(rows, SEQ)

        def bisect(c):
            it, lo, hi, _ = c
            mid = 0.5 * (lo + hi)
            cnt = jnp.sum(jnp.where(aff >= mid, 1.0, 0.0), axis=1, keepdims=True)
            moving = jnp.where(mid == lo, 0.0, jnp.where(mid == hi, 0.0, 1.0))
            enough = cnt >= CAPACITY
            return (it + 1, jnp.where(enough, mid, lo), jnp.where(enough, hi, mid),
                    (jnp.max(moving) > 0).astype(jnp.int32))

        _, lo, hi, _ = lax.while_loop(
            lambda c: (c[0] < SELECT_MAX_ITERS) & (c[3] > 0), bisect,
            (jnp.int32(0), jnp.zeros((rows, 1), F32), jnp.full((rows, 1), 2.0, F32),
             jnp.int32(1)))
        above = jnp.where(aff >= hi, 1.0, 0.0)
        band = jnp.where(aff >= lo, 1.0, 0.0) - above
        need = CAPACITY - jnp.sum(above, axis=1, keepdims=True)
        sel = above + band * jnp.where(_prefix_count(band) < need, 1.0, 0.0)
        slot_s[...] = jnp.where(sel > 0, _prefix_count(sel), -1.0)
        w_s[...] = sel * aff

    r0 = pl.multiple_of(b * N_EXPERTS, N_EXPERTS)
    slot = slot_s[pl.ds(r0, N_EXPERTS), :]
    slot_row_ref[...] = slot.astype(jnp.int32)
    pad = jnp.full((LANES - N_EXPERTS, SEQ), -1.0, F32)
    slot_col_ref[...] = jnp.concatenate([slot, pad], axis=0).T.astype(jnp.int32)
    w_row_ref[...] = w_s[pl.ds(r0, N_EXPERTS), :]
    tm = MOE_FINE
    token = lax.broadcasted_iota(jnp.int32, slot.shape, 1)
    lane = lax.broadcasted_iota(jnp.int32, (N_EXPERTS, LANES), 1)
    starts = jnp.where(lane == SEQ // tm, float(CAPACITY), 0.0)
    for i in range(1, SEQ // tm):
        before = jnp.where(slot >= 0, jnp.where(token < i * tm, 1.0, 0.0), 0.0)
        starts = starts + jnp.where(lane == i, jnp.sum(before, axis=1, keepdims=True), 0.0)
    start_ref[...] = starts.astype(jnp.int32)


def _select(logits):
    return pl.pallas_call(
        _select_kernel,
        grid=(BATCH,),
        in_specs=[_const_spec(logits.shape)],
        out_specs=[pl.BlockSpec((None, N_EXPERTS, SEQ), lambda b: (b, 0, 0)),
                   pl.BlockSpec((None, SEQ, LANES), lambda b: (b, 0, 0)),
                   pl.BlockSpec((None, N_EXPERTS, SEQ), lambda b: (b, 0, 0)),
                   pl.BlockSpec((None, N_EXPERTS, LANES), lambda b: (b, 0, 0))],
        out_shape=[jax.ShapeDtypeStruct((BATCH, N_EXPERTS, SEQ), jnp.int32),
                   jax.ShapeDtypeStruct((BATCH, SEQ, LANES), jnp.int32),
                   jax.ShapeDtypeStruct((BATCH, N_EXPERTS, SEQ), F32),
                   jax.ShapeDtypeStruct((BATCH, N_EXPERTS, LANES), jnp.int32)],
        scratch_shapes=[pltpu.VMEM((BATCH * N_EXPERTS, SEQ), F32),
                        pltpu.VMEM((BATCH * N_EXPERTS, SEQ), F32)],
        compiler_params=_params("arbitrary"),
        name="expert_select",
    )(logits)


MXU_DEPTH = 256
BF16_ROWS = 16


def _slot_window(tokens):
    return 2 * tokens * CAPACITY // SEQ


def _tile_windows(start_ref, b, i, tokens):
    stride = start_ref.shape[0] // (BATCH * N_EXPERTS)
    fine = tokens // MOE_FINE
    window = _slot_window(tokens)
    base, fits = [], None
    for e in range(N_EXPERTS):
        at = (b * N_EXPERTS + e) * stride + i * fine
        lo = jnp.minimum(start_ref[at] // BF16_ROWS * BF16_ROWS, CAPACITY - window)
        ok = start_ref[at + fine] - lo <= window
        fits = ok if fits is None else fits & ok
        base.append(pl.multiple_of(lo, BF16_ROWS))
    return base, fits


def _gather_kernel(start_ref, slot_ref, w_ref, hb_ref, xe_ref, wsl_ref):
    b = pl.program_id(0)
    i = pl.program_id(1)
    tokens = slot_ref.shape[1]

    @pl.when(i == 0)
    def _():
        xe_ref[...] = jnp.zeros(xe_ref.shape, BF16)
        wsl_ref[...] = jnp.zeros(wsl_ref.shape, F32)

    def add_rows(e, rows, hit, picked):
        xe_ref[e, rows, :] = xe_ref[e, rows, :] + picked.astype(BF16)
        wslot = jnp.sum(jnp.where(hit, w_ref[e:e + 1, :], 0.0), axis=1, keepdims=True)
        wsl_ref[e, rows, :] = wsl_ref[e, rows, :] + jnp.broadcast_to(wslot, (hit.shape[0], LANES))

    window = _slot_window(tokens)
    base, fits = _tile_windows(start_ref, b, i, tokens)

    @pl.when(fits)
    def _():
        c = lax.broadcasted_iota(jnp.int32, (window, tokens), 0)
        hits = [slot_ref[e:e + 1, :] - base[e] == c for e in range(N_EXPERTS)]
        onehot = jnp.concatenate([jnp.where(m, 1.0, 0.0).astype(BF16) for m in hits], axis=0)
        picked = _dot(onehot, hb_ref[...])
        for e in range(N_EXPERTS):
            add_rows(e, pl.ds(base[e], window), hits[e], picked[e * window:(e + 1) * window])

    @pl.when(jnp.logical_not(fits))
    def _():
        c = lax.broadcasted_iota(jnp.int32, (CAPACITY, tokens), 0)
        for e in range(N_EXPERTS):
            hit = slot_ref[e:e + 1, :] == c
            picked = _dot(jnp.where(hit, 1.0, 0.0).astype(BF16), hb_ref[...])
            add_rows(e, slice(None), hit, picked)


def _gather(slot_row, w_row, starts, hb):
    tm = TILE["gather"]
    nt = SEQ // tm
    grid_spec = pltpu.PrefetchScalarGridSpec(
        num_scalar_prefetch=1,
        grid=(BATCH, nt),
        in_specs=[pl.BlockSpec((None, N_EXPERTS, tm), lambda b, i, s: (b, 0, i)),
                  pl.BlockSpec((None, N_EXPERTS, tm), lambda b, i, s: (b, 0, i)),
                  pl.BlockSpec((tm, D_MODEL), lambda b, i, s: (b * nt + i, 0))],
        out_specs=[pl.BlockSpec((N_EXPERTS, None, CAPACITY, D_MODEL), lambda b, i, s: (0, b, 0, 0)),
                   pl.BlockSpec((N_EXPERTS, None, CAPACITY, LANES), lambda b, i, s: (0, b, 0, 0))])
    return pl.pallas_call(
        _gather_kernel,
        grid_spec=grid_spec,
        out_shape=[jax.ShapeDtypeStruct((N_EXPERTS, BATCH, CAPACITY, D_MODEL), BF16),
                   jax.ShapeDtypeStruct((N_EXPERTS, BATCH, CAPACITY, LANES), F32)],
        compiler_params=_params("parallel", "arbitrary"),
        name="expert_gather",
    )(starts[:, :, :SEQ // MOE_FINE + 1].reshape(-1), slot_row, w_row, hb)


def _expert_kernel(xe_ref, wsl_ref, wg_ref, wu_ref, wd_ref, ye_ref, wg_s, wu_s, wd_s, *, span):
    ph = pl.program_id(0)
    i = pl.program_id(1)
    rows = pl.ds(pl.multiple_of(i * span, span), span)
    nxt = ph % 2
    wg_s[nxt, rows, :] = wg_ref[rows, :].astype(BF16)
    wu_s[nxt, rows, :] = wu_ref[rows, :].astype(BF16)
    wd_s[nxt, rows, :] = wd_ref[rows, :].astype(BF16)

    @pl.when(ph > 0)
    def _():
        cur = (ph + 1) % 2
        xe = xe_ref[...]
        g = _dot(xe, wg_s[cur])
        u = _dot(xe, wu_s[cur])
        act = (g * jax.nn.sigmoid(g) * u).astype(BF16)
        ye_ref[...] = (_dot(act, wd_s[cur]) * wsl_ref[:, 0:1]).astype(BF16)


def _experts(xe, wsl, wg, wu, wd, layer):
    rows = BATCH * CAPACITY
    tm = TILE["expert"]
    nt = rows // tm
    last = N_EXPERTS - 1
    wspec = lambda a: pl.BlockSpec((None, None) + a.shape[2:],
                                   lambda ph, i: (layer, jnp.minimum(ph, last), 0, 0))
    data = lambda n: pl.BlockSpec(
        (None, tm, n), lambda ph, i: (jnp.maximum(ph - 1, 0), jnp.where(ph > 0, i, 0), 0))
    return pl.pallas_call(
        functools.partial(_expert_kernel, span=D_MODEL // nt),
        grid=(N_EXPERTS + 1, nt),
        in_specs=[data(D_MODEL), data(LANES), wspec(wg), wspec(wu), wspec(wd)],
        out_specs=data(D_MODEL),
        out_shape=jax.ShapeDtypeStruct((N_EXPERTS, rows, D_MODEL), BF16),
        scratch_shapes=[pltpu.VMEM((2, D_MODEL, D_FF), BF16), pltpu.VMEM((2, D_MODEL, D_FF), BF16),
                        pltpu.VMEM((2, D_FF, D_MODEL), BF16)],
        compiler_params=_params("arbitrary", "arbitrary"),
        name="expert_ffn",
    )(xe.reshape(N_EXPERTS, rows, D_MODEL), wsl.reshape(N_EXPERTS, rows, LANES), wg, wu, wd)


def _combine_kernel(start_ref, x_ref, ye_ref, slot_ref, p_ref, g3_ref, wpg_ref, wpp_ref,
                    gf_ref, out_ref, moe_s, *, final_norm):
    slot = slot_ref[...]
    rows = slot.shape[0]
    window = _slot_window(rows)
    base, fits = _tile_windows(start_ref, pl.program_id(0), pl.program_id(1), rows)

    @pl.when(fits)
    def _():
        c = lax.broadcasted_iota(jnp.int32, (rows, window), 1)
        parts = []
        group = MXU_DEPTH // window
        for e0 in range(0, N_EXPERTS, group):
            hot, win = [], []
            for e in range(e0, e0 + group):
                hot.append(jnp.where(slot[:, e:e + 1] - base[e] == c, 1.0, 0.0).astype(BF16))
                win.append(ye_ref[e, pl.ds(base[e], window), :])
            parts.append(_dot(jnp.concatenate(hot, axis=1), jnp.concatenate(win, axis=0)))
        while len(parts) > 1:
            parts = [u + v for u, v in zip(parts[::2], parts[1::2])]
        moe_s[...] = parts[0]

    @pl.when(jnp.logical_not(fits))
    def _():
        c = lax.broadcasted_iota(jnp.int32, (rows, CAPACITY), 1)
        onehot = jnp.concatenate(
            [jnp.where(slot[:, e:e + 1] == c, 1.0, 0.0).astype(BF16) for e in range(N_EXPERTS)],
            axis=1)
        moe_s[...] = _dot(onehot, ye_ref[...].reshape(N_EXPERTS * CAPACITY, D_MODEL))

    acc = x_ref[...] + moe_s[...]
    h = _rms(acc, g3_ref[...]).astype(BF16)
    gate = jax.nn.sigmoid(_dot(h, wpg_ref[...]))
    y = acc + gate * _dot(p_ref[...].astype(BF16), wpp_ref[...])
    if final_norm:
        y = _rms(y, gf_ref[...])
    out_ref[...] = y


def _combine(x, ye, slot_col, starts, p, layer, g3, wpg, wpp, gf, final_norm):
    tm = TILE["combine"]
    nt = SEQ // tm
    p0 = layer * (TOKENS // tm)
    const = lambda a: pl.BlockSpec(a.shape, lambda b, i, s: (0,) * a.ndim,
                                   pipeline_mode=pl.Buffered(1))
    grid_spec = pltpu.PrefetchScalarGridSpec(
        num_scalar_prefetch=1,
        grid=(BATCH, nt),
        in_specs=[pl.BlockSpec((tm, D_MODEL), lambda b, i, s: (b * nt + i, 0)),
                  pl.BlockSpec((N_EXPERTS, None, CAPACITY, D_MODEL), lambda b, i, s: (0, b, 0, 0)),
                  pl.BlockSpec((None, tm, LANES), lambda b, i, s: (b, i, 0)),
                  pl.BlockSpec((tm, PLE_DIM), lambda b, i, s: (p0 + b * nt + i, 0)),
                  const(g3), const(wpg), const(wpp), const(gf)],
        out_specs=pl.BlockSpec((tm, D_MODEL), lambda b, i, s: (b * nt + i, 0)),
        scratch_shapes=[pltpu.VMEM((tm, D_MODEL), F32)])
    return pl.pallas_call(
        functools.partial(_combine_kernel, final_norm=final_norm),
        grid_spec=grid_spec,
        out_shape=jax.ShapeDtypeStruct((TOKENS, D_MODEL), F32),
        compiler_params=_params("parallel", "parallel"),
        name="combine",
    )(starts[:, :, :SEQ // MOE_FINE + 1].reshape(-1), x,
      ye.reshape(N_EXPERTS, BATCH, CAPACITY, D_MODEL),
      slot_col, p, g3, wpg, wpp, gf)


DFT_FINE = 64
DFT_STEP = 4


def _dft_kernel(ca_ref, sa_ref, cb_ref, sb_ref, out_ref):
    cb = cb_ref[...]
    sb = sb_ref[...]
    for r in range(DFT_STEP):
        rows = slice(r * DFT_FINE, (r + 1) * DFT_FINE)
        out_ref[rows, :] = (ca_ref[r] * cb - sa_ref[r] * sb).astype(BF16)


def _dft_tables(n_points, nres):
    blk = SEQ // nres
    t = jnp.arange(SEQ, dtype=jnp.int32).reshape(blk, nres).T
    t = jnp.concatenate([t, t], axis=1).reshape(1, 2 * SEQ)
    is_sin = (jnp.arange(2 * SEQ) // blk % 2 == 1)[None, :]
    coarse = jnp.arange(HALF // DFT_FINE, dtype=jnp.int32)[:, None] * DFT_FINE
    fine = jnp.arange(DFT_FINE, dtype=jnp.int32)[:, None]
    ang = lambda f: ((f * t) % n_points).astype(F32) * (2.0 * math.pi / n_points)
    ca = jnp.cos(ang(coarse))[:, None, :]
    sa = jnp.sin(ang(coarse))[:, None, :]
    cb = jnp.where(is_sin, jnp.sin(ang(fine)), jnp.cos(ang(fine)))
    sb = jnp.where(is_sin, -jnp.cos(ang(fine)), jnp.sin(ang(fine)))
    rows = DFT_STEP * DFT_FINE
    return pl.pallas_call(
        _dft_kernel,
        grid=(HALF // rows,),
        in_specs=[pl.BlockSpec((DFT_STEP, 1, 2 * SEQ), lambda i: (i, 0, 0)),
                  pl.BlockSpec((DFT_STEP, 1, 2 * SEQ), lambda i: (i, 0, 0)),
                  _const_spec(cb.shape), _const_spec(sb.shape)],
        out_specs=pl.BlockSpec((rows, 2 * SEQ), lambda i: (i, 0)),
        out_shape=jax.ShapeDtypeStruct((HALF, 2 * SEQ), BF16),
        compiler_params=_params("parallel"),
        name="dft_tables",
    )(ca, sa, cb, sb)


def _fnet_group_tables():
    gc = D_MIX // FN_GROUPS
    i = lax.broadcasted_iota(jnp.int32, (D_MIX, D_MIX), 0)
    j = lax.broadcasted_iota(jnp.int32, (D_MIX, D_MIX), 1)
    same = (i // gc) == (j // gc)
    ang = (((i % gc) * (j % gc)) % gc).astype(F32) * (2.0 * math.pi / gc)
    return (jnp.where(same, jnp.cos(ang), 0.0).astype(BF16),
            jnp.where(same, jnp.sin(ang), 0.0).astype(BF16))


def _hyena_features():
    t01 = jnp.linspace(0.0, 1.0, SEQ, dtype=F32)[:, None]
    bands = jnp.linspace(1e-4, HY_BANDS - 1, HY_BANDS, dtype=F32)
    ang = 2.0 * math.pi * jnp.arange(SEQ, dtype=F32)[:, None] * bands / SEQ
    z = jnp.concatenate([t01, jnp.cos(ang), -jnp.sin(ang)], axis=-1)
    z = jnp.pad(z, ((0, 0), (0, LANES - HY_EMB)))
    max_decay = math.log(HY_TARGET) / HY_FAST_DECAY
    min_decay = math.log(HY_TARGET) / HY_SLOW_DECAY
    deltas = jnp.linspace(min_decay, max_decay, D_MIX, dtype=F32)
    window = jnp.exp(-t01 * jnp.abs(deltas))
    return z, window


def _rot_cols(w):
    half = w.shape[-1] // 2
    return jnp.concatenate([-w[..., half:], w[..., :half]], axis=-1)


def _rope_tables():
    inv = ROPE_THETA ** (-jnp.arange(0, MLA_ROPE, 2, dtype=F32) / MLA_ROPE)
    ang = jnp.arange(SEQ, dtype=F32)[:, None] * inv
    cos = jnp.concatenate([jnp.cos(ang), jnp.cos(ang)], axis=-1)
    sin = jnp.concatenate([jnp.sin(ang), jnp.sin(ang)], axis=-1)
    scale = (MLA_NOPE + MLA_ROPE) ** -0.5 * LOG2E
    pad = MLA_HEAD_PAD - MLA_NOPE - MLA_ROPE
    one = jnp.ones((SEQ, MLA_NOPE), F32)
    zero = jnp.zeros((SEQ, MLA_NOPE), F32)
    zpad = jnp.zeros((SEQ, pad), F32)
    cosq = jnp.tile(jnp.concatenate([one, cos, zpad], axis=-1) * scale, (1, MLA_HEADS))
    sinq = jnp.tile(jnp.concatenate([zero, sin, zpad], axis=-1) * scale, (1, MLA_HEADS))
    csk = jnp.concatenate([cos, sin, jnp.zeros((SEQ, LANES - 2 * MLA_ROPE), F32)], axis=-1)
    return cosq, sinq, csk


def _mla_weights(w_uq, w_ukv):
    pad = MLA_HEAD_PAD - MLA_NOPE - MLA_ROPE
    wq = w_uq.reshape(MLA_Q_RANK, MLA_HEADS, MLA_NOPE + MLA_ROPE)
    nope, pe = wq[..., :MLA_NOPE], wq[..., MLA_NOPE:]
    zp = jnp.zeros((MLA_Q_RANK, MLA_HEADS, pad), F32)
    wqa = jnp.concatenate([nope, pe, zp], axis=-1).reshape(MLA_Q_RANK, MLA_QK)
    wqb = jnp.concatenate([jnp.zeros_like(nope), _rot_cols(pe), zp], axis=-1)
    wqb = wqb.reshape(MLA_Q_RANK, MLA_QK)
    wkv = w_ukv.reshape(MLA_KV_RANK, MLA_HEADS, MLA_NOPE + MLA_V)
    knope, v = wkv[..., :MLA_NOPE], wkv[..., MLA_NOPE:]
    wk = jnp.concatenate(
        [knope, jnp.zeros((MLA_KV_RANK, MLA_HEADS, MLA_HEAD_PAD - MLA_NOPE), F32)], axis=-1)
    wk = wk.reshape(MLA_KV_RANK, MLA_QK)
    wv = v.reshape(MLA_KV_RANK, MLA_HEADS * MLA_V).T
    r = lax.broadcasted_iota(jnp.int32, (LANES, MLA_QK), 0)
    c = lax.broadcasted_iota(jnp.int32, (LANES, MLA_QK), 1)
    epe = jnp.where((r < 2 * MLA_ROPE) & (c % MLA_HEAD_PAD == MLA_NOPE + r % MLA_ROPE), 1.0, 0.0)
    return (wqa.astype(BF16), wqb.astype(BF16), wk.astype(BF16), wv.astype(BF16),
            epe.astype(BF16))


def kernel(x, p, norm1_g, w_in, b_gate, hy_conv_w, hy_conv_b, hf_w1, hf_b1, hf_freq, hf_w2,
           hf_b2, hf_w3, hy_skip, q_norm_g, w_uq, kv_norm_g, w_ukv, rpb, w_br, w_out, norm2_g,
           w_router, w_e_gate, w_e_up, w_e_down, norm3_g, w_ple_gate, w_ple_proj, final_g):
    conv_tab = _dft_tables(2 * SEQ, 4)
    fnet_tab = _dft_tables(SEQ, 2)
    fnet_cg, fnet_sg = _fnet_group_tables()
    zfeat, window = _hyena_features()
    cosq, sinq, csk = _rope_tables()
    row = lambda a: a.reshape(1, -1)

    xt = x.reshape(TOKENS, D_MODEL)
    w_in_t = jnp.swapaxes(w_in, 1, 2)
    for i in range(DEPTH):
        g1 = row(norm1_g[i])
        wa, wna, wgate = _inproj_weights(w_in_t, i)
        wqa, wqb, wk, wv, epe = _mla_weights(w_uq[i], w_ukv[i])

        u_hy, u_fn, q, k, v, naq, nak, nav = _inproj(
            xt, g1, wa, wna, row(q_norm_g[i]), wqa, wqb, row(kv_norm_g[i]), wk, wv, epe,
            cosq, sinq, csk)

        w1 = jnp.pad(hf_w1[i], ((0, LANES - HY_EMB), (0, 0)))
        kf, kny = _hyena_filter(zfeat, window, w1, row(hf_b1[i]), hf_freq[i], hf_w2[i],
                                row(hf_b2[i]), hf_w3[i], conv_tab)
        conv_b = row(hy_conv_b[i])
        z1 = _hyena_stage(u_hy, 0, u_hy, 2, hy_conv_w[i], conv_b, conv_tab, kf, kny, 0,
                          row(hy_skip[i, 0]), True, F32)
        y_hy = _hyena_stage(u_hy, 1, z1, 0, hy_conv_w[i], conv_b, conv_tab, kf, kny, 1,
                            row(hy_skip[i, 1]), False, BF16)
        y_fn = _fnet(u_fn, fnet_tab, fnet_cg, fnet_sg)
        y_mla = _mla(q, k, v)
        y_na = _neighborhood(naq, nak, nav, _na_bias_tiles(rpb[i]))
        wr_t = jnp.pad(w_router[i].T, ((0, LANES - N_EXPERTS), (0, 0)))
        xt, hb, logits = _merge(xt, g1, y_hy, y_fn, y_mla, y_na, wgate, row(b_gate[i]),
                                w_br[i].astype(BF16), w_out[i].astype(BF16),
                                row(norm2_g[i]), wr_t)
        slot_row, slot_col, w_row, starts = _select(logits)
        xe, wsl = _gather(slot_row, w_row, starts, hb)
        ye = _experts(xe, wsl, w_e_gate, w_e_up, w_e_down, i)
        xt = _combine(xt, ye, slot_col, starts, p.reshape(DEPTH * TOKENS, PLE_DIM), i,
                      row(norm3_g[i]), w_ple_gate[i].astype(BF16),
                      w_ple_proj[i].astype(BF16), row(final_g), i == DEPTH - 1)
    return xt.reshape(BATCH, SEQ, D_MODEL)
```

```python
import functools
import math

import jax
import jax.numpy as jnp
from jax import lax
from jax.experimental import pallas as pl
from jax.experimental.pallas import tpu as pltpu

F32 = jnp.float32
BF16 = jnp.bfloat16

D_MODEL = 1024
BATCH = 8
SEQ = 2048
DEPTH = 2
TOKENS = BATCH * SEQ

GRID_W = 64
GRID_R = SEQ // GRID_W
D_MIX = 256
N_BRANCH = 4
EPS = 1e-6
HY_ORDER = 2
HY_BANDS = 16
HY_EMB = 2 * HY_BANDS + 1
HY_FFN = 64
HY_TARGET = 1e-2
HY_FAST_DECAY = 0.3
HY_SLOW_DECAY = 1.5
FN_GROUPS = 4
MLA_HEADS = 4
MLA_NOPE = 64
MLA_ROPE = 32
MLA_V = 64
MLA_Q_RANK = 256
MLA_KV_RANK = 128
ROPE_THETA = 10000.0
NA_HEADS = 4
NA_HEAD_DIM = D_MIX // NA_HEADS
NA_WIN_R = 8
NA_WIN_C = 16
N_EXPERTS = 16
CAPACITY = 2 * SEQ // N_EXPERTS
D_FF = 1024
PLE_DIM = 256

HY_COLS = 3 * D_MIX
OFF_FN = HY_COLS
OFF_CQ = OFF_FN + D_MIX
OFF_CKV = OFF_CQ + MLA_Q_RANK
OFF_KPE = OFF_CKV + MLA_KV_RANK
OFF_NA = OFF_KPE + MLA_ROPE
OFF_GATE = OFF_NA + 3 * D_MIX

LANES = 128
MLA_HEAD_PAD = 128
MLA_QK = MLA_HEADS * MLA_HEAD_PAD
WA_COLS = 1536
NEG_BIG = -1e30
LOG2E = math.log2(math.e)

TILE = dict(inproj=512, mla=1024, merge=512, gather=256, expert=512, combine=512)
MOE_FINE = 256
NA_QROWS = 4
NA_KROWS = 12
NA_QBLK = NA_QROWS * GRID_W
NA_KBLK = NA_KROWS * GRID_W
VMEM_LIMIT = 56 * 1024 * 1024


def _params(*sem):
    return pltpu.CompilerParams(dimension_semantics=sem, vmem_limit_bytes=VMEM_LIMIT)


def _const_spec(shape):
    nd = len(shape)
    return pl.BlockSpec(shape, lambda *_: (0,) * nd, pipeline_mode=pl.Buffered(1))


def _rms(x, g):
    return x * lax.rsqrt(jnp.mean(x * x, axis=-1, keepdims=True) + EPS) * g


def _dot(a, b):
    return jnp.dot(a, b, preferred_element_type=F32)


def _dot_nt(a, b):
    return lax.dot_general(a, b, (((1,), (1,)), ((), ())), preferred_element_type=F32)


def _split2(x):
    hi = x.astype(BF16)
    lo = (x - hi.astype(F32)).astype(BF16)
    return hi, lo


def _inproj_kernel(x_ref, g1_ref, wa_ref, wna_ref, qg_ref, wqa_ref, wqb_ref, kvg_ref,
                   wk_ref, wv_ref, epe_ref, cosq_ref, sinq_ref, csk_ref,
                   uhy_ref, ufn_ref, q_ref, k_ref, v_ref, naq_ref, nak_ref, nav_ref):
    h = _rms(x_ref[...], g1_ref[...]).astype(BF16)
    ua = _dot_nt(h, wa_ref[...])
    uhy_ref[...] = ua[:, :HY_COLS]
    ufn_ref[...] = ua[:, OFF_FN:OFF_CQ]
    cqn = _rms(ua[:, OFF_CQ:OFF_CKV], qg_ref[...]).astype(BF16)
    q = _dot(cqn, wqa_ref[...]) * cosq_ref[...] + _dot(cqn, wqb_ref[...]) * sinq_ref[...]
    q_ref[...] = q.astype(BF16)
    kvn = _rms(ua[:, OFF_CKV:OFF_KPE], kvg_ref[...]).astype(BF16)
    kpe = ua[:, OFF_KPE:WA_COLS] * csk_ref[...]
    k = _dot(kvn, wk_ref[...]) + _dot(kpe.astype(BF16), epe_ref[...])
    k_ref[...] = k.astype(BF16)
    v_ref[...] = _dot_nt(wv_ref[...], kvn).astype(BF16)
    una = _dot_nt(h, wna_ref[...])
    naq_ref[...] = (una[:, :D_MIX] * (NA_HEAD_DIM ** -0.5 * LOG2E)).astype(BF16)
    nak_ref[...] = una[:, D_MIX:2 * D_MIX].astype(BF16)
    nav_ref[...] = una[:, 2 * D_MIX:].astype(BF16)


def _inproj(x, g1, wa, wna, qg, wqa, wqb, kvg, wk, wv, epe, cosq, sinq, csk):
    tm = TILE["inproj"]
    nt = SEQ // tm
    row = lambda n: pl.BlockSpec((tm, n), lambda i: (i, 0))
    pos = lambda n: pl.BlockSpec((tm, n), lambda i: (i % nt, 0))
    outs = [(HY_COLS, F32), (D_MIX, F32), (MLA_QK, BF16), (MLA_QK, BF16), None,
            (D_MIX, BF16), (D_MIX, BF16), (D_MIX, BF16)]
    vt_spec = pl.BlockSpec((None, D_MIX, tm), lambda i: (i // nt, 0, i % nt))
    vt_shape = jax.ShapeDtypeStruct((BATCH, D_MIX, SEQ), BF16)
    return pl.pallas_call(
        _inproj_kernel,
        grid=(TOKENS // tm,),
        in_specs=[row(D_MODEL), _const_spec(g1.shape), _const_spec(wa.shape),
                  _const_spec(wna.shape), _const_spec(qg.shape), _const_spec(wqa.shape),
                  _const_spec(wqb.shape), _const_spec(kvg.shape), _const_spec(wk.shape),
                  _const_spec(wv.shape), _const_spec(epe.shape),
                  pos(MLA_QK), pos(MLA_QK), pos(LANES)],
        out_specs=[vt_spec if o is None else row(o[0]) for o in outs],
        out_shape=[vt_shape if o is None else jax.ShapeDtypeStruct((TOKENS, o[0]), o[1])
                   for o in outs],
        compiler_params=_params("parallel"),
        name="inproj",
    )(x, g1, wa, wna, qg, wqa, wqb, kvg, wk, wv, epe, cosq, sinq, csk)


WPREP_ROWS = 512


def _inproj_weights_kernel(w_ref, wa_ref, wna_ref, wg_ref):
    i = pl.program_id(0)

    @pl.when(i == 0)
    def _():
        half = MLA_ROPE // 2
        wa_ref[:OFF_NA, :] = w_ref[:OFF_NA, :].astype(BF16)
        wa_ref[OFF_NA:OFF_NA + half, :] = (-w_ref[OFF_KPE + half:OFF_NA, :]).astype(BF16)
        wa_ref[OFF_NA + half:OFF_NA + MLA_ROPE, :] = w_ref[OFF_KPE:OFF_KPE + half, :].astype(BF16)
        wa_ref[OFF_NA + MLA_ROPE:, :] = jnp.zeros((WA_COLS - OFF_NA - MLA_ROPE, D_MODEL), BF16)
        wna_ref[...] = w_ref[OFF_NA:OFF_GATE, :].astype(BF16)

    start = pl.multiple_of(OFF_GATE + i * WPREP_ROWS, MLA_ROPE)
    wg_ref[...] = w_ref[pl.ds(start, WPREP_ROWS), :].astype(BF16)


def _inproj_weights(w_in_t, layer):
    n_in = w_in_t.shape[1]
    rows = (WA_COLS, OFF_GATE - OFF_NA, n_in - OFF_GATE)
    whole = lambda n: pl.BlockSpec((n, D_MODEL), lambda i: (0, 0))
    return pl.pallas_call(
        _inproj_weights_kernel,
        grid=(rows[2] // WPREP_ROWS,),
        in_specs=[pl.BlockSpec((None, n_in, D_MODEL), lambda i: (layer, 0, 0),
                               pipeline_mode=pl.Buffered(1))],
        out_specs=[whole(rows[0]), whole(rows[1]),
                   pl.BlockSpec((WPREP_ROWS, D_MODEL), lambda i: (i, 0))],
        out_shape=[jax.ShapeDtypeStruct((n, D_MODEL), BF16) for n in rows],
        compiler_params=_params("arbitrary"),
        name="inproj_weights",
    )(w_in_t)


HALF = SEQ // 2
HY_SEQS = 1


def _residue_rows(ref, r, nres):
    rows = pl.ds(r, SEQ // nres, stride=nres)
    if isinstance(ref, tuple):
        return jnp.concatenate([h[rows, :] for h in ref], axis=1)
    return jnp.concatenate([ref[j, rows, :] for j in range(ref.shape[0])], axis=1)


def _store_slabs(ref, value, rows=slice(None)):
    for j in range(ref.shape[0]):
        ref[j, rows, :] = value[:, j * LANES:(j + 1) * LANES]


def _table_cols(cs_ref, r, nres, part=None):
    blk = SEQ // nres
    lo = 2 * blk * r
    if part is None:
        return cs_ref[:, lo:lo + 2 * blk]
    return cs_ref[:, lo + part * blk:lo + (part + 1) * blk]


def _dft_fwd(cs_ref, src_ref, want_cos=True, want_sin=True, split=False):
    def prod(part, z, r):
        tab = _table_cols(cs_ref, r, 4, part)
        if split:
            hi, lo = _split2(z)
            return _dot(tab, hi) + _dot(tab, lo)
        return _dot(tab, z.astype(BF16))

    pc, ps = [None] * 4, [None] * 4
    for r in range(4):
        z = _residue_rows(src_ref, r, 4)
        odd = r % 2 == 1
        if want_cos or odd:
            pc[r] = prod(0, z, r)
        if want_sin or odd:
            ps[r] = prod(1, z, r)
    a = b = None
    if want_cos:
        a = ((pc[0] + pc[2]) + (pc[1] + pc[3]), (pc[0] - pc[2]) + (ps[3] - ps[1]))
    if want_sin:
        b = ((ps[0] + ps[2]) + (ps[1] + ps[3]), (ps[0] - ps[2]) + (pc[1] - pc[3]))
    return a, b


def _dft_inv(cs_ref, yre_ref, yim_ref):
    g, h = [], {}
    for r in range(4):
        yr = _residue_rows(yre_ref, r, 4).astype(BF16)
        yi = _residue_rows(yim_ref, r, 4).astype(BF16)
        tab = _table_cols(cs_ref, r, 4)
        g.append(_dot(tab, jnp.concatenate([yr, -yi], axis=0)))
        if r % 2 == 1:
            h[r] = _dot(tab, jnp.concatenate([yi, yr], axis=0))
    return (g[0] + g[2]) + (g[1] + g[3]), (g[0] - g[2]) + (h[3] - h[1])


def _hyena_filter_kernel(z_ref, win_ref, w1_ref, b1_ref, freq_ref, w2_ref, b2_ref, w3_ref,
                         cs_ref, kf_ref, kny_ref, ksum_s, kdif_s, trunk_s):
    hp = lax.Precision.HIGHEST

    @pl.when(pl.program_id(0) == 0)
    def _():
        freq = freq_ref[...]
        hf = jnp.sin(freq[0:1] * (jnp.dot(z_ref[...], w1_ref[...], precision=hp,
                                          preferred_element_type=F32) + b1_ref[...]))
        trunk_s[...] = jnp.sin(freq[1:2] * (jnp.dot(hf, w2_ref[...], precision=hp,
                                                    preferred_element_type=F32) + b2_ref[...]))

    hf = jnp.dot(trunk_s[...], w3_ref[...], precision=hp, preferred_element_type=F32)
    win = win_ref[...]
    t = lax.broadcasted_iota(jnp.int32, (SEQ, D_MIX), 0)
    sgn = (1 - 2 * (t & 1)).astype(F32)
    fwd = hf[:, :D_MIX] * win
    bwd = jnp.where(t == 0, 0.0, hf[:, D_MIX:] * win)
    nrm = lax.rsqrt(jnp.sum(fwd * fwd + bwd * bwd, axis=0, keepdims=True) + EPS)
    ksum = (fwd + bwd) * nrm
    _store_slabs(ksum_s, ksum)
    _store_slabs(kdif_s, (bwd - fwd) * nrm)
    kre, _ = _dft_fwd(cs_ref, ksum_s, want_sin=False, split=True)
    _, kim = _dft_fwd(cs_ref, kdif_s, want_cos=False, split=True)
    wf = 2.0 / (2 * SEQ)
    for part, spec in enumerate((kre, kim)):
        lo = spec[0] * wf
        kf_ref[part, :HALF, :] = lo
        kf_ref[part, 0:1, :] = lo[0:1] * 0.5
        kf_ref[part, HALF:, :] = spec[1] * wf
    kny = jnp.sum(ksum * sgn, axis=0, keepdims=True) * (1.0 / (2 * SEQ))
    kny_ref[...] = jnp.broadcast_to(kny, (8, D_MIX))


def _hyena_filter(zfeat, window, w1, b1, freq, w2, b2, w3, table):
    consts = (zfeat, window, w1, b1, freq, w2, b2)
    return pl.pallas_call(
        _hyena_filter_kernel,
        grid=(HY_ORDER,),
        in_specs=[_const_spec(a.shape) for a in consts]
        + [pl.BlockSpec((HY_FFN, 2 * D_MIX), lambda o: (0, o)),
           _const_spec(table.shape)],
        out_specs=[pl.BlockSpec((None, 2, SEQ, D_MIX), lambda o: (o, 0, 0, 0)),
                   pl.BlockSpec((None, 8, D_MIX), lambda o: (o, 0, 0))],
        out_shape=[jax.ShapeDtypeStruct((HY_ORDER, 2, SEQ, D_MIX), F32),
                   jax.ShapeDtypeStruct((HY_ORDER, 8, D_MIX), F32)],
        scratch_shapes=[pltpu.VMEM((D_MIX // LANES, SEQ, LANES), F32)] * 2
        + [pltpu.VMEM((SEQ, HY_FFN), F32)],
        compiler_params=_params("arbitrary"),
        name="hyena_filter",
    )(*consts, w3, table)


def _short_conv(u, w, b):
    t = lax.broadcasted_iota(jnp.int32, u.shape, 0)
    prev = jnp.where(t == 0, 0.0, pltpu.roll(u, 1, 0))
    nxt = jnp.where(t == SEQ - 1, 0.0, pltpu.roll(u, SEQ - 1, 0))
    return prev * w[0:1] + u * w[1:2] + nxt * w[2:3] + b


def _hyena_stage_kernel(gate_ref, src_ref, wg_ref, bg_ref, ws_ref, bs_ref, cs_ref,
                        kf_ref, kny_ref, skip_ref, out_ref, z_s, yre_s, yim_s, gate_s, rest_s,
                        *, conv_src):
    for q in range(HY_SEQS):
        seq = slice(q * SEQ, (q + 1) * SEQ)
        zq, yre_q, yim_q = z_s.at[q], yre_s.at[q], yim_s.at[q]
        z = src_ref[seq, :]
        if conv_src:
            z = _short_conv(z, ws_ref[...], bs_ref[...])
        _store_slabs(zq, z)
        t = lax.broadcasted_iota(jnp.int32, z.shape, 0)
        sgn = (1 - 2 * (t & 1)).astype(F32)
        nyq = jnp.sum(z * sgn, axis=0, keepdims=True) * kny_ref[0:1]
        gate = _short_conv(gate_ref[seq, :], wg_ref[...], bg_ref[...])
        gate_s[seq, :] = gate
        rest_s[seq, :] = gate * (sgn * nyq + z * skip_ref[...])
        a, b = _dft_fwd(cs_ref, zq)
        for half in range(2):
            rows = slice(half * HALF, (half + 1) * HALF)
            kre = kf_ref[0, rows, :]
            kim = kf_ref[1, rows, :]
            _store_slabs(yre_q, a[half] * kre + b[half] * kim, rows)
            _store_slabs(yim_q, a[half] * kim - b[half] * kre, rows)
        for half, y in enumerate(_dft_inv(cs_ref, yre_q, yim_q)):
            rows = slice(q * SEQ + half * HALF, q * SEQ + (half + 1) * HALF)
            out_ref[rows, :] = (gate_s[rows, :] * y + rest_s[rows, :]).astype(out_ref.dtype)


def _hyena_stage(u_hy, gate_blk, src, src_blk, conv_w, conv_b, table, kf, kny, order, skip,
                 conv_src, out_dtype):
    rows = HY_SEQS * SEQ
    of_order = lambda a: pl.BlockSpec(
        (None,) + a.shape[1:], lambda b: (order,) + (0,) * (a.ndim - 1),
        pipeline_mode=pl.Buffered(1))
    col = lambda blk: pl.BlockSpec((rows, D_MIX), lambda b: (b, blk))
    wcol = lambda blk, r: pl.BlockSpec((r, D_MIX), lambda b: (0, blk))
    ws_blk = src_blk if conv_src else 0
    return pl.pallas_call(
        functools.partial(_hyena_stage_kernel, conv_src=conv_src),
        grid=(BATCH // HY_SEQS,),
        in_specs=[col(gate_blk), col(src_blk), wcol(gate_blk, 3), wcol(gate_blk, 1),
                  wcol(ws_blk, 3), wcol(ws_blk, 1), _const_spec(table.shape),
                  of_order(kf), of_order(kny), _const_spec(skip.shape)],
        out_specs=pl.BlockSpec((rows, D_MIX), lambda b: (b, 0)),
        out_shape=jax.ShapeDtypeStruct((TOKENS, D_MIX), out_dtype),
        scratch_shapes=[pltpu.VMEM((HY_SEQS, D_MIX // LANES, SEQ, LANES), F32)] * 3
        + [pltpu.VMEM((rows, D_MIX), F32)] * 2,
        compiler_params=_params("parallel"),
        name="hyena_stage",
    )(u_hy, src, conv_w, conv_b, conv_w, conv_b, table, kf, kny, skip)


def _fnet_kernel(xa_ref, xb_ref, cs_ref, cg_ref, sg_ref, out_ref):
    parts = []
    for r in range(2):
        xb = _residue_rows((xa_ref, xb_ref), r, 2).astype(BF16)
        xc = _dot(xb, cg_ref[...]).astype(BF16)
        xs = _dot(xb, sg_ref[...]).astype(BF16)
        parts.append(_dot(_table_cols(cs_ref, r, 2), jnp.concatenate([xc, -xs], axis=0)))
    scale = (SEQ * D_MIX // FN_GROUPS) ** -0.5
    out_ref[:HALF, :] = ((parts[0] + parts[1]) * scale).astype(out_ref.dtype)
    out_ref[HALF:, :] = ((parts[0] - parts[1]) * scale).astype(out_ref.dtype)


def _fnet(u_fn, table, cg, sg):
    return pl.pallas_call(
        _fnet_kernel,
        grid=(BATCH,),
        in_specs=[pl.BlockSpec((SEQ, LANES), lambda b: (b, 0)),
                  pl.BlockSpec((SEQ, LANES), lambda b: (b, 1)), _const_spec(table.shape),
                  _const_spec(cg.shape), _const_spec(sg.shape)],
        out_specs=pl.BlockSpec((SEQ, D_MIX), lambda b: (b, 0)),
        out_shape=jax.ShapeDtypeStruct((TOKENS, D_MIX), BF16),
        compiler_params=_params("parallel"),
        name="fnet",
    )(u_fn, u_fn, table, cg, sg)


def _softmax2_pv(s2, v):
    m = jnp.max(s2, axis=-1, keepdims=True)
    p = jnp.exp2(s2 - m)
    l = jnp.sum(p, axis=-1, keepdims=True)
    return _dot(p.astype(BF16), v) / l


def _mla_kernel(q_ref, k_ref, vt_ref, out_ref):
    outs = []
    for h in range(MLA_HEADS):
        sl = slice(h * MLA_HEAD_PAD, (h + 1) * MLA_HEAD_PAD)
        s2 = _dot_nt(k_ref[:, sl], q_ref[:, sl])
        p = jnp.exp2(s2 - jnp.max(s2, axis=0, keepdims=True))
        l = jnp.sum(p, axis=0, keepdims=True)
        o = _dot(vt_ref[h * MLA_V:(h + 1) * MLA_V, :], p.astype(BF16))
        outs.append(o / l)
    out_ref[...] = jnp.concatenate(outs, axis=0).T.astype(out_ref.dtype)


def _mla(q, k, vt):
    tm = TILE["mla"]
    nt = SEQ // tm
    return pl.pallas_call(
        _mla_kernel,
        grid=(BATCH, nt),
        in_specs=[pl.BlockSpec((tm, MLA_QK), lambda b, i: (b * nt + i, 0)),
                  pl.BlockSpec((SEQ, MLA_QK), lambda b, i: (b, 0)),
                  pl.BlockSpec((None, D_MIX, SEQ), lambda b, i: (b, 0, 0))],
        out_specs=pl.BlockSpec((tm, D_MIX), lambda b, i: (b * nt + i, 0)),
        out_shape=jax.ShapeDtypeStruct((TOKENS, D_MIX), BF16),
        compiler_params=_params("parallel", "parallel"),
        name="mla_attention",
    )(q, k, vt)


def _na_key_row0(j):
    return jnp.clip(j * NA_QROWS - NA_WIN_R // 2, 0, GRID_R - NA_KROWS)


NA_PAIRS = 2 * NA_WIN_R


def _na_kernel(q_ref, k_ref, v_ref, tile_ref, out_ref, bias_s):
    j = pl.program_id(0)
    krow0 = _na_key_row0(j)

    @pl.when(pl.program_id(1) == 0)
    def _():
        rq = j * NA_QROWS + lax.broadcasted_iota(jnp.int32, (NA_QBLK, NA_KBLK), 0) // GRID_W
        rk = krow0 + lax.broadcasted_iota(jnp.int32, (NA_QBLK, NA_KBLK), 1) // GRID_W
        rs = jnp.clip(rq - NA_WIN_R // 2, 0, GRID_R - NA_WIN_R)
        rowmask = jnp.where(rk < rs, NEG_BIG, jnp.where(rk >= rs + NA_WIN_R, NEG_BIG, 0.0))
        base = krow0 - j * NA_QROWS + NA_WIN_R
        for h in range(NA_HEADS):
            bias = jnp.concatenate(
                [jnp.concatenate(
                    [tile_ref[h, jnp.clip(base + 2 * kp - r, 0, NA_PAIRS - 1)]
                     for kp in range(NA_KROWS // 2)], axis=1)
                 for r in range(NA_QROWS)], axis=0)
            bias_s[h] = bias + rowmask

    off = pl.multiple_of(krow0 * GRID_W, GRID_W)
    q = q_ref[...]
    k = k_ref[pl.ds(off, NA_KBLK), :]
    v = v_ref[pl.ds(off, NA_KBLK), :]
    head = lax.broadcasted_iota(jnp.int32, (NA_QBLK, D_MIX), 1) // NA_HEAD_DIM
    acc = jnp.zeros((NA_QBLK, D_MIX), F32)
    for h in range(NA_HEADS):
        qh = jnp.where(head == h, q, jnp.zeros_like(q))
        s2 = _dot_nt(qh, k) + bias_s[h]
        acc = jnp.where(head == h, _softmax2_pv(s2, v), acc)
    out_ref[...] = acc.astype(out_ref.dtype)


def _neighborhood(q, k, v, tiles):
    nj = SEQ // NA_QBLK
    return pl.pallas_call(
        _na_kernel,
        grid=(nj, BATCH),
        in_specs=[pl.BlockSpec((NA_QBLK, D_MIX), lambda j, b: (b * nj + j, 0)),
                  pl.BlockSpec((SEQ, D_MIX), lambda j, b: (b, 0)),
                  pl.BlockSpec((SEQ, D_MIX), lambda j, b: (b, 0)),
                  _const_spec(tiles.shape)],
        out_specs=pl.BlockSpec((NA_QBLK, D_MIX), lambda j, b: (b * nj + j, 0)),
        out_shape=jax.ShapeDtypeStruct((TOKENS, D_MIX), BF16),
        scratch_shapes=[pltpu.VMEM((NA_HEADS, NA_QBLK, NA_KBLK), F32)],
        compiler_params=_params("parallel", "arbitrary"),
        name="neighborhood_attention",
    )(q, k, v, tiles)


def _na_bias_tiles(rpb):
    c = jnp.arange(GRID_W)
    cs = jnp.clip(c - NA_WIN_C // 2, 0, GRID_W - NA_WIN_C)
    col_ok = (c[None, :] >= cs[:, None]) & (c[None, :] < cs[:, None] + NA_WIN_C)
    dc = jnp.clip(c[None, :] - c[:, None] + (NA_WIN_C - 1), 0, 2 * NA_WIN_C - 2)
    pick = (dc[None] == jnp.arange(2 * NA_WIN_C - 1)[:, None, None]).astype(F32)
    t = jnp.einsum('hrd,dqk->hrqk', rpb.astype(F32), pick, precision=lax.Precision.HIGHEST)
    t = jnp.where(col_ok, t * LOG2E, NEG_BIG)
    t = jnp.pad(t, ((0, 0), (1, 1), (0, 0), (0, 0)))
    return jnp.concatenate([t[:, :-1], t[:, 1:]], axis=-1)


def _merge_kernel(x_ref, g1_ref, yhy_ref, yfn_ref, ymla_ref, yna_ref, wg_ref, bg_ref,
                  wbr_ref, wout_ref, g2_ref, wr_ref, out_ref, hb_ref, logit_ref):
    x = x_ref[...]
    h = _rms(x, g1_ref[...]).astype(BF16)
    merged = jnp.zeros(x.shape, F32)
    for n, y_ref in enumerate((yhy_ref, yfn_ref, ymla_ref, yna_ref)):
        sl = slice(n * D_MODEL, (n + 1) * D_MODEL)
        gate = jax.nn.sigmoid(_dot_nt(h, wg_ref[sl, :]) + bg_ref[:, sl])
        merged = merged + gate * _dot(y_ref[...], wbr_ref[n])
    x1 = x + _dot(merged.astype(BF16), wout_ref[...])
    out_ref[...] = x1
    h_hi, h_lo = _split2(_rms(x1, g2_ref[...]))
    hb_ref[...] = h_hi
    w_hi, w_lo = _split2(wr_ref[...])
    logits = _dot_nt(w_hi, h_hi) + _dot_nt(w_hi, h_lo) + _dot_nt(w_lo, h_hi)
    logit_ref[...] = logits[:N_EXPERTS]


def _merge(x, g1, yhy, yfn, ymla, yna, wg, bg, wbr, wout, g2, wr_t):
    tm = TILE["merge"]
    nt = SEQ // tm
    row = lambda n: pl.BlockSpec((tm, n), lambda i: (i, 0))
    return pl.pallas_call(
        _merge_kernel,
        grid=(TOKENS // tm,),
        in_specs=[row(D_MODEL), _const_spec(g1.shape), row(D_MIX), row(D_MIX), row(D_MIX),
                  row(D_MIX), _const_spec(wg.shape), _const_spec(bg.shape),
                  _const_spec(wbr.shape), _const_spec(wout.shape), _const_spec(g2.shape),
                  _const_spec(wr_t.shape)],
        out_specs=[row(D_MODEL), row(D_MODEL),
                   pl.BlockSpec((None, N_EXPERTS, tm), lambda i: (i // nt, 0, i % nt))],
        out_shape=[jax.ShapeDtypeStruct((TOKENS, D_MODEL), F32),
                   jax.ShapeDtypeStruct((TOKENS, D_MODEL), BF16),
                   jax.ShapeDtypeStruct((BATCH, N_EXPERTS, SEQ), F32)],
        compiler_params=_params("parallel"),
        name="merge",
    )(x, g1, yhy, yfn, ymla, yna, wg, bg, wbr, wout, g2, wr_t)


def _prefix_count(m):
    r = lax.broadcasted_iota(jnp.int32, (LANES, LANES), 0)
    c = lax.broadcasted_iota(jnp.int32, (LANES, LANES), 1)
    upper = jnp.where(r < c, 1.0, 0.0).astype(BF16)
    run = jnp.zeros((m.shape[0], 1), F32)
    parts = []
    for i in range(SEQ // LANES):
        chunk = m[:, i * LANES:(i + 1) * LANES]
        parts.append(_dot(chunk.astype(BF16), upper) + run)
        run = run + jnp.sum(chunk, axis=1, keepdims=True)
    return jnp.concatenate(parts, axis=1)


SELECT_MAX_ITERS = 192


def _select_kernel(logit_ref, slot_row_ref, slot_col_ref, w_row_ref, start_ref, slot_s, w_s):
    b = pl.program_id(0)
    rows = BATCH * N_EXPERTS

    @pl.when(b == 0)
    def _():
        logits = logit_ref[...]
        ex = jnp.exp(logits - jnp.max(logits, axis=1, keepdims=True))
        aff = (ex / jnp.sum(ex, axis=1, keepdims=True)).reshape(rows, SEQ)

        def bisect(c):
            it, lo, hi, _ = c
            mid = 0.5 * (lo + hi)
            cnt = jnp.sum(jnp.where(aff >= mid, 1.0, 0.0), axis=1, keepdims=True)
            moving = jnp.where(mid == lo, 0.0, jnp.where(mid == hi, 0.0, 1.0))
            enough = cnt >= CAPACITY
            return (it + 1, jnp.where(enough, mid, lo), jnp.where(enough, hi, mid),
                    (jnp.max(moving) > 0).astype(jnp.int32))

        _, lo, hi, _ = lax.while_loop(
            lambda c: (c[0] < SELECT_MAX_ITERS) & (c[3] > 0), bisect,
            (jnp.int32(0), jnp.zeros((rows, 1), F32), jnp.full((rows, 1), 2.0, F32),
             jnp.int32(1)))
        above = jnp.where(aff >= hi, 1.0, 0.0)
        band = jnp.where(aff >= lo, 1.0, 0.0) - above
        need = CAPACITY - jnp.sum(above, axis=1, keepdims=True)
        sel = above + band * jnp.where(_prefix_count(band) < need, 1.0, 0.0)
        slot_s[...] = jnp.where(sel > 0, _prefix_count(sel), -1.0)
        w_s[...] = sel * aff

    r0 = pl.multiple_of(b * N_EXPERTS, N_EXPERTS)
    slot = slot_s[pl.ds(r0, N_EXPERTS), :]
    slot_row_ref[...] = slot.astype(jnp.int32)
    pad = jnp.full((LANES - N_EXPERTS, SEQ), -1.0, F32)
    slot_col_ref[...] = jnp.concatenate([slot, pad], axis=0).T.astype(jnp.int32)
    w_row_ref[...] = w_s[pl.ds(r0, N_EXPERTS), :]
    tm = MOE_FINE
    token = lax.broadcasted_iota(jnp.int32, slot.shape, 1)
    lane = lax.broadcasted_iota(jnp.int32, (N_EXPERTS, LANES), 1)
    starts = jnp.where(lane == SEQ // tm, float(CAPACITY), 0.0)
    for i in range(1, SEQ // tm):
        before = jnp.where(slot >= 0, jnp.where(token < i * tm, 1.0, 0.0), 0.0)
        starts = starts + jnp.where(lane == i, jnp.sum(before, axis=1, keepdims=True), 0.0)
    start_ref[...] = starts.astype(jnp.int32)


def _select(logits):
    return pl.pallas_call(
        _select_kernel,
        grid=(BATCH,),
        in_specs=[_const_spec(logits.shape)],
        out_specs=[pl.BlockSpec((None, N_EXPERTS, SEQ), lambda b: (b, 0, 0)),
                   pl.BlockSpec((None, SEQ, LANES), lambda b: (b, 0, 0)),
                   pl.BlockSpec((None, N_EXPERTS, SEQ), lambda b: (b, 0, 0)),
                   pl.BlockSpec((None, N_EXPERTS, LANES), lambda b: (b, 0, 0))],
        out_shape=[jax.ShapeDtypeStruct((BATCH, N_EXPERTS, SEQ), jnp.int32),
                   jax.ShapeDtypeStruct((BATCH, SEQ, LANES), jnp.int32),
                   jax.ShapeDtypeStruct((BATCH, N_EXPERTS, SEQ), F32),
                   jax.ShapeDtypeStruct((BATCH, N_EXPERTS, LANES), jnp.int32)],
        scratch_shapes=[pltpu.VMEM((BATCH * N_EXPERTS, SEQ), F32),
                        pltpu.VMEM((BATCH * N_EXPERTS, SEQ), F32)],
        compiler_params=_params("arbitrary"),
        name="expert_select",
    )(logits)


MXU_DEPTH = 256
BF16_ROWS = 16


def _slot_window(tokens):
    return 2 * tokens * CAPACITY // SEQ


def _tile_windows(start_ref, b, i, tokens):
    stride = start_ref.shape[0] // (BATCH * N_EXPERTS)
    fine = tokens // MOE_FINE
    window = _slot_window(tokens)
    base, fits = [], None
    for e in range(N_EXPERTS):
        at = (b * N_EXPERTS + e) * stride + i * fine
        lo = jnp.minimum(start_ref[at] // BF16_ROWS * BF16_ROWS, CAPACITY - window)
        ok = start_ref[at + fine] - lo <= window
        fits = ok if fits is None else fits & ok
        base.append(pl.multiple_of(lo, BF16_ROWS))
    return base, fits


def _gather_kernel(start_ref, slot_ref, w_ref, hb_ref, xe_ref, wsl_ref):
    b = pl.program_id(0)
    i = pl.program_id(1)
    tokens = slot_ref.shape[1]

    @pl.when(i == 0)
    def _():
        xe_ref[...] = jnp.zeros(xe_ref.shape, BF16)
        wsl_ref[...] = jnp.zeros(wsl_ref.shape, F32)

    def add_rows(e, rows, hit, picked):
        xe_ref[e, rows, :] = xe_ref[e, rows, :] + picked.astype(BF16)
        wslot = jnp.sum(jnp.where(hit, w_ref[e:e + 1, :], 0.0), axis=1, keepdims=True)
        wsl_ref[e, rows, :] = wsl_ref[e, rows, :] + jnp.broadcast_to(wslot, (hit.shape[0], LANES))

    window = _slot_window(tokens)
    base, fits = _tile_windows(start_ref, b, i, tokens)

    @pl.when(fits)
    def _():
        c = lax.broadcasted_iota(jnp.int32, (window, tokens), 0)
        hits = [slot_ref[e:e + 1, :] - base[e] == c for e in range(N_EXPERTS)]
        onehot = jnp.concatenate([jnp.where(m, 1.0, 0.0).astype(BF16) for m in hits], axis=0)
        picked = _dot(onehot, hb_ref[...])
        for e in range(N_EXPERTS):
            add_rows(e, pl.ds(base[e], window), hits[e], picked[e * window:(e + 1) * window])

    @pl.when(jnp.logical_not(fits))
    def _():
        c = lax.broadcasted_iota(jnp.int32, (CAPACITY, tokens), 0)
        for e in range(N_EXPERTS):
            hit = slot_ref[e:e + 1, :] == c
            picked = _dot(jnp.where(hit, 1.0, 0.0).astype(BF16), hb_ref[...])
            add_rows(e, slice(None), hit, picked)


def _gather(slot_row, w_row, starts, hb):
    tm = TILE["gather"]
    nt = SEQ // tm
    grid_spec = pltpu.PrefetchScalarGridSpec(
        num_scalar_prefetch=1,
        grid=(BATCH, nt),
        in_specs=[pl.BlockSpec((None, N_EXPERTS, tm), lambda b, i, s: (b, 0, i)),
                  pl.BlockSpec((None, N_EXPERTS, tm), lambda b, i, s: (b, 0, i)),
                  pl.BlockSpec((tm, D_MODEL), lambda b, i, s: (b * nt + i, 0))],
        out_specs=[pl.BlockSpec((N_EXPERTS, None, CAPACITY, D_MODEL), lambda b, i, s: (0, b, 0, 0)),
                   pl.BlockSpec((N_EXPERTS, None, CAPACITY, LANES), lambda b, i, s: (0, b, 0, 0))])
    return pl.pallas_call(
        _gather_kernel,
        grid_spec=grid_spec,
        out_shape=[jax.ShapeDtypeStruct((N_EXPERTS, BATCH, CAPACITY, D_MODEL), BF16),
                   jax.ShapeDtypeStruct((N_EXPERTS, BATCH, CAPACITY, LANES), F32)],
        compiler_params=_params("parallel", "arbitrary"),
        name="expert_gather",
    )(starts[:, :, :SEQ // MOE_FINE + 1].reshape(-1), slot_row, w_row, hb)


def _expert_kernel(xe_ref, wsl_ref, wg_ref, wu_ref, wd_ref, ye_ref, wg_s, wu_s, wd_s, *, span):
    ph = pl.program_id(0)
    i = pl.program_id(1)
    rows = pl.ds(pl.multiple_of(i * span, span), span)
    nxt = ph % 2
    wg_s[nxt, rows, :] = wg_ref[rows, :].astype(BF16)
    wu_s[nxt, rows, :] = wu_ref[rows, :].astype(BF16)
    wd_s[nxt, rows, :] = wd_ref[rows, :].astype(BF16)

    @pl.when(ph > 0)
    def _():
        cur = (ph + 1) % 2
        xe = xe_ref[...]
        g = _dot(xe, wg_s[cur])
        u = _dot(xe, wu_s[cur])
        act = (g * jax.nn.sigmoid(g) * u).astype(BF16)
        ye_ref[...] = (_dot(act, wd_s[cur]) * wsl_ref[:, 0:1]).astype(BF16)


def _experts(xe, wsl, wg, wu, wd, layer):
    rows = BATCH * CAPACITY
    tm = TILE["expert"]
    nt = rows // tm
    last = N_EXPERTS - 1
    wspec = lambda a: pl.BlockSpec((None, None) + a.shape[2:],
                                   lambda ph, i: (layer, jnp.minimum(ph, last), 0, 0))
    data = lambda n: pl.BlockSpec(
        (None, tm, n), lambda ph, i: (jnp.maximum(ph - 1, 0), jnp.where(ph > 0, i, 0), 0))
    return pl.pallas_call(
        functools.partial(_expert_kernel, span=D_MODEL // nt),
        grid=(N_EXPERTS + 1, nt),
        in_specs=[data(D_MODEL), data(LANES), wspec(wg), wspec(wu), wspec(wd)],
        out_specs=data(D_MODEL),
        out_shape=jax.ShapeDtypeStruct((N_EXPERTS, rows, D_MODEL), BF16),
        scratch_shapes=[pltpu.VMEM((2, D_MODEL, D_FF), BF16), pltpu.VMEM((2, D_MODEL, D_FF), BF16),
                        pltpu.VMEM((2, D_FF, D_MODEL), BF16)],
        compiler_params=_params("arbitrary", "arbitrary"),
        name="expert_ffn",
    )(xe.reshape(N_EXPERTS, rows, D_MODEL), wsl.reshape(N_EXPERTS, rows, LANES), wg, wu, wd)


def _combine_kernel(start_ref, x_ref, ye_ref, slot_ref, p_ref, g3_ref, wpg_ref, wpp_ref,
                    gf_ref, out_ref, moe_s, *, final_norm):
    window = _slot_window(MOE_FINE)
    group = MXU_DEPTH // window
    for q in range(slot_ref.shape[0] // MOE_FINE):
        sub = slice(q * MOE_FINE, (q + 1) * MOE_FINE)
        slot = slot_ref[sub, :]
        tile = pl.program_id(1) * (slot_ref.shape[0] // MOE_FINE) + q
        base, fits = _tile_windows(start_ref, pl.program_id(0), tile, MOE_FINE)

        @pl.when(fits)
        def _():
            c = lax.broadcasted_iota(jnp.int32, (MOE_FINE, window), 1)
            parts = []
            for e0 in range(0, N_EXPERTS, group):
                hot, win = [], []
                for e in range(e0, e0 + group):
                    hot.append(jnp.where(slot[:, e:e + 1] - base[e] == c, 1.0, 0.0).astype(BF16))
                    win.append(ye_ref[e, pl.ds(base[e], window), :])
                parts.append(_dot(jnp.concatenate(hot, axis=1), jnp.concatenate(win, axis=0)))
            while len(parts) > 1:
                parts = [u + v for u, v in zip(parts[::2], parts[1::2])]
            moe_s[sub, :] = parts[0]

        @pl.when(jnp.logical_not(fits))
        def _():
            c = lax.broadcasted_iota(jnp.int32, (MOE_FINE, CAPACITY), 1)
            onehot = jnp.concatenate(
                [jnp.where(slot[:, e:e + 1] == c, 1.0, 0.0).astype(BF16)
                 for e in range(N_EXPERTS)], axis=1)
            moe_s[sub, :] = _dot(onehot, ye_ref[...].reshape(N_EXPERTS * CAPACITY, D_MODEL))

    acc = x_ref[...] + moe_s[...]
    h = _rms(acc, g3_ref[...]).astype(BF16)
    gate = jax.nn.sigmoid(_dot(h, wpg_ref[...]))
    y = acc + gate * _dot(p_ref[...].astype(BF16), wpp_ref[...])
    if final_norm:
        y = _rms(y, gf_ref[...])
    out_ref[...] = y


def _combine(x, ye, slot_col, starts, p, layer, g3, wpg, wpp, gf, final_norm):
    tm = TILE["combine"]
    nt = SEQ // tm
    p0 = layer * (TOKENS // tm)
    const = lambda a: pl.BlockSpec(a.shape, lambda b, i, s: (0,) * a.ndim,
                                   pipeline_mode=pl.Buffered(1))
    grid_spec = pltpu.PrefetchScalarGridSpec(
        num_scalar_prefetch=1,
        grid=(BATCH, nt),
        in_specs=[pl.BlockSpec((tm, D_MODEL), lambda b, i, s: (b * nt + i, 0)),
                  pl.BlockSpec((N_EXPERTS, None, CAPACITY, D_MODEL), lambda b, i, s: (0, b, 0, 0)),
                  pl.BlockSpec((None, tm, LANES), lambda b, i, s: (b, i, 0)),
                  pl.BlockSpec((tm, PLE_DIM), lambda b, i, s: (p0 + b * nt + i, 0)),
                  const(g3), const(wpg), const(wpp), const(gf)],
        out_specs=pl.BlockSpec((tm, D_MODEL), lambda b, i, s: (b * nt + i, 0)),
        scratch_shapes=[pltpu.VMEM((tm, D_MODEL), F32)])
    return pl.pallas_call(
        functools.partial(_combine_kernel, final_norm=final_norm),
        grid_spec=grid_spec,
        out_shape=jax.ShapeDtypeStruct((TOKENS, D_MODEL), F32),
        compiler_params=_params("parallel", "parallel"),
        name="combine",
    )(starts[:, :, :SEQ // MOE_FINE + 1].reshape(-1), x,
      ye.reshape(N_EXPERTS, BATCH, CAPACITY, D_MODEL),
      slot_col, p, g3, wpg, wpp, gf)


DFT_FINE = 64
DFT_STEP = 4


def _dft_kernel(ca_ref, sa_ref, cb_ref, sb_ref, out_ref):
    cb = cb_ref[...]
    sb = sb_ref[...]
    for r in range(DFT_STEP):
        rows = slice(r * DFT_FINE, (r + 1) * DFT_FINE)
        out_ref[rows, :] = (ca_ref[r] * cb - sa_ref[r] * sb).astype(BF16)


def _dft_tables(n_points, nres):
    blk = SEQ // nres
    t = jnp.arange(SEQ, dtype=jnp.int32).reshape(blk, nres).T
    t = jnp.concatenate([t, t], axis=1).reshape(1, 2 * SEQ)
    is_sin = (jnp.arange(2 * SEQ) // blk % 2 == 1)[None, :]
    coarse = jnp.arange(HALF // DFT_FINE, dtype=jnp.int32)[:, None] * DFT_FINE
    fine = jnp.arange(DFT_FINE, dtype=jnp.int32)[:, None]
    ang = lambda f: ((f * t) % n_points).astype(F32) * (2.0 * math.pi / n_points)
    ca = jnp.cos(ang(coarse))[:, None, :]
    sa = jnp.sin(ang(coarse))[:, None, :]
    cb = jnp.where(is_sin, jnp.sin(ang(fine)), jnp.cos(ang(fine)))
    sb = jnp.where(is_sin, -jnp.cos(ang(fine)), jnp.sin(ang(fine)))
    rows = DFT_STEP * DFT_FINE
    return pl.pallas_call(
        _dft_kernel,
        grid=(HALF // rows,),
        in_specs=[pl.BlockSpec((DFT_STEP, 1, 2 * SEQ), lambda i: (i, 0, 0)),
                  pl.BlockSpec((DFT_STEP, 1, 2 * SEQ), lambda i: (i, 0, 0)),
                  _const_spec(cb.shape), _const_spec(sb.shape)],
        out_specs=pl.BlockSpec((rows, 2 * SEQ), lambda i: (i, 0)),
        out_shape=jax.ShapeDtypeStruct((HALF, 2 * SEQ), BF16),
        compiler_params=_params("parallel"),
        name="dft_tables",
    )(ca, sa, cb, sb)


def _fnet_group_tables():
    gc = D_MIX // FN_GROUPS
    i = lax.broadcasted_iota(jnp.int32, (D_MIX, D_MIX), 0)
    j = lax.broadcasted_iota(jnp.int32, (D_MIX, D_MIX), 1)
    same = (i // gc) == (j // gc)
    ang = (((i % gc) * (j % gc)) % gc).astype(F32) * (2.0 * math.pi / gc)
    return (jnp.where(same, jnp.cos(ang), 0.0).astype(BF16),
            jnp.where(same, jnp.sin(ang), 0.0).astype(BF16))


def _hyena_features():
    t01 = jnp.linspace(0.0, 1.0, SEQ, dtype=F32)[:, None]
    bands = jnp.linspace(1e-4, HY_BANDS - 1, HY_BANDS, dtype=F32)
    ang = 2.0 * math.pi * jnp.arange(SEQ, dtype=F32)[:, None] * bands / SEQ
    z = jnp.concatenate([t01, jnp.cos(ang), -jnp.sin(ang)], axis=-1)
    z = jnp.pad(z, ((0, 0), (0, LANES - HY_EMB)))
    max_decay = math.log(HY_TARGET) / HY_FAST_DECAY
    min_decay = math.log(HY_TARGET) / HY_SLOW_DECAY
    deltas = jnp.linspace(min_decay, max_decay, D_MIX, dtype=F32)
    window = jnp.exp(-t01 * jnp.abs(deltas))
    return z, window


def _rot_cols(w):
    half = w.shape[-1] // 2
    return jnp.concatenate([-w[..., half:], w[..., :half]], axis=-1)


def _rope_tables():
    inv = ROPE_THETA ** (-jnp.arange(0, MLA_ROPE, 2, dtype=F32) / MLA_ROPE)
    ang = jnp.arange(SEQ, dtype=F32)[:, None] * inv
    cos = jnp.concatenate([jnp.cos(ang), jnp.cos(ang)], axis=-1)
    sin = jnp.concatenate([jnp.sin(ang), jnp.sin(ang)], axis=-1)
    scale = (MLA_NOPE + MLA_ROPE) ** -0.5 * LOG2E
    pad = MLA_HEAD_PAD - MLA_NOPE - MLA_ROPE
    one = jnp.ones((SEQ, MLA_NOPE), F32)
    zero = jnp.zeros((SEQ, MLA_NOPE), F32)
    zpad = jnp.zeros((SEQ, pad), F32)
    cosq = jnp.tile(jnp.concatenate([one, cos, zpad], axis=-1) * scale, (1, MLA_HEADS))
    sinq = jnp.tile(jnp.concatenate([zero, sin, zpad], axis=-1) * scale, (1, MLA_HEADS))
    csk = jnp.concatenate([cos, sin, jnp.zeros((SEQ, LANES - 2 * MLA_ROPE), F32)], axis=-1)
    return cosq, sinq, csk


def _mla_weights(w_uq, w_ukv):
    pad = MLA_HEAD_PAD - MLA_NOPE - MLA_ROPE
    wq = w_uq.reshape(MLA_Q_RANK, MLA_HEADS, MLA_NOPE + MLA_ROPE)
    nope, pe = wq[..., :MLA_NOPE], wq[..., MLA_NOPE:]
    zp = jnp.zeros((MLA_Q_RANK, MLA_HEADS, pad), F32)
    wqa = jnp.concatenate([nope, pe, zp], axis=-1).reshape(MLA_Q_RANK, MLA_QK)
    wqb = jnp.concatenate([jnp.zeros_like(nope), _rot_cols(pe), zp], axis=-1)
    wqb = wqb.reshape(MLA_Q_RANK, MLA_QK)
    wkv = w_ukv.reshape(MLA_KV_RANK, MLA_HEADS, MLA_NOPE + MLA_V)
    knope, v = wkv[..., :MLA_NOPE], wkv[..., MLA_NOPE:]
    wk = jnp.concatenate(
        [knope, jnp.zeros((MLA_KV_RANK, MLA_HEADS, MLA_HEAD_PAD - MLA_NOPE), F32)], axis=-1)
    wk = wk.reshape(MLA_KV_RANK, MLA_QK)
    wv = v.reshape(MLA_KV_RANK, MLA_HEADS * MLA_V).T
    r = lax.broadcasted_iota(jnp.int32, (LANES, MLA_QK), 0)
    c = lax.broadcasted_iota(jnp.int32, (LANES, MLA_QK), 1)
    epe = jnp.where((r < 2 * MLA_ROPE) & (c % MLA_HEAD_PAD == MLA_NOPE + r % MLA_ROPE), 1.0, 0.0)
    return (wqa.astype(BF16), wqb.astype(BF16), wk.astype(BF16), wv.astype(BF16),
            epe.astype(BF16))


def kernel(x, p, norm1_g, w_in, b_gate, hy_conv_w, hy_conv_b, hf_w1, hf_b1, hf_freq, hf_w2,
           hf_b2, hf_w3, hy_skip, q_norm_g, w_uq, kv_norm_g, w_ukv, rpb, w_br, w_out, norm2_g,
           w_router, w_e_gate, w_e_up, w_e_down, norm3_g, w_ple_gate, w_ple_proj, final_g):
    conv_tab = _dft_tables(2 * SEQ, 4)
    fnet_tab = _dft_tables(SEQ, 2)
    fnet_cg, fnet_sg = _fnet_group_tables()
    zfeat, window = _hyena_features()
    cosq, sinq, csk = _rope_tables()
    row = lambda a: a.reshape(1, -1)

    xt = x.reshape(TOKENS, D_MODEL)
    w_in_t = jnp.swapaxes(w_in, 1, 2)
    for i in range(DEPTH):
        g1 = row(norm1_g[i])
        wa, wna, wgate = _inproj_weights(w_in_t, i)
        wqa, wqb, wk, wv, epe = _mla_weights(w_uq[i], w_ukv[i])

        u_hy, u_fn, q, k, v, naq, nak, nav = _inproj(
            xt, g1, wa, wna, row(q_norm_g[i]), wqa, wqb, row(kv_norm_g[i]), wk, wv, epe,
            cosq, sinq, csk)

        w1 = jnp.pad(hf_w1[i], ((0, LANES - HY_EMB), (0, 0)))
        kf, kny = _hyena_filter(zfeat, window, w1, row(hf_b1[i]), hf_freq[i], hf_w2[i],
                                row(hf_b2[i]), hf_w3[i], conv_tab)
        conv_b = row(hy_conv_b[i])
        z1 = _hyena_stage(u_hy, 0, u_hy, 2, hy_conv_w[i], conv_b, conv_tab, kf, kny, 0,
                          row(hy_skip[i, 0]), True, F32)
        y_hy = _hyena_stage(u_hy, 1, z1, 0, hy_conv_w[i], conv_b, conv_tab, kf, kny, 1,
                            row(hy_skip[i, 1]), False, BF16)
        y_fn = _fnet(u_fn, fnet_tab, fnet_cg, fnet_sg)
        y_mla = _mla(q, k, v)
        y_na = _neighborhood(naq, nak, nav, _na_bias_tiles(rpb[i]))
        wr_t = jnp.pad(w_router[i].T, ((0, LANES - N_EXPERTS), (0, 0)))
        xt, hb, logits = _merge(xt, g1, y_hy, y_fn, y_mla, y_na, wgate, row(b_gate[i]),
                                w_br[i].astype(BF16), w_out[i].astype(BF16),
                                row(norm2_g[i]), wr_t)
        slot_row, slot_col, w_row, starts = _select(logits)
        xe, wsl = _gather(slot_row, w_row, starts, hb)
        ye = _experts(xe, wsl, w_e_gate, w_e_up, w_e_down, i)
        xt = _combine(xt, ye, slot_col, starts, p.reshape(DEPTH * TOKENS, PLE_DIM), i,
                      row(norm3_g[i]), w_ple_gate[i].astype(BF16),
                      w_ple_proj[i].astype(BF16), row(final_g), i == DEPTH - 1)
    return xt.reshape(BATCH, SEQ, D_MODEL)
```

```python
import functools
import math

import jax
import jax.numpy as jnp
from jax import lax
from jax.experimental import pallas as pl
from jax.experimental.pallas import tpu as pltpu

F32 = jnp.float32
BF16 = jnp.bfloat16

D_MODEL = 1024
BATCH = 8
SEQ = 2048
DEPTH = 2
TOKENS = BATCH * SEQ

GRID_W = 64
GRID_R = SEQ // GRID_W
D_MIX = 256
N_BRANCH = 4
EPS = 1e-6
HY_ORDER = 2
HY_BANDS = 16
HY_EMB = 2 * HY_BANDS + 1
HY_FFN = 64
HY_TARGET = 1e-2
HY_FAST_DECAY = 0.3
HY_SLOW_DECAY = 1.5
FN_GROUPS = 4
MLA_HEADS = 4
MLA_NOPE = 64
MLA_ROPE = 32
MLA_V = 64
MLA_Q_RANK = 256
MLA_KV_RANK = 128
ROPE_THETA = 10000.0
NA_HEADS = 4
NA_HEAD_DIM = D_MIX // NA_HEADS
NA_WIN_R = 8
NA_WIN_C = 16
N_EXPERTS = 16
CAPACITY = 2 * SEQ // N_EXPERTS
D_FF = 1024
PLE_DIM = 256

HY_COLS = 3 * D_MIX
OFF_FN = HY_COLS
OFF_CQ = OFF_FN + D_MIX
OFF_CKV = OFF_CQ + MLA_Q_RANK
OFF_KPE = OFF_CKV + MLA_KV_RANK
OFF_NA = OFF_KPE + MLA_ROPE
OFF_GATE = OFF_NA + 3 * D_MIX

LANES = 128
MLA_HEAD_PAD = 128
MLA_QK = MLA_HEADS * MLA_HEAD_PAD
WA_COLS = 1536
NEG_BIG = -1e30
LOG2E = math.log2(math.e)

TILE = dict(inproj=512, mla=1024, merge=512, gather=256, expert=1024, combine=512)
MOE_FINE = 256
NA_QROWS = 4
NA_KROWS = 12
NA_QBLK = NA_QROWS * GRID_W
NA_KBLK = NA_KROWS * GRID_W
VMEM_LIMIT = 56 * 1024 * 1024


def _params(*sem):
    return pltpu.CompilerParams(dimension_semantics=sem, vmem_limit_bytes=VMEM_LIMIT)


def _const_spec(shape):
    nd = len(shape)
    return pl.BlockSpec(shape, lambda *_: (0,) * nd, pipeline_mode=pl.Buffered(1))


def _rms(x, g):
    return x * lax.rsqrt(jnp.mean(x * x, axis=-1, keepdims=True) + EPS) * g


def _dot(a, b):
    return jnp.dot(a, b, preferred_element_type=F32)


def _dot_nt(a, b):
    return lax.dot_general(a, b, (((1,), (1,)), ((), ())), preferred_element_type=F32)


def _split2(x):
    hi = x.astype(BF16)
    lo = (x - hi.astype(F32)).astype(BF16)
    return hi, lo


def _inproj_kernel(x_ref, g1_ref, wa_ref, wna_ref, qg_ref, wqa_ref, wqb_ref, kvg_ref,
                   wk_ref, wv_ref, epe_ref, cosq_ref, sinq_ref, csk_ref,
                   uhy_ref, ufn_ref, q_ref, k_ref, v_ref, naq_ref, nak_ref, nav_ref):
    h = _rms(x_ref[...], g1_ref[...]).astype(BF16)
    ua = _dot_nt(h, wa_ref[...])
    uhy_ref[...] = ua[:, :HY_COLS]
    ufn_ref[...] = ua[:, OFF_FN:OFF_CQ]
    cqn = _rms(ua[:, OFF_CQ:OFF_CKV], qg_ref[...]).astype(BF16)
    q = _dot(cqn, wqa_ref[...]) * cosq_ref[...] + _dot(cqn, wqb_ref[...]) * sinq_ref[...]
    q_ref[...] = q.astype(BF16)
    kvn = _rms(ua[:, OFF_CKV:OFF_KPE], kvg_ref[...]).astype(BF16)
    kpe = ua[:, OFF_KPE:WA_COLS] * csk_ref[...]
    k = _dot(kvn, wk_ref[...]) + _dot(kpe.astype(BF16), epe_ref[...])
    k_ref[...] = k.astype(BF16)
    v_ref[...] = _dot_nt(wv_ref[...], kvn).astype(BF16)
    una = _dot_nt(h, wna_ref[...])
    naq_ref[...] = (una[:, :D_MIX] * (NA_HEAD_DIM ** -0.5 * LOG2E)).astype(BF16)
    nak_ref[...] = una[:, D_MIX:2 * D_MIX].astype(BF16)
    nav_ref[...] = una[:, 2 * D_MIX:].astype(BF16)


def _inproj(x, g1, wa, wna, qg, wqa, wqb, kvg, wk, wv, epe, cosq, sinq, csk):
    tm = TILE["inproj"]
    nt = SEQ // tm
    row = lambda n: pl.BlockSpec((tm, n), lambda i: (i, 0))
    pos = lambda n: pl.BlockSpec((tm, n), lambda i: (i % nt, 0))
    outs = [(HY_COLS, F32), (D_MIX, F32), (MLA_QK, BF16), (MLA_QK, BF16), None,
            (D_MIX, BF16), (D_MIX, BF16), (D_MIX, BF16)]
    vt_spec = pl.BlockSpec((None, D_MIX, tm), lambda i: (i // nt, 0, i % nt))
    vt_shape = jax.ShapeDtypeStruct((BATCH, D_MIX, SEQ), BF16)
    return pl.pallas_call(
        _inproj_kernel,
        grid=(TOKENS // tm,),
        in_specs=[row(D_MODEL), _const_spec(g1.shape), _const_spec(wa.shape),
                  _const_spec(wna.shape), _const_spec(qg.shape), _const_spec(wqa.shape),
                  _const_spec(wqb.shape), _const_spec(kvg.shape), _const_spec(wk.shape),
                  _const_spec(wv.shape), _const_spec(epe.shape),
                  pos(MLA_QK), pos(MLA_QK), pos(LANES)],
        out_specs=[vt_spec if o is None else row(o[0]) for o in outs],
        out_shape=[vt_shape if o is None else jax.ShapeDtypeStruct((TOKENS, o[0]), o[1])
                   for o in outs],
        compiler_params=_params("parallel"),
        name="inproj",
    )(x, g1, wa, wna, qg, wqa, wqb, kvg, wk, wv, epe, cosq, sinq, csk)


WPREP_ROWS = 512


def _inproj_weights_kernel(w_ref, wa_ref, wna_ref, wg_ref):
    i = pl.program_id(0)

    @pl.when(i == 0)
    def _():
        half = MLA_ROPE // 2
        wa_ref[:OFF_NA, :] = w_ref[:OFF_NA, :].astype(BF16)
        wa_ref[OFF_NA:OFF_NA + half, :] = (-w_ref[OFF_KPE + half:OFF_NA, :]).astype(BF16)
        wa_ref[OFF_NA + half:OFF_NA + MLA_ROPE, :] = w_ref[OFF_KPE:OFF_KPE + half, :].astype(BF16)
        wa_ref[OFF_NA + MLA_ROPE:, :] = jnp.zeros((WA_COLS - OFF_NA - MLA_ROPE, D_MODEL), BF16)
        wna_ref[...] = w_ref[OFF_NA:OFF_GATE, :].astype(BF16)

    start = pl.multiple_of(OFF_GATE + i * WPREP_ROWS, MLA_ROPE)
    wg_ref[...] = w_ref[pl.ds(start, WPREP_ROWS), :].astype(BF16)


def _inproj_weights(w_in_t, layer):
    n_in = w_in_t.shape[1]
    rows = (WA_COLS, OFF_GATE - OFF_NA, n_in - OFF_GATE)
    whole = lambda n: pl.BlockSpec((n, D_MODEL), lambda i: (0, 0))
    return pl.pallas_call(
        _inproj_weights_kernel,
        grid=(rows[2] // WPREP_ROWS,),
        in_specs=[pl.BlockSpec((None, n_in, D_MODEL), lambda i: (layer, 0, 0),
                               pipeline_mode=pl.Buffered(1))],
        out_specs=[whole(rows[0]), whole(rows[1]),
                   pl.BlockSpec((WPREP_ROWS, D_MODEL), lambda i: (i, 0))],
        out_shape=[jax.ShapeDtypeStruct((n, D_MODEL), BF16) for n in rows],
        compiler_params=_params("arbitrary"),
        name="inproj_weights",
    )(w_in_t)


HALF = SEQ // 2
HY_SEQS = 1


def _residue_rows(ref, r, nres):
    rows = pl.ds(r, SEQ // nres, stride=nres)
    if isinstance(ref, tuple):
        return jnp.concatenate([h[rows, :] for h in ref], axis=1)
    return jnp.concatenate([ref[j, rows, :] for j in range(ref.shape[0])], axis=1)


def _store_slabs(ref, value, rows=slice(None)):
    for j in range(ref.shape[0]):
        ref[j, rows, :] = value[:, j * LANES:(j + 1) * LANES]


def _table_cols(cs_ref, r, nres, part=None):
    blk = SEQ // nres
    lo = 2 * blk * r
    if part is None:
        return cs_ref[:, lo:lo + 2 * blk]
    return cs_ref[:, lo + part * blk:lo + (part + 1) * blk]


def _dft_fwd(cs_ref, src_ref, want_cos=True, want_sin=True, split=False):
    def prod(part, z, r):
        tab = _table_cols(cs_ref, r, 4, part)
        if split:
            hi, lo = _split2(z)
            return _dot(tab, hi) + _dot(tab, lo)
        return _dot(tab, z.astype(BF16))

    pc, ps = [None] * 4, [None] * 4
    for r in range(4):
        z = _residue_rows(src_ref, r, 4)
        odd = r % 2 == 1
        if want_cos or odd:
            pc[r] = prod(0, z, r)
        if want_sin or odd:
            ps[r] = prod(1, z, r)
    a = b = None
    if want_cos:
        a = ((pc[0] + pc[2]) + (pc[1] + pc[3]), (pc[0] - pc[2]) + (ps[3] - ps[1]))
    if want_sin:
        b = ((ps[0] + ps[2]) + (ps[1] + ps[3]), (ps[0] - ps[2]) + (pc[1] - pc[3]))
    return a, b


def _dft_inv(cs_ref, yre_ref, yim_ref):
    g, h = [], {}
    for r in range(4):
        yr = _residue_rows(yre_ref, r, 4).astype(BF16)
        yi = _residue_rows(yim_ref, r, 4).astype(BF16)
        tab = _table_cols(cs_ref, r, 4)
        g.append(_dot(tab, jnp.concatenate([yr, -yi], axis=0)))
        if r % 2 == 1:
            h[r] = _dot(tab, jnp.concatenate([yi, yr], axis=0))
    return (g[0] + g[2]) + (g[1] + g[3]), (g[0] - g[2]) + (h[3] - h[1])


def _hyena_filter_kernel(z_ref, win_ref, w1_ref, b1_ref, freq_ref, w2_ref, b2_ref, w3_ref,
                         cs_ref, kf_ref, kny_ref, ksum_s, kdif_s, trunk_s):
    hp = lax.Precision.HIGHEST

    @pl.when(pl.program_id(0) == 0)
    def _():
        freq = freq_ref[...]
        hf = jnp.sin(freq[0:1] * (jnp.dot(z_ref[...], w1_ref[...], precision=hp,
                                          preferred_element_type=F32) + b1_ref[...]))
        trunk_s[...] = jnp.sin(freq[1:2] * (jnp.dot(hf, w2_ref[...], precision=hp,
                                                    preferred_element_type=F32) + b2_ref[...]))

    hf = jnp.dot(trunk_s[...], w3_ref[...], precision=hp, preferred_element_type=F32)
    win = win_ref[...]
    t = lax.broadcasted_iota(jnp.int32, (SEQ, D_MIX), 0)
    sgn = (1 - 2 * (t & 1)).astype(F32)
    fwd = hf[:, :D_MIX] * win
    bwd = jnp.where(t == 0, 0.0, hf[:, D_MIX:] * win)
    nrm = lax.rsqrt(jnp.sum(fwd * fwd + bwd * bwd, axis=0, keepdims=True) + EPS)
    ksum = (fwd + bwd) * nrm
    _store_slabs(ksum_s, ksum)
    _store_slabs(kdif_s, (bwd - fwd) * nrm)
    kre, _ = _dft_fwd(cs_ref, ksum_s, want_sin=False, split=True)
    _, kim = _dft_fwd(cs_ref, kdif_s, want_cos=False, split=True)
    wf = 2.0 / (2 * SEQ)
    for part, spec in enumerate((kre, kim)):
        lo = spec[0] * wf
        kf_ref[part, :HALF, :] = lo
        kf_ref[part, 0:1, :] = lo[0:1] * 0.5
        kf_ref[part, HALF:, :] = spec[1] * wf
    kny = jnp.sum(ksum * sgn, axis=0, keepdims=True) * (1.0 / (2 * SEQ))
    kny_ref[...] = jnp.broadcast_to(kny, (8, D_MIX))


def _hyena_filter(zfeat, window, w1, b1, freq, w2, b2, w3, table):
    consts = (zfeat, window, w1, b1, freq, w2, b2)
    return pl.pallas_call(
        _hyena_filter_kernel,
        grid=(HY_ORDER,),
        in_specs=[_const_spec(a.shape) for a in consts]
        + [pl.BlockSpec((HY_FFN, 2 * D_MIX), lambda o: (0, o)),
           _const_spec(table.shape)],
        out_specs=[pl.BlockSpec((None, 2, SEQ, D_MIX), lambda o: (o, 0, 0, 0)),
                   pl.BlockSpec((None, 8, D_MIX), lambda o: (o, 0, 0))],
        out_shape=[jax.ShapeDtypeStruct((HY_ORDER, 2, SEQ, D_MIX), F32),
                   jax.ShapeDtypeStruct((HY_ORDER, 8, D_MIX), F32)],
        scratch_shapes=[pltpu.VMEM((D_MIX // LANES, SEQ, LANES), F32)] * 2
        + [pltpu.VMEM((SEQ, HY_FFN), F32)],
        compiler_params=_params("arbitrary"),
        name="hyena_filter",
    )(*consts, w3, table)


def _short_conv(u, w, b):
    t = lax.broadcasted_iota(jnp.int32, u.shape, 0)
    prev = jnp.where(t == 0, 0.0, pltpu.roll(u, 1, 0))
    nxt = jnp.where(t == SEQ - 1, 0.0, pltpu.roll(u, SEQ - 1, 0))
    return prev * w[0:1] + u * w[1:2] + nxt * w[2:3] + b


def _hyena_stage_kernel(gate_ref, src_ref, wg_ref, bg_ref, ws_ref, bs_ref, cs_ref,
                        kf_ref, kny_ref, skip_ref, out_ref, z_s, yre_s, yim_s, gate_s, rest_s,
                        *, conv_src):
    for q in range(HY_SEQS):
        seq = slice(q * SEQ, (q + 1) * SEQ)
        zq, yre_q, yim_q = z_s.at[q], yre_s.at[q], yim_s.at[q]
        z = src_ref[seq, :]
        if conv_src:
            z = _short_conv(z, ws_ref[...], bs_ref[...])
        _store_slabs(zq, z)
        t = lax.broadcasted_iota(jnp.int32, z.shape, 0)
        sgn = (1 - 2 * (t & 1)).astype(F32)
        nyq = jnp.sum(z * sgn, axis=0, keepdims=True) * kny_ref[0:1]
        gate = _short_conv(gate_ref[seq, :], wg_ref[...], bg_ref[...])
        gate_s[seq, :] = gate
        rest_s[seq, :] = gate * (sgn * nyq + z * skip_ref[...])
        a, b = _dft_fwd(cs_ref, zq)
        for half in range(2):
            rows = slice(half * HALF, (half + 1) * HALF)
            kre = kf_ref[0, rows, :]
            kim = kf_ref[1, rows, :]
            _store_slabs(yre_q, a[half] * kre + b[half] * kim, rows)
            _store_slabs(yim_q, a[half] * kim - b[half] * kre, rows)
        for half, y in enumerate(_dft_inv(cs_ref, yre_q, yim_q)):
            rows = slice(q * SEQ + half * HALF, q * SEQ + (half + 1) * HALF)
            out_ref[rows, :] = (gate_s[rows, :] * y + rest_s[rows, :]).astype(out_ref.dtype)


def _hyena_stage(u_hy, gate_blk, src, src_blk, conv_w, conv_b, table, kf, kny, order, skip,
                 conv_src, out_dtype):
    rows = HY_SEQS * SEQ
    of_order = lambda a: pl.BlockSpec(
        (None,) + a.shape[1:], lambda b: (order,) + (0,) * (a.ndim - 1),
        pipeline_mode=pl.Buffered(1))
    col = lambda blk: pl.BlockSpec((rows, D_MIX), lambda b: (b, blk))
    wcol = lambda blk, r: pl.BlockSpec((r, D_MIX), lambda b: (0, blk))
    ws_blk = src_blk if conv_src else 0
    return pl.pallas_call(
        functools.partial(_hyena_stage_kernel, conv_src=conv_src),
        grid=(BATCH // HY_SEQS,),
        in_specs=[col(gate_blk), col(src_blk), wcol(gate_blk, 3), wcol(gate_blk, 1),
                  wcol(ws_blk, 3), wcol(ws_blk, 1), _const_spec(table.shape),
                  of_order(kf), of_order(kny), _const_spec(skip.shape)],
        out_specs=pl.BlockSpec((rows, D_MIX), lambda b: (b, 0)),
        out_shape=jax.ShapeDtypeStruct((TOKENS, D_MIX), out_dtype),
        scratch_shapes=[pltpu.VMEM((HY_SEQS, D_MIX // LANES, SEQ, LANES), F32)] * 3
        + [pltpu.VMEM((rows, D_MIX), F32)] * 2,
        compiler_params=_params("parallel"),
        name="hyena_stage",
    )(u_hy, src, conv_w, conv_b, conv_w, conv_b, table, kf, kny, skip)


def _fnet_kernel(xa_ref, xb_ref, cs_ref, cg_ref, sg_ref, out_ref):
    parts = []
    for r in range(2):
        xb = _residue_rows((xa_ref, xb_ref), r, 2).astype(BF16)
        xc = _dot(xb, cg_ref[...]).astype(BF16)
        xs = _dot(xb, sg_ref[...]).astype(BF16)
        parts.append(_dot(_table_cols(cs_ref, r, 2), jnp.concatenate([xc, -xs], axis=0)))
    scale = (SEQ * D_MIX // FN_GROUPS) ** -0.5
    out_ref[:HALF, :] = ((parts[0] + parts[1]) * scale).astype(out_ref.dtype)
    out_ref[HALF:, :] = ((parts[0] - parts[1]) * scale).astype(out_ref.dtype)


def _fnet(u_fn, table, cg, sg):
    return pl.pallas_call(
        _fnet_kernel,
        grid=(BATCH,),
        in_specs=[pl.BlockSpec((SEQ, LANES), lambda b: (b, 0)),
                  pl.BlockSpec((SEQ, LANES), lambda b: (b, 1)), _const_spec(table.shape),
                  _const_spec(cg.shape), _const_spec(sg.shape)],
        out_specs=pl.BlockSpec((SEQ, D_MIX), lambda b: (b, 0)),
        out_shape=jax.ShapeDtypeStruct((TOKENS, D_MIX), BF16),
        compiler_params=_params("parallel"),
        name="fnet",
    )(u_fn, u_fn, table, cg, sg)


def _softmax2_pv(s2, v):
    m = jnp.max(s2, axis=-1, keepdims=True)
    p = jnp.exp2(s2 - m)
    l = jnp.sum(p, axis=-1, keepdims=True)
    return _dot(p.astype(BF16), v) / l


def _mla_kernel(q_ref, k_ref, vt_ref, out_ref):
    outs = []
    for h in range(MLA_HEADS):
        sl = slice(h * MLA_HEAD_PAD, (h + 1) * MLA_HEAD_PAD)
        s2 = _dot_nt(k_ref[:, sl], q_ref[:, sl])
        p = jnp.exp2(s2 - jnp.max(s2, axis=0, keepdims=True))
        l = jnp.sum(p, axis=0, keepdims=True)
        o = _dot(vt_ref[h * MLA_V:(h + 1) * MLA_V, :], p.astype(BF16))
        outs.append(o / l)
    out_ref[...] = jnp.concatenate(outs, axis=0).T.astype(out_ref.dtype)


def _mla(q, k, vt):
    tm = TILE["mla"]
    nt = SEQ // tm
    return pl.pallas_call(
        _mla_kernel,
        grid=(BATCH, nt),
        in_specs=[pl.BlockSpec((tm, MLA_QK), lambda b, i: (b * nt + i, 0)),
                  pl.BlockSpec((SEQ, MLA_QK), lambda b, i: (b, 0)),
                  pl.BlockSpec((None, D_MIX, SEQ), lambda b, i: (b, 0, 0))],
        out_specs=pl.BlockSpec((tm, D_MIX), lambda b, i: (b * nt + i, 0)),
        out_shape=jax.ShapeDtypeStruct((TOKENS, D_MIX), BF16),
        compiler_params=_params("parallel", "parallel"),
        name="mla_attention",
    )(q, k, vt)


def _na_key_row0(j):
    return jnp.clip(j * NA_QROWS - NA_WIN_R // 2, 0, GRID_R - NA_KROWS)


NA_PAIRS = 2 * NA_WIN_R


def _na_kernel(q_ref, k_ref, v_ref, tile_ref, out_ref, bias_s):
    j = pl.program_id(0)
    krow0 = _na_key_row0(j)

    @pl.when(pl.program_id(1) == 0)
    def _():
        rq = j * NA_QROWS + lax.broadcasted_iota(jnp.int32, (NA_QBLK, NA_KBLK), 0) // GRID_W
        rk = krow0 + lax.broadcasted_iota(jnp.int32, (NA_QBLK, NA_KBLK), 1) // GRID_W
        rs = jnp.clip(rq - NA_WIN_R // 2, 0, GRID_R - NA_WIN_R)
        rowmask = jnp.where(rk < rs, NEG_BIG, jnp.where(rk >= rs + NA_WIN_R, NEG_BIG, 0.0))
        base = krow0 - j * NA_QROWS + NA_WIN_R
        for h in range(NA_HEADS):
            bias = jnp.concatenate(
                [jnp.concatenate(
                    [tile_ref[h, jnp.clip(base + 2 * kp - r, 0, NA_PAIRS - 1)]
                     for kp in range(NA_KROWS // 2)], axis=1)
                 for r in range(NA_QROWS)], axis=0)
            bias_s[h] = bias + rowmask

    off = pl.multiple_of(krow0 * GRID_W, GRID_W)
    q = q_ref[...]
    k = k_ref[pl.ds(off, NA_KBLK), :]
    v = v_ref[pl.ds(off, NA_KBLK), :]
    head = lax.broadcasted_iota(jnp.int32, (NA_QBLK, D_MIX), 1) // NA_HEAD_DIM
    acc = jnp.zeros((NA_QBLK, D_MIX), F32)
    for h in range(NA_HEADS):
        qh = jnp.where(head == h, q, jnp.zeros_like(q))
        s2 = _dot_nt(qh, k) + bias_s[h]
        acc = jnp.where(head == h, _softmax2_pv(s2, v), acc)
    out_ref[...] = acc.astype(out_ref.dtype)


def _neighborhood(q, k, v, tiles):
    nj = SEQ // NA_QBLK
    return pl.pallas_call(
        _na_kernel,
        grid=(nj, BATCH),
        in_specs=[pl.BlockSpec((NA_QBLK, D_MIX), lambda j, b: (b * nj + j, 0)),
                  pl.BlockSpec((SEQ, D_MIX), lambda j, b: (b, 0)),
                  pl.BlockSpec((SEQ, D_MIX), lambda j, b: (b, 0)),
                  _const_spec(tiles.shape)],
        out_specs=pl.BlockSpec((NA_QBLK, D_MIX), lambda j, b: (b * nj + j, 0)),
        out_shape=jax.ShapeDtypeStruct((TOKENS, D_MIX), BF16),
        scratch_shapes=[pltpu.VMEM((NA_HEADS, NA_QBLK, NA_KBLK), F32)],
        compiler_params=_params("parallel", "arbitrary"),
        name="neighborhood_attention",
    )(q, k, v, tiles)


def _na_bias_tiles(rpb):
    c = jnp.arange(GRID_W)
    cs = jnp.clip(c - NA_WIN_C // 2, 0, GRID_W - NA_WIN_C)
    col_ok = (c[None, :] >= cs[:, None]) & (c[None, :] < cs[:, None] + NA_WIN_C)
    dc = jnp.clip(c[None, :] - c[:, None] + (NA_WIN_C - 1), 0, 2 * NA_WIN_C - 2)
    pick = (dc[None] == jnp.arange(2 * NA_WIN_C - 1)[:, None, None]).astype(F32)
    t = jnp.einsum('hrd,dqk->hrqk', rpb.astype(F32), pick, precision=lax.Precision.HIGHEST)
    t = jnp.where(col_ok, t * LOG2E, NEG_BIG)
    t = jnp.pad(t, ((0, 0), (1, 1), (0, 0), (0, 0)))
    return jnp.concatenate([t[:, :-1], t[:, 1:]], axis=-1)


def _merge_kernel(x_ref, g1_ref, yhy_ref, yfn_ref, ymla_ref, yna_ref, wg_ref, bg_ref,
                  wbr_ref, wout_ref, g2_ref, wr_ref, out_ref, hb_ref, logit_ref):
    x = x_ref[...]
    h = _rms(x, g1_ref[...]).astype(BF16)
    merged = jnp.zeros(x.shape, F32)
    for n, y_ref in enumerate((yhy_ref, yfn_ref, ymla_ref, yna_ref)):
        sl = slice(n * D_MODEL, (n + 1) * D_MODEL)
        gate = jax.nn.sigmoid(_dot_nt(h, wg_ref[sl, :]) + bg_ref[:, sl])
        merged = merged + gate * _dot(y_ref[...], wbr_ref[n])
    x1 = x + _dot(merged.astype(BF16), wout_ref[...])
    out_ref[...] = x1
    h_hi, h_lo = _split2(_rms(x1, g2_ref[...]))
    hb_ref[...] = h_hi
    w_hi, w_lo = _split2(wr_ref[...])
    logits = _dot_nt(w_hi, h_hi) + _dot_nt(w_hi, h_lo) + _dot_nt(w_lo, h_hi)
    logit_ref[...] = logits[:N_EXPERTS]


def _merge(x, g1, yhy, yfn, ymla, yna, wg, bg, wbr, wout, g2, wr_t):
    tm = TILE["merge"]
    nt = SEQ // tm
    row = lambda n: pl.BlockSpec((tm, n), lambda i: (i, 0))
    return pl.pallas_call(
        _merge_kernel,
        grid=(TOKENS // tm,),
        in_specs=[row(D_MODEL), _const_spec(g1.shape), row(D_MIX), row(D_MIX), row(D_MIX),
                  row(D_MIX), _const_spec(wg.shape), _const_spec(bg.shape),
                  _const_spec(wbr.shape), _const_spec(wout.shape), _const_spec(g2.shape),
                  _const_spec(wr_t.shape)],
        out_specs=[row(D_MODEL), row(D_MODEL),
                   pl.BlockSpec((None, N_EXPERTS, tm), lambda i: (i // nt, 0, i % nt))],
        out_shape=[jax.ShapeDtypeStruct((TOKENS, D_MODEL), F32),
                   jax.ShapeDtypeStruct((TOKENS, D_MODEL), BF16),
                   jax.ShapeDtypeStruct((BATCH, N_EXPERTS, SEQ), F32)],
        compiler_params=_params("parallel"),
        name="merge",
    )(x, g1, yhy, yfn, ymla, yna, wg, bg, wbr, wout, g2, wr_t)


def _prefix_count(m):
    r = lax.broadcasted_iota(jnp.int32, (LANES, LANES), 0)
    c = lax.broadcasted_iota(jnp.int32, (LANES, LANES), 1)
    upper = jnp.where(r < c, 1.0, 0.0).astype(BF16)
    run = jnp.zeros((m.shape[0], 1), F32)
    parts = []
    for i in range(SEQ // LANES):
        chunk = m[:, i * LANES:(i + 1) * LANES]
        parts.append(_dot(chunk.astype(BF16), upper) + run)
        run = run + jnp.sum(chunk, axis=1, keepdims=True)
    return jnp.concatenate(parts, axis=1)


SELECT_MAX_ITERS = 192


def _select_kernel(logit_ref, slot_row_ref, slot_col_ref, w_row_ref, start_ref, slot_s, w_s):
    b = pl.program_id(0)
    rows = BATCH * N_EXPERTS

    @pl.when(b == 0)
    def _():
        logits = logit_ref[...]
        ex = jnp.exp(logits - jnp.max(logits, axis=1, keepdims=True))
        aff = (ex / jnp.sum(ex, axis=1, keepdims=True)).reshape(rows, SEQ)

        def bisect(c):
            it, lo, hi, _ = c
            mid = 0.5 * (lo + hi)
            cnt = jnp.sum(jnp.where(aff >= mid, 1.0, 0.0), axis=1, keepdims=True)
            moving = jnp.where(mid == lo, 0.0, jnp.where(mid == hi, 0.0, 1.0))
            enough = cnt >= CAPACITY
            return (it + 1, jnp.where(enough, mid, lo), jnp.where(enough, hi, mid),
                    (jnp.max(moving) > 0).astype(jnp.int32))

        _, lo, hi, _ = lax.while_loop(
            lambda c: (c[0] < SELECT_MAX_ITERS) & (c[3] > 0), bisect,
            (jnp.int32(0), jnp.zeros((rows, 1), F32), jnp.full((rows, 1), 2.0, F32),
             jnp.int32(1)))
        above = jnp.where(aff >= hi, 1.0, 0.0)
        band = jnp.where(aff >= lo, 1.0, 0.0) - above
        need = CAPACITY - jnp.sum(above, axis=1, keepdims=True)
        sel = above + band * jnp.where(_prefix_count(band) < need, 1.0, 0.0)
        slot_s[...] = jnp.where(sel > 0, _prefix_count(sel), -1.0)
        w_s[...] = sel * aff

    r0 = pl.multiple_of(b * N_EXPERTS, N_EXPERTS)
    slot = slot_s[pl.ds(r0, N_EXPERTS), :]
    slot_row_ref[...] = slot.astype(jnp.int32)
    pad = jnp.full((LANES - N_EXPERTS, SEQ), -1.0, F32)
    slot_col_ref[...] = jnp.concatenate([slot, pad], axis=0).T.astype(jnp.int32)
    w_row_ref[...] = w_s[pl.ds(r0, N_EXPERTS), :]
    tm = MOE_FINE
    token = lax.broadcasted_iota(jnp.int32, slot.shape, 1)
    lane = lax.broadcasted_iota(jnp.int32, (N_EXPERTS, LANES), 1)
    starts = jnp.where(lane == SEQ // tm, float(CAPACITY), 0.0)
    for i in range(1, SEQ // tm):
        before = jnp.where(slot >= 0, jnp.where(token < i * tm, 1.0, 0.0), 0.0)
        starts = starts + jnp.where(lane == i, jnp.sum(before, axis=1, keepdims=True), 0.0)
    start_ref[...] = starts.astype(jnp.int32)


def _select(logits):
    return pl.pallas_call(
        _select_kernel,
        grid=(BATCH,),
        in_specs=[_const_spec(logits.shape)],
        out_specs=[pl.BlockSpec((None, N_EXPERTS, SEQ), lambda b: (b, 0, 0)),
                   pl.BlockSpec((None, SEQ, LANES), lambda b: (b, 0, 0)),
                   pl.BlockSpec((None, N_EXPERTS, SEQ), lambda b: (b, 0, 0)),
                   pl.BlockSpec((None, N_EXPERTS, LANES), lambda b: (b, 0, 0))],
        out_shape=[jax.ShapeDtypeStruct((BATCH, N_EXPERTS, SEQ), jnp.int32),
                   jax.ShapeDtypeStruct((BATCH, SEQ, LANES), jnp.int32),
                   jax.ShapeDtypeStruct((BATCH, N_EXPERTS, SEQ), F32),
                   jax.ShapeDtypeStruct((BATCH, N_EXPERTS, LANES), jnp.int32)],
        scratch_shapes=[pltpu.VMEM((BATCH * N_EXPERTS, SEQ), F32),
                        pltpu.VMEM((BATCH * N_EXPERTS, SEQ), F32)],
        compiler_params=_params("arbitrary"),
        name="expert_select",
    )(logits)


MXU_DEPTH = 256
BF16_ROWS = 16


def _slot_window(tokens):
    return 2 * tokens * CAPACITY // SEQ


def _tile_windows(start_ref, b, i, tokens):
    stride = start_ref.shape[0] // (BATCH * N_EXPERTS)
    fine = tokens // MOE_FINE
    window = _slot_window(tokens)
    base, fits = [], None
    for e in range(N_EXPERTS):
        at = (b * N_EXPERTS + e) * stride + i * fine
        lo = jnp.minimum(start_ref[at] // BF16_ROWS * BF16_ROWS, CAPACITY - window)
        ok = start_ref[at + fine] - lo <= window
        fits = ok if fits is None else fits & ok
        base.append(pl.multiple_of(lo, BF16_ROWS))
    return base, fits


def _gather_kernel(start_ref, slot_ref, w_ref, hb_ref, xe_ref, wsl_ref):
    b = pl.program_id(0)
    i = pl.program_id(1)
    tokens = slot_ref.shape[1]

    @pl.when(i == 0)
    def _():
        xe_ref[...] = jnp.zeros(xe_ref.shape, BF16)
        wsl_ref[...] = jnp.zeros(wsl_ref.shape, F32)

    def add_rows(e, rows, hit, picked):
        xe_ref[e, rows, :] = xe_ref[e, rows, :] + picked.astype(BF16)
        wslot = jnp.sum(jnp.where(hit, w_ref[e:e + 1, :], 0.0), axis=1, keepdims=True)
        wsl_ref[e, rows, :] = wsl_ref[e, rows, :] + jnp.broadcast_to(wslot, (hit.shape[0], LANES))

    window = _slot_window(tokens)
    base, fits = _tile_windows(start_ref, b, i, tokens)

    @pl.when(fits)
    def _():
        c = lax.broadcasted_iota(jnp.int32, (window, tokens), 0)
        hits = [slot_ref[e:e + 1, :] - base[e] == c for e in range(N_EXPERTS)]
        onehot = jnp.concatenate([jnp.where(m, 1.0, 0.0).astype(BF16) for m in hits], axis=0)
        picked = _dot(onehot, hb_ref[...])
        for e in range(N_EXPERTS):
            add_rows(e, pl.ds(base[e], window), hits[e], picked[e * window:(e + 1) * window])

    @pl.when(jnp.logical_not(fits))
    def _():
        c = lax.broadcasted_iota(jnp.int32, (CAPACITY, tokens), 0)
        for e in range(N_EXPERTS):
            hit = slot_ref[e:e + 1, :] == c
            picked = _dot(jnp.where(hit, 1.0, 0.0).astype(BF16), hb_ref[...])
            add_rows(e, slice(None), hit, picked)


def _gather(slot_row, w_row, starts, hb):
    tm = TILE["gather"]
    nt = SEQ // tm
    grid_spec = pltpu.PrefetchScalarGridSpec(
        num_scalar_prefetch=1,
        grid=(BATCH, nt),
        in_specs=[pl.BlockSpec((None, N_EXPERTS, tm), lambda b, i, s: (b, 0, i)),
                  pl.BlockSpec((None, N_EXPERTS, tm), lambda b, i, s: (b, 0, i)),
                  pl.BlockSpec((tm, D_MODEL), lambda b, i, s: (b * nt + i, 0))],
        out_specs=[pl.BlockSpec((N_EXPERTS, None, CAPACITY, D_MODEL), lambda b, i, s: (0, b, 0, 0)),
                   pl.BlockSpec((N_EXPERTS, None, CAPACITY, LANES), lambda b, i, s: (0, b, 0, 0))])
    return pl.pallas_call(
        _gather_kernel,
        grid_spec=grid_spec,
        out_shape=[jax.ShapeDtypeStruct((N_EXPERTS, BATCH, CAPACITY, D_MODEL), BF16),
                   jax.ShapeDtypeStruct((N_EXPERTS, BATCH, CAPACITY, LANES), F32)],
        compiler_params=_params("parallel", "arbitrary"),
        name="expert_gather",
    )(starts[:, :, :SEQ // MOE_FINE + 1].reshape(-1), slot_row, w_row, hb)


def _expert_kernel(xe_ref, wsl_ref, wg_ref, wu_ref, wd_ref, ye_ref, wg_s, wu_s, wd_s, *, span):
    ph = pl.program_id(0)
    i = pl.program_id(1)
    rows = pl.ds(pl.multiple_of(i * span, span), span)
    nxt = ph % 2
    wg_s[nxt, rows, :] = wg_ref[rows, :].astype(BF16)
    wu_s[nxt, rows, :] = wu_ref[rows, :].astype(BF16)
    wd_s[nxt, rows, :] = wd_ref[rows, :].astype(BF16)

    @pl.when(ph > 0)
    def _():
        cur = (ph + 1) % 2
        xe = xe_ref[...]
        g = _dot(xe, wg_s[cur])
        u = _dot(xe, wu_s[cur])
        act = (g * jax.nn.sigmoid(g) * u).astype(BF16)
        ye_ref[...] = (_dot(act, wd_s[cur]) * wsl_ref[:, 0:1]).astype(BF16)


def _experts(xe, wsl, wg, wu, wd, layer):
    rows = BATCH * CAPACITY
    tm = TILE["expert"]
    nt = rows // tm
    last = N_EXPERTS - 1
    wspec = lambda a: pl.BlockSpec((None, None) + a.shape[2:],
                                   lambda ph, i: (layer, jnp.minimum(ph, last), 0, 0))
    data = lambda n: pl.BlockSpec(
        (None, tm, n), lambda ph, i: (jnp.maximum(ph - 1, 0), jnp.where(ph > 0, i, 0), 0))
    return pl.pallas_call(
        functools.partial(_expert_kernel, span=D_MODEL // nt),
        grid=(N_EXPERTS + 1, nt),
        in_specs=[data(D_MODEL), data(LANES), wspec(wg), wspec(wu), wspec(wd)],
        out_specs=data(D_MODEL),
        out_shape=jax.ShapeDtypeStruct((N_EXPERTS, rows, D_MODEL), BF16),
        scratch_shapes=[pltpu.VMEM((2, D_MODEL, D_FF), BF16), pltpu.VMEM((2, D_MODEL, D_FF), BF16),
                        pltpu.VMEM((2, D_FF, D_MODEL), BF16)],
        compiler_params=_params("arbitrary", "arbitrary"),
        name="expert_ffn",
    )(xe.reshape(N_EXPERTS, rows, D_MODEL), wsl.reshape(N_EXPERTS, rows, LANES), wg, wu, wd)


def _combine_kernel(start_ref, x_ref, ye_ref, slot_ref, p_ref, g3_ref, wpg_ref, wpp_ref,
                    gf_ref, out_ref, moe_s, *, final_norm):
    slot = slot_ref[...]
    rows = slot.shape[0]
    window = _slot_window(rows)
    base, fits = _tile_windows(start_ref, pl.program_id(0), pl.program_id(1), rows)

    @pl.when(fits)
    def _():
        c = lax.broadcasted_iota(jnp.int32, (rows, window), 1)
        parts = []
        group = MXU_DEPTH // window
        for e0 in range(0, N_EXPERTS, group):
            hot, win = [], []
            for e in range(e0, e0 + group):
                hot.append(jnp.where(slot[:, e:e + 1] - base[e] == c, 1.0, 0.0).astype(BF16))
                win.append(ye_ref[e, pl.ds(base[e], window), :])
            parts.append(_dot(jnp.concatenate(hot, axis=1), jnp.concatenate(win, axis=0)))
        while len(parts) > 1:
            parts = [u + v for u, v in zip(parts[::2], parts[1::2])]
        moe_s[...] = parts[0]

    @pl.when(jnp.logical_not(fits))
    def _():
        c = lax.broadcasted_iota(jnp.int32, (rows, CAPACITY), 1)
        onehot = jnp.concatenate(
            [jnp.where(slot[:, e:e + 1] == c, 1.0, 0.0).astype(BF16) for e in range(N_EXPERTS)],
            axis=1)
        moe_s[...] = _dot(onehot, ye_ref[...].reshape(N_EXPERTS * CAPACITY, D_MODEL))

    acc = x_ref[...] + moe_s[...]
    h = _rms(acc, g3_ref[...]).astype(BF16)
    gate = jax.nn.sigmoid(_dot(h, wpg_ref[...]))
    y = acc + gate * _dot(p_ref[...].astype(BF16), wpp_ref[...])
    if final_norm:
        y = _rms(y, gf_ref[...])
    out_ref[...] = y


def _combine(x, ye, slot_col, starts, p, layer, g3, wpg, wpp, gf, final_norm):
    tm = TILE["combine"]
    nt = SEQ // tm
    p0 = layer * (TOKENS // tm)
    const = lambda a: pl.BlockSpec(a.shape, lambda b, i, s: (0,) * a.ndim,
                                   pipeline_mode=pl.Buffered(1))
    grid_spec = pltpu.PrefetchScalarGridSpec(
        num_scalar_prefetch=1,
        grid=(BATCH, nt),
        in_specs=[pl.BlockSpec((tm, D_MODEL), lambda b, i, s: (b * nt + i, 0)),
                  pl.BlockSpec((N_EXPERTS, None, CAPACITY, D_MODEL), lambda b, i, s: (0, b, 0, 0)),
                  pl.BlockSpec((None, tm, LANES), lambda b, i, s: (b, i, 0)),
                  pl.BlockSpec((tm, PLE_DIM), lambda b, i, s: (p0 + b * nt + i, 0)),
                  const(g3), const(wpg), const(wpp), const(gf)],
        out_specs=pl.BlockSpec((tm, D_MODEL), lambda b, i, s: (b * nt + i, 0)),
        scratch_shapes=[pltpu.VMEM((tm, D_MODEL), F32)])
    return pl.pallas_call(
        functools.partial(_combine_kernel, final_norm=final_norm),
        grid_spec=grid_spec,
        out_shape=jax.ShapeDtypeStruct((TOKENS, D_MODEL), F32),
        compiler_params=_params("parallel", "parallel"),
        name="combine",
    )(starts[:, :, :SEQ // MOE_FINE + 1].reshape(-1), x,
      ye.reshape(N_EXPERTS, BATCH, CAPACITY, D_MODEL),
      slot_col, p, g3, wpg, wpp, gf)


DFT_FINE = 64
DFT_STEP = 4


def _dft_kernel(ca_ref, sa_ref, cb_ref, sb_ref, out_ref):
    cb = cb_ref[...]
    sb = sb_ref[...]
    for r in range(DFT_STEP):
        rows = slice(r * DFT_FINE, (r + 1) * DFT_FINE)
        out_ref[rows, :] = (ca_ref[r] * cb - sa_ref[r] * sb).astype(BF16)


def _dft_tables(n_points, nres):
    blk = SEQ // nres
    t = jnp.arange(SEQ, dtype=jnp.int32).reshape(blk, nres).T
    t = jnp.concatenate([t, t], axis=1).reshape(1, 2 * SEQ)
    is_sin = (jnp.arange(2 * SEQ) // blk % 2 == 1)[None, :]
    coarse = jnp.arange(HALF // DFT_FINE, dtype=jnp.int32)[:, None] * DFT_FINE
    fine = jnp.arange(DFT_FINE, dtype=jnp.int32)[:, None]
    ang = lambda f: ((f * t) % n_points).astype(F32) * (2.0 * math.pi / n_points)
    ca = jnp.cos(ang(coarse))[:, None, :]
    sa = jnp.sin(ang(coarse))[:, None, :]
    cb = jnp.where(is_sin, jnp.sin(ang(fine)), jnp.cos(ang(fine)))
    sb = jnp.where(is_sin, -jnp.cos(ang(fine)), jnp.sin(ang(fine)))
    rows = DFT_STEP * DFT_FINE
    return pl.pallas_call(
        _dft_kernel,
        grid=(HALF // rows,),
        in_specs=[pl.BlockSpec((DFT_STEP, 1, 2 * SEQ), lambda i: (i, 0, 0)),
                  pl.BlockSpec((DFT_STEP, 1, 2 * SEQ), lambda i: (i, 0, 0)),
                  _const_spec(cb.shape), _const_spec(sb.shape)],
        out_specs=pl.BlockSpec((rows, 2 * SEQ), lambda i: (i, 0)),
        out_shape=jax.ShapeDtypeStruct((HALF, 2 * SEQ), BF16),
        compiler_params=_params("parallel"),
        name="dft_tables",
    )(ca, sa, cb, sb)


def _fnet_group_tables():
    gc = D_MIX // FN_GROUPS
    i = lax.broadcasted_iota(jnp.int32, (D_MIX, D_MIX), 0)
    j = lax.broadcasted_iota(jnp.int32, (D_MIX, D_MIX), 1)
    same = (i // gc) == (j // gc)
    ang = (((i % gc) * (j % gc)) % gc).astype(F32) * (2.0 * math.pi / gc)
    return (jnp.where(same, jnp.cos(ang), 0.0).astype(BF16),
            jnp.where(same, jnp.sin(ang), 0.0).astype(BF16))


def _hyena_features():
    t01 = jnp.linspace(0.0, 1.0, SEQ, dtype=F32)[:, None]
    bands = jnp.linspace(1e-4, HY_BANDS - 1, HY_BANDS, dtype=F32)
    ang = 2.0 * math.pi * jnp.arange(SEQ, dtype=F32)[:, None] * bands / SEQ
    z = jnp.concatenate([t01, jnp.cos(ang), -jnp.sin(ang)], axis=-1)
    z = jnp.pad(z, ((0, 0), (0, LANES - HY_EMB)))
    max_decay = math.log(HY_TARGET) / HY_FAST_DECAY
    min_decay = math.log(HY_TARGET) / HY_SLOW_DECAY
    deltas = jnp.linspace(min_decay, max_decay, D_MIX, dtype=F32)
    window = jnp.exp(-t01 * jnp.abs(deltas))
    return z, window


def _rot_cols(w):
    half = w.shape[-1] // 2
    return jnp.concatenate([-w[..., half:], w[..., :half]], axis=-1)


def _rope_tables():
    inv = ROPE_THETA ** (-jnp.arange(0, MLA_ROPE, 2, dtype=F32) / MLA_ROPE)
    ang = jnp.arange(SEQ, dtype=F32)[:, None] * inv
    cos = jnp.concatenate([jnp.cos(ang), jnp.cos(ang)], axis=-1)
    sin = jnp.concatenate([jnp.sin(ang), jnp.sin(ang)], axis=-1)
    scale = (MLA_NOPE + MLA_ROPE) ** -0.5 * LOG2E
    pad = MLA_HEAD_PAD - MLA_NOPE - MLA_ROPE
    one = jnp.ones((SEQ, MLA_NOPE), F32)
    zero = jnp.zeros((SEQ, MLA_NOPE), F32)
    zpad = jnp.zeros((SEQ, pad), F32)
    cosq = jnp.tile(jnp.concatenate([one, cos, zpad], axis=-1) * scale, (1, MLA_HEADS))
    sinq = jnp.tile(jnp.concatenate([zero, sin, zpad], axis=-1) * scale, (1, MLA_HEADS))
    csk = jnp.concatenate([cos, sin, jnp.zeros((SEQ, LANES - 2 * MLA_ROPE), F32)], axis=-1)
    return cosq, sinq, csk


def _mla_weights(w_uq, w_ukv):
    pad = MLA_HEAD_PAD - MLA_NOPE - MLA_ROPE
    wq = w_uq.reshape(MLA_Q_RANK, MLA_HEADS, MLA_NOPE + MLA_ROPE)
    nope, pe = wq[..., :MLA_NOPE], wq[..., MLA_NOPE:]
    zp = jnp.zeros((MLA_Q_RANK, MLA_HEADS, pad), F32)
    wqa = jnp.concatenate([nope, pe, zp], axis=-1).reshape(MLA_Q_RANK, MLA_QK)
    wqb = jnp.concatenate([jnp.zeros_like(nope), _rot_cols(pe), zp], axis=-1)
    wqb = wqb.reshape(MLA_Q_RANK, MLA_QK)
    wkv = w_ukv.reshape(MLA_KV_RANK, MLA_HEADS, MLA_NOPE + MLA_V)
    knope, v = wkv[..., :MLA_NOPE], wkv[..., MLA_NOPE:]
    wk = jnp.concatenate(
        [knope, jnp.zeros((MLA_KV_RANK, MLA_HEADS, MLA_HEAD_PAD - MLA_NOPE), F32)], axis=-1)
    wk = wk.reshape(MLA_KV_RANK, MLA_QK)
    wv = v.reshape(MLA_KV_RANK, MLA_HEADS * MLA_V).T
    r = lax.broadcasted_iota(jnp.int32, (LANES, MLA_QK), 0)
    c = lax.broadcasted_iota(jnp.int32, (LANES, MLA_QK), 1)
    epe = jnp.where((r < 2 * MLA_ROPE) & (c % MLA_HEAD_PAD == MLA_NOPE + r % MLA_ROPE), 1.0, 0.0)
    return (wqa.astype(BF16), wqb.astype(BF16), wk.astype(BF16), wv.astype(BF16),
            epe.astype(BF16))


def kernel(x, p, norm1_g, w_in, b_gate, hy_conv_w, hy_conv_b, hf_w1, hf_b1, hf_freq, hf_w2,
           hf_b2, hf_w3, hy_skip, q_norm_g, w_uq, kv_norm_g, w_ukv, rpb, w_br, w_out, norm2_g,
           w_router, w_e_gate, w_e_up, w_e_down, norm3_g, w_ple_gate, w_ple_proj, final_g):
    conv_tab = _dft_tables(2 * SEQ, 4)
    fnet_tab = _dft_tables(SEQ, 2)
    fnet_cg, fnet_sg = _fnet_group_tables()
    zfeat, window = _hyena_features()
    cosq, sinq, csk = _rope_tables()
    row = lambda a: a.reshape(1, -1)

    xt = x.reshape(TOKENS, D_MODEL)
    w_in_t = jnp.swapaxes(w_in, 1, 2)
    for i in range(DEPTH):
        g1 = row(norm1_g[i])
        wa, wna, wgate = _inproj_weights(w_in_t, i)
        wqa, wqb, wk, wv, epe = _mla_weights(w_uq[i], w_ukv[i])

        u_hy, u_fn, q, k, v, naq, nak, nav = _inproj(
            xt, g1, wa, wna, row(q_norm_g[i]), wqa, wqb, row(kv_norm_g[i]), wk, wv, epe,
            cosq, sinq, csk)

        w1 = jnp.pad(hf_w1[i], ((0, LANES - HY_EMB), (0, 0)))
        kf, kny = _hyena_filter(zfeat, window, w1, row(hf_b1[i]), hf_freq[i], hf_w2[i],
                                row(hf_b2[i]), hf_w3[i], conv_tab)
        conv_b = row(hy_conv_b[i])
        z1 = _hyena_stage(u_hy, 0, u_hy, 2, hy_conv_w[i], conv_b, conv_tab, kf, kny, 0,
                          row(hy_skip[i, 0]), True, F32)
        y_hy = _hyena_stage(u_hy, 1, z1, 0, hy_conv_w[i], conv_b, conv_tab, kf, kny, 1,
                            row(hy_skip[i, 1]), False, BF16)
        y_fn = _fnet(u_fn, fnet_tab, fnet_cg, fnet_sg)
        y_mla = _mla(q, k, v)
        y_na = _neighborhood(naq, nak, nav, _na_bias_tiles(rpb[i]))
        wr_t = jnp.pad(w_router[i].T, ((0, LANES - N_EXPERTS), (0, 0)))
        xt, hb, logits = _merge(xt, g1, y_hy, y_fn, y_mla, y_na, wgate, row(b_gate[i]),
                                w_br[i].astype(BF16), w_out[i].astype(BF16),
                                row(norm2_g[i]), wr_t)
        slot_row, slot_col, w_row, starts = _select(logits)
        xe, wsl = _gather(slot_row, w_row, starts, hb)
        ye = _experts(xe, wsl, w_e_gate, w_e_up, w_e_down, i)
        xt = _combine(xt, ye, slot_col, starts, p.reshape(DEPTH * TOKENS, PLE_DIM), i,
                      row(norm3_g[i]), w_ple_gate[i].astype(BF16),
                      w_ple_proj[i].astype(BF16), row(final_g), i == DEPTH - 1)
    return xt.reshape(BATCH, SEQ, D_MODEL)
```

```python
import functools
import math

import jax
import jax.numpy as jnp
from jax import lax
from jax.experimental import pallas as pl
from jax.experimental.pallas import tpu as pltpu

F32 = jnp.float32
BF16 = jnp.bfloat16

D_MODEL = 1024
BATCH = 8
SEQ = 2048
DEPTH = 2
TOKENS = BATCH * SEQ

GRID_W = 64
GRID_R = SEQ // GRID_W
D_MIX = 256
N_BRANCH = 4
EPS = 1e-6
HY_ORDER = 2
HY_BANDS = 16
HY_EMB = 2 * HY_BANDS + 1
HY_FFN = 64
HY_TARGET = 1e-2
HY_FAST_DECAY = 0.3
HY_SLOW_DECAY = 1.5
FN_GROUPS = 4
MLA_HEADS = 4
MLA_NOPE = 64
MLA_ROPE = 32
MLA_V = 64
MLA_Q_RANK = 256
MLA_KV_RANK = 128
ROPE_THETA = 10000.0
NA_HEADS = 4
NA_HEAD_DIM = D_MIX // NA_HEADS
NA_WIN_R = 8
NA_WIN_C = 16
N_EXPERTS = 16
CAPACITY = 2 * SEQ // N_EXPERTS
D_FF = 1024
PLE_DIM = 256

HY_COLS = 3 * D_MIX
OFF_FN = HY_COLS
OFF_CQ = OFF_FN + D_MIX
OFF_CKV = OFF_CQ + MLA_Q_RANK
OFF_KPE = OFF_CKV + MLA_KV_RANK
OFF_NA = OFF_KPE + MLA_ROPE
OFF_GATE = OFF_NA + 3 * D_MIX

LANES = 128
MLA_HEAD_PAD = 128
MLA_QK = MLA_HEADS * MLA_HEAD_PAD
WA_COLS = 1536
NEG_BIG = -1e30
LOG2E = math.log2(math.e)

TILE = dict(inproj=1024, mla=1024, merge=1024, gather=256, expert=1024, combine=512)
MOE_FINE = 256
NA_QROWS = 4
NA_KROWS = 12
NA_QBLK = NA_QROWS * GRID_W
NA_KBLK = NA_KROWS * GRID_W
VMEM_LIMIT = 56 * 1024 * 1024


def _params(*sem):
    return pltpu.CompilerParams(dimension_semantics=sem, vmem_limit_bytes=VMEM_LIMIT)


def _const_spec(shape):
    nd = len(shape)
    return pl.BlockSpec(shape, lambda *_: (0,) * nd, pipeline_mode=pl.Buffered(1))


def _rms(x, g):
    return x * lax.rsqrt(jnp.mean(x * x, axis=-1, keepdims=True) + EPS) * g


def _dot(a, b):
    return jnp.dot(a, b, preferred_element_type=F32)


def _dot_nt(a, b):
    return lax.dot_general(a, b, (((1,), (1,)), ((), ())), preferred_element_type=F32)


def _split2(x):
    hi = x.astype(BF16)
    lo = (x - hi.astype(F32)).astype(BF16)
    return hi, lo


def _inproj_kernel(x_ref, g1_ref, wa_ref, wna_ref, qg_ref, wqa_ref, wqb_ref, kvg_ref,
                   wk_ref, wv_ref, epe_ref, cosq_ref, sinq_ref, csk_ref,
                   uhy_ref, ufn_ref, q_ref, k_ref, v_ref, naq_ref, nak_ref, nav_ref):
    h = _rms(x_ref[...], g1_ref[...]).astype(BF16)
    ua = _dot_nt(h, wa_ref[...])
    uhy_ref[...] = ua[:, :HY_COLS]
    ufn_ref[...] = ua[:, OFF_FN:OFF_CQ]
    cqn = _rms(ua[:, OFF_CQ:OFF_CKV], qg_ref[...]).astype(BF16)
    q = _dot(cqn, wqa_ref[...]) * cosq_ref[...] + _dot(cqn, wqb_ref[...]) * sinq_ref[...]
    q_ref[...] = q.astype(BF16)
    kvn = _rms(ua[:, OFF_CKV:OFF_KPE], kvg_ref[...]).astype(BF16)
    kpe = ua[:, OFF_KPE:WA_COLS] * csk_ref[...]
    k = _dot(kvn, wk_ref[...]) + _dot(kpe.astype(BF16), epe_ref[...])
    k_ref[...] = k.astype(BF16)
    v_ref[...] = _dot_nt(wv_ref[...], kvn).astype(BF16)
    una = _dot_nt(h, wna_ref[...])
    naq_ref[...] = (una[:, :D_MIX] * (NA_HEAD_DIM ** -0.5 * LOG2E)).astype(BF16)
    nak_ref[...] = una[:, D_MIX:2 * D_MIX].astype(BF16)
    nav_ref[...] = una[:, 2 * D_MIX:].astype(BF16)


def _inproj(x, g1, wa, wna, qg, wqa, wqb, kvg, wk, wv, epe, cosq, sinq, csk):
    tm = TILE["inproj"]
    nt = SEQ // tm
    row = lambda n: pl.BlockSpec((tm, n), lambda i: (i, 0))
    pos = lambda n: pl.BlockSpec((tm, n), lambda i: (i % nt, 0))
    outs = [(HY_COLS, F32), (D_MIX, F32), (MLA_QK, BF16), (MLA_QK, BF16), None,
            (D_MIX, BF16), (D_MIX, BF16), (D_MIX, BF16)]
    vt_spec = pl.BlockSpec((None, D_MIX, tm), lambda i: (i // nt, 0, i % nt))
    vt_shape = jax.ShapeDtypeStruct((BATCH, D_MIX, SEQ), BF16)
    return pl.pallas_call(
        _inproj_kernel,
        grid=(TOKENS // tm,),
        in_specs=[row(D_MODEL), _const_spec(g1.shape), _const_spec(wa.shape),
                  _const_spec(wna.shape), _const_spec(qg.shape), _const_spec(wqa.shape),
                  _const_spec(wqb.shape), _const_spec(kvg.shape), _const_spec(wk.shape),
                  _const_spec(wv.shape), _const_spec(epe.shape),
                  pos(MLA_QK), pos(MLA_QK), pos(LANES)],
        out_specs=[vt_spec if o is None else row(o[0]) for o in outs],
        out_shape=[vt_shape if o is None else jax.ShapeDtypeStruct((TOKENS, o[0]), o[1])
                   for o in outs],
        compiler_params=_params("parallel"),
        name="inproj",
    )(x, g1, wa, wna, qg, wqa, wqb, kvg, wk, wv, epe, cosq, sinq, csk)


WPREP_ROWS = 512


def _inproj_weights_kernel(w_ref, wa_ref, wna_ref, wg_ref):
    i = pl.program_id(0)

    @pl.when(i == 0)
    def _():
        half = MLA_ROPE // 2
        wa_ref[:OFF_NA, :] = w_ref[:OFF_NA, :].astype(BF16)
        wa_ref[OFF_NA:OFF_NA + half, :] = (-w_ref[OFF_KPE + half:OFF_NA, :]).astype(BF16)
        wa_ref[OFF_NA + half:OFF_NA + MLA_ROPE, :] = w_ref[OFF_KPE:OFF_KPE + half, :].astype(BF16)
        wa_ref[OFF_NA + MLA_ROPE:, :] = jnp.zeros((WA_COLS - OFF_NA - MLA_ROPE, D_MODEL), BF16)
        wna_ref[...] = w_ref[OFF_NA:OFF_GATE, :].astype(BF16)

    start = pl.multiple_of(OFF_GATE + i * WPREP_ROWS, MLA_ROPE)
    wg_ref[...] = w_ref[pl.ds(start, WPREP_ROWS), :].astype(BF16)


def _inproj_weights(w_in_t, layer):
    n_in = w_in_t.shape[1]
    rows = (WA_COLS, OFF_GATE - OFF_NA, n_in - OFF_GATE)
    whole = lambda n: pl.BlockSpec((n, D_MODEL), lambda i: (0, 0))
    return pl.pallas_call(
        _inproj_weights_kernel,
        grid=(rows[2] // WPREP_ROWS,),
        in_specs=[pl.BlockSpec((None, n_in, D_MODEL), lambda i: (layer, 0, 0),
                               pipeline_mode=pl.Buffered(1))],
        out_specs=[whole(rows[0]), whole(rows[1]),
                   pl.BlockSpec((WPREP_ROWS, D_MODEL), lambda i: (i, 0))],
        out_shape=[jax.ShapeDtypeStruct((n, D_MODEL), BF16) for n in rows],
        compiler_params=_params("arbitrary"),
        name="inproj_weights",
    )(w_in_t)


HALF = SEQ // 2
HY_SEQS = 1


def _residue_rows(ref, r, nres):
    rows = pl.ds(r, SEQ // nres, stride=nres)
    if isinstance(ref, tuple):
        return jnp.concatenate([h[rows, :] for h in ref], axis=1)
    return jnp.concatenate([ref[j, rows, :] for j in range(ref.shape[0])], axis=1)


def _store_slabs(ref, value, rows=slice(None)):
    for j in range(ref.shape[0]):
        ref[j, rows, :] = value[:, j * LANES:(j + 1) * LANES]


def _table_cols(cs_ref, r, nres, part=None):
    blk = SEQ // nres
    lo = 2 * blk * r
    if part is None:
        return cs_ref[:, lo:lo + 2 * blk]
    return cs_ref[:, lo + part * blk:lo + (part + 1) * blk]


def _dft_fwd(cs_ref, src_ref, want_cos=True, want_sin=True, split=False):
    def prod(part, z, r):
        tab = _table_cols(cs_ref, r, 4, part)
        if split:
            hi, lo = _split2(z)
            return _dot(tab, hi) + _dot(tab, lo)
        return _dot(tab, z.astype(BF16))

    pc, ps = [None] * 4, [None] * 4
    for r in range(4):
        z = _residue_rows(src_ref, r, 4)
        odd = r % 2 == 1
        if want_cos or odd:
            pc[r] = prod(0, z, r)
        if want_sin or odd:
            ps[r] = prod(1, z, r)
    a = b = None
    if want_cos:
        a = ((pc[0] + pc[2]) + (pc[1] + pc[3]), (pc[0] - pc[2]) + (ps[3] - ps[1]))
    if want_sin:
        b = ((ps[0] + ps[2]) + (ps[1] + ps[3]), (ps[0] - ps[2]) + (pc[1] - pc[3]))
    return a, b


def _dft_inv(cs_ref, yre_ref, yim_ref):
    g, h = [], {}
    for r in range(4):
        yr = _residue_rows(yre_ref, r, 4).astype(BF16)
        yi = _residue_rows(yim_ref, r, 4).astype(BF16)
        tab = _table_cols(cs_ref, r, 4)
        g.append(_dot(tab, jnp.concatenate([yr, -yi], axis=0)))
        if r % 2 == 1:
            h[r] = _dot(tab, jnp.concatenate([yi, yr], axis=0))
    return (g[0] + g[2]) + (g[1] + g[3]), (g[0] - g[2]) + (h[3] - h[1])


def _hyena_filter_kernel(z_ref, win_ref, w1_ref, b1_ref, freq_ref, w2_ref, b2_ref, w3_ref,
                         cs_ref, kf_ref, kny_ref, ksum_s, kdif_s, trunk_s):
    hp = lax.Precision.HIGHEST

    @pl.when(pl.program_id(0) == 0)
    def _():
        freq = freq_ref[...]
        hf = jnp.sin(freq[0:1] * (jnp.dot(z_ref[...], w1_ref[...], precision=hp,
                                          preferred_element_type=F32) + b1_ref[...]))
        trunk_s[...] = jnp.sin(freq[1:2] * (jnp.dot(hf, w2_ref[...], precision=hp,
                                                    preferred_element_type=F32) + b2_ref[...]))

    hf = jnp.dot(trunk_s[...], w3_ref[...], precision=hp, preferred_element_type=F32)
    win = win_ref[...]
    t = lax.broadcasted_iota(jnp.int32, (SEQ, D_MIX), 0)
    sgn = (1 - 2 * (t & 1)).astype(F32)
    fwd = hf[:, :D_MIX] * win
    bwd = jnp.where(t == 0, 0.0, hf[:, D_MIX:] * win)
    nrm = lax.rsqrt(jnp.sum(fwd * fwd + bwd * bwd, axis=0, keepdims=True) + EPS)
    ksum = (fwd + bwd) * nrm
    _store_slabs(ksum_s, ksum)
    _store_slabs(kdif_s, (bwd - fwd) * nrm)
    kre, _ = _dft_fwd(cs_ref, ksum_s, want_sin=False, split=True)
    _, kim = _dft_fwd(cs_ref, kdif_s, want_cos=False, split=True)
    wf = 2.0 / (2 * SEQ)
    for part, spec in enumerate((kre, kim)):
        lo = spec[0] * wf
        kf_ref[part, :HALF, :] = lo
        kf_ref[part, 0:1, :] = lo[0:1] * 0.5
        kf_ref[part, HALF:, :] = spec[1] * wf
    kny = jnp.sum(ksum * sgn, axis=0, keepdims=True) * (1.0 / (2 * SEQ))
    kny_ref[...] = jnp.broadcast_to(kny, (8, D_MIX))


def _hyena_filter(zfeat, window, w1, b1, freq, w2, b2, w3, table):
    consts = (zfeat, window, w1, b1, freq, w2, b2)
    return pl.pallas_call(
        _hyena_filter_kernel,
        grid=(HY_ORDER,),
        in_specs=[_const_spec(a.shape) for a in consts]
        + [pl.BlockSpec((HY_FFN, 2 * D_MIX), lambda o: (0, o)),
           _const_spec(table.shape)],
        out_specs=[pl.BlockSpec((None, 2, SEQ, D_MIX), lambda o: (o, 0, 0, 0)),
                   pl.BlockSpec((None, 8, D_MIX), lambda o: (o, 0, 0))],
        out_shape=[jax.ShapeDtypeStruct((HY_ORDER, 2, SEQ, D_MIX), F32),
                   jax.ShapeDtypeStruct((HY_ORDER, 8, D_MIX), F32)],
        scratch_shapes=[pltpu.VMEM((D_MIX // LANES, SEQ, LANES), F32)] * 2
        + [pltpu.VMEM((SEQ, HY_FFN), F32)],
        compiler_params=_params("arbitrary"),
        name="hyena_filter",
    )(*consts, w3, table)


def _short_conv(u, w, b):
    t = lax.broadcasted_iota(jnp.int32, u.shape, 0)
    prev = jnp.where(t == 0, 0.0, pltpu.roll(u, 1, 0))
    nxt = jnp.where(t == SEQ - 1, 0.0, pltpu.roll(u, SEQ - 1, 0))
    return prev * w[0:1] + u * w[1:2] + nxt * w[2:3] + b


def _hyena_stage_kernel(gate_ref, src_ref, wg_ref, bg_ref, ws_ref, bs_ref, cs_ref,
                        kf_ref, kny_ref, skip_ref, out_ref, z_s, yre_s, yim_s, gate_s, rest_s,
                        *, conv_src):
    for q in range(HY_SEQS):
        seq = slice(q * SEQ, (q + 1) * SEQ)
        zq, yre_q, yim_q = z_s.at[q], yre_s.at[q], yim_s.at[q]
        z = src_ref[seq, :]
        if conv_src:
            z = _short_conv(z, ws_ref[...], bs_ref[...])
        _store_slabs(zq, z)
        t = lax.broadcasted_iota(jnp.int32, z.shape, 0)
        sgn = (1 - 2 * (t & 1)).astype(F32)
        nyq = jnp.sum(z * sgn, axis=0, keepdims=True) * kny_ref[0:1]
        gate = _short_conv(gate_ref[seq, :], wg_ref[...], bg_ref[...])
        gate_s[seq, :] = gate
        rest_s[seq, :] = gate * (sgn * nyq + z * skip_ref[...])
        a, b = _dft_fwd(cs_ref, zq)
        for half in range(2):
            rows = slice(half * HALF, (half + 1) * HALF)
            kre = kf_ref[0, rows, :]
            kim = kf_ref[1, rows, :]
            _store_slabs(yre_q, a[half] * kre + b[half] * kim, rows)
            _store_slabs(yim_q, a[half] * kim - b[half] * kre, rows)
        for half, y in enumerate(_dft_inv(cs_ref, yre_q, yim_q)):
            rows = slice(q * SEQ + half * HALF, q * SEQ + (half + 1) * HALF)
            out_ref[rows, :] = (gate_s[rows, :] * y + rest_s[rows, :]).astype(out_ref.dtype)


def _hyena_stage(u_hy, gate_blk, src, src_blk, conv_w, conv_b, table, kf, kny, order, skip,
                 conv_src, out_dtype):
    rows = HY_SEQS * SEQ
    of_order = lambda a: pl.BlockSpec(
        (None,) + a.shape[1:], lambda b: (order,) + (0,) * (a.ndim - 1),
        pipeline_mode=pl.Buffered(1))
    col = lambda blk: pl.BlockSpec((rows, D_MIX), lambda b: (b, blk))
    wcol = lambda blk, r: pl.BlockSpec((r, D_MIX), lambda b: (0, blk))
    ws_blk = src_blk if conv_src else 0
    return pl.pallas_call(
        functools.partial(_hyena_stage_kernel, conv_src=conv_src),
        grid=(BATCH // HY_SEQS,),
        in_specs=[col(gate_blk), col(src_blk), wcol(gate_blk, 3), wcol(gate_blk, 1),
                  wcol(ws_blk, 3), wcol(ws_blk, 1), _const_spec(table.shape),
                  of_order(kf), of_order(kny), _const_spec(skip.shape)],
        out_specs=pl.BlockSpec((rows, D_MIX), lambda b: (b, 0)),
        out_shape=jax.ShapeDtypeStruct((TOKENS, D_MIX), out_dtype),
        scratch_shapes=[pltpu.VMEM((HY_SEQS, D_MIX // LANES, SEQ, LANES), F32)] * 3
        + [pltpu.VMEM((rows, D_MIX), F32)] * 2,
        compiler_params=_params("parallel"),
        name="hyena_stage",
    )(u_hy, src, conv_w, conv_b, conv_w, conv_b, table, kf, kny, skip)


def _fnet_kernel(xa_ref, xb_ref, cs_ref, cg_ref, sg_ref, out_ref):
    parts = []
    for r in range(2):
        xb = _residue_rows((xa_ref, xb_ref), r, 2).astype(BF16)
        xc = _dot(xb, cg_ref[...]).astype(BF16)
        xs = _dot(xb, sg_ref[...]).astype(BF16)
        parts.append(_dot(_table_cols(cs_ref, r, 2), jnp.concatenate([xc, -xs], axis=0)))
    scale = (SEQ * D_MIX // FN_GROUPS) ** -0.5
    out_ref[:HALF, :] = ((parts[0] + parts[1]) * scale).astype(out_ref.dtype)
    out_ref[HALF:, :] = ((parts[0] - parts[1]) * scale).astype(out_ref.dtype)


def _fnet(u_fn, table, cg, sg):
    return pl.pallas_call(
        _fnet_kernel,
        grid=(BATCH,),
        in_specs=[pl.BlockSpec((SEQ, LANES), lambda b: (b, 0)),
                  pl.BlockSpec((SEQ, LANES), lambda b: (b, 1)), _const_spec(table.shape),
                  _const_spec(cg.shape), _const_spec(sg.shape)],
        out_specs=pl.BlockSpec((SEQ, D_MIX), lambda b: (b, 0)),
        out_shape=jax.ShapeDtypeStruct((TOKENS, D_MIX), BF16),
        compiler_params=_params("parallel"),
        name="fnet",
    )(u_fn, u_fn, table, cg, sg)


def _softmax2_pv(s2, v):
    m = jnp.max(s2, axis=-1, keepdims=True)
    p = jnp.exp2(s2 - m)
    l = jnp.sum(p, axis=-1, keepdims=True)
    return _dot(p.astype(BF16), v) / l


def _mla_kernel(q_ref, k_ref, vt_ref, out_ref):
    outs = []
    for h in range(MLA_HEADS):
        sl = slice(h * MLA_HEAD_PAD, (h + 1) * MLA_HEAD_PAD)
        s2 = _dot_nt(k_ref[:, sl], q_ref[:, sl])
        p = jnp.exp2(s2 - jnp.max(s2, axis=0, keepdims=True))
        l = jnp.sum(p, axis=0, keepdims=True)
        o = _dot(vt_ref[h * MLA_V:(h + 1) * MLA_V, :], p.astype(BF16))
        outs.append(o / l)
    out_ref[...] = jnp.concatenate(outs, axis=0).T.astype(out_ref.dtype)


def _mla(q, k, vt):
    tm = TILE["mla"]
    nt = SEQ // tm
    return pl.pallas_call(
        _mla_kernel,
        grid=(BATCH, nt),
        in_specs=[pl.BlockSpec((tm, MLA_QK), lambda b, i: (b * nt + i, 0)),
                  pl.BlockSpec((SEQ, MLA_QK), lambda b, i: (b, 0)),
                  pl.BlockSpec((None, D_MIX, SEQ), lambda b, i: (b, 0, 0))],
        out_specs=pl.BlockSpec((tm, D_MIX), lambda b, i: (b * nt + i, 0)),
        out_shape=jax.ShapeDtypeStruct((TOKENS, D_MIX), BF16),
        compiler_params=_params("parallel", "parallel"),
        name="mla_attention",
    )(q, k, vt)


def _na_key_row0(j):
    return jnp.clip(j * NA_QROWS - NA_WIN_R // 2, 0, GRID_R - NA_KROWS)


NA_PAIRS = 2 * NA_WIN_R


def _na_kernel(q_ref, k_ref, v_ref, tile_ref, out_ref, bias_s):
    j = pl.program_id(0)
    krow0 = _na_key_row0(j)

    @pl.when(pl.program_id(1) == 0)
    def _():
        rq = j * NA_QROWS + lax.broadcasted_iota(jnp.int32, (NA_QBLK, NA_KBLK), 0) // GRID_W
        rk = krow0 + lax.broadcasted_iota(jnp.int32, (NA_QBLK, NA_KBLK), 1) // GRID_W
        rs = jnp.clip(rq - NA_WIN_R // 2, 0, GRID_R - NA_WIN_R)
        rowmask = jnp.where(rk < rs, NEG_BIG, jnp.where(rk >= rs + NA_WIN_R, NEG_BIG, 0.0))
        base = krow0 - j * NA_QROWS + NA_WIN_R
        for h in range(NA_HEADS):
            bias = jnp.concatenate(
                [jnp.concatenate(
                    [tile_ref[h, jnp.clip(base + 2 * kp - r, 0, NA_PAIRS - 1)]
                     for kp in range(NA_KROWS // 2)], axis=1)
                 for r in range(NA_QROWS)], axis=0)
            bias_s[h] = bias + rowmask

    off = pl.multiple_of(krow0 * GRID_W, GRID_W)
    q = q_ref[...]
    k = k_ref[pl.ds(off, NA_KBLK), :]
    v = v_ref[pl.ds(off, NA_KBLK), :]
    head = lax.broadcasted_iota(jnp.int32, (NA_QBLK, D_MIX), 1) // NA_HEAD_DIM
    acc = jnp.zeros((NA_QBLK, D_MIX), F32)
    for h in range(NA_HEADS):
        qh = jnp.where(head == h, q, jnp.zeros_like(q))
        s2 = _dot_nt(qh, k) + bias_s[h]
        acc = jnp.where(head == h, _softmax2_pv(s2, v), acc)
    out_ref[...] = acc.astype(out_ref.dtype)


def _neighborhood(q, k, v, tiles):
    nj = SEQ // NA_QBLK
    return pl.pallas_call(
        _na_kernel,
        grid=(nj, BATCH),
        in_specs=[pl.BlockSpec((NA_QBLK, D_MIX), lambda j, b: (b * nj + j, 0)),
                  pl.BlockSpec((SEQ, D_MIX), lambda j, b: (b, 0)),
                  pl.BlockSpec((SEQ, D_MIX), lambda j, b: (b, 0)),
                  _const_spec(tiles.shape)],
        out_specs=pl.BlockSpec((NA_QBLK, D_MIX), lambda j, b: (b * nj + j, 0)),
        out_shape=jax.ShapeDtypeStruct((TOKENS, D_MIX), BF16),
        scratch_shapes=[pltpu.VMEM((NA_HEADS, NA_QBLK, NA_KBLK), F32)],
        compiler_params=_params("parallel", "arbitrary"),
        name="neighborhood_attention",
    )(q, k, v, tiles)


def _na_bias_tiles(rpb):
    c = jnp.arange(GRID_W)
    cs = jnp.clip(c - NA_WIN_C // 2, 0, GRID_W - NA_WIN_C)
    col_ok = (c[None, :] >= cs[:, None]) & (c[None, :] < cs[:, None] + NA_WIN_C)
    dc = jnp.clip(c[None, :] - c[:, None] + (NA_WIN_C - 1), 0, 2 * NA_WIN_C - 2)
    pick = (dc[None] == jnp.arange(2 * NA_WIN_C - 1)[:, None, None]).astype(F32)
    t = jnp.einsum('hrd,dqk->hrqk', rpb.astype(F32), pick, precision=lax.Precision.HIGHEST)
    t = jnp.where(col_ok, t * LOG2E, NEG_BIG)
    t = jnp.pad(t, ((0, 0), (1, 1), (0, 0), (0, 0)))
    return jnp.concatenate([t[:, :-1], t[:, 1:]], axis=-1)


def _merge_kernel(x_ref, g1_ref, yhy_ref, yfn_ref, ymla_ref, yna_ref, wg_ref, bg_ref,
                  wbr_ref, wout_ref, g2_ref, wr_ref, out_ref, hb_ref, logit_ref):
    x = x_ref[...]
    h = _rms(x, g1_ref[...]).astype(BF16)
    merged = jnp.zeros(x.shape, F32)
    for n, y_ref in enumerate((yhy_ref, yfn_ref, ymla_ref, yna_ref)):
        sl = slice(n * D_MODEL, (n + 1) * D_MODEL)
        gate = jax.nn.sigmoid(_dot_nt(h, wg_ref[sl, :]) + bg_ref[:, sl])
        merged = merged + gate * _dot(y_ref[...], wbr_ref[n])
    x1 = x + _dot(merged.astype(BF16), wout_ref[...])
    out_ref[...] = x1
    h_hi, h_lo = _split2(_rms(x1, g2_ref[...]))
    hb_ref[...] = h_hi
    w_hi, w_lo = _split2(wr_ref[...])
    logits = _dot_nt(w_hi, h_hi) + _dot_nt(w_hi, h_lo) + _dot_nt(w_lo, h_hi)
    logit_ref[...] = logits[:N_EXPERTS]


def _merge(x, g1, yhy, yfn, ymla, yna, wg, bg, wbr, wout, g2, wr_t):
    tm = TILE["merge"]
    nt = SEQ // tm
    row = lambda n: pl.BlockSpec((tm, n), lambda i: (i, 0))
    return pl.pallas_call(
        _merge_kernel,
        grid=(TOKENS // tm,),
        in_specs=[row(D_MODEL), _const_spec(g1.shape), row(D_MIX), row(D_MIX), row(D_MIX),
                  row(D_MIX), _const_spec(wg.shape), _const_spec(bg.shape),
                  _const_spec(wbr.shape), _const_spec(wout.shape), _const_spec(g2.shape),
                  _const_spec(wr_t.shape)],
        out_specs=[row(D_MODEL), row(D_MODEL),
                   pl.BlockSpec((None, N_EXPERTS, tm), lambda i: (i // nt, 0, i % nt))],
        out_shape=[jax.ShapeDtypeStruct((TOKENS, D_MODEL), F32),
                   jax.ShapeDtypeStruct((TOKENS, D_MODEL), BF16),
                   jax.ShapeDtypeStruct((BATCH, N_EXPERTS, SEQ), F32)],
        compiler_params=_params("parallel"),
        name="merge",
    )(x, g1, yhy, yfn, ymla, yna, wg, bg, wbr, wout, g2, wr_t)


def _prefix_count(m):
    r = lax.broadcasted_iota(jnp.int32, (LANES, LANES), 0)
    c = lax.broadcasted_iota(jnp.int32, (LANES, LANES), 1)
    upper = jnp.where(r < c, 1.0, 0.0).astype(BF16)
    run = jnp.zeros((m.shape[0], 1), F32)
    parts = []
    for i in range(SEQ // LANES):
        chunk = m[:, i * LANES:(i + 1) * LANES]
        parts.append(_dot(chunk.astype(BF16), upper) + run)
        run = run + jnp.sum(chunk, axis=1, keepdims=True)
    return jnp.concatenate(parts, axis=1)


SELECT_MAX_ITERS = 192


def _select_kernel(logit_ref, slot_row_ref, slot_col_ref, w_row_ref, start_ref, slot_s, w_s):
    b = pl.program_id(0)
    rows = BATCH * N_EXPERTS

    @pl.when(b == 0)
    def _():
        logits = logit_ref[...]
        ex = jnp.exp(logits - jnp.max(logits, axis=1, keepdims=True))
        aff = (ex / jnp.sum(ex, axis=1, keepdims=True)).reshape(rows, SEQ)

        def bisect(c):
            it, lo, hi, _ = c
            mid = 0.5 * (lo + hi)
            cnt = jnp.sum(jnp.where(aff >= mid, 1.0, 0.0), axis=1, keepdims=True)
            moving = jnp.where(mid == lo, 0.0, jnp.where(mid == hi, 0.0, 1.0))
            enough = cnt >= CAPACITY
            return (it + 1, jnp.where(enough, mid, lo), jnp.where(enough, hi, mid),
                    (jnp.max(moving) > 0).astype(jnp.int32))

        _, lo, hi, _ = lax.while_loop(
            lambda c: (c[0] < SELECT_MAX_ITERS) & (c[3] > 0), bisect,
            (jnp.int32(0), jnp.zeros((rows, 1), F32), jnp.full((rows, 1), 2.0, F32),
             jnp.int32(1)))
        above = jnp.where(aff >= hi, 1.0, 0.0)
        band = jnp.where(aff >= lo, 1.0, 0.0) - above
        need = CAPACITY - jnp.sum(above, axis=1, keepdims=True)
        sel = above + band * jnp.where(_prefix_count(band) < need, 1.0, 0.0)
        slot_s[...] = jnp.where(sel > 0, _prefix_count(sel), -1.0)
        w_s[...] = sel * aff

    r0 = pl.multiple_of(b * N_EXPERTS, N_EXPERTS)
    slot = slot_s[pl.ds(r0, N_EXPERTS), :]
    slot_row_ref[...] = slot.astype(jnp.int32)
    pad = jnp.full((LANES - N_EXPERTS, SEQ), -1.0, F32)
    slot_col_ref[...] = jnp.concatenate([slot, pad], axis=0).T.astype(jnp.int32)
    w_row_ref[...] = w_s[pl.ds(r0, N_EXPERTS), :]
    tm = MOE_FINE
    token = lax.broadcasted_iota(jnp.int32, slot.shape, 1)
    lane = lax.broadcasted_iota(jnp.int32, (N_EXPERTS, LANES), 1)
    starts = jnp.where(lane == SEQ // tm, float(CAPACITY), 0.0)
    for i in range(1, SEQ // tm):
        before = jnp.where(slot >= 0, jnp.where(token < i * tm, 1.0, 0.0), 0.0)
        starts = starts + jnp.where(lane == i, jnp.sum(before, axis=1, keepdims=True), 0.0)
    start_ref[...] = starts.astype(jnp.int32)


def _select(logits):
    return pl.pallas_call(
        _select_kernel,
        grid=(BATCH,),
        in_specs=[_const_spec(logits.shape)],
        out_specs=[pl.BlockSpec((None, N_EXPERTS, SEQ), lambda b: (b, 0, 0)),
                   pl.BlockSpec((None, SEQ, LANES), lambda b: (b, 0, 0)),
                   pl.BlockSpec((None, N_EXPERTS, SEQ), lambda b: (b, 0, 0)),
                   pl.BlockSpec((None, N_EXPERTS, LANES), lambda b: (b, 0, 0))],
        out_shape=[jax.ShapeDtypeStruct((BATCH, N_EXPERTS, SEQ), jnp.int32),
                   jax.ShapeDtypeStruct((BATCH, SEQ, LANES), jnp.int32),
                   jax.ShapeDtypeStruct((BATCH, N_EXPERTS, SEQ), F32),
                   jax.ShapeDtypeStruct((BATCH, N_EXPERTS, LANES), jnp.int32)],
        scratch_shapes=[pltpu.VMEM((BATCH * N_EXPERTS, SEQ), F32),
                        pltpu.VMEM((BATCH * N_EXPERTS, SEQ), F32)],
        compiler_params=_params("arbitrary"),
        name="expert_select",
    )(logits)


MXU_DEPTH = 256
BF16_ROWS = 16


def _slot_window(tokens):
    return 2 * tokens * CAPACITY // SEQ


def _tile_windows(start_ref, b, i, tokens):
    stride = start_ref.shape[0] // (BATCH * N_EXPERTS)
    fine = tokens // MOE_FINE
    window = _slot_window(tokens)
    base, fits = [], None
    for e in range(N_EXPERTS):
        at = (b * N_EXPERTS + e) * stride + i * fine
        lo = jnp.minimum(start_ref[at] // BF16_ROWS * BF16_ROWS, CAPACITY - window)
        ok = start_ref[at + fine] - lo <= window
        fits = ok if fits is None else fits & ok
        base.append(pl.multiple_of(lo, BF16_ROWS))
    return base, fits


def _gather_kernel(start_ref, slot_ref, w_ref, hb_ref, xe_ref, wsl_ref):
    b = pl.program_id(0)
    i = pl.program_id(1)
    tokens = slot_ref.shape[1]

    @pl.when(i == 0)
    def _():
        xe_ref[...] = jnp.zeros(xe_ref.shape, BF16)
        wsl_ref[...] = jnp.zeros(wsl_ref.shape, F32)

    def add_rows(e, rows, hit, picked):
        xe_ref[e, rows, :] = xe_ref[e, rows, :] + picked.astype(BF16)
        wslot = jnp.sum(jnp.where(hit, w_ref[e:e + 1, :], 0.0), axis=1, keepdims=True)
        wsl_ref[e, rows, :] = wsl_ref[e, rows, :] + jnp.broadcast_to(wslot, (hit.shape[0], LANES))

    window = _slot_window(tokens)
    base, fits = _tile_windows(start_ref, b, i, tokens)

    @pl.when(fits)
    def _():
        c = lax.broadcasted_iota(jnp.int32, (window, tokens), 0)
        hits = [slot_ref[e:e + 1, :] - base[e] == c for e in range(N_EXPERTS)]
        onehot = jnp.concatenate([jnp.where(m, 1.0, 0.0).astype(BF16) for m in hits], axis=0)
        picked = _dot(onehot, hb_ref[...])
        for e in range(N_EXPERTS):
            add_rows(e, pl.ds(base[e], window), hits[e], picked[e * window:(e + 1) * window])

    @pl.when(jnp.logical_not(fits))
    def _():
        c = lax.broadcasted_iota(jnp.int32, (CAPACITY, tokens), 0)
        for e in range(N_EXPERTS):
            hit = slot_ref[e:e + 1, :] == c
            picked = _dot(jnp.where(hit, 1.0, 0.0).astype(BF16), hb_ref[...])
            add_rows(e, slice(None), hit, picked)


def _gather(slot_row, w_row, starts, hb):
    tm = TILE["gather"]
    nt = SEQ // tm
    grid_spec = pltpu.PrefetchScalarGridSpec(
        num_scalar_prefetch=1,
        grid=(BATCH, nt),
        in_specs=[pl.BlockSpec((None, N_EXPERTS, tm), lambda b, i, s: (b, 0, i)),
                  pl.BlockSpec((None, N_EXPERTS, tm), lambda b, i, s: (b, 0, i)),
                  pl.BlockSpec((tm, D_MODEL), lambda b, i, s: (b * nt + i, 0))],
        out_specs=[pl.BlockSpec((N_EXPERTS, None, CAPACITY, D_MODEL), lambda b, i, s: (0, b, 0, 0)),
                   pl.BlockSpec((N_EXPERTS, None, CAPACITY, LANES), lambda b, i, s: (0, b, 0, 0))])
    return pl.pallas_call(
        _gather_kernel,
        grid_spec=grid_spec,
        out_shape=[jax.ShapeDtypeStruct((N_EXPERTS, BATCH, CAPACITY, D_MODEL), BF16),
                   jax.ShapeDtypeStruct((N_EXPERTS, BATCH, CAPACITY, LANES), F32)],
        compiler_params=_params("parallel", "arbitrary"),
        name="expert_gather",
    )(starts[:, :, :SEQ // MOE_FINE + 1].reshape(-1), slot_row, w_row, hb)


def _expert_kernel(xe_ref, wsl_ref, wg_ref, wu_ref, wd_ref, ye_ref, wg_s, wu_s, wd_s, *, span):
    ph = pl.program_id(0)
    i = pl.program_id(1)
    rows = pl.ds(pl.multiple_of(i * span, span), span)
    nxt = ph % 2
    wg_s[nxt, rows, :] = wg_ref[rows, :].astype(BF16)
    wu_s[nxt, rows, :] = wu_ref[rows, :].astype(BF16)
    wd_s[nxt, rows, :] = wd_ref[rows, :].astype(BF16)

    @pl.when(ph > 0)
    def _():
        cur = (ph + 1) % 2
        xe = xe_ref[...]
        g = _dot(xe, wg_s[cur])
        u = _dot(xe, wu_s[cur])
        act = (g * jax.nn.sigmoid(g) * u).astype(BF16)
        ye_ref[...] = (_dot(act, wd_s[cur]) * wsl_ref[:, 0:1]).astype(BF16)


def _experts(xe, wsl, wg, wu, wd, layer):
    rows = BATCH * CAPACITY
    tm = TILE["expert"]
    nt = rows // tm
    last = N_EXPERTS - 1
    wspec = lambda a: pl.BlockSpec((None, None) + a.shape[2:],
                                   lambda ph, i: (layer, jnp.minimum(ph, last), 0, 0))
    data = lambda n: pl.BlockSpec(
        (None, tm, n), lambda ph, i: (jnp.maximum(ph - 1, 0), jnp.where(ph > 0, i, 0), 0))
    return pl.pallas_call(
        functools.partial(_expert_kernel, span=D_MODEL // nt),
        grid=(N_EXPERTS + 1, nt),
        in_specs=[data(D_MODEL), data(LANES), wspec(wg), wspec(wu), wspec(wd)],
        out_specs=data(D_MODEL),
        out_shape=jax.ShapeDtypeStruct((N_EXPERTS, rows, D_MODEL), BF16),
        scratch_shapes=[pltpu.VMEM((2, D_MODEL, D_FF), BF16), pltpu.VMEM((2, D_MODEL, D_FF), BF16),
                        pltpu.VMEM((2, D_FF, D_MODEL), BF16)],
        compiler_params=_params("arbitrary", "arbitrary"),
        name="expert_ffn",
    )(xe.reshape(N_EXPERTS, rows, D_MODEL), wsl.reshape(N_EXPERTS, rows, LANES), wg, wu, wd)


def _combine_kernel(start_ref, x_ref, ye_ref, slot_ref, p_ref, g3_ref, wpg_ref, wpp_ref,
                    gf_ref, out_ref, moe_s, *, final_norm):
    slot = slot_ref[...]
    rows = slot.shape[0]
    window = _slot_window(rows)
    base, fits = _tile_windows(start_ref, pl.program_id(0), pl.program_id(1), rows)

    @pl.when(fits)
    def _():
        c = lax.broadcasted_iota(jnp.int32, (rows, window), 1)
        parts = []
        group = MXU_DEPTH // window
        for e0 in range(0, N_EXPERTS, group):
            hot, win = [], []
            for e in range(e0, e0 + group):
                hot.append(jnp.where(slot[:, e:e + 1] - base[e] == c, 1.0, 0.0).astype(BF16))
                win.append(ye_ref[e, pl.ds(base[e], window), :])
            parts.append(_dot(jnp.concatenate(hot, axis=1), jnp.concatenate(win, axis=0)))
        while len(parts) > 1:
            parts = [u + v for u, v in zip(parts[::2], parts[1::2])]
        moe_s[...] = parts[0]

    @pl.when(jnp.logical_not(fits))
    def _():
        c = lax.broadcasted_iota(jnp.int32, (rows, CAPACITY), 1)
        onehot = jnp.concatenate(
            [jnp.where(slot[:, e:e + 1] == c, 1.0, 0.0).astype(BF16) for e in range(N_EXPERTS)],
            axis=1)
        moe_s[...] = _dot(onehot, ye_ref[...].reshape(N_EXPERTS * CAPACITY, D_MODEL))

    acc = x_ref[...] + moe_s[...]
    h = _rms(acc, g3_ref[...]).astype(BF16)
    gate = jax.nn.sigmoid(_dot(h, wpg_ref[...]))
    y = acc + gate * _dot(p_ref[...].astype(BF16), wpp_ref[...])
    if final_norm:
        y = _rms(y, gf_ref[...])
    out_ref[...] = y


def _combine(x, ye, slot_col, starts, p, layer, g3, wpg, wpp, gf, final_norm):
    tm = TILE["combine"]
    nt = SEQ // tm
    p0 = layer * (TOKENS // tm)
    const = lambda a: pl.BlockSpec(a.shape, lambda b, i, s: (0,) * a.ndim,
                                   pipeline_mode=pl.Buffered(1))
    grid_spec = pltpu.PrefetchScalarGridSpec(
        num_scalar_prefetch=1,
        grid=(BATCH, nt),
        in_specs=[pl.BlockSpec((tm, D_MODEL), lambda b, i, s: (b * nt + i, 0)),
                  pl.BlockSpec((N_EXPERTS, None, CAPACITY, D_MODEL), lambda b, i, s: (0, b, 0, 0)),
                  pl.BlockSpec((None, tm, LANES), lambda b, i, s: (b, i, 0)),
                  pl.BlockSpec((tm, PLE_DIM), lambda b, i, s: (p0 + b * nt + i, 0)),
                  const(g3), const(wpg), const(wpp), const(gf)],
        out_specs=pl.BlockSpec((tm, D_MODEL), lambda b, i, s: (b * nt + i, 0)),
        scratch_shapes=[pltpu.VMEM((tm, D_MODEL), F32)])
    return pl.pallas_call(
        functools.partial(_combine_kernel, final_norm=final_norm),
        grid_spec=grid_spec,
        out_shape=jax.ShapeDtypeStruct((TOKENS, D_MODEL), F32),
        compiler_params=_params("parallel", "parallel"),
        name="combine",
    )(starts[:, :, :SEQ // MOE_FINE + 1].reshape(-1), x,
      ye.reshape(N_EXPERTS, BATCH, CAPACITY, D_MODEL),
      slot_col, p, g3, wpg, wpp, gf)


DFT_FINE = 64
DFT_STEP = 4


def _dft_kernel(ca_ref, sa_ref, cb_ref, sb_ref, out_ref):
    cb = cb_ref[...]
    sb = sb_ref[...]
    for r in range(DFT_STEP):
        rows = slice(r * DFT_FINE, (r + 1) * DFT_FINE)
        out_ref[rows, :] = (ca_ref[r] * cb - sa_ref[r] * sb).astype(BF16)


def _dft_tables(n_points, nres):
    blk = SEQ // nres
    t = jnp.arange(SEQ, dtype=jnp.int32).reshape(blk, nres).T
    t = jnp.concatenate([t, t], axis=1).reshape(1, 2 * SEQ)
    is_sin = (jnp.arange(2 * SEQ) // blk % 2 == 1)[None, :]
    coarse = jnp.arange(HALF // DFT_FINE, dtype=jnp.int32)[:, None] * DFT_FINE
    fine = jnp.arange(DFT_FINE, dtype=jnp.int32)[:, None]
    ang = lambda f: ((f * t) % n_points).astype(F32) * (2.0 * math.pi / n_points)
    ca = jnp.cos(ang(coarse))[:, None, :]
    sa = jnp.sin(ang(coarse))[:, None, :]
    cb = jnp.where(is_sin, jnp.sin(ang(fine)), jnp.cos(ang(fine)))
    sb = jnp.where(is_sin, -jnp.cos(ang(fine)), jnp.sin(ang(fine)))
    rows = DFT_STEP * DFT_FINE
    return pl.pallas_call(
        _dft_kernel,
        grid=(HALF // rows,),
        in_specs=[pl.BlockSpec((DFT_STEP, 1, 2 * SEQ), lambda i: (i, 0, 0)),
                  pl.BlockSpec((DFT_STEP, 1, 2 * SEQ), lambda i: (i, 0, 0)),
                  _const_spec(cb.shape), _const_spec(sb.shape)],
        out_specs=pl.BlockSpec((rows, 2 * SEQ), lambda i: (i, 0)),
        out_shape=jax.ShapeDtypeStruct((HALF, 2 * SEQ), BF16),
        compiler_params=_params("parallel"),
        name="dft_tables",
    )(ca, sa, cb, sb)


def _fnet_group_tables():
    gc = D_MIX // FN_GROUPS
    i = lax.broadcasted_iota(jnp.int32, (D_MIX, D_MIX), 0)
    j = lax.broadcasted_iota(jnp.int32, (D_MIX, D_MIX), 1)
    same = (i // gc) == (j // gc)
    ang = (((i % gc) * (j % gc)) % gc).astype(F32) * (2.0 * math.pi / gc)
    return (jnp.where(same, jnp.cos(ang), 0.0).astype(BF16),
            jnp.where(same, jnp.sin(ang), 0.0).astype(BF16))


def _hyena_features():
    t01 = jnp.linspace(0.0, 1.0, SEQ, dtype=F32)[:, None]
    bands = jnp.linspace(1e-4, HY_BANDS - 1, HY_BANDS, dtype=F32)
    ang = 2.0 * math.pi * jnp.arange(SEQ, dtype=F32)[:, None] * bands / SEQ
    z = jnp.concatenate([t01, jnp.cos(ang), -jnp.sin(ang)], axis=-1)
    z = jnp.pad(z, ((0, 0), (0, LANES - HY_EMB)))
    max_decay = math.log(HY_TARGET) / HY_FAST_DECAY
    min_decay = math.log(HY_TARGET) / HY_SLOW_DECAY
    deltas = jnp.linspace(min_decay, max_decay, D_MIX, dtype=F32)
    window = jnp.exp(-t01 * jnp.abs(deltas))
    return z, window


def _rot_cols(w):
    half = w.shape[-1] // 2
    return jnp.concatenate([-w[..., half:], w[..., :half]], axis=-1)


def _rope_tables():
    inv = ROPE_THETA ** (-jnp.arange(0, MLA_ROPE, 2, dtype=F32) / MLA_ROPE)
    ang = jnp.arange(SEQ, dtype=F32)[:, None] * inv
    cos = jnp.concatenate([jnp.cos(ang), jnp.cos(ang)], axis=-1)
    sin = jnp.concatenate([jnp.sin(ang), jnp.sin(ang)], axis=-1)
    scale = (MLA_NOPE + MLA_ROPE) ** -0.5 * LOG2E
    pad = MLA_HEAD_PAD - MLA_NOPE - MLA_ROPE
    one = jnp.ones((SEQ, MLA_NOPE), F32)
    zero = jnp.zeros((SEQ, MLA_NOPE), F32)
    zpad = jnp.zeros((SEQ, pad), F32)
    cosq = jnp.tile(jnp.concatenate([one, cos, zpad], axis=-1) * scale, (1, MLA_HEADS))
    sinq = jnp.tile(jnp.concatenate([zero, sin, zpad], axis=-1) * scale, (1, MLA_HEADS))
    csk = jnp.concatenate([cos, sin, jnp.zeros((SEQ, LANES - 2 * MLA_ROPE), F32)], axis=-1)
    return cosq, sinq, csk


def _mla_weights(w_uq, w_ukv):
    pad = MLA_HEAD_PAD - MLA_NOPE - MLA_ROPE
    wq = w_uq.reshape(MLA_Q_RANK, MLA_HEADS, MLA_NOPE + MLA_ROPE)
    nope, pe = wq[..., :MLA_NOPE], wq[..., MLA_NOPE:]
    zp = jnp.zeros((MLA_Q_RANK, MLA_HEADS, pad), F32)
    wqa = jnp.concatenate([nope, pe, zp], axis=-1).reshape(MLA_Q_RANK, MLA_QK)
    wqb = jnp.concatenate([jnp.zeros_like(nope), _rot_cols(pe), zp], axis=-1)
    wqb = wqb.reshape(MLA_Q_RANK, MLA_QK)
    wkv = w_ukv.reshape(MLA_KV_RANK, MLA_HEADS, MLA_NOPE + MLA_V)
    knope, v = wkv[..., :MLA_NOPE], wkv[..., MLA_NOPE:]
    wk = jnp.concatenate(
        [knope, jnp.zeros((MLA_KV_RANK, MLA_HEADS, MLA_HEAD_PAD - MLA_NOPE), F32)], axis=-1)
    wk = wk.reshape(MLA_KV_RANK, MLA_QK)
    wv = v.reshape(MLA_KV_RANK, MLA_HEADS * MLA_V).T
    r = lax.broadcasted_iota(jnp.int32, (LANES, MLA_QK), 0)
    c = lax.broadcasted_iota(jnp.int32, (LANES, MLA_QK), 1)
    epe = jnp.where((r < 2 * MLA_ROPE) & (c % MLA_HEAD_PAD == MLA_NOPE + r % MLA_ROPE), 1.0, 0.0)
    return (wqa.astype(BF16), wqb.astype(BF16), wk.astype(BF16), wv.astype(BF16),
            epe.astype(BF16))


def kernel(x, p, norm1_g, w_in, b_gate, hy_conv_w, hy_conv_b, hf_w1, hf_b1, hf_freq, hf_w2,
           hf_b2, hf_w3, hy_skip, q_norm_g, w_uq, kv_norm_g, w_ukv, rpb, w_br, w_out, norm2_g,
           w_router, w_e_gate, w_e_up, w_e_down, norm3_g, w_ple_gate, w_ple_proj, final_g):
    conv_tab = _dft_tables(2 * SEQ, 4)
    fnet_tab = _dft_tables(SEQ, 2)
    fnet_cg, fnet_sg = _fnet_group_tables()
    zfeat, window = _hyena_features()
    cosq, sinq, csk = _rope_tables()
    row = lambda a: a.reshape(1, -1)

    xt = x.reshape(TOKENS, D_MODEL)
    w_in_t = jnp.swapaxes(w_in, 1, 2)
    for i in range(DEPTH):
        g1 = row(norm1_g[i])
        wa, wna, wgate = _inproj_weights(w_in_t, i)
        wqa, wqb, wk, wv, epe = _mla_weights(w_uq[i], w_ukv[i])

        u_hy, u_fn, q, k, v, naq, nak, nav = _inproj(
            xt, g1, wa, wna, row(q_norm_g[i]), wqa, wqb, row(kv_norm_g[i]), wk, wv, epe,
            cosq, sinq, csk)

        w1 = jnp.pad(hf_w1[i], ((0, LANES - HY_EMB), (0, 0)))
        kf, kny = _hyena_filter(zfeat, window, w1, row(hf_b1[i]), hf_freq[i], hf_w2[i],
                                row(hf_b2[i]), hf_w3[i], conv_tab)
        conv_b = row(hy_conv_b[i])
        z1 = _hyena_stage(u_hy, 0, u_hy, 2, hy_conv_w[i], conv_b, conv_tab, kf, kny, 0,
                          row(hy_skip[i, 0]), True, F32)
        y_hy = _hyena_stage(u_hy, 1, z1, 0, hy_conv_w[i], conv_b, conv_tab, kf, kny, 1,
                            row(hy_skip[i, 1]), False, BF16)
        y_fn = _fnet(u_fn, fnet_tab, fnet_cg, fnet_sg)
        y_mla = _mla(q, k, v)
        y_na = _neighborhood(naq, nak, nav, _na_bias_tiles(rpb[i]))
        wr_t = jnp.pad(w_router[i].T, ((0, LANES - N_EXPERTS), (0, 0)))
        xt, hb, logits = _merge(xt, g1, y_hy, y_fn, y_mla, y_na, wgate, row(b_gate[i]),
                                w_br[i].astype(BF16), w_out[i].astype(BF16),
                                row(norm2_g[i]), wr_t)
        slot_row, slot_col, w_row, starts = _select(logits)
        xe, wsl = _gather(slot_row, w_row, starts, hb)
        ye = _experts(xe, wsl, w_e_gate, w_e_up, w_e_down, i)
        xt = _combine(xt, ye, slot_col, starts, p.reshape(DEPTH * TOKENS, PLE_DIM), i,
                      row(norm3_g[i]), w_ple_gate[i].astype(BF16),
                      w_ple_proj[i].astype(BF16), row(final_g), i == DEPTH - 1)
    return xt.reshape(BATCH, SEQ, D_MODEL)
```
